```python
import math
import jax, jax.numpy as jnp
from jax import lax
import numpy as np

D_MODEL = 1024
BATCH = 4
SEQ = 4096
DEPTH = 2

HEAD_DIM = 64
EPS = 1e-6
A_HEADS = 8
A_KV_HEADS = 2
WINDOW = 128
BAND = 128
B_HEADS = 8
B_HEAD_DIM = 64
B_WIDTH = B_HEADS * B_HEAD_DIM
B_DECAY_RANK = 64
B_A_RANK = 64
B_GATE_RANK = 128
B_LN_EPS = 64e-5
C_HEADS = 8
C_KV_HEADS = 2
MOBA_BLOCK = 256
MOBA_TOPK = 3
C_QCHUNK = 128
D_HEADS = 8
IDX_HEADS = 4
IDX_DIM = 64
DSA_TOPK = 256
D_QCHUNK = 128
N_BUCKETS = 32
MAX_DISTANCE = 1024
BIAS_HEADS = 8
MEM_LEN = 256
X_HEADS = 4
X_HEAD_DIM = 128
X_WIDTH = X_HEADS * X_HEAD_DIM
D_FF = 4 * D_MODEL
A_Q = A_HEADS * HEAD_DIM
A_KV = A_KV_HEADS * HEAD_DIM
A_IN = A_Q + 2 * A_KV
B_IN = 3 * B_WIDTH + B_DECAY_RANK + B_A_RANK + B_GATE_RANK
EVEN_IN = A_IN + B_IN
C_Q = C_HEADS * HEAD_DIM
C_KV = C_KV_HEADS * HEAD_DIM
D_Q = D_HEADS * HEAD_DIM
ODD_IN = C_Q + 2 * C_KV + D_Q + 2 * HEAD_DIM + IDX_HEADS * IDX_DIM + IDX_DIM + IDX_HEADS
MIX_WIDTH = A_Q + B_WIDTH

kernel_name = 'hybrid_swa_rwkv7_moba_dsa_block'


def rms_norm(x, g):
    xf = x.astype(jnp.float32)
    y = xf * lax.rsqrt(jnp.mean(xf * xf, axis=-1, keepdims=True) + EPS)
    return (y * g.astype(jnp.float32)).astype(x.dtype)


def rel_bucket(dist):
    n = jnp.maximum(dist, 0)
    exact = N_BUCKETS // 2
    log_ratio = jnp.log(jnp.maximum(n, 1).astype(jnp.float32) / exact) / math.log(MAX_DISTANCE / exact)
    large = jnp.minimum(exact + (log_ratio * (N_BUCKETS - exact)).astype(jnp.int32), N_BUCKETS - 1)
    return jnp.where(n < exact, n, large)


def swa_sink_attention(q, k, v, sinks, rel_bias):
    Bsz, T = q.shape[:2]
    nb = T // BAND
    G = A_HEADS // A_KV_HEADS
    qb = q.reshape(Bsz, nb, BAND, A_KV_HEADS, G, HEAD_DIM)

    def with_prev(z):
        zb = z.reshape(Bsz, nb, BAND, A_KV_HEADS, HEAD_DIM)
        prev = jnp.pad(zb, ((0, 0), (1, 0), (0, 0), (0, 0), (0, 0)))[:, :-1]
        return jnp.concatenate([prev, zb], axis=2)

    kc, vc = with_prev(k), with_prev(v)
    scores = jnp.einsum('bnqhgd,bnkhd->bnhgqk', qb, kc).astype(jnp.float32) * HEAD_DIM ** -0.5
    qi = jnp.arange(BAND)[:, None]
    kj = jnp.arange(2 * BAND)[None, :]
    dist = qi + BAND - kj
    bias = rel_bias[rel_bucket(dist)].reshape(BAND, 2 * BAND, A_KV_HEADS, G).transpose(2, 3, 0, 1)
    first = (jnp.arange(nb) == 0)[:, None, None] & (kj < BAND)[None]
    mask = ((dist >= 0) & (dist < WINDOW))[None] & ~first
    logits = jnp.where(mask[None, :, None, None], scores + bias.astype(jnp.float32), -jnp.inf)
    sink = sinks.astype(jnp.float32).reshape(A_KV_HEADS, G)[None, None, :, :, None, None]
    m = jnp.maximum(jnp.max(logits, axis=-1, keepdims=True), sink)
    p = jnp.exp(logits - m)
    probs = p / (jnp.sum(p, axis=-1, keepdims=True) + jnp.exp(sink - m))
    out = jnp.einsum('bnhgqk,bnkhd->bnqhgd', probs.astype(v.dtype), vc)
    return out.reshape(Bsz, T, A_Q)


def rwkv7_time_mix(p, mu, w0, w2, a0, a2, g2, k_k, k_a, r_k, ln_w, ln_b):
    Bsz, T, _ = p.shape
    H, N, W = B_HEADS, B_HEAD_DIM, B_WIDTH
    p_prev = jnp.pad(p, ((0, 0), (1, 0), (0, 0)))[:, :-1]
    p = p + (p_prev - p) * mu
    r, k, v, w_lo, a_lo, g_lo = jnp.split(p, [W, 2 * W, 3 * W, 3 * W + B_DECAY_RANK, 3 * W + B_DECAY_RANK + B_A_RANK], axis=-1)
    w = -jax.nn.softplus(-(w0 + jnp.tanh(w_lo) @ w2).astype(jnp.float32)) - 0.5
    decay = jnp.exp(-jnp.exp(w))
    a = jax.nn.sigmoid((a0 + a_lo @ a2).astype(jnp.float32))
    g = jax.nn.sigmoid(g_lo) @ g2
    kk = (k * k_k).astype(jnp.float32).reshape(Bsz, T, H, N)
    kk = kk / jnp.maximum(jnp.sqrt(jnp.sum(kk * kk, axis=-1, keepdims=True)), 1e-12)
    k = k.astype(jnp.float32) * (1.0 + (a - 1.0) * k_a.astype(jnp.float32))

    def heads(z):
        return z.astype(jnp.float32).reshape(Bsz, T, H, N)

    rh, wh, kh, vh, ah = heads(r), heads(decay), heads(k), heads(v), heads(a)
    xs = tuple(jnp.moveaxis(z, 1, 0) for z in (rh, wh, kh, vh, kk, ah))

    def step(S, inp):
        r_t, w_t, k_t, v_t, kk_t, a_t = inp
        s_kk = jnp.einsum('bhvk,bhk->bhv', S, kk_t)
        S = S * w_t[:, :, None, :] - s_kk[..., None] * (kk_t * a_t)[:, :, None, :] + v_t[..., None] * k_t[:, :, None, :]
        return S, jnp.einsum('bhvk,bhk->bhv', S, r_t)

    S0 = jnp.zeros((Bsz, H, N, N), jnp.float32)
    _, y = lax.scan(step, S0, xs)
    y = jnp.moveaxis(y, 0, 1)
    mean = jnp.mean(y, axis=-1, keepdims=True)
    var = jnp.mean(jnp.square(y - mean), axis=-1, keepdims=True)
    yn = ((y - mean) * lax.rsqrt(var + B_LN_EPS)).reshape(Bsz, T, W) * ln_w.astype(jnp.float32) + ln_b.astype(jnp.float32)
    bonus = jnp.sum(rh * kh * r_k.astype(jnp.float32), axis=-1, keepdims=True) * vh
    out = (yn + bonus.reshape(Bsz, T, W)) * g.astype(jnp.float32)
    return out.astype(p.dtype)


def moba_attention(q, k, v, rel_bias):
    Bsz, T = q.shape[:2]
    G = C_HEADS // C_KV_HEADS
    nb = -(-T // MOBA_BLOCK)
    Tp = nb * MOBA_BLOCK

    def pad_t(z):
        return jnp.pad(z, ((0, 0), (0, Tp - T)) + ((0, 0),) * (z.ndim - 2))

    q, k, v = pad_t(q), pad_t(k), pad_t(v)
    kbh = jnp.moveaxis(k.reshape(Bsz, nb, MOBA_BLOCK, C_KV_HEADS, HEAD_DIM), 3, 1)
    vbh = jnp.moveaxis(v.reshape(Bsz, nb, MOBA_BLOCK, C_KV_HEADS, HEAD_DIM), 3, 1)
    kmean = jnp.mean(kbh.astype(jnp.float32), axis=3)
    n_sel = min(MOBA_TOPK, nb)
    nq = Tp // C_QCHUNK
    qc = jnp.moveaxis(q.reshape(Bsz, nq, C_QCHUNK, C_KV_HEADS, G, HEAD_DIM), 1, 0)
    rb = rel_bias.reshape(N_BUCKETS, C_KV_HEADS, G)
    bidx = jnp.arange(Bsz)[:, None, None, None]
    hidx = jnp.arange(C_KV_HEADS)[None, None, :, None]
    blk_ids = jnp.arange(nb)
    offs = jnp.arange(MOBA_BLOCK)
    scale = HEAD_DIM ** -0.5

    def one_chunk(args):
        qx, ci = args
        t = ci * C_QCHUNK + jnp.arange(C_QCHUNK)
        own = (ci * C_QCHUNK) // MOBA_BLOCK
        gate = jnp.einsum('bqhgd,bhnd->bqhn', qx.astype(jnp.float32), kmean)
        gate = jnp.where(blk_ids < own, gate, -jnp.inf)
        _, sel = lax.top_k(gate, n_sel)
        ok = sel < own
        kg = kbh[bidx, hidx, sel]
        vg = vbh[bidx, hidx, sel]
        s_sel = jnp.einsum('bqhgd,bqhnkd->bqhgnk', qx, kg).astype(jnp.float32) * scale
        key_pos = sel[..., None] * MOBA_BLOCK + offs
        b_sel = rb[rel_bucket(t[None, :, None, None, None] - key_pos), hidx[..., None]]
        s_sel = jnp.where(ok[:, :, :, None, :, None], s_sel + jnp.moveaxis(b_sel, -1, 3).astype(jnp.float32), -jnp.inf)
        k_own = lax.dynamic_index_in_dim(kbh, own, axis=2, keepdims=False)
        v_own = lax.dynamic_index_in_dim(vbh, own, axis=2, keepdims=False)
        s_own = jnp.einsum('bqhgd,bhkd->bqhgk', qx, k_own).astype(jnp.float32) * scale
        dist = t[:, None] - (own * MOBA_BLOCK + offs)[None, :]
        b_own = jnp.moveaxis(rel_bias[rel_bucket(dist)].reshape(C_QCHUNK, MOBA_BLOCK, C_KV_HEADS, G), 1, 3)
        s_own = jnp.where((dist >= 0)[:, None, None, :], s_own + b_own.astype(jnp.float32), -jnp.inf)
        logits = jnp.concatenate([s_sel.reshape(Bsz, C_QCHUNK, C_KV_HEADS, G, n_sel * MOBA_BLOCK), s_own], axis=-1)
        probs = jax.nn.softmax(logits, axis=-1).astype(v.dtype)
        p_sel = probs[..., :n_sel * MOBA_BLOCK].reshape(Bsz, C_QCHUNK, C_KV_HEADS, G, n_sel, MOBA_BLOCK)
        p_own = probs[..., n_sel * MOBA_BLOCK:]
        return (jnp.einsum('bqhgnk,bqhnkd->bqhgd', p_sel, vg)
                + jnp.einsum('bqhgk,bhkd->bqhgd', p_own, v_own))

    out = lax.map(one_chunk, (qc, jnp.arange(nq)))
    return jnp.moveaxis(out, 0, 1).reshape(Bsz, Tp, C_Q)[:, :T]


def dsa_attention(q, k, v, q_idx, k_idx, w_idx, rel_bias):
    Bsz, T = q.shape[:2]
    n_sel = min(DSA_TOPK, T // 4)
    nq = T // D_QCHUNK
    w_idx = w_idx.astype(jnp.float32) * (IDX_HEADS * IDX_DIM) ** -0.5

    def chunks(z):
        return jnp.moveaxis(z.reshape((Bsz, nq, D_QCHUNK) + z.shape[2:]), 1, 0)

    bidx = jnp.arange(Bsz)[:, None, None]
    key_pos = jnp.arange(T)

    def one_chunk(args):
        qx, qix, wx, ci = args
        t = ci * D_QCHUNK + jnp.arange(D_QCHUNK)
        rel = jax.nn.relu(jnp.einsum('bqhd,bsd->bqhs', qix, k_idx).astype(jnp.float32))
        score = jnp.einsum('bqhs,bqh->bqs', rel, wx)
        score = jnp.where(key_pos[None, None, :] <= t[None, :, None], score, -jnp.inf)
        _, sel = lax.top_k(score, n_sel)
        ok = sel <= t[None, :, None]
        kg = k[bidx, sel]
        vg = v[bidx, sel]
        logits = jnp.einsum('bqhd,bqkd->bqhk', qx, kg).astype(jnp.float32) * HEAD_DIM ** -0.5
        bias = jnp.moveaxis(rel_bias[rel_bucket(t[None, :, None] - sel)], -1, 2).astype(jnp.float32)
        logits = jnp.where(ok[:, :, None, :], logits + bias, -jnp.inf)
        probs = jax.nn.softmax(logits, axis=-1).astype(v.dtype)
        return jnp.einsum('bqhk,bqkd->bqhd', probs, vg)

    out = lax.map(one_chunk, (chunks(q), chunks(q_idx), chunks(w_idx), jnp.arange(nq)))
    return jnp.moveaxis(out, 0, 1).reshape(Bsz, T, D_Q)


def memory_cross_attention(h, mem_n, wq, wkv, wo):
    Bsz, T, _ = h.shape
    M = mem_n.shape[1]
    q = (h @ wq).reshape(Bsz, T, X_HEADS, X_HEAD_DIM)
    kv = (mem_n @ wkv).reshape(Bsz, M, 2, X_HEADS, X_HEAD_DIM)
    k, v = kv[:, :, 0], kv[:, :, 1]
    scores = jnp.einsum('bthd,bmhd->bhtm', q, k).astype(jnp.float32) * X_HEAD_DIM ** -0.5
    probs = jax.nn.softmax(scores, axis=-1).astype(v.dtype)
    out = jnp.einsum('bhtm,bmhd->bthd', probs, v).reshape(Bsz, T, X_WIDTH)
    return out @ wo


def squared_relu_mlp(h, w1, w2):
    return jnp.square(jax.nn.relu(h @ w1)) @ w2


def setup_inputs(seed: int = 0) -> dict:
    key = jax.random.key(seed)
    ks = iter(jax.random.split(key, 32))
    NE = (DEPTH + 1) // 2
    NO = DEPTH // 2
    D = D_MODEL

    def nrm(shape, scale):
        return jax.random.normal(next(ks), shape, jnp.float32) * scale

    def unif(shape, lo, hi):
        return jax.random.uniform(next(ks), shape, jnp.float32, lo, hi)

    return {
        'x': nrm((BATCH, SEQ, D), 1.0),
        'mem': nrm((BATCH, MEM_LEN, D), 1.0),
        'rel_bias': nrm((N_BUCKETS, BIAS_HEADS), 0.5),
        'norm_gains': 1.0 + nrm((DEPTH, 6, D), 0.02),
        'mix_w_out': nrm((DEPTH, MIX_WIDTH, D), MIX_WIDTH ** -0.5),
        'mem_norm': 1.0 + nrm((DEPTH, D), 0.02),
        'x_wq': nrm((DEPTH, D, X_WIDTH), D ** -0.5),
        'x_wkv': nrm((DEPTH, D, 2 * X_WIDTH), D ** -0.5),
        'x_wo': nrm((DEPTH, X_WIDTH, D), X_WIDTH ** -0.5),
        'ff_w1': nrm((DEPTH, D, D_FF), D ** -0.5),
        'ff_w2': nrm((DEPTH, D_FF, D), D_FF ** -0.5),
        'ev_w_in': nrm((NE, D, EVEN_IN), D ** -0.5),
        'a_sinks': nrm((NE, A_HEADS), 0.5),
        'b_mu': unif((NE, B_IN), 0.2, 0.8),
        'b_w0': unif((NE, B_WIDTH), -6.0, 1.0),
        'b_w2': nrm((NE, B_DECAY_RANK, B_WIDTH), 0.1),
        'b_a0': nrm((NE, B_WIDTH), 0.1),
        'b_a2': nrm((NE, B_A_RANK, B_WIDTH), 0.5 * B_A_RANK ** -0.5),
        'b_g2': nrm((NE, B_GATE_RANK, B_WIDTH), B_GATE_RANK ** -0.5),
        'b_k_k': 0.85 + nrm((NE, B_WIDTH), 0.02),
        'b_k_a': 1.0 + nrm((NE, B_WIDTH), 0.02),
        'b_r_k': nrm((NE, B_HEADS, B_HEAD_DIM), 0.1),
        'b_ln_w': 1.0 + nrm((NE, B_WIDTH), 0.02),
        'b_ln_b': nrm((NE, B_WIDTH), 0.02),
        'od_w_in': nrm((NO, D, ODD_IN), D ** -0.5),
        'd_kidx_norm': 1.0 + nrm((NO, IDX_DIM), 0.02),
    }


def reference(x, mem, rel_bias, norm_gains, mix_w_out, mem_norm, x_wq, x_wkv, x_wo, ff_w1, ff_w2,
              ev_w_in, a_sinks, b_mu, b_w0, b_w2, b_a0, b_a2, b_g2, b_k_k, b_k_a, b_r_k, b_ln_w, b_ln_b,
              od_w_in, d_kidx_norm):
    Bsz, T, _ = x.shape
    for layer in range(DEPTH):
        i = layer // 2
        g = norm_gains[layer]
        h = rms_norm(x, g[0])
        if layer % 2 == 0:
            p = h @ ev_w_in[i]
            qa, ka, va, pb = jnp.split(p, [A_Q, A_Q + A_KV, A_IN], axis=-1)
            ya = swa_sink_attention(qa.reshape(Bsz, T, A_HEADS, HEAD_DIM),
                                    ka.reshape(Bsz, T, A_KV_HEADS, HEAD_DIM),
                                    va.reshape(Bsz, T, A_KV_HEADS, HEAD_DIM), a_sinks[i], rel_bias)
            yb = rwkv7_time_mix(pb, b_mu[i], b_w0[i], b_w2[i], b_a0[i], b_a2[i], b_g2[i],
                                b_k_k[i], b_k_a[i], b_r_k[i], b_ln_w[i], b_ln_b[i])
            y = jnp.concatenate([ya, yb], axis=-1)
        else:
            p = h @ od_w_in[i]
            cuts = np.cumsum([C_Q, C_KV, C_KV, D_Q, HEAD_DIM, HEAD_DIM, IDX_HEADS * IDX_DIM, IDX_DIM]).tolist()
            qc, kc, vc, qd, kd, vd, qi, ki, wi = jnp.split(p, cuts, axis=-1)
            yc = moba_attention(qc.reshape(Bsz, T, C_HEADS, HEAD_DIM),
                                kc.reshape(Bsz, T, C_KV_HEADS, HEAD_DIM),
                                vc.reshape(Bsz, T, C_KV_HEADS, HEAD_DIM), rel_bias)
            yd = dsa_attention(qd.reshape(Bsz, T, D_HEADS, HEAD_DIM), kd, vd,
                               qi.reshape(Bsz, T, IDX_HEADS, IDX_DIM), rms_norm(ki, d_kidx_norm[i]), wi, rel_bias)
            y = jnp.concatenate([yc, yd], axis=-1)
        x = x + rms_norm(y @ mix_w_out[layer], g[1])
        h = rms_norm(x, g[2])
        x = x + rms_norm(memory_cross_attention(h, rms_norm(mem, mem_norm[layer]), x_wq[layer], x_wkv[layer], x_wo[layer]), g[3])
        h = rms_norm(x, g[4])
        x = x + rms_norm(squared_relu_mlp(h, ff_w1[layer], ff_w2[layer]), g[5])
    return x
```

```python
import functools
import math

import numpy as np
import jax
import jax.numpy as jnp
from jax import lax
from jax.experimental import pallas as pl
from jax.experimental.pallas import tpu as pltpu

F32 = jnp.float32
BF16 = jnp.bfloat16
HIGHEST = lax.Precision.HIGHEST

LANES = 128
HEAD_DIM = 64
EPS = 1e-6
D_MODEL = 1024
A_HEADS, A_KV_HEADS, BAND = 8, 2, 128
B_HEADS, B_WIDTH = 8, 512
B_DECAY_RANK, B_A_RANK, B_GATE_RANK = 64, 64, 128
B_LN_EPS = 64e-5
RWKV_CHUNK = 64
C_HEADS, C_KV_HEADS, MOBA_BLOCK, MOBA_TOPK, QCHUNK = 8, 2, 256, 3, 128
D_HEADS, IDX_HEADS, IDX_DIM, DSA_TOPK = 8, 4, 64, 256
N_BUCKETS, MAX_DISTANCE, BIAS_HEADS = 32, 1024, 8
BIAS_TILES = 9
X_HEADS, X_HEAD_DIM, X_WIDTH = 4, 128, 512
D_FF = 4096

MASKED = -2e30
M_INIT = -1e30
INT_MIN = -(2 ** 31)
VMEM_LIMIT = 56 * 1024 * 1024


def _cparams(*sem):
    return pltpu.CompilerParams(dimension_semantics=tuple(sem), vmem_limit_bytes=VMEM_LIMIT)


def _dot(a, b):
    return jnp.dot(a, b, preferred_element_type=F32)


def _dot_nt(a, b):
    return lax.dot_general(a, b, (((1,), (1,)), ((), ())), preferred_element_type=F32)


def _dot_tn(a, b):
    return lax.dot_general(a, b, (((0,), (0,)), ((), ())), preferred_element_type=F32)


def _dot_x(a, b):
    return jnp.dot(a, b, preferred_element_type=F32, precision=HIGHEST)


def _dot_nt_x(a, b):
    return lax.dot_general(a, b, (((1,), (1,)), ((), ())), preferred_element_type=F32, precision=HIGHEST)


def _dot_tn_x(a, b):
    return lax.dot_general(a, b, (((0,), (0,)), ((), ())), preferred_element_type=F32, precision=HIGHEST)


def _rms(xf, g):
    ms = jnp.mean(xf * xf, axis=-1, keepdims=True)
    return xf * lax.rsqrt(ms + EPS) * g


def _lane_half_masks(shape):
    lane = lax.broadcasted_iota(jnp.int32, shape, len(shape) - 1)
    lo = lane < HEAD_DIM
    return lo, jnp.logical_not(lo)


def _bucket_of_distance(n):
    exact = N_BUCKETS // 2
    if n < exact:
        return n
    j = 0
    while n ** 8 >= (exact ** 8) * (2 ** (3 * (j + 1))):
        j += 1
    return min(exact + j, N_BUCKETS - 1)


@functools.lru_cache(maxsize=None)
def _bucket_tiles():
    by_dist = np.array([_bucket_of_distance(n) for n in range(BIAS_TILES * LANES + LANES)], np.int32)
    i = np.arange(LANES)[:, None]
    j = np.arange(LANES)[None, :]
    tiles = [by_dist[np.maximum(d * LANES + i - j, 0)] for d in range(BIAS_TILES)]
    return np.stack(tiles).astype(np.int32)


def _bias_table_kernel(rb_ref, bkt_ref, out_ref):
    bkt = bkt_ref[0]
    for h in range(BIAS_HEADS):
        acc = jnp.zeros((LANES, LANES), F32)
        for b in range(N_BUCKETS):
            acc = jnp.where(bkt == b, rb_ref[b, h], acc)
        out_ref[0, h] = acc


def bias_table(rel_bias):
    return pl.pallas_call(
        _bias_table_kernel,
        grid=(BIAS_TILES,),
        in_specs=[pl.BlockSpec(memory_space=pltpu.SMEM),
                  pl.BlockSpec((1, LANES, LANES), lambda d: (d, 0, 0))],
        out_specs=pl.BlockSpec((1, BIAS_HEADS, LANES, LANES), lambda d: (d, 0, 0, 0)),
        out_shape=jax.ShapeDtypeStruct((BIAS_TILES, BIAS_HEADS, LANES, LANES), F32),
        compiler_params=_cparams("arbitrary"),
        name="bias_table",
    )(rel_bias, jnp.asarray(_bucket_tiles()))


def _norm_proj_kernel(x_ref, g_ref, w_ref, *out_refs, splits):
    h = _rms(x_ref[...], g_ref[...]).astype(BF16)
    res = _dot(h, w_ref[...])
    off = 0
    for o_ref, width in zip(out_refs, splits):
        o_ref[...] = res[:, off:off + width].astype(o_ref.dtype)
        off += width


def norm_proj(x2, gain, w, splits, out_dtypes, tm=256):
    n, d = x2.shape
    tm = min(tm, n)
    total = sum(splits)
    assert w.shape == (d, total) and n % tm == 0 and all(s % LANES == 0 for s in splits)
    return pl.pallas_call(
        functools.partial(_norm_proj_kernel, splits=tuple(splits)),
        grid=(n // tm,),
        in_specs=[pl.BlockSpec((tm, d), lambda i: (i, 0)),
                  pl.BlockSpec((1, d), lambda i: (0, 0)),
                  pl.BlockSpec((d, total), lambda i: (0, 0))],
        out_specs=[pl.BlockSpec((tm, s), lambda i: (i, 0)) for s in splits],
        out_shape=[jax.ShapeDtypeStruct((n, s), dt) for s, dt in zip(splits, out_dtypes)],
        compiler_params=_cparams("parallel"),
        name="norm_proj",
    )(x2, gain.reshape(1, d), w)


def _proj_norm_res_kernel(ya_ref, yb_ref, wa_ref, wb_ref, g_ref, x_ref, o_ref):
    z = _dot(ya_ref[...].astype(BF16), wa_ref[...]) + _dot(yb_ref[...].astype(BF16), wb_ref[...])
    o_ref[...] = x_ref[...] + _rms(z, g_ref[...])


def proj_norm_res(ya, yb, w, gain, x2, tm=512):
    n, d = x2.shape
    tm = min(tm, n)
    ka, kb = ya.shape[1], yb.shape[1]
    wa, wb = w[:ka].astype(BF16), w[ka:].astype(BF16)
    return pl.pallas_call(
        _proj_norm_res_kernel,
        grid=(n // tm,),
        in_specs=[pl.BlockSpec((tm, ka), lambda i: (i, 0)),
                  pl.BlockSpec((tm, kb), lambda i: (i, 0)),
                  pl.BlockSpec((ka, d), lambda i: (0, 0)),
                  pl.BlockSpec((kb, d), lambda i: (0, 0)),
                  pl.BlockSpec((1, d), lambda i: (0, 0)),
                  pl.BlockSpec((tm, d), lambda i: (i, 0))],
        out_specs=pl.BlockSpec((tm, d), lambda i: (i, 0)),
        out_shape=jax.ShapeDtypeStruct((n, d), F32),
        compiler_params=_cparams("parallel"),
        name="proj_norm_res",
    )(ya, yb, wa, wb, gain.reshape(1, d), x2)


def _cross_attn_kernel(x_ref, k_ref, v_ref, wq_ref, wo_ref, gpre_ref, gpost_ref, o_ref):
    x = x_ref[0]
    h = _rms(x, gpre_ref[...]).astype(BF16)
    q = _dot(h, wq_ref[...])
    k = k_ref[0]
    v = v_ref[0]
    outs = []
    for hd in range(X_HEADS):
        sl = slice(hd * X_HEAD_DIM, (hd + 1) * X_HEAD_DIM)
        s = _dot_nt(q[:, sl].astype(BF16), k[:, sl]) * (X_HEAD_DIM ** -0.5)
        m = jnp.max(s, axis=-1, keepdims=True)
        p = jnp.exp(s - m)
        p = p / jnp.sum(p, axis=-1, keepdims=True)
        outs.append(_dot(p.astype(BF16), v[:, sl]))
    o = jnp.concatenate(outs, axis=-1).astype(BF16)
    z = _dot(o, wo_ref[...])
    o_ref[0] = x + _rms(z, gpost_ref[...])


def cross_attn(x3, kmem, vmem, wq, wo, gpre, gpost, tq=512):
    b, t, d = x3.shape
    tq = min(tq, t)
    m = kmem.shape[1]
    return pl.pallas_call(
        _cross_attn_kernel,
        grid=(b, t // tq),
        in_specs=[pl.BlockSpec((1, tq, d), lambda i, j: (i, j, 0)),
                  pl.BlockSpec((1, m, X_WIDTH), lambda i, j: (i, 0, 0)),
                  pl.BlockSpec((1, m, X_WIDTH), lambda i, j: (i, 0, 0)),
                  pl.BlockSpec((d, X_WIDTH), lambda i, j: (0, 0)),
                  pl.BlockSpec((X_WIDTH, d), lambda i, j: (0, 0)),
                  pl.BlockSpec((1, d), lambda i, j: (0, 0)),
                  pl.BlockSpec((1, d), lambda i, j: (0, 0))],
        out_specs=pl.BlockSpec((1, tq, d), lambda i, j: (i, j, 0)),
        out_shape=jax.ShapeDtypeStruct((b, t, d), F32),
        compiler_params=_cparams("parallel", "parallel"),
        name="cross_attn",
    )(x3, kmem, vmem, wq.astype(BF16), wo.astype(BF16), gpre.reshape(1, d), gpost.reshape(1, d))


def _mlp_kernel(x_ref, w1_ref, w2_ref, gpre_ref, gpost_ref, o_ref, h_scr, acc_scr):
    j = pl.program_id(1)

    @pl.when(j == 0)
    def _():
        h_scr[...] = _rms(x_ref[...], gpre_ref[...]).astype(BF16)
        acc_scr[...] = jnp.zeros_like(acc_scr)

    a = jnp.maximum(_dot(h_scr[...], w1_ref[...]), 0.0)
    acc_scr[...] += _dot((a * a).astype(BF16), w2_ref[...])

    @pl.when(j == pl.num_programs(1) - 1)
    def _():
        o_ref[...] = x_ref[...] + _rms(acc_scr[...], gpost_ref[...])


def mlp(x2, w1, w2, gpre, gpost, tm=512, tf=1024):
    n, d = x2.shape
    tm = min(tm, n)
    f = w1.shape[1]
    return pl.pallas_call(
        _mlp_kernel,
        grid=(n // tm, f // tf),
        in_specs=[pl.BlockSpec((tm, d), lambda i, j: (i, 0)),
                  pl.BlockSpec((d, tf), lambda i, j: (0, j)),
                  pl.BlockSpec((tf, d), lambda i, j: (j, 0)),
                  pl.BlockSpec((1, d), lambda i, j: (0, 0)),
                  pl.BlockSpec((1, d), lambda i, j: (0, 0))],
        out_specs=pl.BlockSpec((tm, d), lambda i, j: (i, 0)),
        out_shape=jax.ShapeDtypeStruct((n, d), F32),
        scratch_shapes=[pltpu.VMEM((tm, d), BF16), pltpu.VMEM((tm, d), F32)],
        compiler_params=_cparams("parallel", "arbitrary"),
        name="mlp",
    )(x2, w1.astype(BF16), w2.astype(BF16), gpre.reshape(1, d), gpost.reshape(1, d))


def _swa_kernel(sink_ref, q_ref, kc_ref, kp_ref, vc_ref, vp_ref, tab_ref, o_ref):
    hkv = pl.program_id(1)
    n = pl.program_id(2)
    scale = HEAD_DIM ** -0.5
    kc = kc_ref[0].astype(BF16)
    kp = kp_ref[0].astype(BF16)
    vc = vc_ref[0]
    vp = vp_ref[0]
    lo, hi = _lane_half_masks((BAND, LANES))
    qi = lax.broadcasted_iota(jnp.int32, (BAND, BAND), 0)
    kj = lax.broadcasted_iota(jnp.int32, (BAND, BAND), 1)
    cur_ok = kj <= qi
    prev_ok = jnp.logical_and(kj > qi, n > 0)
    group = A_HEADS // A_KV_HEADS
    for pair in range(group // 2):
        qp = q_ref[0, :, pair * LANES:(pair + 1) * LANES]
        out = jnp.zeros((BAND, LANES), F32)
        for half, msk in enumerate((lo, hi)):
            g = pair * 2 + half
            sink = sink_ref[hkv * group + g]
            qh = jnp.where(msk, qp, 0.0).astype(BF16)
            sc = jnp.where(cur_ok, _dot_nt(qh, kc) * scale + tab_ref[0, g], MASKED)
            sp = jnp.where(prev_ok, _dot_nt(qh, kp) * scale + tab_ref[1, g], MASKED)
            m = jnp.maximum(jnp.maximum(jnp.max(sc, axis=-1, keepdims=True),
                                        jnp.max(sp, axis=-1, keepdims=True)), sink)
            pc = jnp.exp(sc - m)
            pp = jnp.exp(sp - m)
            den = jnp.sum(pc, axis=-1, keepdims=True) + jnp.sum(pp, axis=-1, keepdims=True) + jnp.exp(sink - m)
            inv = 1.0 / den
            vch = jnp.where(msk, vc, 0.0).astype(BF16)
            vph = jnp.where(msk, vp, 0.0).astype(BF16)
            out = out + _dot((pc * inv).astype(BF16), vch) + _dot((pp * inv).astype(BF16), vph)
        o_ref[0, :, pair * LANES:(pair + 1) * LANES] = out


def swa_attention(qkv, sinks, tab):
    b, t, _ = qkv.shape
    nb = t // BAND
    qw = (A_HEADS // A_KV_HEADS) * HEAD_DIM
    koff = A_HEADS * HEAD_DIM // LANES
    voff = koff + A_KV_HEADS
    group = A_HEADS // A_KV_HEADS
    return pl.pallas_call(
        _swa_kernel,
        grid=(b, A_KV_HEADS, nb),
        in_specs=[pl.BlockSpec(memory_space=pltpu.SMEM),
                  pl.BlockSpec((1, BAND, qw), lambda i, h, n: (i, n, h)),
                  pl.BlockSpec((1, BAND, LANES), lambda i, h, n: (i, n, koff + h)),
                  pl.BlockSpec((1, BAND, LANES), lambda i, h, n: (i, jnp.maximum(n - 1, 0), koff + h)),
                  pl.BlockSpec((1, BAND, LANES), lambda i, h, n: (i, n, voff + h)),
                  pl.BlockSpec((1, BAND, LANES), lambda i, h, n: (i, jnp.maximum(n - 1, 0), voff + h)),
                  pl.BlockSpec((2, group, LANES, LANES), lambda i, h, n: (0, h, 0, 0))],
        out_specs=pl.BlockSpec((1, BAND, qw), lambda i, h, n: (i, n, h)),
        out_shape=jax.ShapeDtypeStruct((b, t, A_HEADS * HEAD_DIM), F32),
        compiler_params=_cparams("parallel", "parallel", "arbitrary"),
        name="swa_attention",
    )(sinks, qkv, qkv, qkv, qkv, qkv, tab)


def _split3(x):
    p0 = x.astype(BF16)
    r1 = x - p0.astype(F32)
    p1 = r1.astype(BF16)
    p2 = (r1 - p1.astype(F32)).astype(BF16)
    return p0, p1, p2


def _dot_exact_rhs01(x, m01):
    p0, p1, p2 = _split3(x)
    return _dot(p0, m01) + _dot(p1, m01) + _dot(p2, m01)


def _dot_exact_lhs01(m01, x):
    p0, p1, p2 = _split3(x)
    return _dot(m01, p0) + _dot(m01, p1) + _dot(m01, p2)


def _head_sum_matrix(width):
    r = lax.broadcasted_iota(jnp.int32, (width, width), 0) // HEAD_DIM
    c = lax.broadcasted_iota(jnp.int32, (width, width), 1) // HEAD_DIM
    return jnp.where(r == c, 1.0, 0.0).astype(BF16)


def _sigmoid(z):
    return 1.0 / (1.0 + jnp.exp(-z))


def _softplus(z):
    return jnp.maximum(z, 0.0) + jnp.log(1.0 + jnp.exp(-jnp.abs(z)))


def _rwkv_prep_kernel(p_ref, pprev_ref, mu_ref, wa2_ref, g2_ref, w0_ref, a0_ref, kk_ref, ka_ref,
                      r_out, k_out, v_out, lw_out, kk_out, b_out, g_out):
    i = pl.program_id(1)
    p = p_ref[0]
    tc = p.shape[0]
    w = B_WIDTH
    last = pprev_ref[0, 7:8, :]
    last = jnp.where(i > 0, last, 0.0)
    row = lax.broadcasted_iota(jnp.int32, p.shape, 0)
    prev = jnp.where(row == 0, last, pltpu.roll(p, 1, 0))
    xs = p + (prev - p) * mu_ref[...]
    r = xs[:, 0:w]
    k = xs[:, w:2 * w]
    v = xs[:, 2 * w:3 * w]
    wa_lo = xs[:, 3 * w:3 * w + LANES]
    g_lo = xs[:, 3 * w + LANES:3 * w + 2 * LANES]
    lo, _ = _lane_half_masks(wa_lo.shape)
    wa_in = jnp.where(lo, jnp.tanh(wa_lo), wa_lo)
    wa = _dot_x(wa_in, wa2_ref[...])
    wlog = -_softplus(-(w0_ref[...] + wa[:, 0:w])) - 0.5
    a = _sigmoid(a0_ref[...] + wa[:, w:2 * w])
    g = _dot(_sigmoid(g_lo).astype(BF16), g2_ref[...])
    kk = k * kk_ref[...]
    ss = _dot_exact_rhs01(kk * kk, _head_sum_matrix(w))
    kk = kk / jnp.maximum(jnp.sqrt(ss), 1e-12)
    kmod = k * (1.0 + (a - 1.0) * ka_ref[...])
    r_out[0] = r
    k_out[0] = kmod
    v_out[0] = v
    lw_out[0] = -jnp.exp(wlog)
    kk_out[0] = kk
    b_out[0] = kk * a
    g_out[0] = g


def rwkv_prep(pb, mu, w0, w2, a0, a2, g2, k_k, k_a, tc=256):
    b, t, win = pb.shape
    tc = min(tc, t)
    w = B_WIDTH
    wa2 = jnp.zeros((LANES, 2 * w), F32)
    wa2 = wa2.at[:B_DECAY_RANK, :w].set(w2).at[B_DECAY_RANK:, w:].set(a2)
    row = lambda z: z.reshape(1, -1)
    vec = pl.BlockSpec((1, w), lambda i, j: (0, 0))
    outs = [jax.ShapeDtypeStruct((b, t, w), F32)] * 7
    return pl.pallas_call(
        _rwkv_prep_kernel,
        grid=(b, t // tc),
        in_specs=[pl.BlockSpec((1, tc, win), lambda i, j: (i, j, 0)),
                  pl.BlockSpec((1, 8, win), lambda i, j: (i, jnp.maximum(j * (tc // 8) - 1, 0), 0)),
                  pl.BlockSpec((1, win), lambda i, j: (0, 0)),
                  pl.BlockSpec((LANES, 2 * w), lambda i, j: (0, 0)),
                  pl.BlockSpec((B_GATE_RANK, w), lambda i, j: (0, 0)),
                  vec, vec, vec, vec],
        out_specs=[pl.BlockSpec((1, tc, w), lambda i, j: (i, j, 0))] * 7,
        out_shape=outs,
        compiler_params=_cparams("parallel", "arbitrary"),
        name="rwkv_prep",
    )(pb, pb, row(mu), wa2, g2.astype(BF16), row(w0), row(a0), row(k_k), row(k_a))


def _rwkv_scan_kernel(r_ref, k_ref, v_ref, lw_ref, kk_ref, b_ref, g_ref, rk_ref, lnw_ref, lnb_ref,
                      o_ref, st_scr):
    c = pl.program_id(1)
    ch = r_ref.shape[1]

    @pl.when(c == 0)
    def _():
        st_scr[...] = jnp.zeros_like(st_scr)

    ti = lax.broadcasted_iota(jnp.int32, (ch, ch), 0)
    si = lax.broadcasted_iota(jnp.int32, (ch, ch), 1)
    incl = si <= ti
    strict = si < ti
    tri01 = jnp.where(incl, 1.0, 0.0).astype(BF16)
    eye = jnp.where(si == ti, 1.0, 0.0)
    levels = []
    s = 1
    while s < ch:
        same = (ti // (2 * s)) == (si // (2 * s))
        levels.append(jnp.logical_and(same, jnp.logical_and((ti // s) % 2 == 1, (si // s) % 2 == 0)))
        s *= 2
    lo, hi = _lane_half_masks((ch, LANES))
    blk_r = lax.broadcasted_iota(jnp.int32, (LANES, LANES), 0) // HEAD_DIM
    blk_c = lax.broadcasted_iota(jnp.int32, (LANES, LANES), 1) // HEAD_DIM
    same_head = blk_r == blk_c
    hsum = _head_sum_matrix(LANES)

    lw_all = lw_ref[0]
    cum_all = _dot_exact_lhs01(tri01, lw_all)
    for pr in range(B_HEADS // 2):
        sl = slice(pr * LANES, (pr + 1) * LANES)
        r, k, v, kk, bb = r_ref[0, :, sl], k_ref[0, :, sl], v_ref[0, :, sl], kk_ref[0, :, sl], b_ref[0, :, sl]
        lw, cum = lw_all[:, sl], cum_all[:, sl]
        tot = cum[ch - 1:ch, :]
        e_excl = jnp.exp(cum - lw)
        e_neg = jnp.exp(-cum)
        e_pos = jnp.exp(cum)
        e_rest = jnp.exp(tot - cum)
        a_t, k_t, b_t, r_t = kk * e_excl, k * e_neg, bb * e_neg, r * e_pos
        k_h, b_h = k * e_rest, bb * e_rest
        st = st_scr[pr]
        x0 = _dot_nt_x(a_t, st)
        y = _dot_nt_x(r_t, st)
        u = jnp.zeros((ch, LANES), F32)
        for msk in (lo, hi):
            a_m = jnp.where(msk, a_t, 0.0)
            r_m = jnp.where(msk, r_t, 0.0)
            v_m = jnp.where(msk, v, 0.0)
            ab = jnp.where(strict, _dot_nt_x(a_m, b_t), 0.0)
            ak = jnp.where(strict, _dot_nt_x(a_m, k_t), 0.0)
            rk = jnp.where(incl, _dot_nt_x(r_m, k_t), 0.0)
            rb = jnp.where(incl, _dot_nt_x(r_m, b_t), 0.0)
            tinv = eye
            for lvl in levels:
                tinv = tinv - _dot_x(_dot_x(tinv, jnp.where(lvl, ab, 0.0)), tinv)
            u_m = _dot_x(tinv, jnp.where(msk, x0, 0.0) + _dot_x(ak, v_m))
            y = y + _dot_x(rk, v_m) - _dot_x(rb, u_m)
            u = u + u_m
        upd = _dot_tn_x(v, k_h) - _dot_tn_x(u, b_h)
        st_scr[pr] = st * jnp.exp(tot) + jnp.where(same_head, upd, 0.0)
        mean = _dot_exact_rhs01(y, hsum) * (1.0 / HEAD_DIM)
        yc = y - mean
        var = _dot_exact_rhs01(yc * yc, hsum) * (1.0 / HEAD_DIM)
        yn = yc * lax.rsqrt(var + B_LN_EPS) * lnw_ref[:, sl] + lnb_ref[:, sl]
        bonus = _dot_exact_rhs01(r * k * rk_ref[:, sl], hsum) * v
        o_ref[0, :, sl] = (yn + bonus) * g_ref[0, :, sl]


def rwkv_scan(r, k, v, lw, kk, bb, g, r_k, ln_w, ln_b, ch=RWKV_CHUNK):
    b, t, w = r.shape
    ch = min(ch, t)
    seq = pl.BlockSpec((1, ch, w), lambda i, c: (i, c, 0))
    vec = pl.BlockSpec((1, w), lambda i, c: (0, 0))
    row = lambda z: z.reshape(1, w)
    return pl.pallas_call(
        _rwkv_scan_kernel,
        grid=(b, t // ch),
        in_specs=[seq] * 7 + [vec] * 3,
        out_specs=seq,
        out_shape=jax.ShapeDtypeStruct((b, t, w), F32),
        scratch_shapes=[pltpu.VMEM((B_HEADS // 2, LANES, LANES), F32)],
        compiler_params=_cparams("parallel", "arbitrary"),
        name="rwkv_scan",
    )(r, k, v, lw, kk, bb, g, row(r_k), row(ln_w), row(ln_b))


def _moba_kernel(q_ref, k_ref, v_ref, tab_ref, o_ref, k_scr, v_scr, km_scr, m_scr, l_scr, acc_scr):
    c = pl.program_id(2)
    t = k_ref.shape[1]
    nblk = t // MOBA_BLOCK
    group = C_HEADS // C_KV_HEADS
    scale = HEAD_DIM ** -0.5

    @pl.when(c == 0)
    def _():
        kf = k_ref[0]
        k_scr[...] = kf.astype(BF16)
        v_scr[...] = v_ref[0].astype(BF16)
        km_scr[...] = jnp.zeros_like(km_scr)
        km_scr[0:nblk, :] = jnp.mean(kf.reshape(nblk, MOBA_BLOCK, LANES), axis=1)

    own = (c * QCHUNK) // MOBA_BLOCK
    sub = c % (MOBA_BLOCK // QCHUNK)
    lo, hi = _lane_half_masks((QCHUNK, LANES))
    halves = (lo, hi)
    kv_halves = _lane_half_masks((MOBA_BLOCK, LANES))
    qtiles = [q_ref[0, :, p * LANES:(p + 1) * LANES] for p in range(group // 2)]

    qsum = qtiles[0]
    for qt in qtiles[1:]:
        qsum = qsum + qt
    gate = _dot_nt_x(qsum, km_scr[...])
    blk = lax.broadcasted_iota(jnp.int32, (QCHUNK, LANES), 1)
    gate = jnp.where(blk < own, gate, -jnp.inf)
    rank = jnp.zeros((QCHUNK, LANES), jnp.int32)
    for mblk in range(nblk):
        gm = gate[:, mblk:mblk + 1]
        ahead = jnp.logical_or(gm > gate, jnp.logical_and(gm == gate, mblk < blk))
        rank = rank + jnp.where(ahead, 1, 0)
    chosen = jnp.logical_and(rank < MOBA_TOPK, blk < own)
    chosen_f = jnp.where(chosen, 1.0, 0.0)

    qh = []
    for p in range(group // 2):
        for msk in halves:
            qh.append(jnp.where(msk, qtiles[p], 0.0).astype(BF16))

    qi = lax.broadcasted_iota(jnp.int32, (QCHUNK, MOBA_BLOCK), 0)
    kj = lax.broadcasted_iota(jnp.int32, (QCHUNK, MOBA_BLOCK), 1)

    def bias_tile(g, d0):
        da = jnp.clip(d0, 0, BIAS_TILES - 1)
        db = jnp.clip(d0 - 1, 0, BIAS_TILES - 1)
        return jnp.concatenate([tab_ref[da, g], tab_ref[db, g]], axis=-1)

    def attend(blk_idx, ok, first):
        start = pl.multiple_of(blk_idx * MOBA_BLOCK, MOBA_BLOCK)
        kb = k_scr[pl.ds(start, MOBA_BLOCK), :]
        vb = v_scr[pl.ds(start, MOBA_BLOCK), :]
        d0 = c - 2 * blk_idx
        for g in range(group):
            s = jnp.where(ok, _dot_nt(qh[g], kb) * scale + bias_tile(g, d0), MASKED)
            m_old = jnp.full((QCHUNK, 1), M_INIT, F32) if first else m_scr[g]
            m_new = jnp.maximum(m_old, jnp.max(s, axis=-1, keepdims=True))
            p = jnp.exp(s - m_new)
            vm = jnp.where(kv_halves[g % 2], vb, jnp.zeros_like(vb))
            pv = _dot(p.astype(BF16), vm)
            if first:
                l_scr[g] = jnp.sum(p, axis=-1, keepdims=True)
                acc_scr[g] = pv
            else:
                alpha = jnp.exp(m_old - m_new)
                l_scr[g] = alpha * l_scr[g] + jnp.sum(p, axis=-1, keepdims=True)
                acc_scr[g] = alpha * acc_scr[g] + pv
            m_scr[g] = m_new

    attend(own, (sub * QCHUNK + qi - kj) >= 0, True)

    def body(nb, carry):
        pick = jnp.max(jnp.where(blk == nb, chosen_f, 0.0), axis=-1, keepdims=True) > 0.5
        attend(nb, jnp.broadcast_to(pick, (QCHUNK, MOBA_BLOCK)), False)
        return carry

    lax.fori_loop(0, own, body, 0)

    for p in range(group // 2):
        out = acc_scr[2 * p] / l_scr[2 * p] + acc_scr[2 * p + 1] / l_scr[2 * p + 1]
        o_ref[0, :, p * LANES:(p + 1) * LANES] = out


def moba_attention(qc, kvc, tab):
    b, t, _ = qc.shape
    assert t % MOBA_BLOCK == 0
    nq = t // QCHUNK
    group = C_HEADS // C_KV_HEADS
    qw = group * HEAD_DIM
    return pl.pallas_call(
        _moba_kernel,
        grid=(b, C_KV_HEADS, nq),
        in_specs=[pl.BlockSpec((1, QCHUNK, qw), lambda i, h, c: (i, c, h)),
                  pl.BlockSpec((1, t, LANES), lambda i, h, c: (i, 0, h)),
                  pl.BlockSpec((1, t, LANES), lambda i, h, c: (i, 0, C_KV_HEADS + h)),
                  pl.BlockSpec((BIAS_TILES, group, LANES, LANES), lambda i, h, c: (0, h, 0, 0))],
        out_specs=pl.BlockSpec((1, QCHUNK, qw), lambda i, h, c: (i, c, h)),
        out_shape=jax.ShapeDtypeStruct((b, t, C_HEADS * HEAD_DIM), F32),
        scratch_shapes=[pltpu.VMEM((t, LANES), BF16), pltpu.VMEM((t, LANES), BF16),
                        pltpu.VMEM((LANES, LANES), F32),
                        pltpu.VMEM((group, QCHUNK, 1), F32), pltpu.VMEM((group, QCHUNK, 1), F32),
                        pltpu.VMEM((group, QCHUNK, LANES), F32)],
        compiler_params=_cparams("parallel", "parallel", "arbitrary"),
        name="moba_attention",
    )(qc, kvc, kvc, tab)


def _dsa_kernel(q_ref, kv_ref, qi_ref, ki_ref, gk_ref, tab_ref, o_ref,
                k_scr, v_scr, ki_scr, key_scr, m_scr, l_scr, acc_scr, *, topk):
    c = pl.program_id(1)
    scale = HEAD_DIM ** -0.5
    ntile = c + 1

    @pl.when(c == 0)
    def _():
        k_scr[...] = kv_ref[0, :, 0:LANES].astype(BF16)
        v_scr[...] = kv_ref[0, :, LANES:2 * LANES].astype(BF16)
        ki_scr[...] = _rms(ki_ref[0], gk_ref[...])

    lo, hi = _lane_half_masks((QCHUNK, LANES))
    halves = (lo, hi)
    row = lax.broadcasted_iota(jnp.int32, (QCHUNK, LANES), 0)
    col = lax.broadcasted_iota(jnp.int32, (QCHUNK, LANES), 1)

    wi = qi_ref[0, :, IDX_HEADS * IDX_DIM + LANES:IDX_HEADS * IDX_DIM + 2 * LANES] * ((IDX_HEADS * IDX_DIM) ** -0.5)
    qidx = []
    for h in range(IDX_HEADS):
        tile = qi_ref[0, :, (h // 2) * LANES:(h // 2 + 1) * LANES]
        qidx.append(jnp.where(halves[h % 2], tile, 0.0))

    def score_body(j, carry):
        start = pl.multiple_of(j * LANES, LANES)
        kt = ki_scr[pl.ds(start, LANES), :]
        sc = jnp.zeros((QCHUNK, LANES), F32)
        for h in range(IDX_HEADS):
            sc = sc + jnp.maximum(_dot_nt_x(qidx[h], kt), 0.0) * wi[:, h:h + 1]
        sc = jnp.where(sc == 0.0, 0.0, sc)
        bits = pltpu.bitcast(sc, jnp.int32)
        skey = bits ^ (lax.shift_right_arithmetic(bits, 31) & 0x7FFFFFFF)
        visible = (j * LANES + col) <= (c * QCHUNK + row)
        key_scr[:, pl.ds(start, LANES)] = jnp.where(visible, skey, INT_MIN)
        return carry

    lax.fori_loop(0, ntile, score_body, 0)

    def count(pred):
        def body(j, acc):
            start = pl.multiple_of(j * LANES, LANES)
            return acc + jnp.where(pred(key_scr[:, pl.ds(start, LANES)], j), 1, 0)
        acc = lax.fori_loop(0, ntile, body, jnp.zeros((QCHUNK, LANES), jnp.int32))
        return jnp.sum(acc, axis=-1, keepdims=True)

    cnt0 = count(lambda kt, j: kt >= 0)
    thr = jnp.where(cnt0 >= topk, 0, INT_MIN).astype(jnp.int32)

    def bit_body(i, thr):
        cand = thr + lax.shift_left(jnp.int32(1), 30 - i)
        cnt = count(lambda kt, j: kt >= cand)
        return jnp.where(cnt >= topk, cand, thr)

    thr = lax.fori_loop(0, 31, bit_body, thr)
    need = topk - count(lambda kt, j: kt > thr)

    def cut_body(i, cut):
        cand = cut + lax.shift_left(jnp.int32(1), 14 - i)
        cnt = count(lambda kt, j: jnp.logical_and(kt == thr, (j * LANES + col) < cand))
        return jnp.where(cnt < need, cand, cut)

    cut = lax.fori_loop(0, 15, cut_body, jnp.zeros((QCHUNK, 1), jnp.int32))

    qst = []
    for p in range(D_HEADS // 2):
        tile = q_ref[0, :, p * LANES:(p + 1) * LANES]
        for msk in halves:
            qst.append(jnp.where(msk, tile, 0.0).astype(BF16))
    m_scr[...] = jnp.full(m_scr.shape, M_INIT, F32)
    l_scr[...] = jnp.zeros_like(l_scr)
    acc_scr[...] = jnp.zeros_like(acc_scr)

    def att_body(j, carry):
        start = pl.multiple_of(j * LANES, LANES)
        kt = k_scr[pl.ds(start, LANES), :]
        vt = v_scr[pl.ds(start, LANES), :]
        keys = key_scr[:, pl.ds(start, LANES)]
        pos = j * LANES + col
        ok = jnp.logical_or(keys > thr, jnp.logical_and(keys == thr, pos <= cut))
        ok = jnp.logical_and(ok, keys != INT_MIN)
        d = jnp.minimum(c - j, BIAS_TILES - 1)
        for h in range(D_HEADS):
            s = jnp.where(ok, _dot_nt(qst[h], kt) * scale + tab_ref[d, h], MASKED)
            m_old = m_scr[h]
            m_new = jnp.maximum(m_old, jnp.max(s, axis=-1, keepdims=True))
            p = jnp.exp(s - m_new)
            alpha = jnp.exp(m_old - m_new)
            vm = jnp.where(halves[h % 2], vt, jnp.zeros_like(vt))
            l_scr[h] = alpha * l_scr[h] + jnp.sum(p, axis=-1, keepdims=True)
            acc_scr[h] = alpha * acc_scr[h] + _dot(p.astype(BF16), vm)
            m_scr[h] = m_new
        return carry

    lax.fori_loop(0, ntile, att_body, 0)
    for p in range(D_HEADS // 2):
        out = acc_scr[2 * p] / l_scr[2 * p] + acc_scr[2 * p + 1] / l_scr[2 * p + 1]
        o_ref[0, :, p * LANES:(p + 1) * LANES] = out


def dsa_attention(qd, kvd, idx, kidx_gain, tab):
    b, t, _ = qd.shape
    topk = min(DSA_TOPK, t // 4)
    nq = t // QCHUNK
    gk = jnp.concatenate([kidx_gain, kidx_gain]).reshape(1, LANES)
    wq = D_HEADS * HEAD_DIM
    return pl.pallas_call(
        functools.partial(_dsa_kernel, topk=topk),
        grid=(b, nq),
        in_specs=[pl.BlockSpec((1, QCHUNK, wq), lambda i, c: (i, c, 0)),
                  pl.BlockSpec((1, t, 2 * LANES), lambda i, c: (i, 0, 0)),
                  pl.BlockSpec((1, QCHUNK, 4 * LANES), lambda i, c: (i, c, 0)),
                  pl.BlockSpec((1, t, LANES), lambda i, c: (i, 0, 2)),
                  pl.BlockSpec((1, LANES), lambda i, c: (0, 0)),
                  pl.BlockSpec((BIAS_TILES, D_HEADS, LANES, LANES), lambda i, c: (0, 0, 0, 0))],
        out_specs=pl.BlockSpec((1, QCHUNK, wq), lambda i, c: (i, c, 0)),
        out_shape=jax.ShapeDtypeStruct((b, t, wq), F32),
        scratch_shapes=[pltpu.VMEM((t, LANES), BF16), pltpu.VMEM((t, LANES), BF16), pltpu.VMEM((t, LANES), F32),
                        pltpu.VMEM((QCHUNK, t), jnp.int32),
                        pltpu.VMEM((D_HEADS, QCHUNK, 1), F32), pltpu.VMEM((D_HEADS, QCHUNK, 1), F32),
                        pltpu.VMEM((D_HEADS, QCHUNK, LANES), F32)],
        compiler_params=_cparams("parallel", "arbitrary"),
        name="dsa_attention",
    )(qd, kvd, idx, idx, gk, tab)


def _dup_heads(wcols, n_heads):
    d = wcols.shape[0]
    wh = wcols.reshape(d, n_heads, 1, HEAD_DIM)
    return jnp.broadcast_to(wh, (d, n_heads, 2, HEAD_DIM)).reshape(d, n_heads * 2 * HEAD_DIM)


def _even_weight(w_in):
    aq, akv = A_HEADS * HEAD_DIM, A_KV_HEADS * HEAD_DIM
    q = w_in[:, :aq]
    k = _dup_heads(w_in[:, aq:aq + akv], A_KV_HEADS)
    v = _dup_heads(w_in[:, aq + akv:aq + 2 * akv], A_KV_HEADS)
    rest = w_in[:, aq + 2 * akv:]
    return jnp.concatenate([q, k, v, rest], axis=1).astype(BF16)


def _odd_weight(w_in):
    d = w_in.shape[0]
    cq, ckv, dq = C_HEADS * HEAD_DIM, C_KV_HEADS * HEAD_DIM, D_HEADS * HEAD_DIM
    cuts = np.cumsum([cq, ckv, ckv, dq, HEAD_DIM, HEAD_DIM, IDX_HEADS * IDX_DIM, IDX_DIM]).tolist()
    qc, kc, vc, qd, kd, vd, qi, ki, wi = jnp.split(w_in, cuts, axis=1)
    pad = jnp.zeros((d, LANES - IDX_HEADS), w_in.dtype)
    cols = [qc, _dup_heads(kc, C_KV_HEADS), _dup_heads(vc, C_KV_HEADS),
            qd, _dup_heads(kd, 1), _dup_heads(vd, 1),
            qi, _dup_heads(ki, 1), wi, pad]
    return jnp.concatenate(cols, axis=1).astype(BF16)


def kernel(x, mem, rel_bias, norm_gains, mix_w_out, mem_norm, x_wq, x_wkv, x_wo, ff_w1, ff_w2, ev_w_in, a_sinks, b_mu, b_w0, b_w2, b_a0, b_a2, b_g2, b_k_k, b_k_a, b_r_k, b_ln_w, b_ln_b, od_w_in, d_kidx_norm):
    bsz, t, d = x.shape
    n = bsz * t
    depth = norm_gains.shape[0]
    mlen = mem.shape[1]
    tab = bias_table(rel_bias)
    x2 = x.reshape(n, d)
    for layer in range(depth):
        i = layer // 2
        g = norm_gains[layer]
        if layer % 2 == 0:
            b_in = 3 * B_WIDTH + B_DECAY_RANK + B_A_RANK + B_GATE_RANK
            qkv, pb = norm_proj(x2, g[0], _even_weight(ev_w_in[i]), (1024, b_in), (F32, F32))
            ya = swa_attention(qkv.reshape(bsz, t, 1024), a_sinks[i], tab)
            prep = rwkv_prep(pb.reshape(bsz, t, b_in), b_mu[i], b_w0[i], b_w2[i], b_a0[i], b_a2[i], b_g2[i],
                             b_k_k[i], b_k_a[i])
            yb = rwkv_scan(*prep, b_r_k[i].reshape(-1), b_ln_w[i], b_ln_b[i])
        else:
            qc, kvc, qd, kvd, idx = norm_proj(x2, g[0], _odd_weight(od_w_in[i]),
                                              (512, 512, 512, 256, 512), (F32,) * 5)
            ya = moba_attention(qc.reshape(bsz, t, 512), kvc.reshape(bsz, t, 512), tab)
            yb = dsa_attention(qd.reshape(bsz, t, 512), kvd.reshape(bsz, t, 256), idx.reshape(bsz, t, 512),
                               d_kidx_norm[i], tab)
        x2 = proj_norm_res(ya.reshape(n, -1), yb.reshape(n, -1), mix_w_out[layer], g[1], x2)
        kmem, vmem = norm_proj(mem.reshape(bsz * mlen, d), mem_norm[layer], x_wkv[layer].astype(BF16),
                               (X_WIDTH, X_WIDTH), (BF16, BF16))
        x3 = cross_attn(x2.reshape(bsz, t, d), kmem.reshape(bsz, mlen, X_WIDTH), vmem.reshape(bsz, mlen, X_WIDTH),
                        x_wq[layer], x_wo[layer], g[2], g[3])
        x2 = mlp(x3.reshape(n, d), ff_w1[layer], ff_w2[layer], g[4], g[5])
    return x2.reshape(bsz, t, d)
```

```python
import functools
import math

import numpy as np
import jax
import jax.numpy as jnp
from jax import lax
from jax.experimental import pallas as pl
from jax.experimental.pallas import tpu as pltpu

F32 = jnp.float32
BF16 = jnp.bfloat16
HIGHEST = lax.Precision.HIGHEST

LANES = 128
HEAD_DIM = 64
EPS = 1e-6
D_MODEL = 1024
A_HEADS, A_KV_HEADS, BAND = 8, 2, 128
B_HEADS, B_WIDTH = 8, 512
B_DECAY_RANK, B_A_RANK, B_GATE_RANK = 64, 64, 128
B_LN_EPS = 64e-5
RWKV_CHUNK = 64
C_HEADS, C_KV_HEADS, MOBA_BLOCK, MOBA_TOPK, QCHUNK = 8, 2, 256, 3, 128
D_HEADS, IDX_HEADS, IDX_DIM, DSA_TOPK = 8, 4, 64, 256
N_BUCKETS, MAX_DISTANCE, BIAS_HEADS = 32, 1024, 8
BIAS_TILES = 9
X_HEADS, X_HEAD_DIM, X_WIDTH = 4, 128, 512
D_FF = 4096

MASKED = -2e30
M_INIT = -1e30
INT_MIN = -(2 ** 31)
VMEM_LIMIT = 56 * 1024 * 1024


def _cparams(*sem):
    return pltpu.CompilerParams(dimension_semantics=tuple(sem), vmem_limit_bytes=VMEM_LIMIT)


def _dot(a, b):
    return jnp.dot(a, b, preferred_element_type=F32)


def _dot_nt(a, b):
    return lax.dot_general(a, b, (((1,), (1,)), ((), ())), preferred_element_type=F32)


def _dot_tn(a, b):
    return lax.dot_general(a, b, (((0,), (0,)), ((), ())), preferred_element_type=F32)


def _dot_x(a, b):
    return jnp.dot(a, b, preferred_element_type=F32, precision=HIGHEST)


def _dot_nt_x(a, b):
    return lax.dot_general(a, b, (((1,), (1,)), ((), ())), preferred_element_type=F32, precision=HIGHEST)


def _dot_tn_x(a, b):
    return lax.dot_general(a, b, (((0,), (0,)), ((), ())), preferred_element_type=F32, precision=HIGHEST)


def _rms(xf, g):
    ms = jnp.mean(xf * xf, axis=-1, keepdims=True)
    return xf * lax.rsqrt(ms + EPS) * g


def _lane_half_masks(shape):
    lane = lax.broadcasted_iota(jnp.int32, shape, len(shape) - 1)
    lo = lane < HEAD_DIM
    return lo, jnp.logical_not(lo)


def _bucket_of_distance(n):
    exact = N_BUCKETS // 2
    if n < exact:
        return n
    j = 0
    while n ** 8 >= (exact ** 8) * (2 ** (3 * (j + 1))):
        j += 1
    return min(exact + j, N_BUCKETS - 1)


@functools.lru_cache(maxsize=None)
def _bucket_tiles(transposed):
    by_dist = np.array([_bucket_of_distance(n) for n in range(BIAS_TILES * LANES + LANES)], np.int32)
    q = np.arange(LANES)[None, :] if transposed else np.arange(LANES)[:, None]
    k = np.arange(LANES)[:, None] if transposed else np.arange(LANES)[None, :]
    tiles = [by_dist[np.maximum(d * LANES + q - k, 0)] for d in range(BIAS_TILES)]
    return np.stack(tiles).astype(np.int32)


def _bias_table_kernel(rb_ref, bkt_ref, out_ref):
    bkt = bkt_ref[0]
    for h in range(BIAS_HEADS):
        acc = jnp.zeros((LANES, LANES), F32)
        for b in range(N_BUCKETS):
            acc = jnp.where(bkt == b, rb_ref[b, h], acc)
        out_ref[0, h] = acc


def bias_table(rel_bias, transposed=False):
    return pl.pallas_call(
        _bias_table_kernel,
        grid=(BIAS_TILES,),
        in_specs=[pl.BlockSpec(memory_space=pltpu.SMEM),
                  pl.BlockSpec((1, LANES, LANES), lambda d: (d, 0, 0))],
        out_specs=pl.BlockSpec((1, BIAS_HEADS, LANES, LANES), lambda d: (d, 0, 0, 0)),
        out_shape=jax.ShapeDtypeStruct((BIAS_TILES, BIAS_HEADS, LANES, LANES), F32),
        compiler_params=_cparams("arbitrary"),
        name="bias_table",
    )(rel_bias, jnp.asarray(_bucket_tiles(transposed)))


def _norm_proj_kernel(x_ref, g_ref, w_ref, *out_refs, splits):
    h = _rms(x_ref[...], g_ref[...]).astype(BF16)
    res = _dot(h, w_ref[...])
    off = 0
    for o_ref, width in zip(out_refs, splits):
        o_ref[...] = res[:, off:off + width].astype(o_ref.dtype)
        off += width


def norm_proj(x2, gain, w, splits, out_dtypes, tm=256):
    n, d = x2.shape
    tm = min(tm, n)
    total = sum(splits)
    assert w.shape == (d, total) and n % tm == 0 and all(s % LANES == 0 for s in splits)
    return pl.pallas_call(
        functools.partial(_norm_proj_kernel, splits=tuple(splits)),
        grid=(n // tm,),
        in_specs=[pl.BlockSpec((tm, d), lambda i: (i, 0)),
                  pl.BlockSpec((1, d), lambda i: (0, 0)),
                  pl.BlockSpec((d, total), lambda i: (0, 0))],
        out_specs=[pl.BlockSpec((tm, s), lambda i: (i, 0)) for s in splits],
        out_shape=[jax.ShapeDtypeStruct((n, s), dt) for s, dt in zip(splits, out_dtypes)],
        compiler_params=_cparams("parallel"),
        name="norm_proj",
    )(x2, gain.reshape(1, d), w)


def _proj_norm_res_kernel(ya_ref, yb_ref, wa_ref, wb_ref, g_ref, x_ref, o_ref):
    z = _dot(ya_ref[...].astype(BF16), wa_ref[...]) + _dot(yb_ref[...].astype(BF16), wb_ref[...])
    o_ref[...] = x_ref[...] + _rms(z, g_ref[...])


def proj_norm_res(ya, yb, w, gain, x2, tm=512):
    n, d = x2.shape
    tm = min(tm, n)
    ka, kb = ya.shape[1], yb.shape[1]
    wa, wb = w[:ka].astype(BF16), w[ka:].astype(BF16)
    return pl.pallas_call(
        _proj_norm_res_kernel,
        grid=(n // tm,),
        in_specs=[pl.BlockSpec((tm, ka), lambda i: (i, 0)),
                  pl.BlockSpec((tm, kb), lambda i: (i, 0)),
                  pl.BlockSpec((ka, d), lambda i: (0, 0)),
                  pl.BlockSpec((kb, d), lambda i: (0, 0)),
                  pl.BlockSpec((1, d), lambda i: (0, 0)),
                  pl.BlockSpec((tm, d), lambda i: (i, 0))],
        out_specs=pl.BlockSpec((tm, d), lambda i: (i, 0)),
        out_shape=jax.ShapeDtypeStruct((n, d), F32),
        compiler_params=_cparams("parallel"),
        name="proj_norm_res",
    )(ya, yb, wa, wb, gain.reshape(1, d), x2)


def _cross_attn_kernel(x_ref, k_ref, v_ref, wq_ref, wo_ref, gpre_ref, gpost_ref, o_ref):
    x = x_ref[0]
    h = _rms(x, gpre_ref[...]).astype(BF16)
    q = _dot(h, wq_ref[...])
    k = k_ref[0]
    v = v_ref[0]
    outs = []
    for hd in range(X_HEADS):
        sl = slice(hd * X_HEAD_DIM, (hd + 1) * X_HEAD_DIM)
        s = _dot_nt(q[:, sl].astype(BF16), k[:, sl]) * (X_HEAD_DIM ** -0.5)
        m = jnp.max(s, axis=-1, keepdims=True)
        p = jnp.exp(s - m)
        p = p / jnp.sum(p, axis=-1, keepdims=True)
        outs.append(_dot(p.astype(BF16), v[:, sl]))
    o = jnp.concatenate(outs, axis=-1).astype(BF16)
    z = _dot(o, wo_ref[...])
    o_ref[0] = x + _rms(z, gpost_ref[...])


def cross_attn(x3, kmem, vmem, wq, wo, gpre, gpost, tq=512):
    b, t, d = x3.shape
    tq = min(tq, t)
    m = kmem.shape[1]
    return pl.pallas_call(
        _cross_attn_kernel,
        grid=(b, t // tq),
        in_specs=[pl.BlockSpec((1, tq, d), lambda i, j: (i, j, 0)),
                  pl.BlockSpec((1, m, X_WIDTH), lambda i, j: (i, 0, 0)),
                  pl.BlockSpec((1, m, X_WIDTH), lambda i, j: (i, 0, 0)),
                  pl.BlockSpec((d, X_WIDTH), lambda i, j: (0, 0)),
                  pl.BlockSpec((X_WIDTH, d), lambda i, j: (0, 0)),
                  pl.BlockSpec((1, d), lambda i, j: (0, 0)),
                  pl.BlockSpec((1, d), lambda i, j: (0, 0))],
        out_specs=pl.BlockSpec((1, tq, d), lambda i, j: (i, j, 0)),
        out_shape=jax.ShapeDtypeStruct((b, t, d), F32),
        compiler_params=_cparams("parallel", "parallel"),
        name="cross_attn",
    )(x3, kmem, vmem, wq.astype(BF16), wo.astype(BF16), gpre.reshape(1, d), gpost.reshape(1, d))


def _mlp_kernel(x_ref, w1_ref, w2_ref, gpre_ref, gpost_ref, o_ref, h_scr, acc_scr):
    j = pl.program_id(1)

    @pl.when(j == 0)
    def _():
        h_scr[...] = _rms(x_ref[...], gpre_ref[...]).astype(BF16)
        acc_scr[...] = jnp.zeros_like(acc_scr)

    a = jnp.maximum(_dot(h_scr[...], w1_ref[...]), 0.0)
    acc_scr[...] += _dot((a * a).astype(BF16), w2_ref[...])

    @pl.when(j == pl.num_programs(1) - 1)
    def _():
        o_ref[...] = x_ref[...] + _rms(acc_scr[...], gpost_ref[...])


def mlp(x2, w1, w2, gpre, gpost, tm=512, tf=1024):
    n, d = x2.shape
    tm = min(tm, n)
    f = w1.shape[1]
    return pl.pallas_call(
        _mlp_kernel,
        grid=(n // tm, f // tf),
        in_specs=[pl.BlockSpec((tm, d), lambda i, j: (i, 0)),
                  pl.BlockSpec((d, tf), lambda i, j: (0, j)),
                  pl.BlockSpec((tf, d), lambda i, j: (j, 0)),
                  pl.BlockSpec((1, d), lambda i, j: (0, 0)),
                  pl.BlockSpec((1, d), lambda i, j: (0, 0))],
        out_specs=pl.BlockSpec((tm, d), lambda i, j: (i, 0)),
        out_shape=jax.ShapeDtypeStruct((n, d), F32),
        scratch_shapes=[pltpu.VMEM((tm, d), BF16), pltpu.VMEM((tm, d), F32)],
        compiler_params=_cparams("parallel", "arbitrary"),
        name="mlp",
    )(x2, w1.astype(BF16), w2.astype(BF16), gpre.reshape(1, d), gpost.reshape(1, d))


def _swa_kernel(sink_ref, q_ref, kc_ref, kp_ref, vc_ref, vp_ref, tab_ref, o_ref):
    hkv = pl.program_id(1)
    n = pl.program_id(2)
    scale = HEAD_DIM ** -0.5
    kc = kc_ref[0].astype(BF16)
    kp = kp_ref[0].astype(BF16)
    vc = vc_ref[0]
    vp = vp_ref[0]
    lo, hi = _lane_half_masks((BAND, LANES))
    qi = lax.broadcasted_iota(jnp.int32, (BAND, BAND), 0)
    kj = lax.broadcasted_iota(jnp.int32, (BAND, BAND), 1)
    cur_ok = kj <= qi
    prev_ok = jnp.logical_and(kj > qi, n > 0)
    group = A_HEADS // A_KV_HEADS
    for pair in range(group // 2):
        qp = q_ref[0, :, pair * LANES:(pair + 1) * LANES]
        out = jnp.zeros((BAND, LANES), F32)
        for half, msk in enumerate((lo, hi)):
            g = pair * 2 + half
            sink = sink_ref[hkv * group + g]
            qh = jnp.where(msk, qp, 0.0).astype(BF16)
            sc = jnp.where(cur_ok, _dot_nt(qh, kc) * scale + tab_ref[0, g], MASKED)
            sp = jnp.where(prev_ok, _dot_nt(qh, kp) * scale + tab_ref[1, g], MASKED)
            m = jnp.maximum(jnp.maximum(jnp.max(sc, axis=-1, keepdims=True),
                                        jnp.max(sp, axis=-1, keepdims=True)), sink)
            pc = jnp.exp(sc - m)
            pp = jnp.exp(sp - m)
            den = jnp.sum(pc, axis=-1, keepdims=True) + jnp.sum(pp, axis=-1, keepdims=True) + jnp.exp(sink - m)
            inv = 1.0 / den
            vch = jnp.where(msk, vc, 0.0).astype(BF16)
            vph = jnp.where(msk, vp, 0.0).astype(BF16)
            out = out + _dot((pc * inv).astype(BF16), vch) + _dot((pp * inv).astype(BF16), vph)
        o_ref[0, :, pair * LANES:(pair + 1) * LANES] = out


def swa_attention(qkv, sinks, tab):
    b, t, _ = qkv.shape
    nb = t // BAND
    qw = (A_HEADS // A_KV_HEADS) * HEAD_DIM
    koff = A_HEADS * HEAD_DIM // LANES
    voff = koff + A_KV_HEADS
    group = A_HEADS // A_KV_HEADS
    return pl.pallas_call(
        _swa_kernel,
        grid=(b, A_KV_HEADS, nb),
        in_specs=[pl.BlockSpec(memory_space=pltpu.SMEM),
                  pl.BlockSpec((1, BAND, qw), lambda i, h, n: (i, n, h)),
                  pl.BlockSpec((1, BAND, LANES), lambda i, h, n: (i, n, koff + h)),
                  pl.BlockSpec((1, BAND, LANES), lambda i, h, n: (i, jnp.maximum(n - 1, 0), koff + h)),
                  pl.BlockSpec((1, BAND, LANES), lambda i, h, n: (i, n, voff + h)),
                  pl.BlockSpec((1, BAND, LANES), lambda i, h, n: (i, jnp.maximum(n - 1, 0), voff + h)),
                  pl.BlockSpec((2, group, LANES, LANES), lambda i, h, n: (0, h, 0, 0))],
        out_specs=pl.BlockSpec((1, BAND, qw), lambda i, h, n: (i, n, h)),
        out_shape=jax.ShapeDtypeStruct((b, t, A_HEADS * HEAD_DIM), F32),
        compiler_params=_cparams("parallel", "parallel", "arbitrary"),
        name="swa_attention",
    )(sinks, qkv, qkv, qkv, qkv, qkv, tab)


def _split3(x):
    p0 = x.astype(BF16)
    r1 = x - p0.astype(F32)
    p1 = r1.astype(BF16)
    p2 = (r1 - p1.astype(F32)).astype(BF16)
    return p0, p1, p2


def _dot_exact_rhs01(x, m01):
    p0, p1, p2 = _split3(x)
    return _dot(p0, m01) + _dot(p1, m01) + _dot(p2, m01)


def _dot_exact_lhs01(m01, x):
    p0, p1, p2 = _split3(x)
    return _dot(m01, p0) + _dot(m01, p1) + _dot(m01, p2)


def _head_sum_matrix(width):
    r = lax.broadcasted_iota(jnp.int32, (width, width), 0) // HEAD_DIM
    c = lax.broadcasted_iota(jnp.int32, (width, width), 1) // HEAD_DIM
    return jnp.where(r == c, 1.0, 0.0).astype(BF16)


def _sigmoid(z):
    return 1.0 / (1.0 + jnp.exp(-z))


def _softplus(z):
    return jnp.maximum(z, 0.0) + jnp.log(1.0 + jnp.exp(-jnp.abs(z)))


def _rwkv_prep_kernel(p_ref, pprev_ref, mu_ref, wa2_ref, g2_ref, w0_ref, a0_ref, kk_ref, ka_ref,
                      r_out, k_out, v_out, lw_out, kk_out, b_out, g_out):
    i = pl.program_id(1)
    p = p_ref[0]
    tc = p.shape[0]
    w = B_WIDTH
    last = pprev_ref[0, 7:8, :]
    last = jnp.where(i > 0, last, 0.0)
    row = lax.broadcasted_iota(jnp.int32, p.shape, 0)
    prev = jnp.where(row == 0, last, pltpu.roll(p, 1, 0))
    xs = p + (prev - p) * mu_ref[...]
    r = xs[:, 0:w]
    k = xs[:, w:2 * w]
    v = xs[:, 2 * w:3 * w]
    wa_lo = xs[:, 3 * w:3 * w + LANES]
    g_lo = xs[:, 3 * w + LANES:3 * w + 2 * LANES]
    lo, _ = _lane_half_masks(wa_lo.shape)
    wa_in = jnp.where(lo, jnp.tanh(wa_lo), wa_lo)
    wa = _dot_x(wa_in, wa2_ref[...])
    wlog = -_softplus(-(w0_ref[...] + wa[:, 0:w])) - 0.5
    a = _sigmoid(a0_ref[...] + wa[:, w:2 * w])
    g = _dot(_sigmoid(g_lo).astype(BF16), g2_ref[...])
    kk = k * kk_ref[...]
    ss = _dot_exact_rhs01(kk * kk, _head_sum_matrix(w))
    kk = kk / jnp.maximum(jnp.sqrt(ss), 1e-12)
    kmod = k * (1.0 + (a - 1.0) * ka_ref[...])
    r_out[0] = r
    k_out[0] = kmod
    v_out[0] = v
    lw_out[0] = -jnp.exp(wlog)
    kk_out[0] = kk
    b_out[0] = kk * a
    g_out[0] = g


def rwkv_prep(pb, mu, w0, w2, a0, a2, g2, k_k, k_a, tc=256):
    b, t, win = pb.shape
    tc = min(tc, t)
    w = B_WIDTH
    wa2 = jnp.zeros((LANES, 2 * w), F32)
    wa2 = wa2.at[:B_DECAY_RANK, :w].set(w2).at[B_DECAY_RANK:, w:].set(a2)
    row = lambda z: z.reshape(1, -1)
    vec = pl.BlockSpec((1, w), lambda i, j: (0, 0))
    outs = [jax.ShapeDtypeStruct((b, t, w), F32)] * 7
    return pl.pallas_call(
        _rwkv_prep_kernel,
        grid=(b, t // tc),
        in_specs=[pl.BlockSpec((1, tc, win), lambda i, j: (i, j, 0)),
                  pl.BlockSpec((1, 8, win), lambda i, j: (i, jnp.maximum(j * (tc // 8) - 1, 0), 0)),
                  pl.BlockSpec((1, win), lambda i, j: (0, 0)),
                  pl.BlockSpec((LANES, 2 * w), lambda i, j: (0, 0)),
                  pl.BlockSpec((B_GATE_RANK, w), lambda i, j: (0, 0)),
                  vec, vec, vec, vec],
        out_specs=[pl.BlockSpec((1, tc, w), lambda i, j: (i, j, 0))] * 7,
        out_shape=outs,
        compiler_params=_cparams("parallel", "arbitrary"),
        name="rwkv_prep",
    )(pb, pb, row(mu), wa2, g2.astype(BF16), row(w0), row(a0), row(k_k), row(k_a))


def _rwkv_scan_kernel(r_ref, k_ref, v_ref, lw_ref, kk_ref, b_ref, g_ref, rk_ref, lnw_ref, lnb_ref,
                      o_ref, st_scr):
    c = pl.program_id(1)
    ch = r_ref.shape[1]

    @pl.when(c == 0)
    def _():
        st_scr[...] = jnp.zeros_like(st_scr)

    ti = lax.broadcasted_iota(jnp.int32, (ch, ch), 0)
    si = lax.broadcasted_iota(jnp.int32, (ch, ch), 1)
    incl = si <= ti
    strict = si < ti
    tri01 = jnp.where(incl, 1.0, 0.0).astype(BF16)
    eye = jnp.where(si == ti, 1.0, 0.0)
    levels = []
    s = 1
    while s < ch:
        same = (ti // (2 * s)) == (si // (2 * s))
        levels.append(jnp.logical_and(same, jnp.logical_and((ti // s) % 2 == 1, (si // s) % 2 == 0)))
        s *= 2
    lo, hi = _lane_half_masks((ch, LANES))
    blk_r = lax.broadcasted_iota(jnp.int32, (LANES, LANES), 0) // HEAD_DIM
    blk_c = lax.broadcasted_iota(jnp.int32, (LANES, LANES), 1) // HEAD_DIM
    same_head = blk_r == blk_c
    hsum = _head_sum_matrix(LANES)

    lw_all = lw_ref[0]
    cum_all = _dot_exact_lhs01(tri01, lw_all)
    for pr in range(B_HEADS // 2):
        sl = slice(pr * LANES, (pr + 1) * LANES)
        r, k, v, kk, bb = r_ref[0, :, sl], k_ref[0, :, sl], v_ref[0, :, sl], kk_ref[0, :, sl], b_ref[0, :, sl]
        lw, cum = lw_all[:, sl], cum_all[:, sl]
        tot = cum[ch - 1:ch, :]
        e_excl = jnp.exp(cum - lw)
        e_neg = jnp.exp(-cum)
        e_pos = jnp.exp(cum)
        e_rest = jnp.exp(tot - cum)
        a_t, k_t, b_t, r_t = kk * e_excl, k * e_neg, bb * e_neg, r * e_pos
        k_h, b_h = k * e_rest, bb * e_rest
        st = st_scr[pr]
        x0 = _dot_nt_x(a_t, st)
        y = _dot_nt_x(r_t, st)
        u = jnp.zeros((ch, LANES), F32)
        for msk in (lo, hi):
            a_m = jnp.where(msk, a_t, 0.0)
            r_m = jnp.where(msk, r_t, 0.0)
            v_m = jnp.where(msk, v, 0.0)
            ab = jnp.where(strict, _dot_nt_x(a_m, b_t), 0.0)
            ak = jnp.where(strict, _dot_nt_x(a_m, k_t), 0.0)
            rk = jnp.where(incl, _dot_nt_x(r_m, k_t), 0.0)
            rb = jnp.where(incl, _dot_nt_x(r_m, b_t), 0.0)
            tinv = eye
            for lvl in levels:
                tinv = tinv - _dot_x(_dot_x(tinv, jnp.where(lvl, ab, 0.0)), tinv)
            u_m = _dot_x(tinv, jnp.where(msk, x0, 0.0) + _dot_x(ak, v_m))
            y = y + _dot_x(rk, v_m) - _dot_x(rb, u_m)
            u = u + u_m
        upd = _dot_tn_x(v, k_h) - _dot_tn_x(u, b_h)
        st_scr[pr] = st * jnp.exp(tot) + jnp.where(same_head, upd, 0.0)
        mean = _dot_exact_rhs01(y, hsum) * (1.0 / HEAD_DIM)
        yc = y - mean
        var = _dot_exact_rhs01(yc * yc, hsum) * (1.0 / HEAD_DIM)
        yn = yc * lax.rsqrt(var + B_LN_EPS) * lnw_ref[:, sl] + lnb_ref[:, sl]
        bonus = _dot_exact_rhs01(r * k * rk_ref[:, sl], hsum) * v
        o_ref[0, :, sl] = (yn + bonus) * g_ref[0, :, sl]


def rwkv_scan(r, k, v, lw, kk, bb, g, r_k, ln_w, ln_b, ch=RWKV_CHUNK):
    b, t, w = r.shape
    ch = min(ch, t)
    seq = pl.BlockSpec((1, ch, w), lambda i, c: (i, c, 0))
    vec = pl.BlockSpec((1, w), lambda i, c: (0, 0))
    row = lambda z: z.reshape(1, w)
    return pl.pallas_call(
        _rwkv_scan_kernel,
        grid=(b, t // ch),
        in_specs=[seq] * 7 + [vec] * 3,
        out_specs=seq,
        out_shape=jax.ShapeDtypeStruct((b, t, w), F32),
        scratch_shapes=[pltpu.VMEM((B_HEADS // 2, LANES, LANES), F32)],
        compiler_params=_cparams("parallel", "arbitrary"),
        name="rwkv_scan",
    )(r, k, v, lw, kk, bb, g, row(r_k), row(ln_w), row(ln_b))


def _moba_kernel(q_ref, k_ref, v_ref, tab_ref, o_ref, k_scr, v_scr, km_scr, m_scr, l_scr, acc_scr):
    c = pl.program_id(2)
    t = k_ref.shape[1]
    nblk = t // MOBA_BLOCK
    group = C_HEADS // C_KV_HEADS
    scale = HEAD_DIM ** -0.5

    @pl.when(c == 0)
    def _():
        kf = k_ref[0]
        k_scr[...] = kf.astype(BF16)
        v_scr[...] = v_ref[0].astype(BF16)
        km_scr[...] = jnp.zeros_like(km_scr)
        km_scr[0:nblk, :] = jnp.mean(kf.reshape(nblk, MOBA_BLOCK, LANES), axis=1)

    own = (c * QCHUNK) // MOBA_BLOCK
    sub = c % (MOBA_BLOCK // QCHUNK)
    lo, hi = _lane_half_masks((QCHUNK, LANES))
    halves = (lo, hi)
    kv_halves = _lane_half_masks((MOBA_BLOCK, LANES))
    qtiles = [q_ref[0, :, p * LANES:(p + 1) * LANES] for p in range(group // 2)]

    qsum = qtiles[0]
    for qt in qtiles[1:]:
        qsum = qsum + qt
    gate = _dot_nt_x(qsum, km_scr[...])
    blk = lax.broadcasted_iota(jnp.int32, (QCHUNK, LANES), 1)
    gate = jnp.where(blk < own, gate, -jnp.inf)
    rank = jnp.zeros((QCHUNK, LANES), jnp.int32)
    for mblk in range(nblk):
        gm = gate[:, mblk:mblk + 1]
        ahead = jnp.logical_or(gm > gate, jnp.logical_and(gm == gate, mblk < blk))
        rank = rank + jnp.where(ahead, 1, 0)
    chosen = jnp.logical_and(rank < MOBA_TOPK, blk < own)
    chosen_f = jnp.where(chosen, 1.0, 0.0)

    qh = []
    for p in range(group // 2):
        for msk in halves:
            qh.append(jnp.where(msk, qtiles[p], 0.0).astype(BF16))

    qi = lax.broadcasted_iota(jnp.int32, (QCHUNK, MOBA_BLOCK), 0)
    kj = lax.broadcasted_iota(jnp.int32, (QCHUNK, MOBA_BLOCK), 1)

    def bias_tile(g, d0):
        da = jnp.clip(d0, 0, BIAS_TILES - 1)
        db = jnp.clip(d0 - 1, 0, BIAS_TILES - 1)
        return jnp.concatenate([tab_ref[da, g], tab_ref[db, g]], axis=-1)

    def attend(blk_idx, ok, first):
        start = pl.multiple_of(blk_idx * MOBA_BLOCK, MOBA_BLOCK)
        kb = k_scr[pl.ds(start, MOBA_BLOCK), :]
        vb = v_scr[pl.ds(start, MOBA_BLOCK), :]
        d0 = c - 2 * blk_idx
        for g in range(group):
            s = jnp.where(ok, _dot_nt(qh[g], kb) * scale + bias_tile(g, d0), MASKED)
            m_old = jnp.full((QCHUNK, 1), M_INIT, F32) if first else m_scr[g]
            m_new = jnp.maximum(m_old, jnp.max(s, axis=-1, keepdims=True))
            p = jnp.exp(s - m_new)
            vm = jnp.where(kv_halves[g % 2], vb, jnp.zeros_like(vb))
            pv = _dot(p.astype(BF16), vm)
            if first:
                l_scr[g] = jnp.sum(p, axis=-1, keepdims=True)
                acc_scr[g] = pv
            else:
                alpha = jnp.exp(m_old - m_new)
                l_scr[g] = alpha * l_scr[g] + jnp.sum(p, axis=-1, keepdims=True)
                acc_scr[g] = alpha * acc_scr[g] + pv
            m_scr[g] = m_new

    attend(own, (sub * QCHUNK + qi - kj) >= 0, True)

    def body(nb, carry):
        pick = jnp.max(jnp.where(blk == nb, chosen_f, 0.0), axis=-1, keepdims=True) > 0.5
        attend(nb, jnp.broadcast_to(pick, (QCHUNK, MOBA_BLOCK)), False)
        return carry

    lax.fori_loop(0, own, body, 0)

    for p in range(group // 2):
        out = acc_scr[2 * p] / l_scr[2 * p] + acc_scr[2 * p + 1] / l_scr[2 * p + 1]
        o_ref[0, :, p * LANES:(p + 1) * LANES] = out


def moba_attention(qc, kvc, tab):
    b, t, _ = qc.shape
    assert t % MOBA_BLOCK == 0
    nq = t // QCHUNK
    group = C_HEADS // C_KV_HEADS
    qw = group * HEAD_DIM
    return pl.pallas_call(
        _moba_kernel,
        grid=(b, C_KV_HEADS, nq),
        in_specs=[pl.BlockSpec((1, QCHUNK, qw), lambda i, h, c: (i, c, h)),
                  pl.BlockSpec((1, t, LANES), lambda i, h, c: (i, 0, h)),
                  pl.BlockSpec((1, t, LANES), lambda i, h, c: (i, 0, C_KV_HEADS + h)),
                  pl.BlockSpec((BIAS_TILES, group, LANES, LANES), lambda i, h, c: (0, h, 0, 0))],
        out_specs=pl.BlockSpec((1, QCHUNK, qw), lambda i, h, c: (i, c, h)),
        out_shape=jax.ShapeDtypeStruct((b, t, C_HEADS * HEAD_DIM), F32),
        scratch_shapes=[pltpu.VMEM((t, LANES), BF16), pltpu.VMEM((t, LANES), BF16),
                        pltpu.VMEM((LANES, LANES), F32),
                        pltpu.VMEM((group, QCHUNK, 1), F32), pltpu.VMEM((group, QCHUNK, 1), F32),
                        pltpu.VMEM((group, QCHUNK, LANES), F32)],
        compiler_params=_cparams("parallel", "parallel", "arbitrary"),
        name="moba_attention",
    )(qc, kvc, kvc, tab)


def _dsa_kernel(q_ref, kv_ref, qi_ref, ki_ref, gk_ref, tabt_ref, o_ref,
                k_scr, vt_scr, ki_scr, key_scr, qst_scr, qist_scr, m_scr, acc_scr, *, topk):
    c = pl.program_id(1)
    t = k_scr.shape[0]
    scale = HEAD_DIM ** -0.5
    ntile = c + 1
    sub = lax.broadcasted_iota(jnp.int32, (LANES, QCHUNK), 0)
    lane = lax.broadcasted_iota(jnp.int32, (LANES, QCHUNK), 1)

    @pl.when(c == 0)
    def _():
        k_scr[...] = kv_ref[0, :, 0:LANES].astype(BF16)
        ki_scr[...] = _rms(ki_ref[0], gk_ref[...]).astype(BF16)

        def vt_body(j, carry):
            start = pl.multiple_of(j * LANES, LANES)
            vt = kv_ref[0, pl.ds(start, LANES), LANES:2 * LANES].T
            vt_scr[:, pl.ds(start, LANES)] = jnp.where(sub < HEAD_DIM, vt, 1.0).astype(BF16)
            return carry

        lax.fori_loop(0, t // LANES, vt_body, 0)

    lo, hi = _lane_half_masks((QCHUNK, LANES))
    halves = (lo, hi)
    for h in range(D_HEADS):
        tile = q_ref[0, :, (h // 2) * LANES:(h // 2 + 1) * LANES] * scale
        qst_scr[h * QCHUNK:(h + 1) * QCHUNK, :] = jnp.where(halves[h % 2], tile, 0.0).astype(BF16)
    for h in range(IDX_HEADS):
        tile = qi_ref[0, :, (h // 2) * LANES:(h // 2 + 1) * LANES]
        qist_scr[h * QCHUNK:(h + 1) * QCHUNK, :] = jnp.where(halves[h % 2], tile, 0.0).astype(BF16)
    w_t = (qi_ref[0, :, IDX_HEADS * IDX_DIM + LANES:IDX_HEADS * IDX_DIM + 2 * LANES]
           * ((IDX_HEADS * IDX_DIM) ** -0.5)).T
    qpos = c * QCHUNK + lane

    def score_body(j, carry):
        start = pl.multiple_of(j * LANES, LANES)
        rel = jnp.maximum(_dot_nt(ki_scr[pl.ds(start, LANES), :], qist_scr[...]), 0.0)
        sc = jnp.zeros((LANES, QCHUNK), F32)
        for h in range(IDX_HEADS):
            sc = sc + rel[:, h * QCHUNK:(h + 1) * QCHUNK] * w_t[h:h + 1, :]
        sc = jnp.where(sc == 0.0, 0.0, sc)
        bits = pltpu.bitcast(sc, jnp.int32)
        skey = bits ^ (lax.shift_right_arithmetic(bits, 31) & 0x7FFFFFFF)
        key_scr[pl.ds(start, LANES), :] = jnp.where((j * LANES + sub) <= qpos, skey, INT_MIN)
        return carry

    lax.fori_loop(0, ntile, score_body, 0)

    def count(pred):
        def body(j, acc):
            start = pl.multiple_of(j * LANES, LANES)
            return acc + jnp.where(pred(key_scr[pl.ds(start, LANES), :], j), 1, 0)
        acc = lax.fori_loop(0, ntile, body, jnp.zeros((LANES, QCHUNK), jnp.int32))
        return jnp.sum(acc, axis=0, keepdims=True)

    cnt0 = count(lambda kt, j: kt >= 0)
    thr = jnp.where(cnt0 >= topk, 0, INT_MIN).astype(jnp.int32)

    def bit_body(i, thr):
        cand = thr + lax.shift_left(jnp.int32(1), 30 - i)
        cnt = count(lambda kt, j: kt >= cand)
        return jnp.where(cnt >= topk, cand, thr)

    thr = lax.fori_loop(0, 31, bit_body, thr)

    cnt_ge = count(lambda kt, j: kt >= thr)
    tied = jnp.logical_and(cnt_ge > topk, thr > INT_MIN)
    any_tied = jnp.max(jnp.where(tied, 1.0, 0.0)) > 0.5

    def find_cut():
        need = topk - count(lambda kt, j: kt > thr)

        def cut_body(i, cut):
            cand = cut + lax.shift_left(jnp.int32(1), 14 - i)
            cnt = count(lambda kt, j: jnp.logical_and(kt == thr, (j * LANES + sub) < cand))
            return jnp.where(cnt < need, cand, cut)

        return lax.fori_loop(0, 15, cut_body, jnp.zeros((1, QCHUNK), jnp.int32))

    cut = lax.cond(any_tied, find_cut, lambda: jnp.full((1, QCHUNK), 2 ** 30, jnp.int32))

    m_scr[...] = jnp.full(m_scr.shape, M_INIT, F32)
    acc_scr[...] = jnp.zeros_like(acc_scr)

    def att_body(j, carry):
        start = pl.multiple_of(j * LANES, LANES)
        s_all = _dot_nt(k_scr[pl.ds(start, LANES), :], qst_scr[...])
        keys = key_scr[pl.ds(start, LANES), :]
        ok = jnp.logical_or(keys > thr, jnp.logical_and(keys == thr, (j * LANES + sub) <= cut))
        ok = jnp.logical_and(ok, keys != INT_MIN)
        okb = jnp.where(ok, 0.0, MASKED)
        d = jnp.minimum(c - j, BIAS_TILES - 1)
        vt = vt_scr[:, pl.ds(start, LANES)]
        for h in range(D_HEADS):
            s = s_all[:, h * QCHUNK:(h + 1) * QCHUNK] + (tabt_ref[d, h] + okb)
            m_old = m_scr[h]
            m_new = jnp.maximum(m_old, jnp.max(s, axis=0, keepdims=True))
            p = jnp.exp(s - m_new)
            acc_scr[h] = jnp.exp(m_old - m_new) * acc_scr[h] + _dot(vt, p.astype(BF16))
            m_scr[h] = m_new
        return carry

    lax.fori_loop(0, ntile, att_body, 0)
    for p in range(D_HEADS // 2):
        parts = []
        for h in (2 * p, 2 * p + 1):
            a = acc_scr[h]
            parts.append(a[0:HEAD_DIM, :] * (1.0 / a[HEAD_DIM:HEAD_DIM + 1, :]))
        o_ref[0, :, p * LANES:(p + 1) * LANES] = jnp.concatenate(parts, axis=0).T


def dsa_attention(qd, kvd, idx, kidx_gain, tabt):
    b, t, _ = qd.shape
    topk = min(DSA_TOPK, t // 4)
    nq = t // QCHUNK
    gk = jnp.concatenate([kidx_gain, kidx_gain]).reshape(1, LANES)
    wq = D_HEADS * HEAD_DIM
    return pl.pallas_call(
        functools.partial(_dsa_kernel, topk=topk),
        grid=(b, nq),
        in_specs=[pl.BlockSpec((1, QCHUNK, wq), lambda i, c: (i, c, 0)),
                  pl.BlockSpec((1, t, 2 * LANES), lambda i, c: (i, 0, 0)),
                  pl.BlockSpec((1, QCHUNK, 4 * LANES), lambda i, c: (i, c, 0)),
                  pl.BlockSpec((1, t, LANES), lambda i, c: (i, 0, 2)),
                  pl.BlockSpec((1, LANES), lambda i, c: (0, 0)),
                  pl.BlockSpec((BIAS_TILES, D_HEADS, LANES, LANES), lambda i, c: (0, 0, 0, 0))],
        out_specs=pl.BlockSpec((1, QCHUNK, wq), lambda i, c: (i, c, 0)),
        out_shape=jax.ShapeDtypeStruct((b, t, wq), F32),
        scratch_shapes=[pltpu.VMEM((t, LANES), BF16), pltpu.VMEM((LANES, t), BF16), pltpu.VMEM((t, LANES), BF16),
                        pltpu.VMEM((t, QCHUNK), jnp.int32),
                        pltpu.VMEM((D_HEADS * QCHUNK, LANES), BF16), pltpu.VMEM((IDX_HEADS * QCHUNK, LANES), BF16),
                        pltpu.VMEM((D_HEADS, 1, QCHUNK), F32),
                        pltpu.VMEM((D_HEADS, LANES, QCHUNK), F32)],
        compiler_params=_cparams("parallel", "arbitrary"),
        name="dsa_attention",
    )(qd, kvd, idx, idx, gk, tabt)


def _dsa_kernel_rowmajor(q_ref, kv_ref, qi_ref, ki_ref, gk_ref, tab_ref, o_ref,
                k_scr, v_scr, ki_scr, key_scr, m_scr, l_scr, acc_scr, *, topk):
    c = pl.program_id(1)
    scale = HEAD_DIM ** -0.5
    ntile = c + 1

    @pl.when(c == 0)
    def _():
        k_scr[...] = kv_ref[0, :, 0:LANES].astype(BF16)
        v_scr[...] = kv_ref[0, :, LANES:2 * LANES].astype(BF16)
        ki_scr[...] = _rms(ki_ref[0], gk_ref[...])

    lo, hi = _lane_half_masks((QCHUNK, LANES))
    halves = (lo, hi)
    row = lax.broadcasted_iota(jnp.int32, (QCHUNK, LANES), 0)
    col = lax.broadcasted_iota(jnp.int32, (QCHUNK, LANES), 1)

    wi = qi_ref[0, :, IDX_HEADS * IDX_DIM + LANES:IDX_HEADS * IDX_DIM + 2 * LANES] * ((IDX_HEADS * IDX_DIM) ** -0.5)
    qidx = []
    for h in range(IDX_HEADS):
        tile = qi_ref[0, :, (h // 2) * LANES:(h // 2 + 1) * LANES]
        qidx.append(jnp.where(halves[h % 2], tile, 0.0))

    def score_body(j, carry):
        start = pl.multiple_of(j * LANES, LANES)
        kt = ki_scr[pl.ds(start, LANES), :]
        sc = jnp.zeros((QCHUNK, LANES), F32)
        for h in range(IDX_HEADS):
            sc = sc + jnp.maximum(_dot_nt_x(qidx[h], kt), 0.0) * wi[:, h:h + 1]
        sc = jnp.where(sc == 0.0, 0.0, sc)
        bits = pltpu.bitcast(sc, jnp.int32)
        skey = bits ^ (lax.shift_right_arithmetic(bits, 31) & 0x7FFFFFFF)
        visible = (j * LANES + col) <= (c * QCHUNK + row)
        key_scr[:, pl.ds(start, LANES)] = jnp.where(visible, skey, INT_MIN)
        return carry

    lax.fori_loop(0, ntile, score_body, 0)

    def count(pred):
        def body(j, acc):
            start = pl.multiple_of(j * LANES, LANES)
            return acc + jnp.where(pred(key_scr[:, pl.ds(start, LANES)], j), 1, 0)
        acc = lax.fori_loop(0, ntile, body, jnp.zeros((QCHUNK, LANES), jnp.int32))
        return jnp.sum(acc, axis=-1, keepdims=True)

    cnt0 = count(lambda kt, j: kt >= 0)
    thr = jnp.where(cnt0 >= topk, 0, INT_MIN).astype(jnp.int32)

    def bit_body(i, thr):
        cand = thr + lax.shift_left(jnp.int32(1), 30 - i)
        cnt = count(lambda kt, j: kt >= cand)
        return jnp.where(cnt >= topk, cand, thr)

    thr = lax.fori_loop(0, 31, bit_body, thr)
    need = topk - count(lambda kt, j: kt > thr)

    def cut_body(i, cut):
        cand = cut + lax.shift_left(jnp.int32(1), 14 - i)
        cnt = count(lambda kt, j: jnp.logical_and(kt == thr, (j * LANES + col) < cand))
        return jnp.where(cnt < need, cand, cut)

    cut = lax.fori_loop(0, 15, cut_body, jnp.zeros((QCHUNK, 1), jnp.int32))

    qst = []
    for p in range(D_HEADS // 2):
        tile = q_ref[0, :, p * LANES:(p + 1) * LANES]
        for msk in halves:
            qst.append(jnp.where(msk, tile, 0.0).astype(BF16))
    m_scr[...] = jnp.full(m_scr.shape, M_INIT, F32)
    l_scr[...] = jnp.zeros_like(l_scr)
    acc_scr[...] = jnp.zeros_like(acc_scr)

    def att_body(j, carry):
        start = pl.multiple_of(j * LANES, LANES)
        kt = k_scr[pl.ds(start, LANES), :]
        vt = v_scr[pl.ds(start, LANES), :]
        keys = key_scr[:, pl.ds(start, LANES)]
        pos = j * LANES + col
        ok = jnp.logical_or(keys > thr, jnp.logical_and(keys == thr, pos <= cut))
        ok = jnp.logical_and(ok, keys != INT_MIN)
        d = jnp.minimum(c - j, BIAS_TILES - 1)
        for h in range(D_HEADS):
            s = jnp.where(ok, _dot_nt(qst[h], kt) * scale + tab_ref[d, h], MASKED)
            m_old = m_scr[h]
            m_new = jnp.maximum(m_old, jnp.max(s, axis=-1, keepdims=True))
            p = jnp.exp(s - m_new)
            alpha = jnp.exp(m_old - m_new)
            vm = jnp.where(halves[h % 2], vt, jnp.zeros_like(vt))
            l_scr[h] = alpha * l_scr[h] + jnp.sum(p, axis=-1, keepdims=True)
            acc_scr[h] = alpha * acc_scr[h] + _dot(p.astype(BF16), vm)
            m_scr[h] = m_new
        return carry

    lax.fori_loop(0, ntile, att_body, 0)
    for p in range(D_HEADS // 2):
        out = acc_scr[2 * p] / l_scr[2 * p] + acc_scr[2 * p + 1] / l_scr[2 * p + 1]
        o_ref[0, :, p * LANES:(p + 1) * LANES] = out


def dsa_attention_rowmajor(qd, kvd, idx, kidx_gain, tab):
    b, t, _ = qd.shape
    topk = min(DSA_TOPK, t // 4)
    nq = t // QCHUNK
    gk = jnp.concatenate([kidx_gain, kidx_gain]).reshape(1, LANES)
    wq = D_HEADS * HEAD_DIM
    return pl.pallas_call(
        functools.partial(_dsa_kernel_rowmajor, topk=topk),
        grid=(b, nq),
        in_specs=[pl.BlockSpec((1, QCHUNK, wq), lambda i, c: (i, c, 0)),
                  pl.BlockSpec((1, t, 2 * LANES), lambda i, c: (i, 0, 0)),
                  pl.BlockSpec((1, QCHUNK, 4 * LANES), lambda i, c: (i, c, 0)),
                  pl.BlockSpec((1, t, LANES), lambda i, c: (i, 0, 2)),
                  pl.BlockSpec((1, LANES), lambda i, c: (0, 0)),
                  pl.BlockSpec((BIAS_TILES, D_HEADS, LANES, LANES), lambda i, c: (0, 0, 0, 0))],
        out_specs=pl.BlockSpec((1, QCHUNK, wq), lambda i, c: (i, c, 0)),
        out_shape=jax.ShapeDtypeStruct((b, t, wq), F32),
        scratch_shapes=[pltpu.VMEM((t, LANES), BF16), pltpu.VMEM((t, LANES), BF16), pltpu.VMEM((t, LANES), F32),
                        pltpu.VMEM((QCHUNK, t), jnp.int32),
                        pltpu.VMEM((D_HEADS, QCHUNK, 1), F32), pltpu.VMEM((D_HEADS, QCHUNK, 1), F32),
                        pltpu.VMEM((D_HEADS, QCHUNK, LANES), F32)],
        compiler_params=_cparams("parallel", "arbitrary"),
        name="dsa_attention",
    )(qd, kvd, idx, idx, gk, tab)


def _dup_heads(wcols, n_heads):
    d = wcols.shape[0]
    wh = wcols.reshape(d, n_heads, 1, HEAD_DIM)
    return jnp.broadcast_to(wh, (d, n_heads, 2, HEAD_DIM)).reshape(d, n_heads * 2 * HEAD_DIM)


def _even_weight(w_in):
    aq, akv = A_HEADS * HEAD_DIM, A_KV_HEADS * HEAD_DIM
    q = w_in[:, :aq]
    k = _dup_heads(w_in[:, aq:aq + akv], A_KV_HEADS)
    v = _dup_heads(w_in[:, aq + akv:aq + 2 * akv], A_KV_HEADS)
    rest = w_in[:, aq + 2 * akv:]
    return jnp.concatenate([q, k, v, rest], axis=1).astype(BF16)


def _odd_weight(w_in):
    d = w_in.shape[0]
    cq, ckv, dq = C_HEADS * HEAD_DIM, C_KV_HEADS * HEAD_DIM, D_HEADS * HEAD_DIM
    cuts = np.cumsum([cq, ckv, ckv, dq, HEAD_DIM, HEAD_DIM, IDX_HEADS * IDX_DIM, IDX_DIM]).tolist()
    qc, kc, vc, qd, kd, vd, qi, ki, wi = jnp.split(w_in, cuts, axis=1)
    pad = jnp.zeros((d, LANES - IDX_HEADS), w_in.dtype)
    cols = [qc, _dup_heads(kc, C_KV_HEADS), _dup_heads(vc, C_KV_HEADS),
            qd, _dup_heads(kd, 1), _dup_heads(vd, 1),
            qi, _dup_heads(ki, 1), wi, pad]
    return jnp.concatenate(cols, axis=1).astype(BF16)


def kernel(x, mem, rel_bias, norm_gains, mix_w_out, mem_norm, x_wq, x_wkv, x_wo, ff_w1, ff_w2, ev_w_in, a_sinks, b_mu, b_w0, b_w2, b_a0, b_a2, b_g2, b_k_k, b_k_a, b_r_k, b_ln_w, b_ln_b, od_w_in, d_kidx_norm):
    bsz, t, d = x.shape
    n = bsz * t
    depth = norm_gains.shape[0]
    mlen = mem.shape[1]
    tab = bias_table(rel_bias)
    tabt = bias_table(rel_bias, transposed=True)
    x2 = x.reshape(n, d)
    for layer in range(depth):
        i = layer // 2
        g = norm_gains[layer]
        if layer % 2 == 0:
            b_in = 3 * B_WIDTH + B_DECAY_RANK + B_A_RANK + B_GATE_RANK
            qkv, pb = norm_proj(x2, g[0], _even_weight(ev_w_in[i]), (1024, b_in), (F32, F32))
            ya = swa_attention(qkv.reshape(bsz, t, 1024), a_sinks[i], tab)
            prep = rwkv_prep(pb.reshape(bsz, t, b_in), b_mu[i], b_w0[i], b_w2[i], b_a0[i], b_a2[i], b_g2[i],
                             b_k_k[i], b_k_a[i])
            yb = rwkv_scan(*prep, b_r_k[i].reshape(-1), b_ln_w[i], b_ln_b[i])
        else:
            qc, kvc, qd, kvd, idx = norm_proj(x2, g[0], _odd_weight(od_w_in[i]),
                                              (512, 512, 512, 256, 512), (F32,) * 5)
            ya = moba_attention(qc.reshape(bsz, t, 512), kvc.reshape(bsz, t, 512), tab)
            yb = dsa_attention(qd.reshape(bsz, t, 512), kvd.reshape(bsz, t, 256), idx.reshape(bsz, t, 512),
                               d_kidx_norm[i], tabt)
        x2 = proj_norm_res(ya.reshape(n, -1), yb.reshape(n, -1), mix_w_out[layer], g[1], x2)
        kmem, vmem = norm_proj(mem.reshape(bsz * mlen, d), mem_norm[layer], x_wkv[layer].astype(BF16),
                               (X_WIDTH, X_WIDTH), (BF16, BF16))
        x3 = cross_attn(x2.reshape(bsz, t, d), kmem.reshape(bsz, mlen, X_WIDTH), vmem.reshape(bsz, mlen, X_WIDTH),
                        x_wq[layer], x_wo[layer], g[2], g[3])
        x2 = mlp(x3.reshape(n, d), ff_w1[layer], ff_w2[layer], g[4], g[5])
    return x2.reshape(bsz, t, d)
```

```python
import functools

import numpy as np
import jax
import jax.numpy as jnp
from jax import lax
from jax.experimental import pallas as pl
from jax.experimental.pallas import tpu as pltpu

F32 = jnp.float32
BF16 = jnp.bfloat16
HIGHEST = lax.Precision.HIGHEST

LANES = 128
SUBLANES = 8
HEAD_DIM = 64
EPS = 1e-6
A_HEADS, A_KV_HEADS, BAND = 8, 2, 128
B_HEADS, B_WIDTH = 8, 512
B_DECAY_RANK, B_A_RANK, B_GATE_RANK = 64, 64, 128
B_LN_EPS = 64e-5
RWKV_CHUNK = 64
C_HEADS, C_KV_HEADS, MOBA_BLOCK, MOBA_TOPK, QCHUNK = 8, 2, 256, 3, 128
D_HEADS, IDX_HEADS, IDX_DIM, DSA_TOPK = 8, 4, 64, 256
N_BUCKETS, BIAS_HEADS = 32, 8
BIAS_TILES = 9
X_HEADS, X_HEAD_DIM, X_WIDTH = 4, 128, 512

MASKED = -2e30
M_INIT = -1e30
INT_MIN = -(2 ** 31)
VMEM_LIMIT = 56 * 1024 * 1024


def _cparams(*sem):
    return pltpu.CompilerParams(dimension_semantics=tuple(sem), vmem_limit_bytes=VMEM_LIMIT)


def _dot(a, b):
    return jnp.dot(a, b, preferred_element_type=F32)


def _dot_nt(a, b):
    return lax.dot_general(a, b, (((1,), (1,)), ((), ())), preferred_element_type=F32)


def _dot_tn(a, b):
    return lax.dot_general(a, b, (((0,), (0,)), ((), ())), preferred_element_type=F32)


def _dot_x(a, b):
    return jnp.dot(a, b, preferred_element_type=F32, precision=HIGHEST)


def _dot_nt_x(a, b):
    return lax.dot_general(a, b, (((1,), (1,)), ((), ())), preferred_element_type=F32, precision=HIGHEST)


def _rms(xf, g):
    ms = jnp.mean(xf * xf, axis=-1, keepdims=True)
    return xf * lax.rsqrt(ms + EPS) * g


def _lane_half_masks(shape):
    lane = lax.broadcasted_iota(jnp.int32, shape, len(shape) - 1)
    lo = lane < HEAD_DIM
    return lo, jnp.logical_not(lo)


def _bucket_of_distance(n):
    exact = N_BUCKETS // 2
    if n < exact:
        return n
    j = 0
    while n ** 8 >= (exact ** 8) * (2 ** (3 * (j + 1))):
        j += 1
    return min(exact + j, N_BUCKETS - 1)


@functools.lru_cache(maxsize=None)
def _bucket_tiles(transposed):
    by_dist = np.array([_bucket_of_distance(n) for n in range(BIAS_TILES * LANES + LANES)], np.int32)
    q = np.arange(LANES)[None, :] if transposed else np.arange(LANES)[:, None]
    k = np.arange(LANES)[:, None] if transposed else np.arange(LANES)[None, :]
    tiles = [by_dist[np.maximum(d * LANES + q - k, 0)] for d in range(BIAS_TILES)]
    return np.stack(tiles).astype(np.int32)


def _bias_table_kernel(rb_ref, bkt_ref, out_ref):
    bkt = bkt_ref[0]
    for h in range(BIAS_HEADS):
        acc = jnp.zeros((LANES, LANES), F32)
        for b in range(N_BUCKETS):
            acc = jnp.where(bkt == b, rb_ref[b, h], acc)
        out_ref[0, h] = acc


def bias_table(rel_bias, transposed=False):
    return pl.pallas_call(
        _bias_table_kernel,
        grid=(BIAS_TILES,),
        in_specs=[pl.BlockSpec(memory_space=pltpu.SMEM),
                  pl.BlockSpec((1, LANES, LANES), lambda d: (d, 0, 0))],
        out_specs=pl.BlockSpec((1, BIAS_HEADS, LANES, LANES), lambda d: (d, 0, 0, 0)),
        out_shape=jax.ShapeDtypeStruct((BIAS_TILES, BIAS_HEADS, LANES, LANES), F32),
        compiler_params=_cparams("arbitrary"),
        name="bias_table",
    )(rel_bias, jnp.asarray(_bucket_tiles(transposed)))


def _norm_proj_kernel(x_ref, g_ref, w_ref, *out_refs, splits):
    h = _rms(x_ref[...], g_ref[...]).astype(BF16)
    res = _dot(h, w_ref[...])
    off = 0
    for o_ref, width in zip(out_refs, splits):
        o_ref[...] = res[:, off:off + width].astype(o_ref.dtype)
        off += width


def norm_proj(x2, gain, w, splits, out_dtypes, tm=256):
    n, d = x2.shape
    tm = min(tm, n)
    total = sum(splits)
    assert w.shape == (d, total) and n % tm == 0 and all(s % LANES == 0 for s in splits)
    return pl.pallas_call(
        functools.partial(_norm_proj_kernel, splits=tuple(splits)),
        grid=(n // tm,),
        in_specs=[pl.BlockSpec((tm, d), lambda i: (i, 0)),
                  pl.BlockSpec((1, d), lambda i: (0, 0)),
                  pl.BlockSpec((d, total), lambda i: (0, 0))],
        out_specs=[pl.BlockSpec((tm, s), lambda i: (i, 0)) for s in splits],
        out_shape=[jax.ShapeDtypeStruct((n, s), dt) for s, dt in zip(splits, out_dtypes)],
        compiler_params=_cparams("parallel"),
        name="norm_proj",
    )(x2, gain.reshape(1, d), w)


def _proj_norm_res_kernel(ya_ref, yb_ref, wa_ref, wb_ref, g_ref, x_ref, o_ref):
    z = _dot(ya_ref[...].astype(BF16), wa_ref[...]) + _dot(yb_ref[...].astype(BF16), wb_ref[...])
    o_ref[...] = x_ref[...] + _rms(z, g_ref[...])


def proj_norm_res(ya, yb, w, gain, x2, tm=512):
    n, d = x2.shape
    tm = min(tm, n)
    ka, kb = ya.shape[1], yb.shape[1]
    wa, wb = w[:ka].astype(BF16), w[ka:].astype(BF16)
    return pl.pallas_call(
        _proj_norm_res_kernel,
        grid=(n // tm,),
        in_specs=[pl.BlockSpec((tm, ka), lambda i: (i, 0)),
                  pl.BlockSpec((tm, kb), lambda i: (i, 0)),
                  pl.BlockSpec((ka, d), lambda i: (0, 0)),
                  pl.BlockSpec((kb, d), lambda i: (0, 0)),
                  pl.BlockSpec((1, d), lambda i: (0, 0)),
                  pl.BlockSpec((tm, d), lambda i: (i, 0))],
        out_specs=pl.BlockSpec((tm, d), lambda i: (i, 0)),
        out_shape=jax.ShapeDtypeStruct((n, d), F32),
        compiler_params=_cparams("parallel"),
        name="proj_norm_res",
    )(ya, yb, wa, wb, gain.reshape(1, d), x2)


def _cross_attn_kernel(x_ref, k_ref, v_ref, wq_ref, wo_ref, gpre_ref, gpost_ref, o_ref):
    x = x_ref[0]
    h = _rms(x, gpre_ref[...]).astype(BF16)
    q = _dot(h, wq_ref[...])
    k = k_ref[0]
    v = v_ref[0]
    outs = []
    for hd in range(X_HEADS):
        sl = slice(hd * X_HEAD_DIM, (hd + 1) * X_HEAD_DIM)
        s = _dot_nt(q[:, sl].astype(BF16), k[:, sl]) * (X_HEAD_DIM ** -0.5)
        m = jnp.max(s, axis=-1, keepdims=True)
        p = jnp.exp(s - m)
        p = p / jnp.sum(p, axis=-1, keepdims=True)
        outs.append(_dot(p.astype(BF16), v[:, sl]))
    o = jnp.concatenate(outs, axis=-1).astype(BF16)
    z = _dot(o, wo_ref[...])
    o_ref[0] = x + _rms(z, gpost_ref[...])


def cross_attn(x3, kmem, vmem, wq, wo, gpre, gpost, tq=512):
    b, t, d = x3.shape
    tq = min(tq, t)
    m = kmem.shape[1]
    return pl.pallas_call(
        _cross_attn_kernel,
        grid=(b, t // tq),
        in_specs=[pl.BlockSpec((1, tq, d), lambda i, j: (i, j, 0)),
                  pl.BlockSpec((1, m, X_WIDTH), lambda i, j: (i, 0, 0)),
                  pl.BlockSpec((1, m, X_WIDTH), lambda i, j: (i, 0, 0)),
                  pl.BlockSpec((d, X_WIDTH), lambda i, j: (0, 0)),
                  pl.BlockSpec((X_WIDTH, d), lambda i, j: (0, 0)),
                  pl.BlockSpec((1, d), lambda i, j: (0, 0)),
                  pl.BlockSpec((1, d), lambda i, j: (0, 0))],
        out_specs=pl.BlockSpec((1, tq, d), lambda i, j: (i, j, 0)),
        out_shape=jax.ShapeDtypeStruct((b, t, d), F32),
        compiler_params=_cparams("parallel", "parallel"),
        name="cross_attn",
    )(x3, kmem, vmem, wq.astype(BF16), wo.astype(BF16), gpre.reshape(1, d), gpost.reshape(1, d))


def _mlp_kernel(x_ref, w1_ref, w2_ref, gpre_ref, gpost_ref, o_ref, h_scr, acc_scr):
    j = pl.program_id(1)

    @pl.when(j == 0)
    def _():
        h_scr[...] = _rms(x_ref[...], gpre_ref[...]).astype(BF16)
        acc_scr[...] = jnp.zeros_like(acc_scr)

    a = jnp.maximum(_dot(h_scr[...], w1_ref[...]), 0.0)
    acc_scr[...] += _dot((a * a).astype(BF16), w2_ref[...])

    @pl.when(j == pl.num_programs(1) - 1)
    def _():
        o_ref[...] = x_ref[...] + _rms(acc_scr[...], gpost_ref[...])


def mlp(x2, w1, w2, gpre, gpost, tm=512, tf=1024):
    n, d = x2.shape
    tm = min(tm, n)
    f = w1.shape[1]
    return pl.pallas_call(
        _mlp_kernel,
        grid=(n // tm, f // tf),
        in_specs=[pl.BlockSpec((tm, d), lambda i, j: (i, 0)),
                  pl.BlockSpec((d, tf), lambda i, j: (0, j)),
                  pl.BlockSpec((tf, d), lambda i, j: (j, 0)),
                  pl.BlockSpec((1, d), lambda i, j: (0, 0)),
                  pl.BlockSpec((1, d), lambda i, j: (0, 0))],
        out_specs=pl.BlockSpec((tm, d), lambda i, j: (i, 0)),
        out_shape=jax.ShapeDtypeStruct((n, d), F32),
        scratch_shapes=[pltpu.VMEM((tm, d), BF16), pltpu.VMEM((tm, d), F32)],
        compiler_params=_cparams("parallel", "arbitrary"),
        name="mlp",
    )(x2, w1.astype(BF16), w2.astype(BF16), gpre.reshape(1, d), gpost.reshape(1, d))


def _swa_kernel(sink_ref, q_ref, kc_ref, kp_ref, vc_ref, vp_ref, tab_ref, o_ref):
    hkv = pl.program_id(1)
    n = pl.program_id(2)
    scale = HEAD_DIM ** -0.5
    kc = kc_ref[0].astype(BF16)
    kp = kp_ref[0].astype(BF16)
    vc = vc_ref[0]
    vp = vp_ref[0]
    lo, hi = _lane_half_masks((BAND, LANES))
    qi = lax.broadcasted_iota(jnp.int32, (BAND, BAND), 0)
    kj = lax.broadcasted_iota(jnp.int32, (BAND, BAND), 1)
    cur_ok = kj <= qi
    prev_ok = jnp.logical_and(kj > qi, n > 0)
    group = A_HEADS // A_KV_HEADS
    for pair in range(group // 2):
        qp = q_ref[0, :, pair * LANES:(pair + 1) * LANES]
        out = jnp.zeros((BAND, LANES), F32)
        for half, msk in enumerate((lo, hi)):
            g = pair * 2 + half
            sink = sink_ref[hkv * group + g]
            qh = jnp.where(msk, qp, 0.0).astype(BF16)
            sc = jnp.where(cur_ok, _dot_nt(qh, kc) * scale + tab_ref[0, g], MASKED)
            sp = jnp.where(prev_ok, _dot_nt(qh, kp) * scale + tab_ref[1, g], MASKED)
            m = jnp.maximum(jnp.maximum(jnp.max(sc, axis=-1, keepdims=True),
                                        jnp.max(sp, axis=-1, keepdims=True)), sink)
            pc = jnp.exp(sc - m)
            pp = jnp.exp(sp - m)
            den = jnp.sum(pc, axis=-1, keepdims=True) + jnp.sum(pp, axis=-1, keepdims=True) + jnp.exp(sink - m)
            inv = 1.0 / den
            vch = jnp.where(msk, vc, 0.0).astype(BF16)
            vph = jnp.where(msk, vp, 0.0).astype(BF16)
            out = out + _dot((pc * inv).astype(BF16), vch) + _dot((pp * inv).astype(BF16), vph)
        o_ref[0, :, pair * LANES:(pair + 1) * LANES] = out


def swa_attention(qkv, sinks, tab):
    b, t, _ = qkv.shape
    nb = t // BAND
    qw = (A_HEADS // A_KV_HEADS) * HEAD_DIM
    koff = A_HEADS * HEAD_DIM // LANES
    voff = koff + A_KV_HEADS
    group = A_HEADS // A_KV_HEADS
    return pl.pallas_call(
        _swa_kernel,
        grid=(b, A_KV_HEADS, nb),
        in_specs=[pl.BlockSpec(memory_space=pltpu.SMEM),
                  pl.BlockSpec((1, BAND, qw), lambda i, h, n: (i, n, h)),
                  pl.BlockSpec((1, BAND, LANES), lambda i, h, n: (i, n, koff + h)),
                  pl.BlockSpec((1, BAND, LANES), lambda i, h, n: (i, jnp.maximum(n - 1, 0), koff + h)),
                  pl.BlockSpec((1, BAND, LANES), lambda i, h, n: (i, n, voff + h)),
                  pl.BlockSpec((1, BAND, LANES), lambda i, h, n: (i, jnp.maximum(n - 1, 0), voff + h)),
                  pl.BlockSpec((2, group, LANES, LANES), lambda i, h, n: (0, h, 0, 0))],
        out_specs=pl.BlockSpec((1, BAND, qw), lambda i, h, n: (i, n, h)),
        out_shape=jax.ShapeDtypeStruct((b, t, A_HEADS * HEAD_DIM), F32),
        compiler_params=_cparams("parallel", "parallel", "arbitrary"),
        name="swa_attention",
    )(sinks, qkv, qkv, qkv, qkv, qkv, tab)


def _split3(x):
    p0 = x.astype(BF16)
    r1 = x - p0.astype(F32)
    p1 = r1.astype(BF16)
    p2 = (r1 - p1.astype(F32)).astype(BF16)
    return p0, p1, p2


def _dot_exact_rhs01(x, m01):
    p0, p1, p2 = _split3(x)
    return _dot(p0, m01) + _dot(p1, m01) + _dot(p2, m01)


def _dot_exact_lhs01(m01, x):
    p0, p1, p2 = _split3(x)
    return _dot(m01, p0) + _dot(m01, p1) + _dot(m01, p2)


def _head_sum_matrix(width):
    r = lax.broadcasted_iota(jnp.int32, (width, width), 0) // HEAD_DIM
    c = lax.broadcasted_iota(jnp.int32, (width, width), 1) // HEAD_DIM
    return jnp.where(r == c, 1.0, 0.0).astype(BF16)


def _sigmoid(z):
    return 1.0 / (1.0 + jnp.exp(-z))


def _softplus(z):
    return jnp.maximum(z, 0.0) + jnp.log(1.0 + jnp.exp(-jnp.abs(z)))


def _rwkv_prep_kernel(p_ref, pprev_ref, mu_ref, wa2_ref, g2_ref, w0_ref, a0_ref, kk_ref, ka_ref,
                      r_out, k_out, v_out, lw_out, kk_out, b_out, g_out):
    i = pl.program_id(1)
    p = p_ref[0]
    w = B_WIDTH
    last = pprev_ref[0, SUBLANES - 1:SUBLANES, :]
    last = jnp.where(i > 0, last, 0.0)
    row = lax.broadcasted_iota(jnp.int32, p.shape, 0)
    prev = jnp.where(row == 0, last, pltpu.roll(p, 1, 0))
    xs = p + (prev - p) * mu_ref[...]
    r = xs[:, 0:w]
    k = xs[:, w:2 * w]
    v = xs[:, 2 * w:3 * w]
    wa_lo = xs[:, 3 * w:3 * w + LANES]
    g_lo = xs[:, 3 * w + LANES:3 * w + 2 * LANES]
    lo, _ = _lane_half_masks(wa_lo.shape)
    wa_in = jnp.where(lo, jnp.tanh(wa_lo), wa_lo)
    wa = _dot_x(wa_in, wa2_ref[...])
    wlog = -_softplus(-(w0_ref[...] + wa[:, 0:w])) - 0.5
    a = _sigmoid(a0_ref[...] + wa[:, w:2 * w])
    g = _dot(_sigmoid(g_lo).astype(BF16), g2_ref[...])
    kk = k * kk_ref[...]
    ss = _dot_exact_rhs01(kk * kk, _head_sum_matrix(w))
    kk = kk / jnp.maximum(jnp.sqrt(ss), 1e-12)
    kmod = k * (1.0 + (a - 1.0) * ka_ref[...])
    r_out[0] = r
    k_out[0] = kmod
    v_out[0] = v
    lw_out[0] = -jnp.exp(wlog)
    kk_out[0] = kk
    b_out[0] = kk * a
    g_out[0] = g


def rwkv_prep(pb, mu, w0, w2, a0, a2, g2, k_k, k_a, tc=256):
    b, t, win = pb.shape
    tc = min(tc, t)
    w = B_WIDTH
    wa2 = jnp.zeros((LANES, 2 * w), F32)
    wa2 = wa2.at[:B_DECAY_RANK, :w].set(w2).at[B_DECAY_RANK:, w:].set(a2)
    row = lambda z: z.reshape(1, -1)
    vec = pl.BlockSpec((1, w), lambda i, j: (0, 0))
    outs = [jax.ShapeDtypeStruct((b, t, w), F32)] * 7
    return pl.pallas_call(
        _rwkv_prep_kernel,
        grid=(b, t // tc),
        in_specs=[pl.BlockSpec((1, tc, win), lambda i, j: (i, j, 0)),
                  pl.BlockSpec((1, SUBLANES, win), lambda i, j: (i, jnp.maximum(j * (tc // SUBLANES) - 1, 0), 0)),
                  pl.BlockSpec((1, win), lambda i, j: (0, 0)),
                  pl.BlockSpec((LANES, 2 * w), lambda i, j: (0, 0)),
                  pl.BlockSpec((B_GATE_RANK, w), lambda i, j: (0, 0)),
                  vec, vec, vec, vec],
        out_specs=[pl.BlockSpec((1, tc, w), lambda i, j: (i, j, 0))] * 7,
        out_shape=outs,
        compiler_params=_cparams("parallel", "arbitrary"),
        name="rwkv_prep",
    )(pb, pb, row(mu), wa2, g2.astype(BF16), row(w0), row(a0), row(k_k), row(k_a))


def _rwkv_scan_kernel(r_ref, k_ref, v_ref, lw_ref, kk_ref, b_ref, g_ref, rk_ref, lnw_ref, lnb_ref,
                      o_ref, st_scr, *, ch, nsub):
    step = pl.program_id(1)
    n2 = 2 * ch

    @pl.when(step == 0)
    def _():
        st_scr[...] = jnp.zeros_like(st_scr)

    ti = lax.broadcasted_iota(jnp.int32, (n2, n2), 0)
    si = lax.broadcasted_iota(jnp.int32, (n2, n2), 1)
    same = (ti // ch) == (si // ch)
    incl = jnp.logical_and(same, si <= ti)
    strict = jnp.logical_and(same, si < ti)
    eye = jnp.where(si == ti, 1.0, 0.0)
    levels = []
    s = 1
    while s < ch:
        same2s = (ti // (2 * s)) == (si // (2 * s))
        levels.append(jnp.logical_and(same2s, jnp.logical_and((ti // s) % 2 == 1, (si // s) % 2 == 0)))
        s *= 2
    tr = lax.broadcasted_iota(jnp.int32, (ch, ch), 0)
    tc = lax.broadcasted_iota(jnp.int32, (ch, ch), 1)
    tri01 = jnp.where(tc <= tr, 1.0, 0.0).astype(BF16)
    lo, hi = _lane_half_masks((ch, LANES))
    hsum = _head_sum_matrix(LANES)

    def stack2(z):
        return jnp.concatenate([jnp.where(lo, z, 0.0), jnp.where(hi, z, 0.0)], axis=0)

    def dup2(z):
        return jnp.concatenate([z, z], axis=0)

    states = [st_scr[pr] for pr in range(B_HEADS // 2)]
    for sub in range(nsub):
        rows = slice(sub * ch, (sub + 1) * ch)
        lw_all = lw_ref[0, rows, :]
        cum_all = _dot_exact_lhs01(tri01, lw_all)
        for pr in range(B_HEADS // 2):
            sl = slice(pr * LANES, (pr + 1) * LANES)
            r, k, v = r_ref[0, rows, sl], k_ref[0, rows, sl], v_ref[0, rows, sl]
            kk, bb = kk_ref[0, rows, sl], b_ref[0, rows, sl]
            lw, cum = lw_all[:, sl], cum_all[:, sl]
            tot = cum[ch - 1:ch, :]
            e_excl = jnp.exp(cum - lw)
            e_neg = jnp.exp(-cum)
            e_pos = jnp.exp(cum)
            e_rest = jnp.exp(tot - cum)
            r_t = r * e_pos
            a2 = stack2(kk * e_excl).astype(BF16)
            r2 = stack2(r_t)
            b2 = dup2(bb * e_neg).astype(BF16)
            k2 = dup2(k * e_neg).astype(BF16)
            v2 = stack2(v).astype(BF16)
            bh2 = stack2(bb * e_rest).astype(BF16)
            kh2 = stack2(k * e_rest).astype(BF16)
            r2b = r2.astype(BF16)
            ab = jnp.where(strict, _dot_nt(a2, b2), 0.0)
            ak = jnp.where(strict, _dot_nt(a2, k2), 0.0).astype(BF16)
            rk = jnp.where(incl, _dot_nt(r2b, k2), 0.0).astype(BF16)
            rb = jnp.where(incl, _dot_nt(r2b, b2), 0.0).astype(BF16)
            tinv = eye
            for lvl in levels:
                tb = tinv.astype(BF16)
                tinv = tinv - _dot(_dot(tb, jnp.where(lvl, ab, 0.0).astype(BF16)).astype(BF16), tb)
            tb = tinv.astype(BF16)
            w2 = _dot(tb, a2).astype(BF16)
            u0 = _dot(tb, _dot(ak, v2).astype(BF16)).astype(BF16)
            rw = (r2 - _dot(rb, w2)).astype(BF16)
            y0 = _dot(rk, v2) - _dot(rb, u0)
            gm = _dot_tn(bh2, w2).astype(BF16)
            s0t = _dot_tn(v2, kh2) - _dot_tn(u0, bh2)
            st = states[pr]
            stb = st.astype(BF16)
            y2 = _dot_nt(rw, stb) + y0
            y = y2[0:ch, :] + y2[ch:n2, :]
            states[pr] = st * jnp.exp(tot) - _dot_nt(stb, gm) + s0t
            mean = _dot_exact_rhs01(y, hsum) * (1.0 / HEAD_DIM)
            yc = y - mean
            var = _dot_exact_rhs01(yc * yc, hsum) * (1.0 / HEAD_DIM)
            yn = yc * lax.rsqrt(var + B_LN_EPS) * lnw_ref[:, sl] + lnb_ref[:, sl]
            bonus = _dot_exact_rhs01(r * k * rk_ref[:, sl], hsum) * v
            o_ref[0, rows, sl] = (yn + bonus) * g_ref[0, rows, sl]
    for pr in range(B_HEADS // 2):
        st_scr[pr] = states[pr]


def rwkv_scan(r, k, v, lw, kk, bb, g, r_k, ln_w, ln_b, ch=RWKV_CHUNK, nsub=2):
    b, t, w = r.shape
    ch = min(ch, t)
    nsub = min(nsub, t // ch)
    tstep = ch * nsub
    seq = pl.BlockSpec((1, tstep, w), lambda i, c: (i, c, 0))
    vec = pl.BlockSpec((1, w), lambda i, c: (0, 0))
    row = lambda z: z.reshape(1, w)
    return pl.pallas_call(
        functools.partial(_rwkv_scan_kernel, ch=ch, nsub=nsub),
        grid=(b, t // tstep),
        in_specs=[seq] * 7 + [vec] * 3,
        out_specs=seq,
        out_shape=jax.ShapeDtypeStruct((b, t, w), F32),
        scratch_shapes=[pltpu.VMEM((B_HEADS // 2, LANES, LANES), F32)],
        compiler_params=_cparams("parallel", "arbitrary"),
        name="rwkv_scan",
    )(r, k, v, lw, kk, bb, g, row(r_k), row(ln_w), row(ln_b))


def _fill_vt(v_ref_slice, vt_scr, t):
    sub = lax.broadcasted_iota(jnp.int32, (LANES, LANES), 0)

    def body(j, carry):
        start = pl.multiple_of(j * LANES, LANES)
        vt = v_ref_slice(start).T
        vt_scr[:, pl.ds(start, LANES)] = jnp.where(sub < HEAD_DIM, vt, 1.0).astype(BF16)
        return carry

    lax.fori_loop(0, t // LANES, body, 0)


def _stack_heads(src_ref, dst_scr, n_heads, scale):
    lo, hi = _lane_half_masks((QCHUNK, LANES))
    for h in range(n_heads):
        tile = src_ref[0, :, (h // 2) * LANES:(h // 2 + 1) * LANES]
        if scale != 1.0:
            tile = tile * scale
        dst_scr[h * QCHUNK:(h + 1) * QCHUNK, :] = jnp.where((lo, hi)[h % 2], tile, 0.0).astype(BF16)


def _online_update(h, s, vt, m_scr, acc_scr):
    m_old = m_scr[h]
    m_new = jnp.maximum(m_old, jnp.max(s, axis=0, keepdims=True))
    p = jnp.exp(s - m_new)
    acc_scr[h] = jnp.exp(m_old - m_new) * acc_scr[h] + _dot(vt, p.astype(BF16))
    m_scr[h] = m_new


def _write_heads(o_ref, acc_scr, n_heads):
    for p in range(n_heads // 2):
        parts = []
        for h in (2 * p, 2 * p + 1):
            a = acc_scr[h]
            parts.append(a[0:HEAD_DIM, :] * (1.0 / a[HEAD_DIM:HEAD_DIM + 1, :]))
        o_ref[0, :, p * LANES:(p + 1) * LANES] = jnp.concatenate(parts, axis=0).T


def _moba_kernel(q_ref, k_ref, v_ref, tabt_ref, o_ref, k_scr, vt_scr, km_scr, qst_scr, pick_scr, m_scr, acc_scr):
    c = pl.program_id(2)
    t = k_ref.shape[1]
    nblk = t // MOBA_BLOCK
    nbp = km_scr.shape[0]
    group = C_HEADS // C_KV_HEADS

    @pl.when(c == 0)
    def _():
        kf = k_ref[0]
        k_scr[...] = kf.astype(BF16)
        km_scr[...] = jnp.zeros_like(km_scr)
        km_scr[0:nblk, :] = jnp.mean(kf.reshape(nblk, MOBA_BLOCK, LANES), axis=1)
        _fill_vt(lambda start: v_ref[0, pl.ds(start, LANES), :], vt_scr, t)

    own = (c * QCHUNK) // MOBA_BLOCK
    part = c % (MOBA_BLOCK // QCHUNK)
    _stack_heads(q_ref, qst_scr, group, HEAD_DIM ** -0.5)

    qsum = q_ref[0, :, 0:LANES]
    for p in range(1, group // 2):
        qsum = qsum + q_ref[0, :, p * LANES:(p + 1) * LANES]
    gate = _dot_nt_x(km_scr[...], qsum)
    blk = lax.broadcasted_iota(jnp.int32, (nbp, QCHUNK), 0)
    gate = jnp.where(blk < own, gate, -jnp.inf)
    rank = jnp.zeros((nbp, QCHUNK), jnp.int32)
    for mblk in range(nblk):
        gm = gate[mblk:mblk + 1, :]
        ahead = jnp.logical_or(gm > gate, jnp.logical_and(gm == gate, mblk < blk))
        rank = rank + jnp.where(ahead, 1, 0)
    chosen = jnp.logical_and(rank < MOBA_TOPK, blk < own)
    pick_scr[...] = jnp.where(chosen, 0.0, MASKED)

    m_scr[...] = jnp.full(m_scr.shape, M_INIT, F32)
    acc_scr[...] = jnp.zeros_like(acc_scr)
    kpos = lax.broadcasted_iota(jnp.int32, (MOBA_BLOCK, QCHUNK), 0)
    qpos = part * QCHUNK + lax.broadcasted_iota(jnp.int32, (MOBA_BLOCK, QCHUNK), 1)

    def attend(blk_idx, okb):
        start = pl.multiple_of(blk_idx * MOBA_BLOCK, MOBA_BLOCK)
        s_all = _dot_nt(k_scr[pl.ds(start, MOBA_BLOCK), :], qst_scr[...])
        vt = vt_scr[:, pl.ds(start, MOBA_BLOCK)]
        d0 = c - 2 * blk_idx
        da = jnp.clip(d0, 0, BIAS_TILES - 1)
        db = jnp.clip(d0 - 1, 0, BIAS_TILES - 1)
        for g in range(group):
            bias = jnp.concatenate([tabt_ref[da, g], tabt_ref[db, g]], axis=0)
            _online_update(g, s_all[:, g * QCHUNK:(g + 1) * QCHUNK] + (bias + okb), vt, m_scr, acc_scr)

    attend(own, jnp.where(kpos <= qpos, 0.0, MASKED))

    def body(nb, carry):
        attend(nb, pick_scr[pl.ds(nb, 1), :])
        return carry

    lax.fori_loop(0, own, body, 0)
    _write_heads(o_ref, acc_scr, group)


def moba_attention(qc, kvc, tabt):
    b, t, _ = qc.shape
    assert t % MOBA_BLOCK == 0
    nq = t // QCHUNK
    group = C_HEADS // C_KV_HEADS
    qw = group * HEAD_DIM
    nbp = -(-(t // MOBA_BLOCK) // SUBLANES) * SUBLANES
    return pl.pallas_call(
        _moba_kernel,
        grid=(b, C_KV_HEADS, nq),
        in_specs=[pl.BlockSpec((1, QCHUNK, qw), lambda i, h, c: (i, c, h)),
                  pl.BlockSpec((1, t, LANES), lambda i, h, c: (i, 0, h)),
                  pl.BlockSpec((1, t, LANES), lambda i, h, c: (i, 0, C_KV_HEADS + h)),
                  pl.BlockSpec((BIAS_TILES, group, LANES, LANES), lambda i, h, c: (0, h, 0, 0))],
        out_specs=pl.BlockSpec((1, QCHUNK, qw), lambda i, h, c: (i, c, h)),
        out_shape=jax.ShapeDtypeStruct((b, t, C_HEADS * HEAD_DIM), F32),
        scratch_shapes=[pltpu.VMEM((t, LANES), BF16), pltpu.VMEM((LANES, t), BF16),
                        pltpu.VMEM((nbp, LANES), F32),
                        pltpu.VMEM((group * QCHUNK, LANES), BF16),
                        pltpu.VMEM((nbp, QCHUNK), F32),
                        pltpu.VMEM((group, 1, QCHUNK), F32),
                        pltpu.VMEM((group, LANES, QCHUNK), F32)],
        compiler_params=_cparams("parallel", "parallel", "arbitrary"),
        name="moba_attention",
    )(qc, kvc, kvc, tabt)


def _dsa_kernel(q_ref, kv_ref, qi_ref, ki_ref, gk_ref, tabt_ref, o_ref,
                k_scr, vt_scr, ki_scr, key_scr, qst_scr, qist_scr, m_scr, acc_scr, *, topk):
    c = pl.program_id(1)
    t = k_scr.shape[0]
    ntile = c + 1
    sub = lax.broadcasted_iota(jnp.int32, (LANES, QCHUNK), 0)
    lane = lax.broadcasted_iota(jnp.int32, (LANES, QCHUNK), 1)

    @pl.when(c == 0)
    def _():
        k_scr[...] = kv_ref[0, :, 0:LANES].astype(BF16)
        ki_scr[...] = _rms(ki_ref[0], gk_ref[...]).astype(BF16)
        _fill_vt(lambda start: kv_ref[0, pl.ds(start, LANES), LANES:2 * LANES], vt_scr, t)

    _stack_heads(q_ref, qst_scr, D_HEADS, HEAD_DIM ** -0.5)
    _stack_heads(qi_ref, qist_scr, IDX_HEADS, 1.0)
    w_t = (qi_ref[0, :, IDX_HEADS * IDX_DIM + LANES:IDX_HEADS * IDX_DIM + 2 * LANES]
           * ((IDX_HEADS * IDX_DIM) ** -0.5)).T
    qpos = c * QCHUNK + lane

    def score_body(j, carry):
        start = pl.multiple_of(j * LANES, LANES)
        rel = jnp.maximum(_dot_nt(ki_scr[pl.ds(start, LANES), :], qist_scr[...]), 0.0)
        sc = jnp.zeros((LANES, QCHUNK), F32)
        for h in range(IDX_HEADS):
            sc = sc + rel[:, h * QCHUNK:(h + 1) * QCHUNK] * w_t[h:h + 1, :]
        sc = jnp.where(sc == 0.0, 0.0, sc)
        bits = pltpu.bitcast(sc, jnp.int32)
        skey = bits ^ (lax.shift_right_arithmetic(bits, 31) & 0x7FFFFFFF)
        key_scr[pl.ds(start, LANES), :] = jnp.where((j * LANES + sub) <= qpos, skey, INT_MIN)
        return carry

    lax.fori_loop(0, ntile, score_body, 0)

    def count(pred):
        def body(j, acc):
            start = pl.multiple_of(j * LANES, LANES)
            return acc + jnp.where(pred(key_scr[pl.ds(start, LANES), :], j), 1, 0)
        acc = lax.fori_loop(0, ntile, body, jnp.zeros((LANES, QCHUNK), jnp.int32))
        return jnp.sum(acc, axis=0, keepdims=True)

    cnt0 = count(lambda kt, j: kt >= 0)
    thr = jnp.where(cnt0 >= topk, 0, INT_MIN).astype(jnp.int32)

    def bit_body(i, thr):
        cand = thr + lax.shift_left(jnp.int32(1), 30 - i)
        cnt = count(lambda kt, j: kt >= cand)
        return jnp.where(cnt >= topk, cand, thr)

    thr = lax.fori_loop(0, 31, bit_body, thr)

    cnt_ge = count(lambda kt, j: kt >= thr)
    tied = jnp.logical_and(cnt_ge > topk, thr > INT_MIN)
    any_tied = jnp.max(jnp.where(tied, 1.0, 0.0)) > 0.5

    def find_cut():
        need = topk - count(lambda kt, j: kt > thr)

        def cut_body(i, cut):
            cand = cut + lax.shift_left(jnp.int32(1), 14 - i)
            cnt = count(lambda kt, j: jnp.logical_and(kt == thr, (j * LANES + sub) < cand))
            return jnp.where(cnt < need, cand, cut)

        return lax.fori_loop(0, 15, cut_body, jnp.zeros((1, QCHUNK), jnp.int32))

    cut = lax.cond(any_tied, find_cut, lambda: jnp.full((1, QCHUNK), 2 ** 30, jnp.int32))

    m_scr[...] = jnp.full(m_scr.shape, M_INIT, F32)
    acc_scr[...] = jnp.zeros_like(acc_scr)

    def att_body(j, carry):
        start = pl.multiple_of(j * LANES, LANES)
        s_all = _dot_nt(k_scr[pl.ds(start, LANES), :], qst_scr[...])
        keys = key_scr[pl.ds(start, LANES), :]
        ok = jnp.logical_or(keys > thr, jnp.logical_and(keys == thr, (j * LANES + sub) <= cut))
        ok = jnp.logical_and(ok, keys != INT_MIN)
        okb = jnp.where(ok, 0.0, MASKED)
        d = jnp.minimum(c - j, BIAS_TILES - 1)
        vt = vt_scr[:, pl.ds(start, LANES)]
        for h in range(D_HEADS):
            _online_update(h, s_all[:, h * QCHUNK:(h + 1) * QCHUNK] + (tabt_ref[d, h] + okb), vt, m_scr, acc_scr)
        return carry

    lax.fori_loop(0, ntile, att_body, 0)
    _write_heads(o_ref, acc_scr, D_HEADS)


def dsa_attention(qd, kvd, idx, kidx_gain, tabt):
    b, t, _ = qd.shape
    topk = min(DSA_TOPK, t // 4)
    nq = t // QCHUNK
    gk = jnp.concatenate([kidx_gain, kidx_gain]).reshape(1, LANES)
    wq = D_HEADS * HEAD_DIM
    return pl.pallas_call(
        functools.partial(_dsa_kernel, topk=topk),
        grid=(b, nq),
        in_specs=[pl.BlockSpec((1, QCHUNK, wq), lambda i, c: (i, c, 0)),
                  pl.BlockSpec((1, t, 2 * LANES), lambda i, c: (i, 0, 0)),
                  pl.BlockSpec((1, QCHUNK, 4 * LANES), lambda i, c: (i, c, 0)),
                  pl.BlockSpec((1, t, LANES), lambda i, c: (i, 0, 2)),
                  pl.BlockSpec((1, LANES), lambda i, c: (0, 0)),
                  pl.BlockSpec((BIAS_TILES, D_HEADS, LANES, LANES), lambda i, c: (0, 0, 0, 0))],
        out_specs=pl.BlockSpec((1, QCHUNK, wq), lambda i, c: (i, c, 0)),
        out_shape=jax.ShapeDtypeStruct((b, t, wq), F32),
        scratch_shapes=[pltpu.VMEM((t, LANES), BF16), pltpu.VMEM((LANES, t), BF16), pltpu.VMEM((t, LANES), BF16),
                        pltpu.VMEM((t, QCHUNK), jnp.int32),
                        pltpu.VMEM((D_HEADS * QCHUNK, LANES), BF16), pltpu.VMEM((IDX_HEADS * QCHUNK, LANES), BF16),
                        pltpu.VMEM((D_HEADS, 1, QCHUNK), F32),
                        pltpu.VMEM((D_HEADS, LANES, QCHUNK), F32)],
        compiler_params=_cparams("parallel", "arbitrary"),
        name="dsa_attention",
    )(qd, kvd, idx, idx, gk, tabt)


def _dup_heads(wcols, n_heads):
    d = wcols.shape[0]
    wh = wcols.reshape(d, n_heads, 1, HEAD_DIM)
    return jnp.broadcast_to(wh, (d, n_heads, 2, HEAD_DIM)).reshape(d, n_heads * 2 * HEAD_DIM)


def _even_weight(w_in):
    aq, akv = A_HEADS * HEAD_DIM, A_KV_HEADS * HEAD_DIM
    q = w_in[:, :aq]
    k = _dup_heads(w_in[:, aq:aq + akv], A_KV_HEADS)
    v = _dup_heads(w_in[:, aq + akv:aq + 2 * akv], A_KV_HEADS)
    rest = w_in[:, aq + 2 * akv:]
    return jnp.concatenate([q, k, v, rest], axis=1).astype(BF16)


def _odd_weight(w_in):
    d = w_in.shape[0]
    cq, ckv, dq = C_HEADS * HEAD_DIM, C_KV_HEADS * HEAD_DIM, D_HEADS * HEAD_DIM
    cuts = np.cumsum([cq, ckv, ckv, dq, HEAD_DIM, HEAD_DIM, IDX_HEADS * IDX_DIM, IDX_DIM]).tolist()
    qc, kc, vc, qd, kd, vd, qi, ki, wi = jnp.split(w_in, cuts, axis=1)
    pad = jnp.zeros((d, LANES - IDX_HEADS), w_in.dtype)
    cols = [qc, _dup_heads(kc, C_KV_HEADS), _dup_heads(vc, C_KV_HEADS),
            qd, _dup_heads(kd, 1), _dup_heads(vd, 1),
            qi, _dup_heads(ki, 1), wi, pad]
    return jnp.concatenate(cols, axis=1).astype(BF16)


def kernel(x, mem, rel_bias, norm_gains, mix_w_out, mem_norm, x_wq, x_wkv, x_wo, ff_w1, ff_w2, ev_w_in, a_sinks, b_mu, b_w0, b_w2, b_a0, b_a2, b_g2, b_k_k, b_k_a, b_r_k, b_ln_w, b_ln_b, od_w_in, d_kidx_norm):
    bsz, t, d = x.shape
    n = bsz * t
    depth = norm_gains.shape[0]
    mlen = mem.shape[1]
    tab = bias_table(rel_bias)
    tabt = bias_table(rel_bias, transposed=True)
    x2 = x.reshape(n, d)
    for layer in range(depth):
        i = layer // 2
        g = norm_gains[layer]
        if layer % 2 == 0:
            b_in = 3 * B_WIDTH + B_DECAY_RANK + B_A_RANK + B_GATE_RANK
            qkv, pb = norm_proj(x2, g[0], _even_weight(ev_w_in[i]), (1024, b_in), (F32, F32))
            ya = swa_attention(qkv.reshape(bsz, t, 1024), a_sinks[i], tab)
            prep = rwkv_prep(pb.reshape(bsz, t, b_in), b_mu[i], b_w0[i], b_w2[i], b_a0[i], b_a2[i], b_g2[i],
                             b_k_k[i], b_k_a[i])
            yb = rwkv_scan(*prep, b_r_k[i].reshape(-1), b_ln_w[i], b_ln_b[i])
        else:
            qc, kvc, qd, kvd, idx = norm_proj(x2, g[0], _odd_weight(od_w_in[i]),
                                              (512, 512, 512, 256, 512), (F32,) * 5)
            ya = moba_attention(qc.reshape(bsz, t, 512), kvc.reshape(bsz, t, 512), tabt)
            yb = dsa_attention(qd.reshape(bsz, t, 512), kvd.reshape(bsz, t, 256), idx.reshape(bsz, t, 512),
                               d_kidx_norm[i], tabt)
        x2 = proj_norm_res(ya.reshape(n, -1), yb.reshape(n, -1), mix_w_out[layer], g[1], x2)
        kmem, vmem = norm_proj(mem.reshape(bsz * mlen, d), mem_norm[layer], x_wkv[layer].astype(BF16),
                               (X_WIDTH, X_WIDTH), (BF16, BF16))
        x3 = cross_attn(x2.reshape(bsz, t, d), kmem.reshape(bsz, mlen, X_WIDTH), vmem.reshape(bsz, mlen, X_WIDTH),
                        x_wq[layer], x_wo[layer], g[2], g[3])
        x2 = mlp(x3.reshape(n, d), ff_w1[layer], ff_w2[layer], g[4], g[5])
    return x2.reshape(bsz, t, d)
```

```python
import functools

import numpy as np
import jax
import jax.numpy as jnp
from jax import lax
from jax.experimental import pallas as pl
from jax.experimental.pallas import tpu as pltpu

F32 = jnp.float32
BF16 = jnp.bfloat16
HIGHEST = lax.Precision.HIGHEST

LANES = 128
SUBLANES = 8
HEAD_DIM = 64
EPS = 1e-6
A_HEADS, A_KV_HEADS, BAND = 8, 2, 128
B_HEADS, B_WIDTH = 8, 512
B_DECAY_RANK, B_A_RANK, B_GATE_RANK = 64, 64, 128
B_LN_EPS = 64e-5
RWKV_CHUNK = 64
C_HEADS, C_KV_HEADS, MOBA_BLOCK, MOBA_TOPK, QCHUNK = 8, 2, 256, 3, 128
D_HEADS, IDX_HEADS, IDX_DIM, DSA_TOPK = 8, 4, 64, 256
N_BUCKETS, BIAS_HEADS = 32, 8
BIAS_TILES = 9
X_HEADS, X_HEAD_DIM, X_WIDTH = 4, 128, 512

LOG2E = 1.4426950408889634
KEY_GROUP = 4
MASKED = -2e30
M_INIT = -1e30
INT_MIN = -(2 ** 31)
VMEM_LIMIT = 56 * 1024 * 1024


def _cparams(*sem):
    return pltpu.CompilerParams(dimension_semantics=tuple(sem), vmem_limit_bytes=VMEM_LIMIT)


def _dot(a, b):
    return jnp.dot(a, b, preferred_element_type=F32)


def _dot_nt(a, b):
    return lax.dot_general(a, b, (((1,), (1,)), ((), ())), preferred_element_type=F32)


def _dot_tn(a, b):
    return lax.dot_general(a, b, (((0,), (0,)), ((), ())), preferred_element_type=F32)


def _dot_x(a, b):
    return jnp.dot(a, b, preferred_element_type=F32, precision=HIGHEST)


def _dot_nt_x(a, b):
    return lax.dot_general(a, b, (((1,), (1,)), ((), ())), preferred_element_type=F32, precision=HIGHEST)


def _rms(xf, g):
    ms = jnp.mean(xf * xf, axis=-1, keepdims=True)
    return xf * lax.rsqrt(ms + EPS) * g


def _lane_half_masks(shape):
    lane = lax.broadcasted_iota(jnp.int32, shape, len(shape) - 1)
    lo = lane < HEAD_DIM
    return lo, jnp.logical_not(lo)


def _bucket_of_distance(n):
    exact = N_BUCKETS // 2
    if n < exact:
        return n
    j = 0
    while n ** 8 >= (exact ** 8) * (2 ** (3 * (j + 1))):
        j += 1
    return min(exact + j, N_BUCKETS - 1)


@functools.lru_cache(maxsize=None)
def _bucket_tiles(transposed):
    by_dist = np.array([_bucket_of_distance(n) for n in range(BIAS_TILES * LANES + LANES)], np.int32)
    q = np.arange(LANES)[None, :] if transposed else np.arange(LANES)[:, None]
    k = np.arange(LANES)[:, None] if transposed else np.arange(LANES)[None, :]
    tiles = [by_dist[np.maximum(d * LANES + q - k, 0)] for d in range(BIAS_TILES)]
    return np.stack(tiles).astype(np.int32)


def _bias_table_kernel(rb_ref, bkt_ref, out_ref, *, base2_shifted):
    bkt = bkt_ref[0]
    for h in range(BIAS_HEADS):
        acc = jnp.zeros((LANES, LANES), F32)
        for b in range(N_BUCKETS):
            acc = jnp.where(bkt == b, rb_ref[b, h], acc)
        if base2_shifted:
            acc = (acc - rb_ref[N_BUCKETS - 1, h]) * LOG2E
        out_ref[0, h] = acc


def bias_table(rel_bias, transposed=False):
    return pl.pallas_call(
        functools.partial(_bias_table_kernel, base2_shifted=transposed),
        grid=(BIAS_TILES,),
        in_specs=[pl.BlockSpec(memory_space=pltpu.SMEM),
                  pl.BlockSpec((1, LANES, LANES), lambda d: (d, 0, 0))],
        out_specs=pl.BlockSpec((1, BIAS_HEADS, LANES, LANES), lambda d: (d, 0, 0, 0)),
        out_shape=jax.ShapeDtypeStruct((BIAS_TILES, BIAS_HEADS, LANES, LANES), F32),
        compiler_params=_cparams("arbitrary"),
        name="bias_table",
    )(rel_bias, jnp.asarray(_bucket_tiles(transposed)))


def _norm_proj_kernel(x_ref, g_ref, w_ref, *out_refs, splits):
    h = _rms(x_ref[...], g_ref[...]).astype(BF16)
    res = _dot(h, w_ref[...])
    off = 0
    for o_ref, width in zip(out_refs, splits):
        o_ref[...] = res[:, off:off + width].astype(o_ref.dtype)
        off += width


def norm_proj(x2, gain, w, splits, out_dtypes, tm=256):
    n, d = x2.shape
    tm = min(tm, n)
    total = sum(splits)
    assert w.shape == (d, total) and n % tm == 0 and all(s % LANES == 0 for s in splits)
    return pl.pallas_call(
        functools.partial(_norm_proj_kernel, splits=tuple(splits)),
        grid=(n // tm,),
        in_specs=[pl.BlockSpec((tm, d), lambda i: (i, 0)),
                  pl.BlockSpec((1, d), lambda i: (0, 0)),
                  pl.BlockSpec((d, total), lambda i: (0, 0))],
        out_specs=[pl.BlockSpec((tm, s), lambda i: (i, 0)) for s in splits],
        out_shape=[jax.ShapeDtypeStruct((n, s), dt) for s, dt in zip(splits, out_dtypes)],
        compiler_params=_cparams("parallel"),
        name="norm_proj",
    )(x2, gain.reshape(1, d), w)


def _proj_norm_res_kernel(ya_ref, yb_ref, wa_ref, wb_ref, g_ref, x_ref, o_ref):
    z = _dot(ya_ref[...].astype(BF16), wa_ref[...]) + _dot(yb_ref[...].astype(BF16), wb_ref[...])
    o_ref[...] = x_ref[...] + _rms(z, g_ref[...])


def proj_norm_res(ya, yb, w, gain, x2, tm=512):
    n, d = x2.shape
    tm = min(tm, n)
    ka, kb = ya.shape[1], yb.shape[1]
    wa, wb = w[:ka].astype(BF16), w[ka:].astype(BF16)
    return pl.pallas_call(
        _proj_norm_res_kernel,
        grid=(n // tm,),
        in_specs=[pl.BlockSpec((tm, ka), lambda i: (i, 0)),
                  pl.BlockSpec((tm, kb), lambda i: (i, 0)),
                  pl.BlockSpec((ka, d), lambda i: (0, 0)),
                  pl.BlockSpec((kb, d), lambda i: (0, 0)),
                  pl.BlockSpec((1, d), lambda i: (0, 0)),
                  pl.BlockSpec((tm, d), lambda i: (i, 0))],
        out_specs=pl.BlockSpec((tm, d), lambda i: (i, 0)),
        out_shape=jax.ShapeDtypeStruct((n, d), F32),
        compiler_params=_cparams("parallel"),
        name="proj_norm_res",
    )(ya, yb, wa, wb, gain.reshape(1, d), x2)


def _cross_attn_kernel(x_ref, k_ref, v_ref, wq_ref, wo_ref, gpre_ref, gpost_ref, o_ref):
    x = x_ref[0]
    h = _rms(x, gpre_ref[...]).astype(BF16)
    q = _dot(h, wq_ref[...])
    k = k_ref[0]
    v = v_ref[0]
    outs = []
    for hd in range(X_HEADS):
        sl = slice(hd * X_HEAD_DIM, (hd + 1) * X_HEAD_DIM)
        s = _dot_nt(q[:, sl].astype(BF16), k[:, sl]) * (X_HEAD_DIM ** -0.5)
        m = jnp.max(s, axis=-1, keepdims=True)
        p = jnp.exp(s - m)
        p = p / jnp.sum(p, axis=-1, keepdims=True)
        outs.append(_dot(p.astype(BF16), v[:, sl]))
    o = jnp.concatenate(outs, axis=-1).astype(BF16)
    z = _dot(o, wo_ref[...])
    o_ref[0] = x + _rms(z, gpost_ref[...])


def cross_attn(x3, kmem, vmem, wq, wo, gpre, gpost, tq=512):
    b, t, d = x3.shape
    tq = min(tq, t)
    m = kmem.shape[1]
    return pl.pallas_call(
        _cross_attn_kernel,
        grid=(b, t // tq),
        in_specs=[pl.BlockSpec((1, tq, d), lambda i, j: (i, j, 0)),
                  pl.BlockSpec((1, m, X_WIDTH), lambda i, j: (i, 0, 0)),
                  pl.BlockSpec((1, m, X_WIDTH), lambda i, j: (i, 0, 0)),
                  pl.BlockSpec((d, X_WIDTH), lambda i, j: (0, 0)),
                  pl.BlockSpec((X_WIDTH, d), lambda i, j: (0, 0)),
                  pl.BlockSpec((1, d), lambda i, j: (0, 0)),
                  pl.BlockSpec((1, d), lambda i, j: (0, 0))],
        out_specs=pl.BlockSpec((1, tq, d), lambda i, j: (i, j, 0)),
        out_shape=jax.ShapeDtypeStruct((b, t, d), F32),
        compiler_params=_cparams("parallel", "parallel"),
        name="cross_attn",
    )(x3, kmem, vmem, wq.astype(BF16), wo.astype(BF16), gpre.reshape(1, d), gpost.reshape(1, d))


def _mlp_kernel(x_ref, w1_ref, w2_ref, gpre_ref, gpost_ref, o_ref, h_scr, acc_scr):
    j = pl.program_id(1)

    @pl.when(j == 0)
    def _():
        h_scr[...] = _rms(x_ref[...], gpre_ref[...]).astype(BF16)
        acc_scr[...] = jnp.zeros_like(acc_scr)

    a = jnp.maximum(_dot(h_scr[...], w1_ref[...]), 0.0)
    acc_scr[...] += _dot((a * a).astype(BF16), w2_ref[...])

    @pl.when(j == pl.num_programs(1) - 1)
    def _():
        o_ref[...] = x_ref[...] + _rms(acc_scr[...], gpost_ref[...])


def mlp(x2, w1, w2, gpre, gpost, tm=512, tf=1024):
    n, d = x2.shape
    tm = min(tm, n)
    f = w1.shape[1]
    return pl.pallas_call(
        _mlp_kernel,
        grid=(n // tm, f // tf),
        in_specs=[pl.BlockSpec((tm, d), lambda i, j: (i, 0)),
                  pl.BlockSpec((d, tf), lambda i, j: (0, j)),
                  pl.BlockSpec((tf, d), lambda i, j: (j, 0)),
                  pl.BlockSpec((1, d), lambda i, j: (0, 0)),
                  pl.BlockSpec((1, d), lambda i, j: (0, 0))],
        out_specs=pl.BlockSpec((tm, d), lambda i, j: (i, 0)),
        out_shape=jax.ShapeDtypeStruct((n, d), F32),
        scratch_shapes=[pltpu.VMEM((tm, d), BF16), pltpu.VMEM((tm, d), F32)],
        compiler_params=_cparams("parallel", "arbitrary"),
        name="mlp",
    )(x2, w1.astype(BF16), w2.astype(BF16), gpre.reshape(1, d), gpost.reshape(1, d))


def _swa_kernel(sink_ref, q_ref, kc_ref, kp_ref, vc_ref, vp_ref, tab_ref, o_ref):
    hkv = pl.program_id(1)
    n = pl.program_id(2)
    scale = HEAD_DIM ** -0.5
    kc = kc_ref[0].astype(BF16)
    kp = kp_ref[0].astype(BF16)
    vc = vc_ref[0]
    vp = vp_ref[0]
    lo, hi = _lane_half_masks((BAND, LANES))
    qi = lax.broadcasted_iota(jnp.int32, (BAND, BAND), 0)
    kj = lax.broadcasted_iota(jnp.int32, (BAND, BAND), 1)
    cur_ok = kj <= qi
    prev_ok = jnp.logical_and(kj > qi, n > 0)
    group = A_HEADS // A_KV_HEADS
    for pair in range(group // 2):
        qp = q_ref[0, :, pair * LANES:(pair + 1) * LANES]
        out = jnp.zeros((BAND, LANES), F32)
        for half, msk in enumerate((lo, hi)):
            g = pair * 2 + half
            sink = sink_ref[hkv * group + g]
            qh = jnp.where(msk, qp, 0.0).astype(BF16)
            sc = jnp.where(cur_ok, _dot_nt(qh, kc) * scale + tab_ref[0, g], MASKED)
            sp = jnp.where(prev_ok, _dot_nt(qh, kp) * scale + tab_ref[1, g], MASKED)
            m = jnp.maximum(jnp.maximum(jnp.max(sc, axis=-1, keepdims=True),
                                        jnp.max(sp, axis=-1, keepdims=True)), sink)
            pc = jnp.exp(sc - m)
            pp = jnp.exp(sp - m)
            den = jnp.sum(pc, axis=-1, keepdims=True) + jnp.sum(pp, axis=-1, keepdims=True) + jnp.exp(sink - m)
            inv = 1.0 / den
            vch = jnp.where(msk, vc, 0.0).astype(BF16)
            vph = jnp.where(msk, vp, 0.0).astype(BF16)
            out = out + _dot((pc * inv).astype(BF16), vch) + _dot((pp * inv).astype(BF16), vph)
        o_ref[0, :, pair * LANES:(pair + 1) * LANES] = out


def swa_attention(qkv, sinks, tab):
    b, t, _ = qkv.shape
    nb = t // BAND
    qw = (A_HEADS // A_KV_HEADS) * HEAD_DIM
    koff = A_HEADS * HEAD_DIM // LANES
    voff = koff + A_KV_HEADS
    group = A_HEADS // A_KV_HEADS
    return pl.pallas_call(
        _swa_kernel,
        grid=(b, A_KV_HEADS, nb),
        in_specs=[pl.BlockSpec(memory_space=pltpu.SMEM),
                  pl.BlockSpec((1, BAND, qw), lambda i, h, n: (i, n, h)),
                  pl.BlockSpec((1, BAND, LANES), lambda i, h, n: (i, n, koff + h)),
                  pl.BlockSpec((1, BAND, LANES), lambda i, h, n: (i, jnp.maximum(n - 1, 0), koff + h)),
                  pl.BlockSpec((1, BAND, LANES), lambda i, h, n: (i, n, voff + h)),
                  pl.BlockSpec((1, BAND, LANES), lambda i, h, n: (i, jnp.maximum(n - 1, 0), voff + h)),
                  pl.BlockSpec((2, group, LANES, LANES), lambda i, h, n: (0, h, 0, 0))],
        out_specs=pl.BlockSpec((1, BAND, qw), lambda i, h, n: (i, n, h)),
        out_shape=jax.ShapeDtypeStruct((b, t, A_HEADS * HEAD_DIM), F32),
        compiler_params=_cparams("parallel", "parallel", "arbitrary"),
        name="swa_attention",
    )(sinks, qkv, qkv, qkv, qkv, qkv, tab)


def _split3(x):
    p0 = x.astype(BF16)
    r1 = x - p0.astype(F32)
    p1 = r1.astype(BF16)
    p2 = (r1 - p1.astype(F32)).astype(BF16)
    return p0, p1, p2


def _dot_exact_rhs01(x, m01):
    p0, p1, p2 = _split3(x)
    return _dot(p0, m01) + _dot(p1, m01) + _dot(p2, m01)


def _dot_exact_lhs01(m01, x):
    p0, p1, p2 = _split3(x)
    return _dot(m01, p0) + _dot(m01, p1) + _dot(m01, p2)


def _head_sum_matrix(width):
    r = lax.broadcasted_iota(jnp.int32, (width, width), 0) // HEAD_DIM
    c = lax.broadcasted_iota(jnp.int32, (width, width), 1) // HEAD_DIM
    return jnp.where(r == c, 1.0, 0.0).astype(BF16)


def _sigmoid(z):
    return 1.0 / (1.0 + jnp.exp(-z))


def _softplus(z):
    return jnp.maximum(z, 0.0) + jnp.log(1.0 + jnp.exp(-jnp.abs(z)))


def _rwkv_prep_kernel(p_ref, pprev_ref, mu_ref, wa2_ref, g2_ref, w0_ref, a0_ref, kk_ref, ka_ref,
                      r_out, k_out, v_out, lw_out, kk_out, b_out, g_out):
    i = pl.program_id(1)
    p = p_ref[0]
    w = B_WIDTH
    last = pprev_ref[0, SUBLANES - 1:SUBLANES, :]
    last = jnp.where(i > 0, last, 0.0)
    row = lax.broadcasted_iota(jnp.int32, p.shape, 0)
    prev = jnp.where(row == 0, last, pltpu.roll(p, 1, 0))
    xs = p + (prev - p) * mu_ref[...]
    r = xs[:, 0:w]
    k = xs[:, w:2 * w]
    v = xs[:, 2 * w:3 * w]
    wa_lo = xs[:, 3 * w:3 * w + LANES]
    g_lo = xs[:, 3 * w + LANES:3 * w + 2 * LANES]
    lo, _ = _lane_half_masks(wa_lo.shape)
    wa_in = jnp.where(lo, jnp.tanh(wa_lo), wa_lo)
    wa = _dot_x(wa_in, wa2_ref[...])
    wlog = -_softplus(-(w0_ref[...] + wa[:, 0:w])) - 0.5
    a = _sigmoid(a0_ref[...] + wa[:, w:2 * w])
    g = _dot(_sigmoid(g_lo).astype(BF16), g2_ref[...])
    kk = k * kk_ref[...]
    ss = _dot_exact_rhs01(kk * kk, _head_sum_matrix(w))
    kk = kk / jnp.maximum(jnp.sqrt(ss), 1e-12)
    kmod = k * (1.0 + (a - 1.0) * ka_ref[...])
    r_out[0] = r
    k_out[0] = kmod
    v_out[0] = v
    lw_out[0] = -jnp.exp(wlog)
    kk_out[0] = kk
    b_out[0] = kk * a
    g_out[0] = g


def rwkv_prep(pb, mu, w0, w2, a0, a2, g2, k_k, k_a, tc=256):
    b, t, win = pb.shape
    tc = min(tc, t)
    w = B_WIDTH
    wa2 = jnp.zeros((LANES, 2 * w), F32)
    wa2 = wa2.at[:B_DECAY_RANK, :w].set(w2).at[B_DECAY_RANK:, w:].set(a2)
    row = lambda z: z.reshape(1, -1)
    vec = pl.BlockSpec((1, w), lambda i, j: (0, 0))
    outs = [jax.ShapeDtypeStruct((b, t, w), F32)] * 7
    return pl.pallas_call(
        _rwkv_prep_kernel,
        grid=(b, t // tc),
        in_specs=[pl.BlockSpec((1, tc, win), lambda i, j: (i, j, 0)),
                  pl.BlockSpec((1, SUBLANES, win), lambda i, j: (i, jnp.maximum(j * (tc // SUBLANES) - 1, 0), 0)),
                  pl.BlockSpec((1, win), lambda i, j: (0, 0)),
                  pl.BlockSpec((LANES, 2 * w), lambda i, j: (0, 0)),
                  pl.BlockSpec((B_GATE_RANK, w), lambda i, j: (0, 0)),
                  vec, vec, vec, vec],
        out_specs=[pl.BlockSpec((1, tc, w), lambda i, j: (i, j, 0))] * 7,
        out_shape=outs,
        compiler_params=_cparams("parallel", "arbitrary"),
        name="rwkv_prep",
    )(pb, pb, row(mu), wa2, g2.astype(BF16), row(w0), row(a0), row(k_k), row(k_a))


def _rwkv_scan_kernel(r_ref, k_ref, v_ref, lw_ref, kk_ref, b_ref, g_ref, rk_ref, lnw_ref, lnb_ref,
                      o_ref, st_scr, *, ch, nsub):
    step = pl.program_id(1)
    n2 = 2 * ch

    @pl.when(step == 0)
    def _():
        st_scr[...] = jnp.zeros_like(st_scr)

    ti = lax.broadcasted_iota(jnp.int32, (n2, n2), 0)
    si = lax.broadcasted_iota(jnp.int32, (n2, n2), 1)
    same = (ti // ch) == (si // ch)
    incl = jnp.logical_and(same, si <= ti)
    strict = jnp.logical_and(same, si < ti)
    eye = jnp.where(si == ti, 1.0, 0.0)
    levels = []
    s = 1
    while s < ch:
        same2s = (ti // (2 * s)) == (si // (2 * s))
        levels.append(jnp.logical_and(same2s, jnp.logical_and((ti // s) % 2 == 1, (si // s) % 2 == 0)))
        s *= 2
    tr = lax.broadcasted_iota(jnp.int32, (ch, ch), 0)
    tc = lax.broadcasted_iota(jnp.int32, (ch, ch), 1)
    tri01 = jnp.where(tc <= tr, 1.0, 0.0).astype(BF16)
    lo, hi = _lane_half_masks((ch, LANES))
    hsum = _head_sum_matrix(LANES)

    def stack2(z):
        return jnp.concatenate([jnp.where(lo, z, 0.0), jnp.where(hi, z, 0.0)], axis=0)

    def dup2(z):
        return jnp.concatenate([z, z], axis=0)

    npair = B_HEADS // 2
    chains = [(sub, pr) for sub in range(nsub) for pr in range(npair)]
    cums = [_dot_exact_lhs01(tri01, lw_ref[0, sub * ch:(sub + 1) * ch, :]) for sub in range(nsub)]
    a2, r2, b2, k2, v2, bh2, kh2, decay = [], [], [], [], [], [], [], []
    for sub, pr in chains:
        rows = slice(sub * ch, (sub + 1) * ch)
        sl = slice(pr * LANES, (pr + 1) * LANES)
        k, bb = k_ref[0, rows, sl], b_ref[0, rows, sl]
        cum = cums[sub][:, sl]
        tot = cum[ch - 1:ch, :]
        e_neg = jnp.exp(-cum)
        e_rest = jnp.exp(tot - cum)
        a2.append(stack2(kk_ref[0, rows, sl] * jnp.exp(cum - lw_ref[0, rows, sl])).astype(BF16))
        r2.append(stack2(r_ref[0, rows, sl] * jnp.exp(cum)))
        b2.append(dup2(bb * e_neg).astype(BF16))
        k2.append(dup2(k * e_neg).astype(BF16))
        v2.append(stack2(v_ref[0, rows, sl]).astype(BF16))
        bh2.append(stack2(bb * e_rest).astype(BF16))
        kh2.append(stack2(k * e_rest).astype(BF16))
        decay.append(jnp.exp(tot))
    nc = len(chains)
    gram = [_dot_nt(jnp.concatenate([a2[i], r2[i].astype(BF16)], axis=0), jnp.concatenate([b2[i], k2[i]], axis=0))
            for i in range(nc)]
    ab = [jnp.where(strict, g[0:n2, 0:n2], 0.0) for g in gram]
    ak = [jnp.where(strict, g[0:n2, n2:2 * n2], 0.0).astype(BF16) for g in gram]
    rb = [jnp.where(incl, g[n2:2 * n2, 0:n2], 0.0).astype(BF16) for g in gram]
    rk = [jnp.where(incl, g[n2:2 * n2, n2:2 * n2], 0.0).astype(BF16) for g in gram]
    tinv = [eye] * nc
    for lvl in levels:
        tb = [t.astype(BF16) for t in tinv]
        tm = [_dot(tb[i], jnp.where(lvl, ab[i], 0.0).astype(BF16)).astype(BF16) for i in range(nc)]
        tinv = [tinv[i] - _dot(tm[i], tb[i]) for i in range(nc)]
    tb = [t.astype(BF16) for t in tinv]
    akv = [_dot(ak[i], v2[i]).astype(BF16) for i in range(nc)]
    wu = [_dot(tb[i], jnp.concatenate([a2[i], akv[i]], axis=1)).astype(BF16) for i in range(nc)]
    rwu = [_dot(rb[i], wu[i]) for i in range(nc)]
    rw = [(r2[i] - rwu[i][:, 0:LANES]).astype(BF16) for i in range(nc)]
    y0 = [_dot(rk[i], v2[i]) - rwu[i][:, LANES:2 * LANES] for i in range(nc)]
    gm = [_dot_tn(bh2[i], wu[i][:, 0:LANES]).astype(BF16) for i in range(nc)]
    s0t = [_dot_tn(jnp.concatenate([v2[i], wu[i][:, LANES:2 * LANES]], axis=0),
                   jnp.concatenate([kh2[i], -bh2[i]], axis=0)) for i in range(nc)]
    states = [st_scr[pr] for pr in range(npair)]
    ys = [None] * nc
    for i, (sub, pr) in enumerate(chains):
        st = states[pr]
        stb = st.astype(BF16)
        y2 = _dot_nt(rw[i], stb) + y0[i]
        ys[i] = y2[0:ch, :] + y2[ch:n2, :]
        states[pr] = st * decay[i] - _dot_nt(stb, gm[i]) + s0t[i]
    for pr in range(npair):
        st_scr[pr] = states[pr]
    tstep = nsub * ch
    y_st = jnp.concatenate([ys[sub * npair + pr] for pr in range(npair) for sub in range(nsub)], axis=0)
    rkr_st = jnp.concatenate([r_ref[0, :, pr * LANES:(pr + 1) * LANES] * k_ref[0, :, pr * LANES:(pr + 1) * LANES]
                              * rk_ref[:, pr * LANES:(pr + 1) * LANES] for pr in range(npair)], axis=0)
    sums = _dot_exact_rhs01(jnp.concatenate([y_st, rkr_st], axis=0), hsum)
    yc = y_st - sums[0:npair * tstep, :] * (1.0 / HEAD_DIM)
    var = _dot_exact_rhs01(yc * yc, hsum) * (1.0 / HEAD_DIM)
    yn = yc * lax.rsqrt(var + B_LN_EPS)
    for pr in range(npair):
        sl = slice(pr * LANES, (pr + 1) * LANES)
        rows = slice(pr * tstep, (pr + 1) * tstep)
        bonus = sums[npair * tstep + pr * tstep:npair * tstep + (pr + 1) * tstep, :] * v_ref[0, :, sl]
        o_ref[0, :, sl] = (yn[rows, :] * lnw_ref[:, sl] + lnb_ref[:, sl] + bonus) * g_ref[0, :, sl]


def rwkv_scan(r, k, v, lw, kk, bb, g, r_k, ln_w, ln_b, ch=RWKV_CHUNK, nsub=2):
    b, t, w = r.shape
    ch = min(ch, t)
    nsub = min(nsub, t // ch)
    tstep = ch * nsub
    seq = pl.BlockSpec((1, tstep, w), lambda i, c: (i, c, 0))
    vec = pl.BlockSpec((1, w), lambda i, c: (0, 0))
    row = lambda z: z.reshape(1, w)
    return pl.pallas_call(
        functools.partial(_rwkv_scan_kernel, ch=ch, nsub=nsub),
        grid=(b, t // tstep),
        in_specs=[seq] * 7 + [vec] * 3,
        out_specs=seq,
        out_shape=jax.ShapeDtypeStruct((b, t, w), F32),
        scratch_shapes=[pltpu.VMEM((B_HEADS // 2, LANES, LANES), F32)],
        compiler_params=_cparams("parallel", "arbitrary"),
        name="rwkv_scan",
    )(r, k, v, lw, kk, bb, g, row(r_k), row(ln_w), row(ln_b))


def _fill_vt(v_ref_slice, vt_scr, t):
    sub = lax.broadcasted_iota(jnp.int32, (LANES, LANES), 0)

    def body(j, carry):
        start = pl.multiple_of(j * LANES, LANES)
        vt = v_ref_slice(start).T
        vt_scr[:, pl.ds(start, LANES)] = jnp.where(sub < HEAD_DIM, vt, 1.0).astype(BF16)
        return carry

    lax.fori_loop(0, t // LANES, body, 0)


def _stack_heads(src_ref, dst_scr, n_heads, scale):
    lo, hi = _lane_half_masks((QCHUNK, LANES))
    for h in range(n_heads):
        tile = src_ref[0, :, (h // 2) * LANES:(h // 2 + 1) * LANES]
        if scale != 1.0:
            tile = tile * scale
        dst_scr[h * QCHUNK:(h + 1) * QCHUNK, :] = jnp.where((lo, hi)[h % 2], tile, 0.0).astype(BF16)


def _online_update(scores, vt, m_scr, acc_scr):
    nh = len(scores)
    m_old = [m_scr[h] for h in range(nh)]
    m_new = [jnp.maximum(m_old[h], jnp.max(scores[h], axis=0, keepdims=True)) for h in range(nh)]
    p = [jnp.exp2(scores[h] - m_new[h]).astype(BF16) for h in range(nh)]
    pv = [_dot(vt, p[h]) for h in range(nh)]
    for h in range(nh):
        acc_scr[h] = jnp.exp2(m_old[h] - m_new[h]) * acc_scr[h] + pv[h]
        m_scr[h] = m_new[h]


def _bias_rows(tabt_ref, h, d_first, n_tiles):
    tiles = [tabt_ref[jnp.clip(d_first - u, 0, BIAS_TILES - 1), h] for u in range(n_tiles)]
    return jnp.concatenate(tiles, axis=0)


def _write_heads(o_ref, acc_scr, n_heads):
    for p in range(n_heads // 2):
        parts = []
        for h in (2 * p, 2 * p + 1):
            a = acc_scr[h]
            parts.append(a[0:HEAD_DIM, :] * (1.0 / a[HEAD_DIM:HEAD_DIM + 1, :]))
        o_ref[0, :, p * LANES:(p + 1) * LANES] = jnp.concatenate(parts, axis=0).T


def _moba_kernel(q_ref, k_ref, v_ref, tabt_ref, o_ref, k_scr, vt_scr, km_scr, qst_scr, pick_scr, m_scr, acc_scr):
    c = pl.program_id(2)
    t = k_ref.shape[1]
    nblk = t // MOBA_BLOCK
    nbp = km_scr.shape[0]
    group = C_HEADS // C_KV_HEADS

    @pl.when(c == 0)
    def _():
        kf = k_ref[0]
        k_scr[...] = kf.astype(BF16)
        km_scr[...] = jnp.zeros_like(km_scr)
        km_scr[0:nblk, :] = jnp.mean(kf.reshape(nblk, MOBA_BLOCK, LANES), axis=1)
        _fill_vt(lambda start: v_ref[0, pl.ds(start, LANES), :], vt_scr, t)

    own = (c * QCHUNK) // MOBA_BLOCK
    part = c % (MOBA_BLOCK // QCHUNK)
    _stack_heads(q_ref, qst_scr, group, HEAD_DIM ** -0.5 * LOG2E)

    qsum = q_ref[0, :, 0:LANES]
    for p in range(1, group // 2):
        qsum = qsum + q_ref[0, :, p * LANES:(p + 1) * LANES]
    gate = _dot_nt_x(km_scr[...], qsum)
    blk = lax.broadcasted_iota(jnp.int32, (nbp, QCHUNK), 0)
    gate = jnp.where(blk < own, gate, -jnp.inf)
    rank = jnp.zeros((nbp, QCHUNK), jnp.int32)
    for mblk in range(nblk):
        gm = gate[mblk:mblk + 1, :]
        ahead = jnp.logical_or(gm > gate, jnp.logical_and(gm == gate, mblk < blk))
        rank = rank + jnp.where(ahead, 1, 0)
    chosen = jnp.logical_and(rank < MOBA_TOPK, blk < own)
    pick_scr[...] = jnp.where(chosen, 0.0, MASKED)

    m_scr[...] = jnp.full(m_scr.shape, M_INIT, F32)
    acc_scr[...] = jnp.zeros_like(acc_scr)
    tiles_per_blk = MOBA_BLOCK // LANES
    kpos = lax.broadcasted_iota(jnp.int32, (MOBA_BLOCK, QCHUNK), 0)
    qpos = part * QCHUNK + lax.broadcasted_iota(jnp.int32, (MOBA_BLOCK, QCHUNK), 1)
    second = lax.broadcasted_iota(jnp.int32, (2 * MOBA_BLOCK, QCHUNK), 0) >= MOBA_BLOCK

    def attend(first_blk, n_blk, okb):
        rows = n_blk * MOBA_BLOCK
        start = pl.multiple_of(first_blk * MOBA_BLOCK, MOBA_BLOCK)
        s_all = _dot_nt(k_scr[pl.ds(start, rows), :], qst_scr[...])
        vt = vt_scr[:, pl.ds(start, rows)]
        d_first = c - tiles_per_blk * first_blk
        scores = [s_all[:, g * QCHUNK:(g + 1) * QCHUNK]
                  + (_bias_rows(tabt_ref, g, d_first, n_blk * tiles_per_blk) + okb) for g in range(group)]
        _online_update(scores, vt, m_scr, acc_scr)

    attend(own, 1, jnp.where(kpos <= qpos, 0.0, MASKED))

    def pair_body(i, carry):
        okb = jnp.where(second, pick_scr[pl.ds(2 * i + 1, 1), :], pick_scr[pl.ds(2 * i, 1), :])
        attend(2 * i, 2, okb)
        return carry

    lax.fori_loop(0, own // 2, pair_body, 0)

    @pl.when(own % 2 == 1)
    def _():
        attend(own - 1, 1, pick_scr[pl.ds(own - 1, 1), :])

    _write_heads(o_ref, acc_scr, group)


def moba_attention(qc, kvc, tabt):
    b, t, _ = qc.shape
    assert t % MOBA_BLOCK == 0
    nq = t // QCHUNK
    group = C_HEADS // C_KV_HEADS
    qw = group * HEAD_DIM
    nbp = -(-(t // MOBA_BLOCK) // SUBLANES) * SUBLANES
    return pl.pallas_call(
        _moba_kernel,
        grid=(b, C_KV_HEADS, nq),
        in_specs=[pl.BlockSpec((1, QCHUNK, qw), lambda i, h, c: (i, c, h)),
                  pl.BlockSpec((1, t, LANES), lambda i, h, c: (i, 0, h)),
                  pl.BlockSpec((1, t, LANES), lambda i, h, c: (i, 0, C_KV_HEADS + h)),
                  pl.BlockSpec((BIAS_TILES, group, LANES, LANES), lambda i, h, c: (0, h, 0, 0))],
        out_specs=pl.BlockSpec((1, QCHUNK, qw), lambda i, h, c: (i, c, h)),
        out_shape=jax.ShapeDtypeStruct((b, t, C_HEADS * HEAD_DIM), F32),
        scratch_shapes=[pltpu.VMEM((t, LANES), BF16), pltpu.VMEM((LANES, t), BF16),
                        pltpu.VMEM((nbp, LANES), F32),
                        pltpu.VMEM((group * QCHUNK, LANES), BF16),
                        pltpu.VMEM((nbp, QCHUNK), F32),
                        pltpu.VMEM((group, 1, QCHUNK), F32),
                        pltpu.VMEM((group, LANES, QCHUNK), F32)],
        compiler_params=_cparams("parallel", "parallel", "arbitrary"),
        name="moba_attention",
    )(qc, kvc, kvc, tabt)


def _dsa_kernel(q_ref, kv_ref, qi_ref, ki_ref, gk_ref, tabt_ref, o_ref,
                k_scr, vt_scr, ki_scr, key_scr, qst_scr, qist_scr, m_scr, acc_scr, *, topk):
    c = pl.program_id(1)
    t = k_scr.shape[0]
    grp = KEY_GROUP * LANES
    ngroup = c // KEY_GROUP + 1
    sub = lax.broadcasted_iota(jnp.int32, (LANES, QCHUNK), 0)
    gsub = lax.broadcasted_iota(jnp.int32, (grp, QCHUNK), 0)
    qpos = c * QCHUNK + lax.broadcasted_iota(jnp.int32, (grp, QCHUNK), 1)

    @pl.when(c == 0)
    def _():
        k_scr[...] = kv_ref[0, :, 0:LANES].astype(BF16)
        ki_scr[...] = _rms(ki_ref[0], gk_ref[...]).astype(BF16)
        _fill_vt(lambda start: kv_ref[0, pl.ds(start, LANES), LANES:2 * LANES], vt_scr, t)

    _stack_heads(q_ref, qst_scr, D_HEADS, HEAD_DIM ** -0.5 * LOG2E)
    _stack_heads(qi_ref, qist_scr, IDX_HEADS, 1.0)
    w_t = (qi_ref[0, :, IDX_HEADS * IDX_DIM + LANES:IDX_HEADS * IDX_DIM + 2 * LANES]
           * ((IDX_HEADS * IDX_DIM) ** -0.5)).T

    def score_body(gi, carry):
        start = pl.multiple_of(gi * grp, grp)
        rel = jnp.maximum(_dot_nt(ki_scr[pl.ds(start, grp), :], qist_scr[...]), 0.0)
        sc = jnp.zeros((grp, QCHUNK), F32)
        for h in range(IDX_HEADS):
            sc = sc + rel[:, h * QCHUNK:(h + 1) * QCHUNK] * w_t[h:h + 1, :]
        sc = jnp.where(sc == 0.0, 0.0, sc)
        bits = pltpu.bitcast(sc, jnp.int32)
        skey = bits ^ (lax.shift_right_arithmetic(bits, 31) & 0x7FFFFFFF)
        key_scr[pl.ds(start, grp), :] = jnp.where((gi * grp + gsub) <= qpos, skey, INT_MIN)
        return carry

    lax.fori_loop(0, ngroup, score_body, 0)

    def count(pred):
        def body(gi, acc):
            for u in range(KEY_GROUP):
                start = pl.multiple_of(gi * grp + u * LANES, LANES)
                acc = jnp.where(pred(key_scr[pl.ds(start, LANES), :], start + sub), acc + 1, acc)
            return acc
        acc = lax.fori_loop(0, ngroup, body, jnp.zeros((LANES, QCHUNK), jnp.int32))
        return jnp.sum(acc, axis=0, keepdims=True)

    cnt0 = count(lambda kt, pos: kt >= 0)
    thr = jnp.where(cnt0 >= topk, 0, INT_MIN).astype(jnp.int32)

    def bit_body(i, thr):
        cand = thr + lax.shift_left(jnp.int32(1), 30 - i)
        cnt = count(lambda kt, pos: kt >= cand)
        return jnp.where(cnt >= topk, cand, thr)

    thr = lax.fori_loop(0, 31, bit_body, thr)

    cnt_ge = count(lambda kt, pos: kt >= thr)
    tied = jnp.logical_and(cnt_ge > topk, thr > INT_MIN)
    any_tied = jnp.max(jnp.where(tied, 1.0, 0.0)) > 0.5

    def find_cut():
        need = topk - count(lambda kt, pos: kt > thr)

        def cut_body(i, cut):
            cand = cut + lax.shift_left(jnp.int32(1), 14 - i)
            cnt = count(lambda kt, pos: jnp.logical_and(kt == thr, pos < cand))
            return jnp.where(cnt < need, cand, cut)

        return lax.fori_loop(0, 15, cut_body, jnp.zeros((1, QCHUNK), jnp.int32))

    cut = lax.cond(any_tied, find_cut, lambda: jnp.full((1, QCHUNK), 2 ** 30, jnp.int32))

    m_scr[...] = jnp.full(m_scr.shape, M_INIT, F32)
    acc_scr[...] = jnp.zeros_like(acc_scr)

    def attend(gi, with_bias):
        start = pl.multiple_of(gi * grp, grp)
        s_all = _dot_nt(k_scr[pl.ds(start, grp), :], qst_scr[...])
        keys = key_scr[pl.ds(start, grp), :]
        ok = jnp.logical_or(keys > thr, jnp.logical_and(keys == thr, (gi * grp + gsub) <= cut))
        ok = jnp.logical_and(ok, keys != INT_MIN)
        okb = jnp.where(ok, 0.0, MASKED)
        vt = vt_scr[:, pl.ds(start, grp)]
        if with_bias:
            d_first = c - gi * KEY_GROUP
            scores = [s_all[:, h * QCHUNK:(h + 1) * QCHUNK] + (_bias_rows(tabt_ref, h, d_first, KEY_GROUP) + okb)
                      for h in range(D_HEADS)]
        else:
            scores = [s_all[:, h * QCHUNK:(h + 1) * QCHUNK] + okb for h in range(D_HEADS)]
        _online_update(scores, vt, m_scr, acc_scr)

    nfar = jnp.maximum((c - (BIAS_TILES - 1) - (KEY_GROUP - 1)) // KEY_GROUP + 1, 0)

    def far_body(gi, carry):
        attend(gi, False)
        return carry

    def near_body(gi, carry):
        attend(gi, True)
        return carry

    lax.fori_loop(0, nfar, far_body, 0)
    lax.fori_loop(nfar, ngroup, near_body, 0)
    _write_heads(o_ref, acc_scr, D_HEADS)


def dsa_attention(qd, kvd, idx, kidx_gain, tabt):
    b, t, _ = qd.shape
    topk = min(DSA_TOPK, t // 4)
    nq = t // QCHUNK
    gk = jnp.concatenate([kidx_gain, kidx_gain]).reshape(1, LANES)
    wq = D_HEADS * HEAD_DIM
    return pl.pallas_call(
        functools.partial(_dsa_kernel, topk=topk),
        grid=(b, nq),
        in_specs=[pl.BlockSpec((1, QCHUNK, wq), lambda i, c: (i, c, 0)),
                  pl.BlockSpec((1, t, 2 * LANES), lambda i, c: (i, 0, 0)),
                  pl.BlockSpec((1, QCHUNK, 4 * LANES), lambda i, c: (i, c, 0)),
                  pl.BlockSpec((1, t, LANES), lambda i, c: (i, 0, 2)),
                  pl.BlockSpec((1, LANES), lambda i, c: (0, 0)),
                  pl.BlockSpec((BIAS_TILES, D_HEADS, LANES, LANES), lambda i, c: (0, 0, 0, 0))],
        out_specs=pl.BlockSpec((1, QCHUNK, wq), lambda i, c: (i, c, 0)),
        out_shape=jax.ShapeDtypeStruct((b, t, wq), F32),
        scratch_shapes=[pltpu.VMEM((t, LANES), BF16), pltpu.VMEM((LANES, t), BF16), pltpu.VMEM((t, LANES), BF16),
                        pltpu.VMEM((t, QCHUNK), jnp.int32),
                        pltpu.VMEM((D_HEADS * QCHUNK, LANES), BF16), pltpu.VMEM((IDX_HEADS * QCHUNK, LANES), BF16),
                        pltpu.VMEM((D_HEADS, 1, QCHUNK), F32),
                        pltpu.VMEM((D_HEADS, LANES, QCHUNK), F32)],
        compiler_params=_cparams("parallel", "arbitrary"),
        name="dsa_attention",
    )(qd, kvd, idx, idx, gk, tabt)


def _dup_heads(wcols, n_heads):
    d = wcols.shape[0]
    wh = wcols.reshape(d, n_heads, 1, HEAD_DIM)
    return jnp.broadcast_to(wh, (d, n_heads, 2, HEAD_DIM)).reshape(d, n_heads * 2 * HEAD_DIM)


def _even_weight(w_in):
    aq, akv = A_HEADS * HEAD_DIM, A_KV_HEADS * HEAD_DIM
    q = w_in[:, :aq]
    k = _dup_heads(w_in[:, aq:aq + akv], A_KV_HEADS)
    v = _dup_heads(w_in[:, aq + akv:aq + 2 * akv], A_KV_HEADS)
    rest = w_in[:, aq + 2 * akv:]
    return jnp.concatenate([q, k, v, rest], axis=1).astype(BF16)


def _odd_weight(w_in):
    d = w_in.shape[0]
    cq, ckv, dq = C_HEADS * HEAD_DIM, C_KV_HEADS * HEAD_DIM, D_HEADS * HEAD_DIM
    cuts = np.cumsum([cq, ckv, ckv, dq, HEAD_DIM, HEAD_DIM, IDX_HEADS * IDX_DIM, IDX_DIM]).tolist()
    qc, kc, vc, qd, kd, vd, qi, ki, wi = jnp.split(w_in, cuts, axis=1)
    pad = jnp.zeros((d, LANES - IDX_HEADS), w_in.dtype)
    cols = [qc, _dup_heads(kc, C_KV_HEADS), _dup_heads(vc, C_KV_HEADS),
            qd, _dup_heads(kd, 1), _dup_heads(vd, 1),
            qi, _dup_heads(ki, 1), wi, pad]
    return jnp.concatenate(cols, axis=1).astype(BF16)


def kernel(x, mem, rel_bias, norm_gains, mix_w_out, mem_norm, x_wq, x_wkv, x_wo, ff_w1, ff_w2, ev_w_in, a_sinks, b_mu, b_w0, b_w2, b_a0, b_a2, b_g2, b_k_k, b_k_a, b_r_k, b_ln_w, b_ln_b, od_w_in, d_kidx_norm):
    bsz, t, d = x.shape
    n = bsz * t
    depth = norm_gains.shape[0]
    mlen = mem.shape[1]
    tab = bias_table(rel_bias)
    tabt = bias_table(rel_bias, transposed=True)
    x2 = x.reshape(n, d)
    for layer in range(depth):
        i = layer // 2
        g = norm_gains[layer]
        if layer % 2 == 0:
            b_in = 3 * B_WIDTH + B_DECAY_RANK + B_A_RANK + B_GATE_RANK
            qkv, pb = norm_proj(x2, g[0], _even_weight(ev_w_in[i]), (1024, b_in), (F32, F32))
            ya = swa_attention(qkv.reshape(bsz, t, 1024), a_sinks[i], tab)
            prep = rwkv_prep(pb.reshape(bsz, t, b_in), b_mu[i], b_w0[i], b_w2[i], b_a0[i], b_a2[i], b_g2[i],
                             b_k_k[i], b_k_a[i])
            yb = rwkv_scan(*prep, b_r_k[i].reshape(-1), b_ln_w[i], b_ln_b[i])
        else:
            qc, kvc, qd, kvd, idx = norm_proj(x2, g[0], _odd_weight(od_w_in[i]),
                                              (512, 512, 512, 256, 512), (F32,) * 5)
            ya = moba_attention(qc.reshape(bsz, t, 512), kvc.reshape(bsz, t, 512), tabt)
            yb = dsa_attention(qd.reshape(bsz, t, 512), kvd.reshape(bsz, t, 256), idx.reshape(bsz, t, 512),
                               d_kidx_norm[i], tabt)
        x2 = proj_norm_res(ya.reshape(n, -1), yb.reshape(n, -1), mix_w_out[layer], g[1], x2)
        kmem, vmem = norm_proj(mem.reshape(bsz * mlen, d), mem_norm[layer], x_wkv[layer].astype(BF16),
                               (X_WIDTH, X_WIDTH), (BF16, BF16))
        x3 = cross_attn(x2.reshape(bsz, t, d), kmem.reshape(bsz, mlen, X_WIDTH), vmem.reshape(bsz, mlen, X_WIDTH),
                        x_wq[layer], x_wo[layer], g[2], g[3])
        x2 = mlp(x3.reshape(n, d), ff_w1[layer], ff_w2[layer], g[4], g[5])
    return x2.reshape(bsz, t, d)
```

```python
import functools

import numpy as np
import jax
import jax.numpy as jnp
from jax import lax
from jax.experimental import pallas as pl
from jax.experimental.pallas import tpu as pltpu

F32 = jnp.float32
BF16 = jnp.bfloat16
HIGHEST = lax.Precision.HIGHEST

LANES = 128
SUBLANES = 8
HEAD_DIM = 64
EPS = 1e-6
A_HEADS, A_KV_HEADS, BAND = 8, 2, 128
B_HEADS, B_WIDTH = 8, 512
B_DECAY_RANK, B_A_RANK, B_GATE_RANK = 64, 64, 128
B_LN_EPS = 64e-5
RWKV_CHUNK = 64
C_HEADS, C_KV_HEADS, MOBA_BLOCK, MOBA_TOPK, QCHUNK = 8, 2, 256, 3, 128
D_HEADS, IDX_HEADS, IDX_DIM, DSA_TOPK = 8, 4, 64, 256
N_BUCKETS, BIAS_HEADS = 32, 8
BIAS_TILES = 9
X_HEADS, X_HEAD_DIM, X_WIDTH = 4, 128, 512

LOG2E = 1.4426950408889634
KEY_GROUP = 4
MASKED = -2e30
M_INIT = -1e30
INT_MIN = -(2 ** 31)
HALF_BIAS = 2 ** 15
VMEM_LIMIT = 56 * 1024 * 1024


def _cparams(*sem):
    return pltpu.CompilerParams(dimension_semantics=tuple(sem), vmem_limit_bytes=VMEM_LIMIT)


def _dot(a, b):
    return jnp.dot(a, b, preferred_element_type=F32)


def _dot_nt(a, b):
    return lax.dot_general(a, b, (((1,), (1,)), ((), ())), preferred_element_type=F32)


def _dot_tn(a, b):
    return lax.dot_general(a, b, (((0,), (0,)), ((), ())), preferred_element_type=F32)


def _dot_x(a, b):
    return jnp.dot(a, b, preferred_element_type=F32, precision=HIGHEST)


def _dot_nt_x(a, b):
    return lax.dot_general(a, b, (((1,), (1,)), ((), ())), preferred_element_type=F32, precision=HIGHEST)


def _rms(xf, g):
    ms = jnp.mean(xf * xf, axis=-1, keepdims=True)
    return xf * lax.rsqrt(ms + EPS) * g


def _lane_half_masks(shape):
    lane = lax.broadcasted_iota(jnp.int32, shape, len(shape) - 1)
    lo = lane < HEAD_DIM
    return lo, jnp.logical_not(lo)


def _bucket_of_distance(n):
    exact = N_BUCKETS // 2
    if n < exact:
        return n
    j = 0
    while n ** 8 >= (exact ** 8) * (2 ** (3 * (j + 1))):
        j += 1
    return min(exact + j, N_BUCKETS - 1)


@functools.lru_cache(maxsize=None)
def _bucket_tiles(transposed):
    by_dist = np.array([_bucket_of_distance(n) for n in range(BIAS_TILES * LANES + LANES)], np.int32)
    q = np.arange(LANES)[None, :] if transposed else np.arange(LANES)[:, None]
    k = np.arange(LANES)[:, None] if transposed else np.arange(LANES)[None, :]
    tiles = [by_dist[np.maximum(d * LANES + q - k, 0)] for d in range(BIAS_TILES)]
    return np.stack(tiles).astype(np.int32)


def _bias_table_kernel(rb_ref, bkt_ref, out_ref, *, base2_shifted):
    bkt = bkt_ref[0]
    for h in range(BIAS_HEADS):
        acc = jnp.zeros((LANES, LANES), F32)
        for b in range(N_BUCKETS):
            acc = jnp.where(bkt == b, rb_ref[b, h], acc)
        if base2_shifted:
            acc = (acc - rb_ref[N_BUCKETS - 1, h]) * LOG2E
        out_ref[0, h] = acc


def bias_table(rel_bias, transposed=False):
    return pl.pallas_call(
        functools.partial(_bias_table_kernel, base2_shifted=transposed),
        grid=(BIAS_TILES,),
        in_specs=[pl.BlockSpec(memory_space=pltpu.SMEM),
                  pl.BlockSpec((1, LANES, LANES), lambda d: (d, 0, 0))],
        out_specs=pl.BlockSpec((1, BIAS_HEADS, LANES, LANES), lambda d: (d, 0, 0, 0)),
        out_shape=jax.ShapeDtypeStruct((BIAS_TILES, BIAS_HEADS, LANES, LANES), F32),
        compiler_params=_cparams("arbitrary"),
        name="bias_table",
    )(rel_bias, jnp.asarray(_bucket_tiles(transposed)))


def _norm_proj_kernel(x_ref, g_ref, w_ref, *out_refs, splits):
    h = _rms(x_ref[...], g_ref[...]).astype(BF16)
    res = _dot(h, w_ref[...])
    off = 0
    for o_ref, width in zip(out_refs, splits):
        o_ref[...] = res[:, off:off + width].astype(o_ref.dtype)
        off += width


def norm_proj(x2, gain, w, splits, out_dtypes, tm=256):
    n, d = x2.shape
    tm = min(tm, n)
    total = sum(splits)
    assert w.shape == (d, total) and n % tm == 0 and all(s % LANES == 0 for s in splits)
    return pl.pallas_call(
        functools.partial(_norm_proj_kernel, splits=tuple(splits)),
        grid=(n // tm,),
        in_specs=[pl.BlockSpec((tm, d), lambda i: (i, 0)),
                  pl.BlockSpec((1, d), lambda i: (0, 0)),
                  pl.BlockSpec((d, total), lambda i: (0, 0))],
        out_specs=[pl.BlockSpec((tm, s), lambda i: (i, 0)) for s in splits],
        out_shape=[jax.ShapeDtypeStruct((n, s), dt) for s, dt in zip(splits, out_dtypes)],
        compiler_params=_cparams("parallel"),
        name="norm_proj",
    )(x2, gain.reshape(1, d), w)


def _proj_norm_res_kernel(ya_ref, yb_ref, wa_ref, wb_ref, g_ref, x_ref, o_ref):
    z = _dot(ya_ref[...].astype(BF16), wa_ref[...]) + _dot(yb_ref[...].astype(BF16), wb_ref[...])
    o_ref[...] = x_ref[...] + _rms(z, g_ref[...])


def proj_norm_res(ya, yb, w, gain, x2, tm=512):
    n, d = x2.shape
    tm = min(tm, n)
    ka, kb = ya.shape[1], yb.shape[1]
    wa, wb = w[:ka].astype(BF16), w[ka:].astype(BF16)
    return pl.pallas_call(
        _proj_norm_res_kernel,
        grid=(n // tm,),
        in_specs=[pl.BlockSpec((tm, ka), lambda i: (i, 0)),
                  pl.BlockSpec((tm, kb), lambda i: (i, 0)),
                  pl.BlockSpec((ka, d), lambda i: (0, 0)),
                  pl.BlockSpec((kb, d), lambda i: (0, 0)),
                  pl.BlockSpec((1, d), lambda i: (0, 0)),
                  pl.BlockSpec((tm, d), lambda i: (i, 0))],
        out_specs=pl.BlockSpec((tm, d), lambda i: (i, 0)),
        out_shape=jax.ShapeDtypeStruct((n, d), F32),
        compiler_params=_cparams("parallel"),
        name="proj_norm_res",
    )(ya, yb, wa, wb, gain.reshape(1, d), x2)


def _cross_attn_kernel(x_ref, k_ref, v_ref, wq_ref, wo_ref, gpre_ref, gpost_ref, o_ref):
    x = x_ref[0]
    h = _rms(x, gpre_ref[...]).astype(BF16)
    q = _dot(h, wq_ref[...])
    k = k_ref[0]
    v = v_ref[0]
    outs = []
    for hd in range(X_HEADS):
        sl = slice(hd * X_HEAD_DIM, (hd + 1) * X_HEAD_DIM)
        s = _dot_nt(q[:, sl].astype(BF16), k[:, sl]) * (X_HEAD_DIM ** -0.5)
        m = jnp.max(s, axis=-1, keepdims=True)
        p = jnp.exp(s - m)
        p = p / jnp.sum(p, axis=-1, keepdims=True)
        outs.append(_dot(p.astype(BF16), v[:, sl]))
    o = jnp.concatenate(outs, axis=-1).astype(BF16)
    z = _dot(o, wo_ref[...])
    o_ref[0] = x + _rms(z, gpost_ref[...])


def cross_attn(x3, kmem, vmem, wq, wo, gpre, gpost, tq=512):
    b, t, d = x3.shape
    tq = min(tq, t)
    m = kmem.shape[1]
    return pl.pallas_call(
        _cross_attn_kernel,
        grid=(b, t // tq),
        in_specs=[pl.BlockSpec((1, tq, d), lambda i, j: (i, j, 0)),
                  pl.BlockSpec((1, m, X_WIDTH), lambda i, j: (i, 0, 0)),
                  pl.BlockSpec((1, m, X_WIDTH), lambda i, j: (i, 0, 0)),
                  pl.BlockSpec((d, X_WIDTH), lambda i, j: (0, 0)),
                  pl.BlockSpec((X_WIDTH, d), lambda i, j: (0, 0)),
                  pl.BlockSpec((1, d), lambda i, j: (0, 0)),
                  pl.BlockSpec((1, d), lambda i, j: (0, 0))],
        out_specs=pl.BlockSpec((1, tq, d), lambda i, j: (i, j, 0)),
        out_shape=jax.ShapeDtypeStruct((b, t, d), F32),
        compiler_params=_cparams("parallel", "parallel"),
        name="cross_attn",
    )(x3, kmem, vmem, wq.astype(BF16), wo.astype(BF16), gpre.reshape(1, d), gpost.reshape(1, d))


def _mlp_kernel(x_ref, w1_ref, w2_ref, gpre_ref, gpost_ref, o_ref, h_scr, acc_scr):
    j = pl.program_id(1)

    @pl.when(j == 0)
    def _():
        h_scr[...] = _rms(x_ref[...], gpre_ref[...]).astype(BF16)
        acc_scr[...] = jnp.zeros_like(acc_scr)

    a = jnp.maximum(_dot(h_scr[...], w1_ref[...]), 0.0)
    acc_scr[...] += _dot((a * a).astype(BF16), w2_ref[...])

    @pl.when(j == pl.num_programs(1) - 1)
    def _():
        o_ref[...] = x_ref[...] + _rms(acc_scr[...], gpost_ref[...])


def mlp(x2, w1, w2, gpre, gpost, tm=512, tf=1024):
    n, d = x2.shape
    tm = min(tm, n)
    f = w1.shape[1]
    return pl.pallas_call(
        _mlp_kernel,
        grid=(n // tm, f // tf),
        in_specs=[pl.BlockSpec((tm, d), lambda i, j: (i, 0)),
                  pl.BlockSpec((d, tf), lambda i, j: (0, j)),
                  pl.BlockSpec((tf, d), lambda i, j: (j, 0)),
                  pl.BlockSpec((1, d), lambda i, j: (0, 0)),
                  pl.BlockSpec((1, d), lambda i, j: (0, 0))],
        out_specs=pl.BlockSpec((tm, d), lambda i, j: (i, 0)),
        out_shape=jax.ShapeDtypeStruct((n, d), F32),
        scratch_shapes=[pltpu.VMEM((tm, d), BF16), pltpu.VMEM((tm, d), F32)],
        compiler_params=_cparams("parallel", "arbitrary"),
        name="mlp",
    )(x2, w1.astype(BF16), w2.astype(BF16), gpre.reshape(1, d), gpost.reshape(1, d))


def _swa_kernel(sink_ref, q_ref, kc_ref, kp_ref, vc_ref, vp_ref, tab_ref, o_ref):
    hkv = pl.program_id(1)
    n = pl.program_id(2)
    scale = HEAD_DIM ** -0.5
    kc = kc_ref[0].astype(BF16)
    kp = kp_ref[0].astype(BF16)
    vc = vc_ref[0]
    vp = vp_ref[0]
    lo, hi = _lane_half_masks((BAND, LANES))
    qi = lax.broadcasted_iota(jnp.int32, (BAND, BAND), 0)
    kj = lax.broadcasted_iota(jnp.int32, (BAND, BAND), 1)
    cur_ok = kj <= qi
    prev_ok = jnp.logical_and(kj > qi, n > 0)
    group = A_HEADS // A_KV_HEADS
    for pair in range(group // 2):
        qp = q_ref[0, :, pair * LANES:(pair + 1) * LANES]
        out = jnp.zeros((BAND, LANES), F32)
        for half, msk in enumerate((lo, hi)):
            g = pair * 2 + half
            sink = sink_ref[hkv * group + g]
            qh = jnp.where(msk, qp, 0.0).astype(BF16)
            sc = jnp.where(cur_ok, _dot_nt(qh, kc) * scale + tab_ref[0, g], MASKED)
            sp = jnp.where(prev_ok, _dot_nt(qh, kp) * scale + tab_ref[1, g], MASKED)
            m = jnp.maximum(jnp.maximum(jnp.max(sc, axis=-1, keepdims=True),
                                        jnp.max(sp, axis=-1, keepdims=True)), sink)
            pc = jnp.exp(sc - m)
            pp = jnp.exp(sp - m)
            den = jnp.sum(pc, axis=-1, keepdims=True) + jnp.sum(pp, axis=-1, keepdims=True) + jnp.exp(sink - m)
            inv = 1.0 / den
            vch = jnp.where(msk, vc, 0.0).astype(BF16)
            vph = jnp.where(msk, vp, 0.0).astype(BF16)
            out = out + _dot((pc * inv).astype(BF16), vch) + _dot((pp * inv).astype(BF16), vph)
        o_ref[0, :, pair * LANES:(pair + 1) * LANES] = out


def swa_attention(qkv, sinks, tab):
    b, t, _ = qkv.shape
    nb = t // BAND
    qw = (A_HEADS // A_KV_HEADS) * HEAD_DIM
    koff = A_HEADS * HEAD_DIM // LANES
    voff = koff + A_KV_HEADS
    group = A_HEADS // A_KV_HEADS
    return pl.pallas_call(
        _swa_kernel,
        grid=(b, A_KV_HEADS, nb),
        in_specs=[pl.BlockSpec(memory_space=pltpu.SMEM),
                  pl.BlockSpec((1, BAND, qw), lambda i, h, n: (i, n, h)),
                  pl.BlockSpec((1, BAND, LANES), lambda i, h, n: (i, n, koff + h)),
                  pl.BlockSpec((1, BAND, LANES), lambda i, h, n: (i, jnp.maximum(n - 1, 0), koff + h)),
                  pl.BlockSpec((1, BAND, LANES), lambda i, h, n: (i, n, voff + h)),
                  pl.BlockSpec((1, BAND, LANES), lambda i, h, n: (i, jnp.maximum(n - 1, 0), voff + h)),
                  pl.BlockSpec((2, group, LANES, LANES), lambda i, h, n: (0, h, 0, 0))],
        out_specs=pl.BlockSpec((1, BAND, qw), lambda i, h, n: (i, n, h)),
        out_shape=jax.ShapeDtypeStruct((b, t, A_HEADS * HEAD_DIM), F32),
        compiler_params=_cparams("parallel", "parallel", "arbitrary"),
        name="swa_attention",
    )(sinks, qkv, qkv, qkv, qkv, qkv, tab)


def _split3(x):
    p0 = x.astype(BF16)
    r1 = x - p0.astype(F32)
    p1 = r1.astype(BF16)
    p2 = (r1 - p1.astype(F32)).astype(BF16)
    return p0, p1, p2


def _dot_exact_rhs01(x, m01):
    p0, p1, p2 = _split3(x)
    return _dot(p0, m01) + _dot(p1, m01) + _dot(p2, m01)


def _dot_exact_lhs01(m01, x):
    p0, p1, p2 = _split3(x)
    return _dot(m01, p0) + _dot(m01, p1) + _dot(m01, p2)


def _head_sum_matrix(width):
    r = lax.broadcasted_iota(jnp.int32, (width, width), 0) // HEAD_DIM
    c = lax.broadcasted_iota(jnp.int32, (width, width), 1) // HEAD_DIM
    return jnp.where(r == c, 1.0, 0.0).astype(BF16)


def _sigmoid(z):
    return 1.0 / (1.0 + jnp.exp(-z))


def _softplus(z):
    return jnp.maximum(z, 0.0) + jnp.log(1.0 + jnp.exp(-jnp.abs(z)))


def _rwkv_prep_kernel(p_ref, pprev_ref, mu_ref, wa2_ref, g2_ref, w0_ref, a0_ref, kk_ref, ka_ref,
                      r_out, k_out, v_out, lw_out, kk_out, b_out, g_out):
    i = pl.program_id(1)
    p = p_ref[0]
    w = B_WIDTH
    last = pprev_ref[0, SUBLANES - 1:SUBLANES, :]
    last = jnp.where(i > 0, last, 0.0)
    row = lax.broadcasted_iota(jnp.int32, p.shape, 0)
    prev = jnp.where(row == 0, last, pltpu.roll(p, 1, 0))
    xs = p + (prev - p) * mu_ref[...]
    r = xs[:, 0:w]
    k = xs[:, w:2 * w]
    v = xs[:, 2 * w:3 * w]
    wa_lo = xs[:, 3 * w:3 * w + LANES]
    g_lo = xs[:, 3 * w + LANES:3 * w + 2 * LANES]
    lo, _ = _lane_half_masks(wa_lo.shape)
    wa_in = jnp.where(lo, jnp.tanh(wa_lo), wa_lo)
    wa = _dot_x(wa_in, wa2_ref[...])
    wlog = -_softplus(-(w0_ref[...] + wa[:, 0:w])) - 0.5
    a = _sigmoid(a0_ref[...] + wa[:, w:2 * w])
    g = _dot(_sigmoid(g_lo).astype(BF16), g2_ref[...])
    kk = k * kk_ref[...]
    ss = _dot_exact_rhs01(kk * kk, _head_sum_matrix(w))
    kk = kk / jnp.maximum(jnp.sqrt(ss), 1e-12)
    kmod = k * (1.0 + (a - 1.0) * ka_ref[...])
    r_out[0] = r
    k_out[0] = kmod
    v_out[0] = v
    lw_out[0] = -jnp.exp(wlog)
    kk_out[0] = kk
    b_out[0] = kk * a
    g_out[0] = g


def rwkv_prep(pb, mu, w0, w2, a0, a2, g2, k_k, k_a, tc=256):
    b, t, win = pb.shape
    tc = min(tc, t)
    w = B_WIDTH
    wa2 = jnp.zeros((LANES, 2 * w), F32)
    wa2 = wa2.at[:B_DECAY_RANK, :w].set(w2).at[B_DECAY_RANK:, w:].set(a2)
    row = lambda z: z.reshape(1, -1)
    vec = pl.BlockSpec((1, w), lambda i, j: (0, 0))
    outs = [jax.ShapeDtypeStruct((b, t, w), F32)] * 7
    return pl.pallas_call(
        _rwkv_prep_kernel,
        grid=(b, t // tc),
        in_specs=[pl.BlockSpec((1, tc, win), lambda i, j: (i, j, 0)),
                  pl.BlockSpec((1, SUBLANES, win), lambda i, j: (i, jnp.maximum(j * (tc // SUBLANES) - 1, 0), 0)),
                  pl.BlockSpec((1, win), lambda i, j: (0, 0)),
                  pl.BlockSpec((LANES, 2 * w), lambda i, j: (0, 0)),
                  pl.BlockSpec((B_GATE_RANK, w), lambda i, j: (0, 0)),
                  vec, vec, vec, vec],
        out_specs=[pl.BlockSpec((1, tc, w), lambda i, j: (i, j, 0))] * 7,
        out_shape=outs,
        compiler_params=_cparams("parallel", "arbitrary"),
        name="rwkv_prep",
    )(pb, pb, row(mu), wa2, g2.astype(BF16), row(w0), row(a0), row(k_k), row(k_a))


def _rwkv_scan_kernel(r_ref, k_ref, v_ref, lw_ref, kk_ref, b_ref, g_ref, rk_ref, lnw_ref, lnb_ref,
                      o_ref, st_scr, *, ch, nsub):
    step = pl.program_id(1)
    n2 = 2 * ch

    @pl.when(step == 0)
    def _():
        st_scr[...] = jnp.zeros_like(st_scr)

    ti = lax.broadcasted_iota(jnp.int32, (n2, n2), 0)
    si = lax.broadcasted_iota(jnp.int32, (n2, n2), 1)
    same = (ti // ch) == (si // ch)
    incl = jnp.logical_and(same, si <= ti)
    strict = jnp.logical_and(same, si < ti)
    eye = jnp.where(si == ti, 1.0, 0.0)
    levels = []
    s = 1
    while s < ch:
        same2s = (ti // (2 * s)) == (si // (2 * s))
        levels.append(jnp.logical_and(same2s, jnp.logical_and((ti // s) % 2 == 1, (si // s) % 2 == 0)))
        s *= 2
    tr = lax.broadcasted_iota(jnp.int32, (ch, ch), 0)
    tc = lax.broadcasted_iota(jnp.int32, (ch, ch), 1)
    tri01 = jnp.where(tc <= tr, 1.0, 0.0).astype(BF16)
    lo, hi = _lane_half_masks((ch, LANES))
    hsum = _head_sum_matrix(LANES)

    def stack2(z):
        return jnp.concatenate([jnp.where(lo, z, 0.0), jnp.where(hi, z, 0.0)], axis=0)

    def dup2(z):
        return jnp.concatenate([z, z], axis=0)

    npair = B_HEADS // 2
    chains = [(sub, pr) for sub in range(nsub) for pr in range(npair)]
    cums = [_dot_exact_lhs01(tri01, lw_ref[0, sub * ch:(sub + 1) * ch, :]) for sub in range(nsub)]
    a2, r2, b2, k2, v2, bh2, kh2, decay = [], [], [], [], [], [], [], []
    for sub, pr in chains:
        rows = slice(sub * ch, (sub + 1) * ch)
        sl = slice(pr * LANES, (pr + 1) * LANES)
        k, bb = k_ref[0, rows, sl], b_ref[0, rows, sl]
        cum = cums[sub][:, sl]
        tot = cum[ch - 1:ch, :]
        e_neg = jnp.exp(-cum)
        e_rest = jnp.exp(tot - cum)
        a2.append(stack2(kk_ref[0, rows, sl] * jnp.exp(cum - lw_ref[0, rows, sl])).astype(BF16))
        r2.append(stack2(r_ref[0, rows, sl] * jnp.exp(cum)))
        b2.append(dup2(bb * e_neg).astype(BF16))
        k2.append(dup2(k * e_neg).astype(BF16))
        v2.append(stack2(v_ref[0, rows, sl]).astype(BF16))
        bh2.append(stack2(bb * e_rest).astype(BF16))
        kh2.append(stack2(k * e_rest).astype(BF16))
        decay.append(jnp.exp(tot))
    nc = len(chains)
    gram = [_dot_nt(jnp.concatenate([a2[i], r2[i].astype(BF16)], axis=0), jnp.concatenate([b2[i], k2[i]], axis=0))
            for i in range(nc)]
    ab = [jnp.where(strict, g[0:n2, 0:n2], 0.0) for g in gram]
    ak = [jnp.where(strict, g[0:n2, n2:2 * n2], 0.0).astype(BF16) for g in gram]
    rb = [jnp.where(incl, g[n2:2 * n2, 0:n2], 0.0).astype(BF16) for g in gram]
    rk = [jnp.where(incl, g[n2:2 * n2, n2:2 * n2], 0.0).astype(BF16) for g in gram]
    tinv = [eye] * nc
    for lvl in levels:
        tb = [t.astype(BF16) for t in tinv]
        tm = [_dot(tb[i], jnp.where(lvl, ab[i], 0.0).astype(BF16)).astype(BF16) for i in range(nc)]
        tinv = [tinv[i] - _dot(tm[i], tb[i]) for i in range(nc)]
    tb = [t.astype(BF16) for t in tinv]
    akv = [_dot(ak[i], v2[i]).astype(BF16) for i in range(nc)]
    wu = [_dot(tb[i], jnp.concatenate([a2[i], akv[i]], axis=1)).astype(BF16) for i in range(nc)]
    rwu = [_dot(rb[i], wu[i]) for i in range(nc)]
    rw = [(r2[i] - rwu[i][:, 0:LANES]).astype(BF16) for i in range(nc)]
    y0 = [_dot(rk[i], v2[i]) - rwu[i][:, LANES:2 * LANES] for i in range(nc)]
    gm = [_dot_tn(bh2[i], wu[i][:, 0:LANES]).astype(BF16) for i in range(nc)]
    s0t = [_dot_tn(jnp.concatenate([v2[i], wu[i][:, LANES:2 * LANES]], axis=0),
                   jnp.concatenate([kh2[i], -bh2[i]], axis=0)) for i in range(nc)]
    states = [st_scr[pr] for pr in range(npair)]
    ys = [None] * nc
    for i, (sub, pr) in enumerate(chains):
        st = states[pr]
        stb = st.astype(BF16)
        y2 = _dot_nt(rw[i], stb) + y0[i]
        ys[i] = y2[0:ch, :] + y2[ch:n2, :]
        states[pr] = st * decay[i] - _dot_nt(stb, gm[i]) + s0t[i]
    for pr in range(npair):
        st_scr[pr] = states[pr]
    tstep = nsub * ch
    y_st = jnp.concatenate([ys[sub * npair + pr] for pr in range(npair) for sub in range(nsub)], axis=0)
    rkr_st = jnp.concatenate([r_ref[0, :, pr * LANES:(pr + 1) * LANES] * k_ref[0, :, pr * LANES:(pr + 1) * LANES]
                              * rk_ref[:, pr * LANES:(pr + 1) * LANES] for pr in range(npair)], axis=0)
    sums = _dot_exact_rhs01(jnp.concatenate([y_st, rkr_st], axis=0), hsum)
    yc = y_st - sums[0:npair * tstep, :] * (1.0 / HEAD_DIM)
    var = _dot_exact_rhs01(yc * yc, hsum) * (1.0 / HEAD_DIM)
    yn = yc * lax.rsqrt(var + B_LN_EPS)
    for pr in range(npair):
        sl = slice(pr * LANES, (pr + 1) * LANES)
        rows = slice(pr * tstep, (pr + 1) * tstep)
        bonus = sums[npair * tstep + pr * tstep:npair * tstep + (pr + 1) * tstep, :] * v_ref[0, :, sl]
        o_ref[0, :, sl] = (yn[rows, :] * lnw_ref[:, sl] + lnb_ref[:, sl] + bonus) * g_ref[0, :, sl]


def rwkv_scan(r, k, v, lw, kk, bb, g, r_k, ln_w, ln_b, ch=RWKV_CHUNK, nsub=2):
    b, t, w = r.shape
    ch = min(ch, t)
    nsub = min(nsub, t // ch)
    tstep = ch * nsub
    seq = pl.BlockSpec((1, tstep, w), lambda i, c: (i, c, 0))
    vec = pl.BlockSpec((1, w), lambda i, c: (0, 0))
    row = lambda z: z.reshape(1, w)
    return pl.pallas_call(
        functools.partial(_rwkv_scan_kernel, ch=ch, nsub=nsub),
        grid=(b, t // tstep),
        in_specs=[seq] * 7 + [vec] * 3,
        out_specs=seq,
        out_shape=jax.ShapeDtypeStruct((b, t, w), F32),
        scratch_shapes=[pltpu.VMEM((B_HEADS // 2, LANES, LANES), F32)],
        compiler_params=_cparams("parallel", "arbitrary"),
        name="rwkv_scan",
    )(r, k, v, lw, kk, bb, g, row(r_k), row(ln_w), row(ln_b))


def _fill_vt(v_ref_slice, vt_scr, t):
    sub = lax.broadcasted_iota(jnp.int32, (LANES, LANES), 0)

    def body(j, carry):
        start = pl.multiple_of(j * LANES, LANES)
        vt = v_ref_slice(start).T
        vt_scr[:, pl.ds(start, LANES)] = jnp.where(sub < HEAD_DIM, vt, 1.0).astype(BF16)
        return carry

    lax.fori_loop(0, t // LANES, body, 0)


def _stack_heads(src_ref, dst_scr, n_heads, scale):
    lo, hi = _lane_half_masks((QCHUNK, LANES))
    for h in range(n_heads):
        tile = src_ref[0, :, (h // 2) * LANES:(h // 2 + 1) * LANES]
        if scale != 1.0:
            tile = tile * scale
        dst_scr[h * QCHUNK:(h + 1) * QCHUNK, :] = jnp.where((lo, hi)[h % 2], tile, 0.0).astype(BF16)


def _online_update(scores, vt, m_scr, acc_scr):
    m_old = m_scr[...]
    m_new = jnp.maximum(m_old, jnp.concatenate([jnp.max(s, axis=0, keepdims=True) for s in scores], axis=1))
    p = jnp.concatenate([jnp.exp2(s - m_new[:, h * QCHUNK:(h + 1) * QCHUNK]).astype(BF16)
                         for h, s in enumerate(scores)], axis=1)
    acc_scr[...] = jnp.exp2(m_old - m_new) * acc_scr[...] + _dot(vt, p)
    m_scr[...] = m_new


def _score_stage(k_slab, qst_scr, mask_bias, n_heads, s_scr, mx_scr, slot):
    for pair in range(n_heads // 2):
        d = _dot_nt(k_slab, qst_scr[2 * pair * QCHUNK:(2 * pair + 2) * QCHUNK, :])
        for e in range(2):
            h = 2 * pair + e
            cols = slice(h * QCHUNK, (h + 1) * QCHUNK)
            s = d[:, e * QCHUNK:(e + 1) * QCHUNK] + mask_bias(h)
            s_scr[slot, :, cols] = s
            mx_scr[slot, :, cols] = jnp.max(s, axis=0, keepdims=True)


def _softmax_stage(vt, n_heads, s_scr, mx_scr, slot, m_scr, acc_scr):
    m_old = m_scr[...]
    m_new = jnp.maximum(m_old, mx_scr[slot])
    p = jnp.concatenate([jnp.exp2(s_scr[slot, :, h * QCHUNK:(h + 1) * QCHUNK]
                                  - m_new[:, h * QCHUNK:(h + 1) * QCHUNK]).astype(BF16) for h in range(n_heads)], axis=1)
    acc_scr[...] = jnp.exp2(m_old - m_new) * acc_scr[...] + _dot(vt, p)
    m_scr[...] = m_new


def _pipelined_groups(ngroup, s_matmul, process):
    last = ngroup - 1
    s_matmul(0, 0)

    def pair_body(i, carry):
        g0 = 2 * i
        s_matmul(g0 + 1, 1)
        process(g0, 0)
        s_matmul(jnp.minimum(g0 + 2, last), 0)
        process(g0 + 1, 1)
        return carry

    lax.fori_loop(0, ngroup // 2, pair_body, 0)

    @pl.when(ngroup % 2 == 1)
    def _():
        process(last, 0)


def _bias_rows(tabt_ref, h, d_first, n_tiles):
    tiles = [tabt_ref[jnp.clip(d_first - u, 0, BIAS_TILES - 1), h] for u in range(n_tiles)]
    return jnp.concatenate(tiles, axis=0)


def _write_heads(o_ref, acc_scr, n_heads):
    for p in range(n_heads // 2):
        parts = []
        for h in (2 * p, 2 * p + 1):
            a = acc_scr[:, h * QCHUNK:(h + 1) * QCHUNK]
            parts.append(a[0:HEAD_DIM, :] * (1.0 / a[HEAD_DIM:HEAD_DIM + 1, :]))
        o_ref[0, :, p * LANES:(p + 1) * LANES] = jnp.concatenate(parts, axis=0).T


def _moba_kernel(q_ref, k_ref, v_ref, tabt_ref, o_ref,
                 k_scr, vt_scr, km_scr, qst_scr, pick_scr, s_scr, mx_scr, m_scr, acc_scr):
    c = pl.program_id(2)
    t = k_ref.shape[1]
    nblk = t // MOBA_BLOCK
    nbp = km_scr.shape[0]
    group = C_HEADS // C_KV_HEADS

    @pl.when(c == 0)
    def _():
        kf = k_ref[0]
        k_scr[...] = kf.astype(BF16)
        km_scr[...] = jnp.zeros_like(km_scr)
        km_scr[0:nblk, :] = jnp.mean(kf.reshape(nblk, MOBA_BLOCK, LANES), axis=1)
        _fill_vt(lambda start: v_ref[0, pl.ds(start, LANES), :], vt_scr, t)

    own = (c * QCHUNK) // MOBA_BLOCK
    _stack_heads(q_ref, qst_scr, group, HEAD_DIM ** -0.5 * LOG2E)

    qsum = q_ref[0, :, 0:LANES]
    for p in range(1, group // 2):
        qsum = qsum + q_ref[0, :, p * LANES:(p + 1) * LANES]
    gate = _dot_nt_x(km_scr[...], qsum)
    blk = lax.broadcasted_iota(jnp.int32, (nbp, QCHUNK), 0)
    gate = jnp.where(blk < own, gate, -jnp.inf)
    rank = jnp.zeros((nbp, QCHUNK), jnp.int32)
    for mblk in range(nblk):
        gm = gate[mblk:mblk + 1, :]
        ahead = jnp.logical_or(gm > gate, jnp.logical_and(gm == gate, mblk < blk))
        rank = rank + jnp.where(ahead, 1, 0)
    chosen = jnp.logical_and(rank < MOBA_TOPK, blk < own)
    pick_scr[...] = jnp.where(chosen, 0.0, MASKED)

    m_scr[...] = jnp.full(m_scr.shape, M_INIT, F32)
    acc_scr[...] = jnp.zeros_like(acc_scr)
    grp = 2 * MOBA_BLOCK
    tiles_per_grp = grp // LANES
    ngroup = own // 2 + 1
    grow = lax.broadcasted_iota(jnp.int32, (grp, QCHUNK), 0)
    second = grow >= MOBA_BLOCK
    qpos = c * QCHUNK + lax.broadcasted_iota(jnp.int32, (grp, QCHUNK), 1)

    def score_stage(gi, slot):
        start = pl.multiple_of(gi * grp, grp)
        causal = jnp.where((gi * grp + grow) <= qpos, 0.0, MASKED)
        first = jnp.where(2 * gi < own, pick_scr[pl.ds(2 * gi, 1), :], causal)
        later = jnp.where(2 * gi + 1 < own, pick_scr[pl.ds(2 * gi + 1, 1), :], causal)
        okb = jnp.where(second, later, first)
        d_first = c - tiles_per_grp * gi
        _score_stage(k_scr[pl.ds(start, grp), :], qst_scr,
                     lambda g: _bias_rows(tabt_ref, g, d_first, tiles_per_grp) + okb, group, s_scr, mx_scr, slot)

    def softmax_stage(gi, slot):
        start = pl.multiple_of(gi * grp, grp)
        _softmax_stage(vt_scr[:, pl.ds(start, grp)], group, s_scr, mx_scr, slot, m_scr, acc_scr)

    _pipelined_groups(ngroup, score_stage, softmax_stage)
    _write_heads(o_ref, acc_scr, group)


def moba_attention(qc, kvc, tabt):
    b, t, _ = qc.shape
    assert t % MOBA_BLOCK == 0
    nq = t // QCHUNK
    group = C_HEADS // C_KV_HEADS
    qw = group * HEAD_DIM
    nbp = -(-(t // MOBA_BLOCK) // SUBLANES) * SUBLANES
    return pl.pallas_call(
        _moba_kernel,
        grid=(b, C_KV_HEADS, nq),
        in_specs=[pl.BlockSpec((1, QCHUNK, qw), lambda i, h, c: (i, c, h)),
                  pl.BlockSpec((1, t, LANES), lambda i, h, c: (i, 0, h)),
                  pl.BlockSpec((1, t, LANES), lambda i, h, c: (i, 0, C_KV_HEADS + h)),
                  pl.BlockSpec((BIAS_TILES, group, LANES, LANES), lambda i, h, c: (0, h, 0, 0))],
        out_specs=pl.BlockSpec((1, QCHUNK, qw), lambda i, h, c: (i, c, h)),
        out_shape=jax.ShapeDtypeStruct((b, t, C_HEADS * HEAD_DIM), F32),
        scratch_shapes=[pltpu.VMEM((t, LANES), BF16), pltpu.VMEM((LANES, t), BF16),
                        pltpu.VMEM((nbp, LANES), F32),
                        pltpu.VMEM((group * QCHUNK, LANES), BF16),
                        pltpu.VMEM((nbp, QCHUNK), F32),
                        pltpu.VMEM((2, 2 * MOBA_BLOCK, group * QCHUNK), F32),
                        pltpu.VMEM((2, 1, group * QCHUNK), F32),
                        pltpu.VMEM((1, group * QCHUNK), F32),
                        pltpu.VMEM((LANES, group * QCHUNK), F32)],
        compiler_params=_cparams("parallel", "parallel", "arbitrary"),
        name="moba_attention",
    )(qc, kvc, kvc, tabt)


def _dsa_kernel(q_ref, kv_ref, qi_ref, ki_ref, gk_ref, tabt_ref, o_ref,
                k_scr, vt_scr, ki_scr, key_scr, hi_scr, lo_scr, qst_scr, qist_scr, pre_scr, s_scr, mx_scr, m_scr, acc_scr,
                *, topk):
    c = pl.program_id(1)
    t = k_scr.shape[0]
    grp = KEY_GROUP * LANES
    ngroup = c // KEY_GROUP + 1
    sub = lax.broadcasted_iota(jnp.int32, (LANES, QCHUNK), 0)
    gsub = lax.broadcasted_iota(jnp.int32, (grp, QCHUNK), 0)
    qpos = c * QCHUNK + lax.broadcasted_iota(jnp.int32, (grp, QCHUNK), 1)

    @pl.when(c == 0)
    def _():
        k_scr[...] = kv_ref[0, :, 0:LANES].astype(BF16)
        ki_scr[...] = _rms(ki_ref[0], gk_ref[...]).astype(BF16)
        _fill_vt(lambda start: kv_ref[0, pl.ds(start, LANES), LANES:2 * LANES], vt_scr, t)

    _stack_heads(q_ref, qst_scr, D_HEADS, HEAD_DIM ** -0.5 * LOG2E)
    _stack_heads(qi_ref, qist_scr, IDX_HEADS, 1.0)
    w_t = (qi_ref[0, :, IDX_HEADS * IDX_DIM + LANES:IDX_HEADS * IDX_DIM + 2 * LANES]
           * ((IDX_HEADS * IDX_DIM) ** -0.5)).T

    def score_body(gi, carry):
        start = pl.multiple_of(gi * grp, grp)
        rel = jnp.maximum(_dot_nt(ki_scr[pl.ds(start, grp), :], qist_scr[...]), 0.0)
        sc = jnp.zeros((grp, QCHUNK), F32)
        for h in range(IDX_HEADS):
            sc = sc + rel[:, h * QCHUNK:(h + 1) * QCHUNK] * w_t[h:h + 1, :]
        sc = jnp.where(sc == 0.0, 0.0, sc)
        bits = pltpu.bitcast(sc, jnp.int32)
        skey = bits ^ (lax.shift_right_arithmetic(bits, 31) & 0x7FFFFFFF)
        key = jnp.where((gi * grp + gsub) <= qpos, skey, INT_MIN)
        key_scr[pl.ds(start, grp), :] = key
        hi_scr[pl.ds(start, grp), :] = lax.shift_right_arithmetic(key, 16).astype(jnp.int16)
        lo_scr[pl.ds(start, grp), :] = ((key & 0xFFFF) - HALF_BIAS).astype(jnp.int16)
        return carry

    lax.fori_loop(0, ngroup, score_body, 0)

    def count(pred):
        def body(gi, acc):
            for u in range(KEY_GROUP):
                start = pl.multiple_of(gi * grp + u * LANES, LANES)
                acc = jnp.where(pred(key_scr[pl.ds(start, LANES), :], start + sub), acc + 1, acc)
            return acc
        acc = lax.fori_loop(0, ngroup, body, jnp.zeros((LANES, QCHUNK), jnp.int32))
        return jnp.sum(acc, axis=0, keepdims=True)

    def count16(half_scr, pred):
        def body(gi, acc):
            for u in range(KEY_GROUP):
                start = pl.multiple_of(gi * grp + u * LANES, LANES)
                acc = jnp.where(pred(half_scr[pl.ds(start, LANES), :]), acc + 1, acc)
            return acc
        acc = lax.fori_loop(0, ngroup, body, jnp.zeros((LANES, QCHUNK), jnp.int16))
        return jnp.sum(acc.astype(jnp.int32), axis=0, keepdims=True)

    def kth_half(half_scr, want):
        cnt0 = count16(half_scr, lambda x: x >= 0)
        best = jnp.where(cnt0 >= want, 0, -HALF_BIAS).astype(jnp.int32)

        def bit_body(i, best):
            cand = best + lax.shift_left(jnp.int32(1), 14 - i)
            cnt = count16(half_scr, lambda x: x >= cand.astype(jnp.int16))
            return jnp.where(cnt >= want, cand, best)

        return lax.fori_loop(0, 15, bit_body, best)

    thr_hi = kth_half(hi_scr, topk)
    thr_hi16 = thr_hi.astype(jnp.int16)
    want_lo = topk - count16(hi_scr, lambda x: x > thr_hi16)

    def keep_lo_body(gi, carry):
        start = pl.multiple_of(gi * grp, grp)
        same_hi = hi_scr[pl.ds(start, grp), :] == thr_hi16
        lo_scr[pl.ds(start, grp), :] = jnp.where(same_hi, lo_scr[pl.ds(start, grp), :], jnp.int16(-HALF_BIAS))
        return carry

    lax.fori_loop(0, ngroup, keep_lo_body, 0)
    thr_lo = kth_half(lo_scr, want_lo)
    thr = thr_hi * (2 * HALF_BIAS) + (thr_lo + HALF_BIAS)

    cnt_ge = count(lambda kt, pos: kt >= thr)
    tied = jnp.logical_and(cnt_ge > topk, thr > INT_MIN)

    @pl.when(jnp.max(jnp.where(tied, 1.0, 0.0)) > 0.5)
    def _():
        def tally_body(j, carry):
            ties_before, above = carry
            start = pl.multiple_of(j * LANES, LANES)
            kt = key_scr[pl.ds(start, LANES), :]
            pre_scr[pl.ds(j, 1), :] = ties_before
            ties_before = ties_before + jnp.sum(jnp.where(kt == thr, 1.0, 0.0), axis=0, keepdims=True)
            above = above + jnp.sum(jnp.where(kt > thr, 1.0, 0.0), axis=0, keepdims=True)
            return ties_before, above

        zero_row = jnp.zeros((1, QCHUNK), F32)
        _, above = lax.fori_loop(0, ngroup * KEY_GROUP, tally_body, (zero_row, zero_row))
        need = jnp.where(tied, topk - above, 3.0e38)
        tri = jnp.where(lax.broadcasted_iota(jnp.int32, (LANES, LANES), 1) <= sub, 1.0, 0.0).astype(BF16)

        def demote_body(gi, carry):
            tiles = []
            for u in range(KEY_GROUP):
                start = pl.multiple_of(gi * grp + u * LANES, LANES)
                tiles.append(key_scr[pl.ds(start, LANES), :])
            eq01 = jnp.concatenate([jnp.where(kt == thr, 1.0, 0.0).astype(BF16) for kt in tiles], axis=1)
            within = _dot(tri, eq01)
            for u, kt in enumerate(tiles):
                start = pl.multiple_of(gi * grp + u * LANES, LANES)
                rank = pre_scr[pl.ds(gi * KEY_GROUP + u, 1), :] + within[:, u * QCHUNK:(u + 1) * QCHUNK]
                drop = jnp.logical_and(kt == thr, rank > need)
                key_scr[pl.ds(start, LANES), :] = jnp.where(drop, kt - 1, kt)
            return carry

        lax.fori_loop(0, ngroup, demote_body, 0)

    thr_eff = jnp.maximum(thr, INT_MIN + 1)

    m_scr[...] = jnp.full(m_scr.shape, M_INIT, F32)
    acc_scr[...] = jnp.zeros_like(acc_scr)

    def score_stage(gi, slot):
        start = pl.multiple_of(gi * grp, grp)
        okb = jnp.where(key_scr[pl.ds(start, grp), :] >= thr_eff, 0.0, MASKED)
        d_first = c - gi * KEY_GROUP
        _score_stage(k_scr[pl.ds(start, grp), :], qst_scr,
                     lambda h: _bias_rows(tabt_ref, h, d_first, KEY_GROUP) + okb, D_HEADS, s_scr, mx_scr, slot)

    def softmax_stage(gi, slot):
        start = pl.multiple_of(gi * grp, grp)
        _softmax_stage(vt_scr[:, pl.ds(start, grp)], D_HEADS, s_scr, mx_scr, slot, m_scr, acc_scr)

    _pipelined_groups(ngroup, score_stage, softmax_stage)
    _write_heads(o_ref, acc_scr, D_HEADS)


def dsa_attention(qd, kvd, idx, kidx_gain, tabt):
    b, t, _ = qd.shape
    topk = min(DSA_TOPK, t // 4)
    nq = t // QCHUNK
    gk = jnp.concatenate([kidx_gain, kidx_gain]).reshape(1, LANES)
    wq = D_HEADS * HEAD_DIM
    return pl.pallas_call(
        functools.partial(_dsa_kernel, topk=topk),
        grid=(b, nq),
        in_specs=[pl.BlockSpec((1, QCHUNK, wq), lambda i, c: (i, c, 0)),
                  pl.BlockSpec((1, t, 2 * LANES), lambda i, c: (i, 0, 0)),
                  pl.BlockSpec((1, QCHUNK, 4 * LANES), lambda i, c: (i, c, 0)),
                  pl.BlockSpec((1, t, LANES), lambda i, c: (i, 0, 2)),
                  pl.BlockSpec((1, LANES), lambda i, c: (0, 0)),
                  pl.BlockSpec((BIAS_TILES, D_HEADS, LANES, LANES), lambda i, c: (0, 0, 0, 0))],
        out_specs=pl.BlockSpec((1, QCHUNK, wq), lambda i, c: (i, c, 0)),
        out_shape=jax.ShapeDtypeStruct((b, t, wq), F32),
        scratch_shapes=[pltpu.VMEM((t, LANES), BF16), pltpu.VMEM((LANES, t), BF16), pltpu.VMEM((t, LANES), BF16),
                        pltpu.VMEM((t, QCHUNK), jnp.int32),
                        pltpu.VMEM((t, QCHUNK), jnp.int16), pltpu.VMEM((t, QCHUNK), jnp.int16),
                        pltpu.VMEM((D_HEADS * QCHUNK, LANES), BF16), pltpu.VMEM((IDX_HEADS * QCHUNK, LANES), BF16),
                        pltpu.VMEM((t // LANES, QCHUNK), F32),
                        pltpu.VMEM((2, KEY_GROUP * LANES, D_HEADS * QCHUNK), F32),
                        pltpu.VMEM((2, 1, D_HEADS * QCHUNK), F32),
                        pltpu.VMEM((1, D_HEADS * QCHUNK), F32),
                        pltpu.VMEM((LANES, D_HEADS * QCHUNK), F32)],
        compiler_params=_cparams("parallel", "arbitrary"),
        name="dsa_attention",
    )(qd, kvd, idx, idx, gk, tabt)


def _dup_heads(wcols, n_heads):
    d = wcols.shape[0]
    wh = wcols.reshape(d, n_heads, 1, HEAD_DIM)
    return jnp.broadcast_to(wh, (d, n_heads, 2, HEAD_DIM)).reshape(d, n_heads * 2 * HEAD_DIM)


def _even_weight(w_in):
    aq, akv = A_HEADS * HEAD_DIM, A_KV_HEADS * HEAD_DIM
    q = w_in[:, :aq]
    k = _dup_heads(w_in[:, aq:aq + akv], A_KV_HEADS)
    v = _dup_heads(w_in[:, aq + akv:aq + 2 * akv], A_KV_HEADS)
    rest = w_in[:, aq + 2 * akv:]
    return jnp.concatenate([q, k, v, rest], axis=1).astype(BF16)


def _odd_weight(w_in):
    d = w_in.shape[0]
    cq, ckv, dq = C_HEADS * HEAD_DIM, C_KV_HEADS * HEAD_DIM, D_HEADS * HEAD_DIM
    cuts = np.cumsum([cq, ckv, ckv, dq, HEAD_DIM, HEAD_DIM, IDX_HEADS * IDX_DIM, IDX_DIM]).tolist()
    qc, kc, vc, qd, kd, vd, qi, ki, wi = jnp.split(w_in, cuts, axis=1)
    pad = jnp.zeros((d, LANES - IDX_HEADS), w_in.dtype)
    cols = [qc, _dup_heads(kc, C_KV_HEADS), _dup_heads(vc, C_KV_HEADS),
            qd, _dup_heads(kd, 1), _dup_heads(vd, 1),
            qi, _dup_heads(ki, 1), wi, pad]
    return jnp.concatenate(cols, axis=1).astype(BF16)


def kernel(x, mem, rel_bias, norm_gains, mix_w_out, mem_norm, x_wq, x_wkv, x_wo, ff_w1, ff_w2, ev_w_in, a_sinks, b_mu, b_w0, b_w2, b_a0, b_a2, b_g2, b_k_k, b_k_a, b_r_k, b_ln_w, b_ln_b, od_w_in, d_kidx_norm):
    bsz, t, d = x.shape
    n = bsz * t
    depth = norm_gains.shape[0]
    mlen = mem.shape[1]
    tab = bias_table(rel_bias)
    tabt = bias_table(rel_bias, transposed=True)
    x2 = x.reshape(n, d)
    for layer in range(depth):
        i = layer // 2
        g = norm_gains[layer]
        if layer % 2 == 0:
            b_in = 3 * B_WIDTH + B_DECAY_RANK + B_A_RANK + B_GATE_RANK
            qkv, pb = norm_proj(x2, g[0], _even_weight(ev_w_in[i]), (1024, b_in), (F32, F32))
            ya = swa_attention(qkv.reshape(bsz, t, 1024), a_sinks[i], tab)
            prep = rwkv_prep(pb.reshape(bsz, t, b_in), b_mu[i], b_w0[i], b_w2[i], b_a0[i], b_a2[i], b_g2[i],
                             b_k_k[i], b_k_a[i])
            yb = rwkv_scan(*prep, b_r_k[i].reshape(-1), b_ln_w[i], b_ln_b[i])
        else:
            qc, kvc, qd, kvd, idx = norm_proj(x2, g[0], _odd_weight(od_w_in[i]),
                                              (512, 512, 512, 256, 512), (F32,) * 5)
            ya = moba_attention(qc.reshape(bsz, t, 512), kvc.reshape(bsz, t, 512), tabt)
            yb = dsa_attention(qd.reshape(bsz, t, 512), kvd.reshape(bsz, t, 256), idx.reshape(bsz, t, 512),
                               d_kidx_norm[i], tabt)
        x2 = proj_norm_res(ya.reshape(n, -1), yb.reshape(n, -1), mix_w_out[layer], g[1], x2)
        kmem, vmem = norm_proj(mem.reshape(bsz * mlen, d), mem_norm[layer], x_wkv[layer].astype(BF16),
                               (X_WIDTH, X_WIDTH), (BF16, BF16))
        x3 = cross_attn(x2.reshape(bsz, t, d), kmem.reshape(bsz, mlen, X_WIDTH), vmem.reshape(bsz, mlen, X_WIDTH),
                        x_wq[layer], x_wo[layer], g[2], g[3])
        x2 = mlp(x3.reshape(n, d), ff_w1[layer], ff_w2[layer], g[4], g[5])
    return x2.reshape(bsz, t, d)
```

```python
import functools

import numpy as np
import jax
import jax.numpy as jnp
from jax import lax
from jax.experimental import pallas as pl
from jax.experimental.pallas import tpu as pltpu

F32 = jnp.float32
BF16 = jnp.bfloat16
HIGHEST = lax.Precision.HIGHEST

LANES = 128
SUBLANES = 8
HEAD_DIM = 64
EPS = 1e-6
A_HEADS, A_KV_HEADS, BAND = 8, 2, 128
B_HEADS, B_WIDTH = 8, 512
B_DECAY_RANK, B_A_RANK, B_GATE_RANK = 64, 64, 128
B_LN_EPS = 64e-5
RWKV_CHUNK = 64
C_HEADS, C_KV_HEADS, MOBA_BLOCK, MOBA_TOPK, QCHUNK = 8, 2, 256, 3, 128
D_HEADS, IDX_HEADS, IDX_DIM, DSA_TOPK = 8, 4, 64, 256
N_BUCKETS, BIAS_HEADS = 32, 8
BIAS_TILES = 9
X_HEADS, X_HEAD_DIM, X_WIDTH = 4, 128, 512

LOG2E = 1.4426950408889634
KEY_GROUP = 4
MASKED = -2e30
M_INIT = -1e30
INT_MIN = -(2 ** 31)
HALF_BIAS = 2 ** 15
VMEM_LIMIT = 56 * 1024 * 1024


def _cparams(*sem):
    return pltpu.CompilerParams(dimension_semantics=tuple(sem), vmem_limit_bytes=VMEM_LIMIT)


def _dot(a, b):
    return jnp.dot(a, b, preferred_element_type=F32)


def _dot_nt(a, b):
    return lax.dot_general(a, b, (((1,), (1,)), ((), ())), preferred_element_type=F32)


def _dot_tn(a, b):
    return lax.dot_general(a, b, (((0,), (0,)), ((), ())), preferred_element_type=F32)


def _dot_x(a, b):
    return jnp.dot(a, b, preferred_element_type=F32, precision=HIGHEST)


def _dot_nt_x(a, b):
    return lax.dot_general(a, b, (((1,), (1,)), ((), ())), preferred_element_type=F32, precision=HIGHEST)


def _rms(xf, g):
    ms = jnp.mean(xf * xf, axis=-1, keepdims=True)
    return xf * lax.rsqrt(ms + EPS) * g


def _lane_half_masks(shape):
    lane = lax.broadcasted_iota(jnp.int32, shape, len(shape) - 1)
    lo = lane < HEAD_DIM
    return lo, jnp.logical_not(lo)


def _bucket_of_distance(n):
    exact = N_BUCKETS // 2
    if n < exact:
        return n
    j = 0
    while n ** 8 >= (exact ** 8) * (2 ** (3 * (j + 1))):
        j += 1
    return min(exact + j, N_BUCKETS - 1)


@functools.lru_cache(maxsize=None)
def _bucket_tiles(transposed):
    by_dist = np.array([_bucket_of_distance(n) for n in range(BIAS_TILES * LANES + LANES)], np.int32)
    q = np.arange(LANES)[None, :] if transposed else np.arange(LANES)[:, None]
    k = np.arange(LANES)[:, None] if transposed else np.arange(LANES)[None, :]
    tiles = [by_dist[np.maximum(d * LANES + q - k, 0)] for d in range(BIAS_TILES)]
    return np.stack(tiles).astype(np.int32)


def _bias_table_kernel(rb_ref, bkt_ref, out_ref, *, base2_shifted):
    bkt = bkt_ref[0]
    for h in range(BIAS_HEADS):
        acc = jnp.zeros((LANES, LANES), F32)
        for b in range(N_BUCKETS):
            acc = jnp.where(bkt == b, rb_ref[b, h], acc)
        if base2_shifted:
            acc = (acc - rb_ref[N_BUCKETS - 1, h]) * LOG2E
        out_ref[0, h] = acc


def bias_table(rel_bias, transposed=False):
    return pl.pallas_call(
        functools.partial(_bias_table_kernel, base2_shifted=transposed),
        grid=(BIAS_TILES,),
        in_specs=[pl.BlockSpec(memory_space=pltpu.SMEM),
                  pl.BlockSpec((1, LANES, LANES), lambda d: (d, 0, 0))],
        out_specs=pl.BlockSpec((1, BIAS_HEADS, LANES, LANES), lambda d: (d, 0, 0, 0)),
        out_shape=jax.ShapeDtypeStruct((BIAS_TILES, BIAS_HEADS, LANES, LANES), F32),
        compiler_params=_cparams("arbitrary"),
        name="bias_table",
    )(rel_bias, jnp.asarray(_bucket_tiles(transposed)))


def _norm_proj_kernel(x_ref, g_ref, w_ref, *out_refs, splits):
    h = _rms(x_ref[...], g_ref[...]).astype(BF16)
    res = _dot(h, w_ref[...])
    off = 0
    for o_ref, width in zip(out_refs, splits):
        o_ref[...] = res[:, off:off + width].astype(o_ref.dtype)
        off += width


def norm_proj(x2, gain, w, splits, out_dtypes, tm=256):
    n, d = x2.shape
    tm = min(tm, n)
    total = sum(splits)
    assert w.shape == (d, total) and n % tm == 0 and all(s % LANES == 0 for s in splits)
    return pl.pallas_call(
        functools.partial(_norm_proj_kernel, splits=tuple(splits)),
        grid=(n // tm,),
        in_specs=[pl.BlockSpec((tm, d), lambda i: (i, 0)),
                  pl.BlockSpec((1, d), lambda i: (0, 0)),
                  pl.BlockSpec((d, total), lambda i: (0, 0))],
        out_specs=[pl.BlockSpec((tm, s), lambda i: (i, 0)) for s in splits],
        out_shape=[jax.ShapeDtypeStruct((n, s), dt) for s, dt in zip(splits, out_dtypes)],
        compiler_params=_cparams("parallel"),
        name="norm_proj",
    )(x2, gain.reshape(1, d), w)


def _proj_norm_res_kernel(ya_ref, yb_ref, wa_ref, wb_ref, g_ref, x_ref, o_ref):
    z = _dot(ya_ref[...].astype(BF16), wa_ref[...]) + _dot(yb_ref[...].astype(BF16), wb_ref[...])
    o_ref[...] = x_ref[...] + _rms(z, g_ref[...])


def proj_norm_res(ya, yb, w, gain, x2, tm=512):
    n, d = x2.shape
    tm = min(tm, n)
    ka, kb = ya.shape[1], yb.shape[1]
    wa, wb = w[:ka].astype(BF16), w[ka:].astype(BF16)
    return pl.pallas_call(
        _proj_norm_res_kernel,
        grid=(n // tm,),
        in_specs=[pl.BlockSpec((tm, ka), lambda i: (i, 0)),
                  pl.BlockSpec((tm, kb), lambda i: (i, 0)),
                  pl.BlockSpec((ka, d), lambda i: (0, 0)),
                  pl.BlockSpec((kb, d), lambda i: (0, 0)),
                  pl.BlockSpec((1, d), lambda i: (0, 0)),
                  pl.BlockSpec((tm, d), lambda i: (i, 0))],
        out_specs=pl.BlockSpec((tm, d), lambda i: (i, 0)),
        out_shape=jax.ShapeDtypeStruct((n, d), F32),
        compiler_params=_cparams("parallel"),
        name="proj_norm_res",
    )(ya, yb, wa, wb, gain.reshape(1, d), x2)


def _cross_attn_kernel(x_ref, k_ref, v_ref, wq_ref, wo_ref, gpre_ref, gpost_ref, o_ref):
    x = x_ref[0]
    h = _rms(x, gpre_ref[...]).astype(BF16)
    q = _dot(h, wq_ref[...])
    k = k_ref[0]
    v = v_ref[0]
    outs = []
    for hd in range(X_HEADS):
        sl = slice(hd * X_HEAD_DIM, (hd + 1) * X_HEAD_DIM)
        s = _dot_nt(q[:, sl].astype(BF16), k[:, sl]) * (X_HEAD_DIM ** -0.5)
        m = jnp.max(s, axis=-1, keepdims=True)
        p = jnp.exp(s - m)
        p = p / jnp.sum(p, axis=-1, keepdims=True)
        outs.append(_dot(p.astype(BF16), v[:, sl]))
    o = jnp.concatenate(outs, axis=-1).astype(BF16)
    z = _dot(o, wo_ref[...])
    o_ref[0] = x + _rms(z, gpost_ref[...])


def cross_attn(x3, kmem, vmem, wq, wo, gpre, gpost, tq=512):
    b, t, d = x3.shape
    tq = min(tq, t)
    m = kmem.shape[1]
    return pl.pallas_call(
        _cross_attn_kernel,
        grid=(b, t // tq),
        in_specs=[pl.BlockSpec((1, tq, d), lambda i, j: (i, j, 0)),
                  pl.BlockSpec((1, m, X_WIDTH), lambda i, j: (i, 0, 0)),
                  pl.BlockSpec((1, m, X_WIDTH), lambda i, j: (i, 0, 0)),
                  pl.BlockSpec((d, X_WIDTH), lambda i, j: (0, 0)),
                  pl.BlockSpec((X_WIDTH, d), lambda i, j: (0, 0)),
                  pl.BlockSpec((1, d), lambda i, j: (0, 0)),
                  pl.BlockSpec((1, d), lambda i, j: (0, 0))],
        out_specs=pl.BlockSpec((1, tq, d), lambda i, j: (i, j, 0)),
        out_shape=jax.ShapeDtypeStruct((b, t, d), F32),
        compiler_params=_cparams("parallel", "parallel"),
        name="cross_attn",
    )(x3, kmem, vmem, wq.astype(BF16), wo.astype(BF16), gpre.reshape(1, d), gpost.reshape(1, d))


def _mlp_kernel(x_ref, w1_ref, w2_ref, gpre_ref, gpost_ref, o_ref, *, tf):
    x = x_ref[...]
    h = _rms(x, gpre_ref[...]).astype(BF16)
    acc = jnp.zeros(x.shape, F32)
    for j in range(w1_ref.shape[1] // tf):
        a = jnp.maximum(_dot(h, w1_ref[:, j * tf:(j + 1) * tf]), 0.0)
        acc = acc + _dot((a * a).astype(BF16), w2_ref[j * tf:(j + 1) * tf, :])
    o_ref[...] = x + _rms(acc, gpost_ref[...])


def mlp(x2, w1, w2, gpre, gpost, tm=512, tf=1024):
    n, d = x2.shape
    tm = min(tm, n)
    f = w1.shape[1]
    once = pl.Buffered(1)
    return pl.pallas_call(
        functools.partial(_mlp_kernel, tf=tf),
        grid=(n // tm,),
        in_specs=[pl.BlockSpec((tm, d), lambda i: (i, 0)),
                  pl.BlockSpec((d, f), lambda i: (0, 0), pipeline_mode=once),
                  pl.BlockSpec((f, d), lambda i: (0, 0), pipeline_mode=once),
                  pl.BlockSpec((1, d), lambda i: (0, 0)),
                  pl.BlockSpec((1, d), lambda i: (0, 0))],
        out_specs=pl.BlockSpec((tm, d), lambda i: (i, 0)),
        out_shape=jax.ShapeDtypeStruct((n, d), F32),
        compiler_params=_cparams("parallel"),
        name="mlp",
    )(x2, w1.astype(BF16), w2.astype(BF16), gpre.reshape(1, d), gpost.reshape(1, d))


def _swa_kernel(sink_ref, rb_ref, q_ref, kv_ref, tabt_ref, o_ref, qst_scr, vt_scr):
    n = pl.program_id(1)
    group = A_HEADS // A_KV_HEADS
    first_band = jnp.maximum(n - 1, 0)
    start = pl.multiple_of(first_band * BAND, BAND)
    slab = 2 * BAND
    _stack_heads(q_ref, qst_scr, A_HEADS, HEAD_DIM ** -0.5 * LOG2E)
    kpos = start + lax.broadcasted_iota(jnp.int32, (slab, BAND), 0)
    qpos = n * BAND + lax.broadcasted_iota(jnp.int32, (slab, BAND), 1)
    dist = qpos - kpos
    okb = jnp.where(jnp.logical_and(dist >= 0, dist < BAND), 0.0, MASKED)
    sub = lax.broadcasted_iota(jnp.int32, (LANES, slab), 0)
    koff = 0
    voff = A_KV_HEADS * LANES
    d_first = n - first_band
    scores, vts = [], []
    for hkv in range(A_KV_HEADS):
        k2 = kv_ref[0, pl.ds(start, slab), koff + hkv * LANES:koff + (hkv + 1) * LANES].astype(BF16)
        v2 = kv_ref[0, pl.ds(start, slab), voff + hkv * LANES:voff + (hkv + 1) * LANES]
        vt_scr[hkv] = jnp.where(sub < HEAD_DIM, v2.T, 1.0).astype(BF16)
        vts.append(vt_scr[hkv])
        s_all = _dot_nt(k2, qst_scr[hkv * group * BAND:(hkv + 1) * group * BAND, :])
        for g in range(group):
            h = hkv * group + g
            scores.append(s_all[:, g * BAND:(g + 1) * BAND] + (_bias_rows(tabt_ref, h, d_first, 2) + okb))
    sinks = [(sink_ref[h] - rb_ref[N_BUCKETS - 1, h]) * LOG2E for h in range(A_HEADS)]
    m = [jnp.maximum(jnp.max(scores[h], axis=0, keepdims=True), sinks[h]) for h in range(A_HEADS)]
    p = [jnp.exp2(scores[h] - m[h]).astype(BF16) for h in range(A_HEADS)]
    outs = []
    for hkv in range(A_KV_HEADS):
        pv = _dot(vts[hkv], jnp.concatenate(p[hkv * group:(hkv + 1) * group], axis=1))
        for g in range(group):
            h = hkv * group + g
            a = pv[:, g * BAND:(g + 1) * BAND]
            den = a[HEAD_DIM:HEAD_DIM + 1, :] + jnp.exp2(sinks[h] - m[h])
            outs.append(a[0:HEAD_DIM, :] * (1.0 / den))
    for pair in range(A_HEADS // 2):
        o_ref[0, :, pair * LANES:(pair + 1) * LANES] = jnp.concatenate(outs[2 * pair:2 * pair + 2], axis=0).T


def swa_attention(qkv, sinks, rel_bias, tabt):
    b, t, w = qkv.shape
    qw = A_HEADS * HEAD_DIM
    return pl.pallas_call(
        _swa_kernel,
        grid=(b, t // BAND),
        in_specs=[pl.BlockSpec(memory_space=pltpu.SMEM),
                  pl.BlockSpec(memory_space=pltpu.SMEM),
                  pl.BlockSpec((1, BAND, qw), lambda i, n: (i, n, 0)),
                  pl.BlockSpec((1, t, w - qw), lambda i, n: (i, 0, 1)),
                  pl.BlockSpec((2, A_HEADS, LANES, LANES), lambda i, n: (0, 0, 0, 0))],
        out_specs=pl.BlockSpec((1, BAND, qw), lambda i, n: (i, n, 0)),
        out_shape=jax.ShapeDtypeStruct((b, t, qw), F32),
        scratch_shapes=[pltpu.VMEM((A_HEADS * BAND, LANES), BF16), pltpu.VMEM((A_KV_HEADS, LANES, 2 * BAND), BF16)],
        compiler_params=_cparams("parallel", "arbitrary"),
        name="swa_attention",
    )(sinks, rel_bias, qkv, qkv, tabt)


def _split3(x):
    p0 = x.astype(BF16)
    r1 = x - p0.astype(F32)
    p1 = r1.astype(BF16)
    p2 = (r1 - p1.astype(F32)).astype(BF16)
    return p0, p1, p2


def _dot_exact_rhs01(x, m01):
    p0, p1, p2 = _split3(x)
    return _dot(p0, m01) + _dot(p1, m01) + _dot(p2, m01)


def _dot_exact_lhs01(m01, x):
    p0, p1, p2 = _split3(x)
    return _dot(m01, p0) + _dot(m01, p1) + _dot(m01, p2)


def _head_sum_matrix(width):
    r = lax.broadcasted_iota(jnp.int32, (width, width), 0) // HEAD_DIM
    c = lax.broadcasted_iota(jnp.int32, (width, width), 1) // HEAD_DIM
    return jnp.where(r == c, 1.0, 0.0).astype(BF16)


def _sigmoid(z):
    return 1.0 / (1.0 + jnp.exp(-z))


def _softplus(z):
    return jnp.maximum(z, 0.0) + jnp.log(1.0 + jnp.exp(-jnp.abs(z)))


def _rwkv_prep_kernel(p_ref, pprev_ref, mu_ref, wa2_ref, g2_ref, w0_ref, a0_ref, kk_ref, ka_ref,
                      r_out, k_out, v_out, lw_out, kk_out, b_out, g_out):
    i = pl.program_id(1)
    p = p_ref[0]
    w = B_WIDTH
    last = pprev_ref[0, SUBLANES - 1:SUBLANES, :]
    last = jnp.where(i > 0, last, 0.0)
    row = lax.broadcasted_iota(jnp.int32, p.shape, 0)
    prev = jnp.where(row == 0, last, pltpu.roll(p, 1, 0))
    xs = p + (prev - p) * mu_ref[...]
    r = xs[:, 0:w]
    k = xs[:, w:2 * w]
    v = xs[:, 2 * w:3 * w]
    wa_lo = xs[:, 3 * w:3 * w + LANES]
    g_lo = xs[:, 3 * w + LANES:3 * w + 2 * LANES]
    lo, _ = _lane_half_masks(wa_lo.shape)
    wa_in = jnp.where(lo, jnp.tanh(wa_lo), wa_lo)
    wa = _dot_x(wa_in, wa2_ref[...])
    wlog = -_softplus(-(w0_ref[...] + wa[:, 0:w])) - 0.5
    a = _sigmoid(a0_ref[...] + wa[:, w:2 * w])
    g = _dot(_sigmoid(g_lo).astype(BF16), g2_ref[...])
    kk = k * kk_ref[...]
    ss = _dot_exact_rhs01(kk * kk, _head_sum_matrix(w))
    kk = kk / jnp.maximum(jnp.sqrt(ss), 1e-12)
    kmod = k * (1.0 + (a - 1.0) * ka_ref[...])
    r_out[0] = r
    k_out[0] = kmod
    v_out[0] = v
    lw_out[0] = -jnp.exp(wlog)
    kk_out[0] = kk
    b_out[0] = kk * a
    g_out[0] = g


def rwkv_prep(pb, mu, w0, w2, a0, a2, g2, k_k, k_a, tc=256):
    b, t, win = pb.shape
    tc = min(tc, t)
    w = B_WIDTH
    wa2 = jnp.zeros((LANES, 2 * w), F32)
    wa2 = wa2.at[:B_DECAY_RANK, :w].set(w2).at[B_DECAY_RANK:, w:].set(a2)
    row = lambda z: z.reshape(1, -1)
    vec = pl.BlockSpec((1, w), lambda i, j: (0, 0))
    outs = [jax.ShapeDtypeStruct((b, t, w), F32)] * 7
    return pl.pallas_call(
        _rwkv_prep_kernel,
        grid=(b, t // tc),
        in_specs=[pl.BlockSpec((1, tc, win), lambda i, j: (i, j, 0)),
                  pl.BlockSpec((1, SUBLANES, win), lambda i, j: (i, jnp.maximum(j * (tc // SUBLANES) - 1, 0), 0)),
                  pl.BlockSpec((1, win), lambda i, j: (0, 0)),
                  pl.BlockSpec((LANES, 2 * w), lambda i, j: (0, 0)),
                  pl.BlockSpec((B_GATE_RANK, w), lambda i, j: (0, 0)),
                  vec, vec, vec, vec],
        out_specs=[pl.BlockSpec((1, tc, w), lambda i, j: (i, j, 0))] * 7,
        out_shape=outs,
        compiler_params=_cparams("parallel", "arbitrary"),
        name="rwkv_prep",
    )(pb, pb, row(mu), wa2, g2.astype(BF16), row(w0), row(a0), row(k_k), row(k_a))


def _rwkv_scan_kernel(r_ref, k_ref, v_ref, lw_ref, kk_ref, b_ref, g_ref, rk_ref, lnw_ref, lnb_ref,
                      o_ref, st_scr, *, ch, nsub):
    step = pl.program_id(1)
    n2 = 2 * ch

    @pl.when(step == 0)
    def _():
        st_scr[...] = jnp.zeros_like(st_scr)

    ti = lax.broadcasted_iota(jnp.int32, (n2, n2), 0)
    si = lax.broadcasted_iota(jnp.int32, (n2, n2), 1)
    same = (ti // ch) == (si // ch)
    incl = jnp.logical_and(same, si <= ti)
    strict = jnp.logical_and(same, si < ti)
    eye = jnp.where(si == ti, 1.0, 0.0)
    levels = []
    s = 1
    while s < ch:
        same2s = (ti // (2 * s)) == (si // (2 * s))
        levels.append(jnp.logical_and(same2s, jnp.logical_and((ti // s) % 2 == 1, (si // s) % 2 == 0)))
        s *= 2
    tr = lax.broadcasted_iota(jnp.int32, (ch, ch), 0)
    tc = lax.broadcasted_iota(jnp.int32, (ch, ch), 1)
    tri01 = jnp.where(tc <= tr, 1.0, 0.0).astype(BF16)
    lo, hi = _lane_half_masks((ch, LANES))
    hsum = _head_sum_matrix(LANES)

    def stack2(z):
        return jnp.concatenate([jnp.where(lo, z, 0.0), jnp.where(hi, z, 0.0)], axis=0)

    def dup2(z):
        return jnp.concatenate([z, z], axis=0)

    npair = B_HEADS // 2
    chains = [(sub, pr) for sub in range(nsub) for pr in range(npair)]
    cums = [_dot_exact_lhs01(tri01, lw_ref[0, sub * ch:(sub + 1) * ch, :]) for sub in range(nsub)]
    a2, r2, b2, k2, v2, bh2, kh2, decay = [], [], [], [], [], [], [], []
    for sub, pr in chains:
        rows = slice(sub * ch, (sub + 1) * ch)
        sl = slice(pr * LANES, (pr + 1) * LANES)
        k, bb = k_ref[0, rows, sl], b_ref[0, rows, sl]
        cum = cums[sub][:, sl]
        tot = cum[ch - 1:ch, :]
        e_neg = jnp.exp(-cum)
        e_rest = jnp.exp(tot - cum)
        a2.append(stack2(kk_ref[0, rows, sl] * jnp.exp(cum - lw_ref[0, rows, sl])).astype(BF16))
        r2.append(stack2(r_ref[0, rows, sl] * jnp.exp(cum)))
        b2.append(dup2(bb * e_neg).astype(BF16))
        k2.append(dup2(k * e_neg).astype(BF16))
        v2.append(stack2(v_ref[0, rows, sl]).astype(BF16))
        bh2.append(stack2(bb * e_rest).astype(BF16))
        kh2.append(stack2(k * e_rest).astype(BF16))
        decay.append(jnp.exp(tot))
    nc = len(chains)
    gram = [_dot_nt(jnp.concatenate([a2[i], r2[i].astype(BF16)], axis=0), jnp.concatenate([b2[i], k2[i]], axis=0))
            for i in range(nc)]
    ab = [jnp.where(strict, g[0:n2, 0:n2], 0.0) for g in gram]
    ak = [jnp.where(strict, g[0:n2, n2:2 * n2], 0.0).astype(BF16) for g in gram]
    rb = [jnp.where(incl, g[n2:2 * n2, 0:n2], 0.0).astype(BF16) for g in gram]
    rk = [jnp.where(incl, g[n2:2 * n2, n2:2 * n2], 0.0).astype(BF16) for g in gram]
    tinv = [eye] * nc
    for lvl in levels:
        tb = [t.astype(BF16) for t in tinv]
        tm = [_dot(tb[i], jnp.where(lvl, ab[i], 0.0).astype(BF16)).astype(BF16) for i in range(nc)]
        tinv = [tinv[i] - _dot(tm[i], tb[i]) for i in range(nc)]
    tb = [t.astype(BF16) for t in tinv]
    akv = [_dot(ak[i], v2[i]).astype(BF16) for i in range(nc)]
    wu = [_dot(tb[i], jnp.concatenate([a2[i], akv[i]], axis=1)).astype(BF16) for i in range(nc)]
    rwu = [_dot(rb[i], wu[i]) for i in range(nc)]
    rw = [(r2[i] - rwu[i][:, 0:LANES]).astype(BF16) for i in range(nc)]
    y0 = [_dot(rk[i], v2[i]) - rwu[i][:, LANES:2 * LANES] for i in range(nc)]
    gm = [_dot_tn(bh2[i], wu[i][:, 0:LANES]).astype(BF16) for i in range(nc)]
    s0t = [_dot_tn(jnp.concatenate([v2[i], wu[i][:, LANES:2 * LANES]], axis=0),
                   jnp.concatenate([kh2[i], -bh2[i]], axis=0)) for i in range(nc)]
    states = [st_scr[pr] for pr in range(npair)]
    ys = [None] * nc
    for i, (sub, pr) in enumerate(chains):
        st = states[pr]
        stb = st.astype(BF16)
        y2 = _dot_nt(rw[i], stb) + y0[i]
        ys[i] = y2[0:ch, :] + y2[ch:n2, :]
        states[pr] = st * decay[i] - _dot_nt(stb, gm[i]) + s0t[i]
    for pr in range(npair):
        st_scr[pr] = states[pr]
    tstep = nsub * ch
    y_st = jnp.concatenate([ys[sub * npair + pr] for pr in range(npair) for sub in range(nsub)], axis=0)
    rkr_st = jnp.concatenate([r_ref[0, :, pr * LANES:(pr + 1) * LANES] * k_ref[0, :, pr * LANES:(pr + 1) * LANES]
                              * rk_ref[:, pr * LANES:(pr + 1) * LANES] for pr in range(npair)], axis=0)
    sums = _dot_exact_rhs01(jnp.concatenate([y_st, rkr_st], axis=0), hsum)
    yc = y_st - sums[0:npair * tstep, :] * (1.0 / HEAD_DIM)
    var = _dot_exact_rhs01(yc * yc, hsum) * (1.0 / HEAD_DIM)
    yn = yc * lax.rsqrt(var + B_LN_EPS)
    for pr in range(npair):
        sl = slice(pr * LANES, (pr + 1) * LANES)
        rows = slice(pr * tstep, (pr + 1) * tstep)
        bonus = sums[npair * tstep + pr * tstep:npair * tstep + (pr + 1) * tstep, :] * v_ref[0, :, sl]
        o_ref[0, :, sl] = (yn[rows, :] * lnw_ref[:, sl] + lnb_ref[:, sl] + bonus) * g_ref[0, :, sl]


def rwkv_scan(r, k, v, lw, kk, bb, g, r_k, ln_w, ln_b, ch=RWKV_CHUNK, nsub=2):
    b, t, w = r.shape
    ch = min(ch, t)
    nsub = min(nsub, t // ch)
    tstep = ch * nsub
    seq = pl.BlockSpec((1, tstep, w), lambda i, c: (i, c, 0))
    vec = pl.BlockSpec((1, w), lambda i, c: (0, 0))
    row = lambda z: z.reshape(1, w)
    return pl.pallas_call(
        functools.partial(_rwkv_scan_kernel, ch=ch, nsub=nsub),
        grid=(b, t // tstep),
        in_specs=[seq] * 7 + [vec] * 3,
        out_specs=seq,
        out_shape=jax.ShapeDtypeStruct((b, t, w), F32),
        scratch_shapes=[pltpu.VMEM((B_HEADS // 2, LANES, LANES), F32)],
        compiler_params=_cparams("parallel", "arbitrary"),
        name="rwkv_scan",
    )(r, k, v, lw, kk, bb, g, row(r_k), row(ln_w), row(ln_b))


def _fill_vt(v_ref_slice, vt_scr, t):
    sub = lax.broadcasted_iota(jnp.int32, (LANES, LANES), 0)

    def body(j, carry):
        start = pl.multiple_of(j * LANES, LANES)
        vt = v_ref_slice(start).T
        vt_scr[:, pl.ds(start, LANES)] = jnp.where(sub < HEAD_DIM, vt, 1.0).astype(BF16)
        return carry

    lax.fori_loop(0, t // LANES, body, 0)


def _stack_heads(src_ref, dst_scr, n_heads, scale):
    lo, hi = _lane_half_masks((QCHUNK, LANES))
    for h in range(n_heads):
        tile = src_ref[0, :, (h // 2) * LANES:(h // 2 + 1) * LANES]
        if scale != 1.0:
            tile = tile * scale
        dst_scr[h * QCHUNK:(h + 1) * QCHUNK, :] = jnp.where((lo, hi)[h % 2], tile, 0.0).astype(BF16)


def _online_update(scores, vt, m_scr, acc_scr):
    m_old = m_scr[...]
    m_new = jnp.maximum(m_old, jnp.concatenate([jnp.max(s, axis=0, keepdims=True) for s in scores], axis=1))
    p = jnp.concatenate([jnp.exp2(s - m_new[:, h * QCHUNK:(h + 1) * QCHUNK]).astype(BF16)
                         for h, s in enumerate(scores)], axis=1)
    acc_scr[...] = jnp.exp2(m_old - m_new) * acc_scr[...] + _dot(vt, p)
    m_scr[...] = m_new


def _score_stage(k_slab, qst_scr, mask_bias, n_heads, s_scr, mx_scr, slot):
    for pair in range(n_heads // 2):
        d = _dot_nt(k_slab, qst_scr[2 * pair * QCHUNK:(2 * pair + 2) * QCHUNK, :])
        for e in range(2):
            h = 2 * pair + e
            cols = slice(h * QCHUNK, (h + 1) * QCHUNK)
            s = d[:, e * QCHUNK:(e + 1) * QCHUNK] + mask_bias(h)
            s_scr[slot, :, cols] = s
            mx_scr[slot, :, cols] = jnp.max(s, axis=0, keepdims=True)


def _softmax_stage(vt, n_heads, s_scr, mx_scr, slot, m_scr, acc_scr):
    m_old = m_scr[...]
    m_new = jnp.maximum(m_old, mx_scr[slot])
    p = jnp.concatenate([jnp.exp2(s_scr[slot, :, h * QCHUNK:(h + 1) * QCHUNK]
                                  - m_new[:, h * QCHUNK:(h + 1) * QCHUNK]).astype(BF16) for h in range(n_heads)], axis=1)
    acc_scr[...] = jnp.exp2(m_old - m_new) * acc_scr[...] + _dot(vt, p)
    m_scr[...] = m_new


def _pipelined_groups(ngroup, s_matmul, process):
    last = ngroup - 1
    s_matmul(0, 0)

    def pair_body(i, carry):
        g0 = 2 * i
        s_matmul(g0 + 1, 1)
        process(g0, 0)
        s_matmul(jnp.minimum(g0 + 2, last), 0)
        process(g0 + 1, 1)
        return carry

    lax.fori_loop(0, ngroup // 2, pair_body, 0)

    @pl.when(ngroup % 2 == 1)
    def _():
        process(last, 0)


def _bias_rows(tabt_ref, h, d_first, n_tiles):
    tiles = [tabt_ref[jnp.clip(d_first - u, 0, BIAS_TILES - 1), h] for u in range(n_tiles)]
    return jnp.concatenate(tiles, axis=0)


def _write_heads(o_ref, acc_scr, n_heads):
    for p in range(n_heads // 2):
        parts = []
        for h in (2 * p, 2 * p + 1):
            a = acc_scr[:, h * QCHUNK:(h + 1) * QCHUNK]
            parts.append(a[0:HEAD_DIM, :] * (1.0 / a[HEAD_DIM:HEAD_DIM + 1, :]))
        o_ref[0, :, p * LANES:(p + 1) * LANES] = jnp.concatenate(parts, axis=0).T


def _moba_kernel(q_ref, k_ref, v_ref, tabt_ref, o_ref,
                 k_scr, vt_scr, km_scr, qst_scr, pick_scr, s_scr, mx_scr, m_scr, acc_scr):
    c = pl.program_id(2)
    t = k_ref.shape[1]
    nblk = t // MOBA_BLOCK
    nbp = km_scr.shape[0]
    group = C_HEADS // C_KV_HEADS

    @pl.when(c == 0)
    def _():
        kf = k_ref[0]
        k_scr[...] = kf.astype(BF16)
        km_scr[...] = jnp.zeros_like(km_scr)
        km_scr[0:nblk, :] = jnp.mean(kf.reshape(nblk, MOBA_BLOCK, LANES), axis=1)
        _fill_vt(lambda start: v_ref[0, pl.ds(start, LANES), :], vt_scr, t)

    own = (c * QCHUNK) // MOBA_BLOCK
    _stack_heads(q_ref, qst_scr, group, HEAD_DIM ** -0.5 * LOG2E)

    qsum = q_ref[0, :, 0:LANES]
    for p in range(1, group // 2):
        qsum = qsum + q_ref[0, :, p * LANES:(p + 1) * LANES]
    gate = _dot_nt_x(km_scr[...], qsum)
    blk = lax.broadcasted_iota(jnp.int32, (nbp, QCHUNK), 0)
    gate = jnp.where(blk < own, gate, -jnp.inf)
    rank = jnp.zeros((nbp, QCHUNK), jnp.int32)
    for mblk in range(nblk):
        gm = gate[mblk:mblk + 1, :]
        ahead = jnp.logical_or(gm > gate, jnp.logical_and(gm == gate, mblk < blk))
        rank = rank + jnp.where(ahead, 1, 0)
    chosen = jnp.logical_and(rank < MOBA_TOPK, blk < own)
    pick_scr[...] = jnp.where(chosen, 0.0, MASKED)

    m_scr[...] = jnp.full(m_scr.shape, M_INIT, F32)
    acc_scr[...] = jnp.zeros_like(acc_scr)
    grp = 2 * MOBA_BLOCK
    tiles_per_grp = grp // LANES
    ngroup = own // 2 + 1
    grow = lax.broadcasted_iota(jnp.int32, (grp, QCHUNK), 0)
    second = grow >= MOBA_BLOCK
    qpos = c * QCHUNK + lax.broadcasted_iota(jnp.int32, (grp, QCHUNK), 1)

    def score_stage(gi, slot):
        start = pl.multiple_of(gi * grp, grp)
        causal = jnp.where((gi * grp + grow) <= qpos, 0.0, MASKED)
        first = jnp.where(2 * gi < own, pick_scr[pl.ds(2 * gi, 1), :], causal)
        later = jnp.where(2 * gi + 1 < own, pick_scr[pl.ds(2 * gi + 1, 1), :], causal)
        okb = jnp.where(second, later, first)
        d_first = c - tiles_per_grp * gi
        _score_stage(k_scr[pl.ds(start, grp), :], qst_scr,
                     lambda g: _bias_rows(tabt_ref, g, d_first, tiles_per_grp) + okb, group, s_scr, mx_scr, slot)

    def softmax_stage(gi, slot):
        start = pl.multiple_of(gi * grp, grp)
        _softmax_stage(vt_scr[:, pl.ds(start, grp)], group, s_scr, mx_scr, slot, m_scr, acc_scr)

    _pipelined_groups(ngroup, score_stage, softmax_stage)
    _write_heads(o_ref, acc_scr, group)


def moba_attention(qc, kvc, tabt):
    b, t, _ = qc.shape
    assert t % MOBA_BLOCK == 0
    nq = t // QCHUNK
    group = C_HEADS // C_KV_HEADS
    qw = group * HEAD_DIM
    nbp = -(-(t // MOBA_BLOCK) // SUBLANES) * SUBLANES
    return pl.pallas_call(
        _moba_kernel,
        grid=(b, C_KV_HEADS, nq),
        in_specs=[pl.BlockSpec((1, QCHUNK, qw), lambda i, h, c: (i, c, h)),
                  pl.BlockSpec((1, t, LANES), lambda i, h, c: (i, 0, h)),
                  pl.BlockSpec((1, t, LANES), lambda i, h, c: (i, 0, C_KV_HEADS + h)),
                  pl.BlockSpec((BIAS_TILES, group, LANES, LANES), lambda i, h, c: (0, h, 0, 0))],
        out_specs=pl.BlockSpec((1, QCHUNK, qw), lambda i, h, c: (i, c, h)),
        out_shape=jax.ShapeDtypeStruct((b, t, C_HEADS * HEAD_DIM), F32),
        scratch_shapes=[pltpu.VMEM((t, LANES), BF16), pltpu.VMEM((LANES, t), BF16),
                        pltpu.VMEM((nbp, LANES), F32),
                        pltpu.VMEM((group * QCHUNK, LANES), BF16),
                        pltpu.VMEM((nbp, QCHUNK), F32),
                        pltpu.VMEM((2, 2 * MOBA_BLOCK, group * QCHUNK), F32),
                        pltpu.VMEM((2, 1, group * QCHUNK), F32),
                        pltpu.VMEM((1, group * QCHUNK), F32),
                        pltpu.VMEM((LANES, group * QCHUNK), F32)],
        compiler_params=_cparams("parallel", "parallel", "arbitrary"),
        name="moba_attention",
    )(qc, kvc, kvc, tabt)


def _dsa_kernel(q_ref, kv_ref, qi_ref, ki_ref, gk_ref, tabt_ref, o_ref,
                k_scr, vt_scr, ki_scr, key_scr, hi_scr, lo_scr, qst_scr, qist_scr, pre_scr, s_scr, mx_scr, m_scr, acc_scr,
                *, topk):
    c = pl.program_id(1)
    t = k_scr.shape[0]
    grp = KEY_GROUP * LANES
    ngroup = c // KEY_GROUP + 1
    sub = lax.broadcasted_iota(jnp.int32, (LANES, QCHUNK), 0)
    gsub = lax.broadcasted_iota(jnp.int32, (grp, QCHUNK), 0)
    qpos = c * QCHUNK + lax.broadcasted_iota(jnp.int32, (grp, QCHUNK), 1)

    @pl.when(c == 0)
    def _():
        k_scr[...] = kv_ref[0, :, 0:LANES].astype(BF16)
        ki_scr[...] = _rms(ki_ref[0], gk_ref[...]).astype(BF16)
        _fill_vt(lambda start: kv_ref[0, pl.ds(start, LANES), LANES:2 * LANES], vt_scr, t)

    _stack_heads(q_ref, qst_scr, D_HEADS, HEAD_DIM ** -0.5 * LOG2E)
    _stack_heads(qi_ref, qist_scr, IDX_HEADS, 1.0)
    w_t = (qi_ref[0, :, IDX_HEADS * IDX_DIM + LANES:IDX_HEADS * IDX_DIM + 2 * LANES]
           * ((IDX_HEADS * IDX_DIM) ** -0.5)).T

    def score_body(gi, carry):
        start = pl.multiple_of(gi * grp, grp)
        rel = jnp.maximum(_dot_nt(ki_scr[pl.ds(start, grp), :], qist_scr[...]), 0.0)
        sc = jnp.zeros((grp, QCHUNK), F32)
        for h in range(IDX_HEADS):
            sc = sc + rel[:, h * QCHUNK:(h + 1) * QCHUNK] * w_t[h:h + 1, :]
        sc = jnp.where(sc == 0.0, 0.0, sc)
        bits = pltpu.bitcast(sc, jnp.int32)
        skey = bits ^ (lax.shift_right_arithmetic(bits, 31) & 0x7FFFFFFF)
        key = jnp.where((gi * grp + gsub) <= qpos, skey, INT_MIN)
        key_scr[pl.ds(start, grp), :] = key
        hi_scr[pl.ds(start, grp), :] = lax.shift_right_arithmetic(key, 16).astype(jnp.int16)
        lo_scr[pl.ds(start, grp), :] = ((key & 0xFFFF) - HALF_BIAS).astype(jnp.int16)
        return carry

    lax.fori_loop(0, ngroup, score_body, 0)

    def count(pred):
        def body(gi, acc):
            for u in range(KEY_GROUP):
                start = pl.multiple_of(gi * grp + u * LANES, LANES)
                acc = jnp.where(pred(key_scr[pl.ds(start, LANES), :], start + sub), acc + 1, acc)
            return acc
        acc = lax.fori_loop(0, ngroup, body, jnp.zeros((LANES, QCHUNK), jnp.int32))
        return jnp.sum(acc, axis=0, keepdims=True)

    def count16(half_scr, pred):
        def body(gi, acc):
            for u in range(KEY_GROUP):
                start = pl.multiple_of(gi * grp + u * LANES, LANES)
                acc = jnp.where(pred(half_scr[pl.ds(start, LANES), :]), acc + 1, acc)
            return acc
        acc = lax.fori_loop(0, ngroup, body, jnp.zeros((LANES, QCHUNK), jnp.int16))
        return jnp.sum(acc.astype(jnp.int32), axis=0, keepdims=True)

    def kth_half(half_scr, want):
        cnt0 = count16(half_scr, lambda x: x >= 0)
        best = jnp.where(cnt0 >= want, 0, -HALF_BIAS).astype(jnp.int32)

        def bit_body(i, best):
            cand = best + lax.shift_left(jnp.int32(1), 14 - i)
            cnt = count16(half_scr, lambda x: x >= cand.astype(jnp.int16))
            return jnp.where(cnt >= want, cand, best)

        return lax.fori_loop(0, 15, bit_body, best)

    thr_hi = kth_half(hi_scr, topk)
    thr_hi16 = thr_hi.astype(jnp.int16)
    want_lo = topk - count16(hi_scr, lambda x: x > thr_hi16)

    def keep_lo_body(gi, carry):
        start = pl.multiple_of(gi * grp, grp)
        same_hi = hi_scr[pl.ds(start, grp), :] == thr_hi16
        lo_scr[pl.ds(start, grp), :] = jnp.where(same_hi, lo_scr[pl.ds(start, grp), :], jnp.int16(-HALF_BIAS))
        return carry

    lax.fori_loop(0, ngroup, keep_lo_body, 0)
    thr_lo = kth_half(lo_scr, want_lo)
    thr = thr_hi * (2 * HALF_BIAS) + (thr_lo + HALF_BIAS)

    cnt_ge = count(lambda kt, pos: kt >= thr)
    tied = jnp.logical_and(cnt_ge > topk, thr > INT_MIN)

    @pl.when(jnp.max(jnp.where(tied, 1.0, 0.0)) > 0.5)
    def _():
        def tally_body(j, carry):
            ties_before, above = carry
            start = pl.multiple_of(j * LANES, LANES)
            kt = key_scr[pl.ds(start, LANES), :]
            pre_scr[pl.ds(j, 1), :] = ties_before
            ties_before = ties_before + jnp.sum(jnp.where(kt == thr, 1.0, 0.0), axis=0, keepdims=True)
            above = above + jnp.sum(jnp.where(kt > thr, 1.0, 0.0), axis=0, keepdims=True)
            return ties_before, above

        zero_row = jnp.zeros((1, QCHUNK), F32)
        _, above = lax.fori_loop(0, ngroup * KEY_GROUP, tally_body, (zero_row, zero_row))
        need = jnp.where(tied, topk - above, 3.0e38)
        tri = jnp.where(lax.broadcasted_iota(jnp.int32, (LANES, LANES), 1) <= sub, 1.0, 0.0).astype(BF16)

        def demote_body(gi, carry):
            tiles = []
            for u in range(KEY_GROUP):
                start = pl.multiple_of(gi * grp + u * LANES, LANES)
                tiles.append(key_scr[pl.ds(start, LANES), :])
            eq01 = jnp.concatenate([jnp.where(kt == thr, 1.0, 0.0).astype(BF16) for kt in tiles], axis=1)
            within = _dot(tri, eq01)
            for u, kt in enumerate(tiles):
                start = pl.multiple_of(gi * grp + u * LANES, LANES)
                rank = pre_scr[pl.ds(gi * KEY_GROUP + u, 1), :] + within[:, u * QCHUNK:(u + 1) * QCHUNK]
                drop = jnp.logical_and(kt == thr, rank > need)
                key_scr[pl.ds(start, LANES), :] = jnp.where(drop, kt - 1, kt)
            return carry

        lax.fori_loop(0, ngroup, demote_body, 0)

    thr_eff = jnp.maximum(thr, INT_MIN + 1)

    m_scr[...] = jnp.full(m_scr.shape, M_INIT, F32)
    acc_scr[...] = jnp.zeros_like(acc_scr)

    def score_stage(gi, slot):
        start = pl.multiple_of(gi * grp, grp)
        okb = jnp.where(key_scr[pl.ds(start, grp), :] >= thr_eff, 0.0, MASKED)
        d_first = c - gi * KEY_GROUP
        _score_stage(k_scr[pl.ds(start, grp), :], qst_scr,
                     lambda h: _bias_rows(tabt_ref, h, d_first, KEY_GROUP) + okb, D_HEADS, s_scr, mx_scr, slot)

    def softmax_stage(gi, slot):
        start = pl.multiple_of(gi * grp, grp)
        _softmax_stage(vt_scr[:, pl.ds(start, grp)], D_HEADS, s_scr, mx_scr, slot, m_scr, acc_scr)

    _pipelined_groups(ngroup, score_stage, softmax_stage)
    _write_heads(o_ref, acc_scr, D_HEADS)


def dsa_attention(qd, kvd, idx, kidx_gain, tabt):
    b, t, _ = qd.shape
    topk = min(DSA_TOPK, t // 4)
    nq = t // QCHUNK
    gk = jnp.concatenate([kidx_gain, kidx_gain]).reshape(1, LANES)
    wq = D_HEADS * HEAD_DIM
    return pl.pallas_call(
        functools.partial(_dsa_kernel, topk=topk),
        grid=(b, nq),
        in_specs=[pl.BlockSpec((1, QCHUNK, wq), lambda i, c: (i, c, 0)),
                  pl.BlockSpec((1, t, 2 * LANES), lambda i, c: (i, 0, 0)),
                  pl.BlockSpec((1, QCHUNK, 4 * LANES), lambda i, c: (i, c, 0)),
                  pl.BlockSpec((1, t, LANES), lambda i, c: (i, 0, 2)),
                  pl.BlockSpec((1, LANES), lambda i, c: (0, 0)),
                  pl.BlockSpec((BIAS_TILES, D_HEADS, LANES, LANES), lambda i, c: (0, 0, 0, 0))],
        out_specs=pl.BlockSpec((1, QCHUNK, wq), lambda i, c: (i, c, 0)),
        out_shape=jax.ShapeDtypeStruct((b, t, wq), F32),
        scratch_shapes=[pltpu.VMEM((t, LANES), BF16), pltpu.VMEM((LANES, t), BF16), pltpu.VMEM((t, LANES), BF16),
                        pltpu.VMEM((t, QCHUNK), jnp.int32),
                        pltpu.VMEM((t, QCHUNK), jnp.int16), pltpu.VMEM((t, QCHUNK), jnp.int16),
                        pltpu.VMEM((D_HEADS * QCHUNK, LANES), BF16), pltpu.VMEM((IDX_HEADS * QCHUNK, LANES), BF16),
                        pltpu.VMEM((t // LANES, QCHUNK), F32),
                        pltpu.VMEM((2, KEY_GROUP * LANES, D_HEADS * QCHUNK), F32),
                        pltpu.VMEM((2, 1, D_HEADS * QCHUNK), F32),
                        pltpu.VMEM((1, D_HEADS * QCHUNK), F32),
                        pltpu.VMEM((LANES, D_HEADS * QCHUNK), F32)],
        compiler_params=_cparams("parallel", "arbitrary"),
        name="dsa_attention",
    )(qd, kvd, idx, idx, gk, tabt)


def _dup_heads(wcols, n_heads):
    d = wcols.shape[0]
    wh = wcols.reshape(d, n_heads, 1, HEAD_DIM)
    return jnp.broadcast_to(wh, (d, n_heads, 2, HEAD_DIM)).reshape(d, n_heads * 2 * HEAD_DIM)


def _even_weight(w_in):
    aq, akv = A_HEADS * HEAD_DIM, A_KV_HEADS * HEAD_DIM
    q = w_in[:, :aq]
    k = _dup_heads(w_in[:, aq:aq + akv], A_KV_HEADS)
    v = _dup_heads(w_in[:, aq + akv:aq + 2 * akv], A_KV_HEADS)
    rest = w_in[:, aq + 2 * akv:]
    return jnp.concatenate([q, k, v, rest], axis=1).astype(BF16)


def _odd_weight(w_in):
    d = w_in.shape[0]
    cq, ckv, dq = C_HEADS * HEAD_DIM, C_KV_HEADS * HEAD_DIM, D_HEADS * HEAD_DIM
    cuts = np.cumsum([cq, ckv, ckv, dq, HEAD_DIM, HEAD_DIM, IDX_HEADS * IDX_DIM, IDX_DIM]).tolist()
    qc, kc, vc, qd, kd, vd, qi, ki, wi = jnp.split(w_in, cuts, axis=1)
    pad = jnp.zeros((d, LANES - IDX_HEADS), w_in.dtype)
    cols = [qc, _dup_heads(kc, C_KV_HEADS), _dup_heads(vc, C_KV_HEADS),
            qd, _dup_heads(kd, 1), _dup_heads(vd, 1),
            qi, _dup_heads(ki, 1), wi, pad]
    return jnp.concatenate(cols, axis=1).astype(BF16)


def kernel(x, mem, rel_bias, norm_gains, mix_w_out, mem_norm, x_wq, x_wkv, x_wo, ff_w1, ff_w2, ev_w_in, a_sinks, b_mu, b_w0, b_w2, b_a0, b_a2, b_g2, b_k_k, b_k_a, b_r_k, b_ln_w, b_ln_b, od_w_in, d_kidx_norm):
    bsz, t, d = x.shape
    n = bsz * t
    depth = norm_gains.shape[0]
    mlen = mem.shape[1]
    tabt = bias_table(rel_bias, transposed=True)
    x2 = x.reshape(n, d)
    for layer in range(depth):
        i = layer // 2
        g = norm_gains[layer]
        if layer % 2 == 0:
            b_in = 3 * B_WIDTH + B_DECAY_RANK + B_A_RANK + B_GATE_RANK
            qkv, pb = norm_proj(x2, g[0], _even_weight(ev_w_in[i]), (1024, b_in), (F32, F32))
            ya = swa_attention(qkv.reshape(bsz, t, 1024), a_sinks[i], rel_bias, tabt)
            prep = rwkv_prep(pb.reshape(bsz, t, b_in), b_mu[i], b_w0[i], b_w2[i], b_a0[i], b_a2[i], b_g2[i],
                             b_k_k[i], b_k_a[i])
            yb = rwkv_scan(*prep, b_r_k[i].reshape(-1), b_ln_w[i], b_ln_b[i])
        else:
            qc, kvc, qd, kvd, idx = norm_proj(x2, g[0], _odd_weight(od_w_in[i]),
                                              (512, 512, 512, 256, 512), (F32,) * 5)
            ya = moba_attention(qc.reshape(bsz, t, 512), kvc.reshape(bsz, t, 512), tabt)
            yb = dsa_attention(qd.reshape(bsz, t, 512), kvd.reshape(bsz, t, 256), idx.reshape(bsz, t, 512),
                               d_kidx_norm[i], tabt)
        x2 = proj_norm_res(ya.reshape(n, -1), yb.reshape(n, -1), mix_w_out[layer], g[1], x2)
        kmem, vmem = norm_proj(mem.reshape(bsz * mlen, d), mem_norm[layer], x_wkv[layer].astype(BF16),
                               (X_WIDTH, X_WIDTH), (BF16, BF16))
        x3 = cross_attn(x2.reshape(bsz, t, d), kmem.reshape(bsz, mlen, X_WIDTH), vmem.reshape(bsz, mlen, X_WIDTH),
                        x_wq[layer], x_wo[layer], g[2], g[3])
        x2 = mlp(x3.reshape(n, d), ff_w1[layer], ff_w2[layer], g[4], g[5])
    return x2.reshape(bsz, t, d)
```

```python
import functools

import numpy as np
import jax
import jax.numpy as jnp
from jax import lax
from jax.experimental import pallas as pl
from jax.experimental.pallas import tpu as pltpu

F32 = jnp.float32
BF16 = jnp.bfloat16
HIGHEST = lax.Precision.HIGHEST

LANES = 128
SUBLANES = 8
HEAD_DIM = 64
EPS = 1e-6
A_HEADS, A_KV_HEADS, BAND = 8, 2, 128
B_HEADS, B_WIDTH = 8, 512
B_DECAY_RANK, B_A_RANK, B_GATE_RANK = 64, 64, 128
B_LN_EPS = 64e-5
RWKV_CHUNK = 64
C_HEADS, C_KV_HEADS, MOBA_BLOCK, MOBA_TOPK, QCHUNK = 8, 2, 256, 3, 128
D_HEADS, IDX_HEADS, IDX_DIM, DSA_TOPK = 8, 4, 64, 256
N_BUCKETS, BIAS_HEADS = 32, 8
BIAS_TILES = 9
X_HEADS, X_HEAD_DIM, X_WIDTH = 4, 128, 512

LOG2E = 1.4426950408889634
KEY_GROUP = 4
MASKED = -2e30
M_INIT = -1e30
INT_MIN = -(2 ** 31)
LOWEST_F32 = -3.4028234663852886e38
VMEM_LIMIT = 56 * 1024 * 1024


def _cparams(*sem):
    return pltpu.CompilerParams(dimension_semantics=tuple(sem), vmem_limit_bytes=VMEM_LIMIT)


def _dot(a, b):
    return jnp.dot(a, b, preferred_element_type=F32)


def _dot_nt(a, b):
    return lax.dot_general(a, b, (((1,), (1,)), ((), ())), preferred_element_type=F32)


def _dot_tn(a, b):
    return lax.dot_general(a, b, (((0,), (0,)), ((), ())), preferred_element_type=F32)


def _dot_x(a, b):
    return jnp.dot(a, b, preferred_element_type=F32, precision=HIGHEST)


def _dot_nt_x(a, b):
    return lax.dot_general(a, b, (((1,), (1,)), ((), ())), preferred_element_type=F32, precision=HIGHEST)


def _rms(xf, g):
    ms = jnp.mean(xf * xf, axis=-1, keepdims=True)
    return xf * lax.rsqrt(ms + EPS) * g


def _lane_half_masks(shape):
    lane = lax.broadcasted_iota(jnp.int32, shape, len(shape) - 1)
    lo = lane < HEAD_DIM
    return lo, jnp.logical_not(lo)


def _bucket_of_distance(n):
    exact = N_BUCKETS // 2
    if n < exact:
        return n
    j = 0
    while n ** 8 >= (exact ** 8) * (2 ** (3 * (j + 1))):
        j += 1
    return min(exact + j, N_BUCKETS - 1)


@functools.lru_cache(maxsize=None)
def _bucket_tiles(transposed):
    by_dist = np.array([_bucket_of_distance(n) for n in range(BIAS_TILES * LANES + LANES)], np.int32)
    q = np.arange(LANES)[None, :] if transposed else np.arange(LANES)[:, None]
    k = np.arange(LANES)[:, None] if transposed else np.arange(LANES)[None, :]
    tiles = [by_dist[np.maximum(d * LANES + q - k, 0)] for d in range(BIAS_TILES)]
    return np.stack(tiles).astype(np.int32)


def _bias_table_kernel(rb_ref, bkt_ref, out_ref, *, base2_shifted):
    bkt = bkt_ref[0]
    for h in range(BIAS_HEADS):
        acc = jnp.zeros((LANES, LANES), F32)
        for b in range(N_BUCKETS):
            acc = jnp.where(bkt == b, rb_ref[b, h], acc)
        if base2_shifted:
            acc = (acc - rb_ref[N_BUCKETS - 1, h]) * LOG2E
        out_ref[0, h] = acc


def bias_table(rel_bias, transposed=False):
    return pl.pallas_call(
        functools.partial(_bias_table_kernel, base2_shifted=transposed),
        grid=(BIAS_TILES,),
        in_specs=[pl.BlockSpec(memory_space=pltpu.SMEM),
                  pl.BlockSpec((1, LANES, LANES), lambda d: (d, 0, 0))],
        out_specs=pl.BlockSpec((1, BIAS_HEADS, LANES, LANES), lambda d: (d, 0, 0, 0)),
        out_shape=jax.ShapeDtypeStruct((BIAS_TILES, BIAS_HEADS, LANES, LANES), F32),
        compiler_params=_cparams("arbitrary"),
        name="bias_table",
    )(rel_bias, jnp.asarray(_bucket_tiles(transposed)))


def _norm_proj_kernel(x_ref, g_ref, w_ref, *out_refs, splits):
    h = _rms(x_ref[...], g_ref[...]).astype(BF16)
    res = _dot(h, w_ref[...])
    off = 0
    for o_ref, width in zip(out_refs, splits):
        o_ref[...] = res[:, off:off + width].astype(o_ref.dtype)
        off += width


def norm_proj(x2, gain, w, splits, out_dtypes, tm=256):
    n, d = x2.shape
    tm = min(tm, n)
    total = sum(splits)
    assert w.shape == (d, total) and n % tm == 0 and all(s % LANES == 0 for s in splits)
    return pl.pallas_call(
        functools.partial(_norm_proj_kernel, splits=tuple(splits)),
        grid=(n // tm,),
        in_specs=[pl.BlockSpec((tm, d), lambda i: (i, 0)),
                  pl.BlockSpec((1, d), lambda i: (0, 0)),
                  pl.BlockSpec((d, total), lambda i: (0, 0))],
        out_specs=[pl.BlockSpec((tm, s), lambda i: (i, 0)) for s in splits],
        out_shape=[jax.ShapeDtypeStruct((n, s), dt) for s, dt in zip(splits, out_dtypes)],
        compiler_params=_cparams("parallel"),
        name="norm_proj",
    )(x2, gain.reshape(1, d), w)


def _post_mixer_kernel(ya_ref, yb_ref, x_ref, k_ref, v_ref, wa_ref, wb_ref, wq_ref, wo_ref, w1_ref, w2_ref, g_ref,
                       o_ref, *, tf):
    x = x_ref[0]
    z = _dot(ya_ref[0], wa_ref[...]) + _dot(yb_ref[0], wb_ref[...])
    x = x + _rms(z, g_ref[1:2, :])
    q = _dot(_rms(x, g_ref[2:3, :]).astype(BF16), wq_ref[...])
    k = k_ref[0]
    v = v_ref[0]
    outs = []
    for hd in range(X_HEADS):
        sl = slice(hd * X_HEAD_DIM, (hd + 1) * X_HEAD_DIM)
        s = _dot_nt(q[:, sl].astype(BF16), k[:, sl]) * (X_HEAD_DIM ** -0.5)
        m = jnp.max(s, axis=-1, keepdims=True)
        p = jnp.exp(s - m)
        p = p / jnp.sum(p, axis=-1, keepdims=True)
        outs.append(_dot(p.astype(BF16), v[:, sl]))
    z = _dot(jnp.concatenate(outs, axis=-1).astype(BF16), wo_ref[...])
    x = x + _rms(z, g_ref[3:4, :])
    h = _rms(x, g_ref[4:5, :]).astype(BF16)
    acc = jnp.zeros(x.shape, F32)
    for j in range(w1_ref.shape[1] // tf):
        a = jnp.maximum(_dot(h, w1_ref[:, j * tf:(j + 1) * tf]), 0.0)
        acc = acc + _dot((a * a).astype(BF16), w2_ref[j * tf:(j + 1) * tf, :])
    o_ref[0] = x + _rms(acc, g_ref[5:6, :])


def post_mixer(ya, yb, x3, kmem, vmem, w_out, wq, wo, w1, w2, gains, tq=512, tf=1024):
    b, t, d = x3.shape
    tq = min(tq, t)
    m = kmem.shape[1]
    ka, kb = ya.shape[2], yb.shape[2]
    f = w1.shape[1]
    once = pl.Buffered(1)
    const = lambda shape: pl.BlockSpec(shape, lambda i, j: (0, 0), pipeline_mode=once)
    return pl.pallas_call(
        functools.partial(_post_mixer_kernel, tf=tf),
        grid=(b, t // tq),
        in_specs=[pl.BlockSpec((1, tq, ka), lambda i, j: (i, j, 0)),
                  pl.BlockSpec((1, tq, kb), lambda i, j: (i, j, 0)),
                  pl.BlockSpec((1, tq, d), lambda i, j: (i, j, 0)),
                  pl.BlockSpec((1, m, X_WIDTH), lambda i, j: (i, 0, 0)),
                  pl.BlockSpec((1, m, X_WIDTH), lambda i, j: (i, 0, 0)),
                  const((ka, d)), const((kb, d)), const((d, X_WIDTH)), const((X_WIDTH, d)),
                  const((d, f)), const((f, d)), const(gains.shape)],
        out_specs=pl.BlockSpec((1, tq, d), lambda i, j: (i, j, 0)),
        out_shape=jax.ShapeDtypeStruct((b, t, d), F32),
        compiler_params=_cparams("parallel", "parallel"),
        name="post_mixer",
    )(ya, yb, x3, kmem, vmem, w_out[:ka].astype(BF16), w_out[ka:].astype(BF16), wq.astype(BF16), wo.astype(BF16),
      w1.astype(BF16), w2.astype(BF16), gains)


def _swa_kernel(sink_ref, rb_ref, q_ref, kv_ref, tabt_ref, o_ref, qst_scr, vt_scr):
    n = pl.program_id(1)
    group = A_HEADS // A_KV_HEADS
    first_band = jnp.maximum(n - 1, 0)
    start = pl.multiple_of(first_band * BAND, BAND)
    slab = 2 * BAND
    _stack_heads(q_ref, qst_scr, A_HEADS, HEAD_DIM ** -0.5 * LOG2E)
    kpos = start + lax.broadcasted_iota(jnp.int32, (slab, BAND), 0)
    qpos = n * BAND + lax.broadcasted_iota(jnp.int32, (slab, BAND), 1)
    dist = qpos - kpos
    okb = jnp.where(jnp.logical_and(dist >= 0, dist < BAND), 0.0, MASKED)
    sub = lax.broadcasted_iota(jnp.int32, (LANES, slab), 0)
    koff = 0
    voff = A_KV_HEADS * LANES
    d_first = n - first_band
    scores, vts = [], []
    for hkv in range(A_KV_HEADS):
        k2 = kv_ref[0, pl.ds(start, slab), koff + hkv * LANES:koff + (hkv + 1) * LANES].astype(BF16)
        v2 = kv_ref[0, pl.ds(start, slab), voff + hkv * LANES:voff + (hkv + 1) * LANES]
        vt_scr[hkv] = jnp.where(sub < HEAD_DIM, v2.T, 1.0).astype(BF16)
        vts.append(vt_scr[hkv])
        s_all = _dot_nt(k2, qst_scr[hkv * group * BAND:(hkv + 1) * group * BAND, :])
        for g in range(group):
            h = hkv * group + g
            scores.append(s_all[:, g * BAND:(g + 1) * BAND] + (_bias_rows(tabt_ref, h, d_first, 2) + okb))
    sinks = [(sink_ref[h] - rb_ref[N_BUCKETS - 1, h]) * LOG2E for h in range(A_HEADS)]
    m = [jnp.maximum(jnp.max(scores[h], axis=0, keepdims=True), sinks[h]) for h in range(A_HEADS)]
    p = [jnp.exp2(scores[h] - m[h]).astype(BF16) for h in range(A_HEADS)]
    outs = []
    for hkv in range(A_KV_HEADS):
        pv = _dot(vts[hkv], jnp.concatenate(p[hkv * group:(hkv + 1) * group], axis=1))
        for g in range(group):
            h = hkv * group + g
            a = pv[:, g * BAND:(g + 1) * BAND]
            den = a[HEAD_DIM:HEAD_DIM + 1, :] + jnp.exp2(sinks[h] - m[h])
            outs.append(a[0:HEAD_DIM, :] * (1.0 / den))
    for pair in range(A_HEADS // 2):
        tile = jnp.concatenate(outs[2 * pair:2 * pair + 2], axis=0).T
        o_ref[0, :, pair * LANES:(pair + 1) * LANES] = tile.astype(o_ref.dtype)


def swa_attention(qkv, sinks, rel_bias, tabt):
    b, t, w = qkv.shape
    qw = A_HEADS * HEAD_DIM
    return pl.pallas_call(
        _swa_kernel,
        grid=(b, t // BAND),
        in_specs=[pl.BlockSpec(memory_space=pltpu.SMEM),
                  pl.BlockSpec(memory_space=pltpu.SMEM),
                  pl.BlockSpec((1, BAND, qw), lambda i, n: (i, n, 0)),
                  pl.BlockSpec((1, t, w - qw), lambda i, n: (i, 0, 1)),
                  pl.BlockSpec((2, A_HEADS, LANES, LANES), lambda i, n: (0, 0, 0, 0))],
        out_specs=pl.BlockSpec((1, BAND, qw), lambda i, n: (i, n, 0)),
        out_shape=jax.ShapeDtypeStruct((b, t, qw), BF16),
        scratch_shapes=[pltpu.VMEM((A_HEADS * BAND, LANES), BF16), pltpu.VMEM((A_KV_HEADS, LANES, 2 * BAND), BF16)],
        compiler_params=_cparams("parallel", "arbitrary"),
        name="swa_attention",
    )(sinks, rel_bias, qkv, qkv, tabt)


def _split3(x):
    p0 = x.astype(BF16)
    r1 = x - p0.astype(F32)
    p1 = r1.astype(BF16)
    p2 = (r1 - p1.astype(F32)).astype(BF16)
    return p0, p1, p2


def _dot_exact_rhs01(x, m01):
    p0, p1, p2 = _split3(x)
    return _dot(p0, m01) + _dot(p1, m01) + _dot(p2, m01)


def _dot_exact_lhs01(m01, x):
    p0, p1, p2 = _split3(x)
    return _dot(m01, p0) + _dot(m01, p1) + _dot(m01, p2)


def _head_sum_matrix(width):
    r = lax.broadcasted_iota(jnp.int32, (width, width), 0) // HEAD_DIM
    c = lax.broadcasted_iota(jnp.int32, (width, width), 1) // HEAD_DIM
    return jnp.where(r == c, 1.0, 0.0).astype(BF16)


def _sigmoid(z):
    return 1.0 / (1.0 + jnp.exp(-z))


def _softplus(z):
    return jnp.maximum(z, 0.0) + jnp.log(1.0 + jnp.exp(-jnp.abs(z)))


def _rwkv_prep_kernel(p_ref, pprev_ref, mu_ref, wa2_ref, g2_ref, w0_ref, a0_ref, kk_ref, ka_ref,
                      r_out, k_out, v_out, lw_out, kk_out, b_out, g_out):
    i = pl.program_id(1)
    p = p_ref[0]
    w = B_WIDTH
    last = pprev_ref[0, SUBLANES - 1:SUBLANES, :]
    last = jnp.where(i > 0, last, 0.0)
    row = lax.broadcasted_iota(jnp.int32, p.shape, 0)
    prev = jnp.where(row == 0, last, pltpu.roll(p, 1, 0))
    xs = p + (prev - p) * mu_ref[...]
    r = xs[:, 0:w]
    k = xs[:, w:2 * w]
    v = xs[:, 2 * w:3 * w]
    wa_lo = xs[:, 3 * w:3 * w + LANES]
    g_lo = xs[:, 3 * w + LANES:3 * w + 2 * LANES]
    lo, _ = _lane_half_masks(wa_lo.shape)
    wa_in = jnp.where(lo, jnp.tanh(wa_lo), wa_lo)
    wa = _dot_x(wa_in, wa2_ref[...])
    wlog = -_softplus(-(w0_ref[...] + wa[:, 0:w])) - 0.5
    a = _sigmoid(a0_ref[...] + wa[:, w:2 * w])
    g = _dot(_sigmoid(g_lo).astype(BF16), g2_ref[...])
    kk = k * kk_ref[...]
    ss = _dot_exact_rhs01(kk * kk, _head_sum_matrix(w))
    kk = kk / jnp.maximum(jnp.sqrt(ss), 1e-12)
    kmod = k * (1.0 + (a - 1.0) * ka_ref[...])
    r_out[0] = r
    k_out[0] = kmod
    v_out[0] = v
    lw_out[0] = -jnp.exp(wlog)
    kk_out[0] = kk
    b_out[0] = kk * a
    g_out[0] = g


def rwkv_prep(pb, mu, w0, w2, a0, a2, g2, k_k, k_a, tc=256):
    b, t, win = pb.shape
    tc = min(tc, t)
    w = B_WIDTH
    wa2 = jnp.zeros((LANES, 2 * w), F32)
    wa2 = wa2.at[:B_DECAY_RANK, :w].set(w2).at[B_DECAY_RANK:, w:].set(a2)
    row = lambda z: z.reshape(1, -1)
    vec = pl.BlockSpec((1, w), lambda i, j: (0, 0))
    outs = [jax.ShapeDtypeStruct((b, t, w), F32)] * 7
    return pl.pallas_call(
        _rwkv_prep_kernel,
        grid=(b, t // tc),
        in_specs=[pl.BlockSpec((1, tc, win), lambda i, j: (i, j, 0)),
                  pl.BlockSpec((1, SUBLANES, win), lambda i, j: (i, jnp.maximum(j * (tc // SUBLANES) - 1, 0), 0)),
                  pl.BlockSpec((1, win), lambda i, j: (0, 0)),
                  pl.BlockSpec((LANES, 2 * w), lambda i, j: (0, 0)),
                  pl.BlockSpec((B_GATE_RANK, w), lambda i, j: (0, 0)),
                  vec, vec, vec, vec],
        out_specs=[pl.BlockSpec((1, tc, w), lambda i, j: (i, j, 0))] * 7,
        out_shape=outs,
        compiler_params=_cparams("parallel", "arbitrary"),
        name="rwkv_prep",
    )(pb, pb, row(mu), wa2, g2.astype(BF16), row(w0), row(a0), row(k_k), row(k_a))


def _rwkv_scan_kernel(r_ref, k_ref, v_ref, lw_ref, kk_ref, b_ref, g_ref, rk_ref, lnw_ref, lnb_ref,
                      o_ref, st_scr, *, ch, nsub):
    step = pl.program_id(1)
    n2 = 2 * ch

    @pl.when(step == 0)
    def _():
        st_scr[...] = jnp.zeros_like(st_scr)

    ti = lax.broadcasted_iota(jnp.int32, (n2, n2), 0)
    si = lax.broadcasted_iota(jnp.int32, (n2, n2), 1)
    same = (ti // ch) == (si // ch)
    incl = jnp.logical_and(same, si <= ti)
    strict = jnp.logical_and(same, si < ti)
    eye = jnp.where(si == ti, 1.0, 0.0)
    levels = []
    s = 1
    while s < ch:
        same2s = (ti // (2 * s)) == (si // (2 * s))
        levels.append(jnp.logical_and(same2s, jnp.logical_and((ti // s) % 2 == 1, (si // s) % 2 == 0)))
        s *= 2
    tr = lax.broadcasted_iota(jnp.int32, (ch, ch), 0)
    tc = lax.broadcasted_iota(jnp.int32, (ch, ch), 1)
    tri01 = jnp.where(tc <= tr, 1.0, 0.0).astype(BF16)
    lo, hi = _lane_half_masks((ch, LANES))
    hsum = _head_sum_matrix(LANES)

    def stack2(z):
        return jnp.concatenate([jnp.where(lo, z, 0.0), jnp.where(hi, z, 0.0)], axis=0)

    def dup2(z):
        return jnp.concatenate([z, z], axis=0)

    npair = B_HEADS // 2
    chains = [(sub, pr) for sub in range(nsub) for pr in range(npair)]
    cums = [_dot_exact_lhs01(tri01, lw_ref[0, sub * ch:(sub + 1) * ch, :]) for sub in range(nsub)]
    a2, r2, b2, k2, v2, bh2, kh2, decay = [], [], [], [], [], [], [], []
    for sub, pr in chains:
        rows = slice(sub * ch, (sub + 1) * ch)
        sl = slice(pr * LANES, (pr + 1) * LANES)
        k, bb = k_ref[0, rows, sl], b_ref[0, rows, sl]
        cum = cums[sub][:, sl]
        tot = cum[ch - 1:ch, :]
        e_neg = jnp.exp(-cum)
        e_rest = jnp.exp(tot - cum)
        a2.append(stack2(kk_ref[0, rows, sl] * jnp.exp(cum - lw_ref[0, rows, sl])).astype(BF16))
        r2.append(stack2(r_ref[0, rows, sl] * jnp.exp(cum)))
        b2.append(dup2(bb * e_neg).astype(BF16))
        k2.append(dup2(k * e_neg).astype(BF16))
        v2.append(stack2(v_ref[0, rows, sl]).astype(BF16))
        bh2.append(stack2(bb * e_rest).astype(BF16))
        kh2.append(stack2(k * e_rest).astype(BF16))
        decay.append(jnp.exp(tot))
    nc = len(chains)
    gram = [_dot_nt(jnp.concatenate([a2[i], r2[i].astype(BF16)], axis=0), jnp.concatenate([b2[i], k2[i]], axis=0))
            for i in range(nc)]
    ab = [jnp.where(strict, g[0:n2, 0:n2], 0.0) for g in gram]
    ak = [jnp.where(strict, g[0:n2, n2:2 * n2], 0.0).astype(BF16) for g in gram]
    rb = [jnp.where(incl, g[n2:2 * n2, 0:n2], 0.0).astype(BF16) for g in gram]
    rk = [jnp.where(incl, g[n2:2 * n2, n2:2 * n2], 0.0).astype(BF16) for g in gram]
    tinv = [eye] * nc
    for lvl in levels:
        tb = [t.astype(BF16) for t in tinv]
        tm = [_dot(tb[i], jnp.where(lvl, ab[i], 0.0).astype(BF16)).astype(BF16) for i in range(nc)]
        tinv = [tinv[i] - _dot(tm[i], tb[i]) for i in range(nc)]
    tb = [t.astype(BF16) for t in tinv]
    akv = [_dot(ak[i], v2[i]).astype(BF16) for i in range(nc)]
    wu = [_dot(tb[i], jnp.concatenate([a2[i], akv[i]], axis=1)).astype(BF16) for i in range(nc)]
    rwu = [_dot(rb[i], wu[i]) for i in range(nc)]
    rw = [(r2[i] - rwu[i][:, 0:LANES]).astype(BF16) for i in range(nc)]
    y0 = [_dot(rk[i], v2[i]) - rwu[i][:, LANES:2 * LANES] for i in range(nc)]
    gm = [_dot_tn(bh2[i], wu[i][:, 0:LANES]).astype(BF16) for i in range(nc)]
    s0t = [_dot_tn(jnp.concatenate([v2[i], wu[i][:, LANES:2 * LANES]], axis=0),
                   jnp.concatenate([kh2[i], -bh2[i]], axis=0)) for i in range(nc)]
    states = [st_scr[pr] for pr in range(npair)]
    ys = [None] * nc
    for i, (sub, pr) in enumerate(chains):
        st = states[pr]
        stb = st.astype(BF16)
        y2 = _dot_nt(rw[i], stb) + y0[i]
        ys[i] = y2[0:ch, :] + y2[ch:n2, :]
        states[pr] = st * decay[i] - _dot_nt(stb, gm[i]) + s0t[i]
    for pr in range(npair):
        st_scr[pr] = states[pr]
    tstep = nsub * ch
    y_st = jnp.concatenate([ys[sub * npair + pr] for pr in range(npair) for sub in range(nsub)], axis=0)
    rkr_st = jnp.concatenate([r_ref[0, :, pr * LANES:(pr + 1) * LANES] * k_ref[0, :, pr * LANES:(pr + 1) * LANES]
                              * rk_ref[:, pr * LANES:(pr + 1) * LANES] for pr in range(npair)], axis=0)
    sums = _dot_exact_rhs01(jnp.concatenate([y_st, rkr_st], axis=0), hsum)
    yc = y_st - sums[0:npair * tstep, :] * (1.0 / HEAD_DIM)
    var = _dot_exact_rhs01(yc * yc, hsum) * (1.0 / HEAD_DIM)
    yn = yc * lax.rsqrt(var + B_LN_EPS)
    for pr in range(npair):
        sl = slice(pr * LANES, (pr + 1) * LANES)
        rows = slice(pr * tstep, (pr + 1) * tstep)
        bonus = sums[npair * tstep + pr * tstep:npair * tstep + (pr + 1) * tstep, :] * v_ref[0, :, sl]
        out = (yn[rows, :] * lnw_ref[:, sl] + lnb_ref[:, sl] + bonus) * g_ref[0, :, sl]
        o_ref[0, :, sl] = out.astype(o_ref.dtype)


def rwkv_scan(r, k, v, lw, kk, bb, g, r_k, ln_w, ln_b, ch=RWKV_CHUNK, nsub=4):
    b, t, w = r.shape
    ch = min(ch, t)
    nsub = min(nsub, t // ch)
    tstep = ch * nsub
    seq = pl.BlockSpec((1, tstep, w), lambda i, c: (i, c, 0))
    vec = pl.BlockSpec((1, w), lambda i, c: (0, 0))
    row = lambda z: z.reshape(1, w)
    return pl.pallas_call(
        functools.partial(_rwkv_scan_kernel, ch=ch, nsub=nsub),
        grid=(b, t // tstep),
        in_specs=[seq] * 7 + [vec] * 3,
        out_specs=seq,
        out_shape=jax.ShapeDtypeStruct((b, t, w), BF16),
        scratch_shapes=[pltpu.VMEM((B_HEADS // 2, LANES, LANES), F32)],
        compiler_params=_cparams("parallel", "arbitrary"),
        name="rwkv_scan",
    )(r, k, v, lw, kk, bb, g, row(r_k), row(ln_w), row(ln_b))


def _fill_vt(v_ref_slice, vt_scr, t):
    sub = lax.broadcasted_iota(jnp.int32, (LANES, LANES), 0)

    def body(j, carry):
        start = pl.multiple_of(j * LANES, LANES)
        vt = v_ref_slice(start).T
        vt_scr[:, pl.ds(start, LANES)] = jnp.where(sub < HEAD_DIM, vt, 1.0).astype(BF16)
        return carry

    lax.fori_loop(0, t // LANES, body, 0)


def _stack_heads(src_ref, dst_scr, n_heads, scale):
    lo, hi = _lane_half_masks((QCHUNK, LANES))
    for h in range(n_heads):
        tile = src_ref[0, :, (h // 2) * LANES:(h // 2 + 1) * LANES]
        if scale != 1.0:
            tile = tile * scale
        dst_scr[h * QCHUNK:(h + 1) * QCHUNK, :] = jnp.where((lo, hi)[h % 2], tile, 0.0).astype(dst_scr.dtype)


def _online_update(scores, vt, m_scr, acc_scr):
    m_old = m_scr[...]
    m_new = jnp.maximum(m_old, jnp.concatenate([jnp.max(s, axis=0, keepdims=True) for s in scores], axis=1))
    p = jnp.concatenate([jnp.exp2(s - m_new[:, h * QCHUNK:(h + 1) * QCHUNK]).astype(BF16)
                         for h, s in enumerate(scores)], axis=1)
    acc_scr[...] = jnp.exp2(m_old - m_new) * acc_scr[...] + _dot(vt, p)
    m_scr[...] = m_new


def _score_stage(k_slab, qst_scr, mask_bias, n_heads, s_scr, mx_scr, slot):
    for pair in range(n_heads // 2):
        d = _dot_nt(k_slab, qst_scr[2 * pair * QCHUNK:(2 * pair + 2) * QCHUNK, :])
        for e in range(2):
            h = 2 * pair + e
            cols = slice(h * QCHUNK, (h + 1) * QCHUNK)
            s = d[:, e * QCHUNK:(e + 1) * QCHUNK] + mask_bias(h)
            s_scr[slot, :, cols] = s
            mx_scr[slot, :, cols] = jnp.max(s, axis=0, keepdims=True)


def _softmax_stage(vt, n_heads, s_scr, mx_scr, slot, m_scr, acc_scr):
    m_old = m_scr[...]
    m_new = jnp.maximum(m_old, mx_scr[slot])
    alpha = jnp.exp2(m_old - m_new)
    m_scr[...] = m_new
    for pair in range(n_heads // 2):
        cols = slice(2 * pair * QCHUNK, (2 * pair + 2) * QCHUNK)
        p = jnp.exp2(s_scr[slot, :, cols] - m_new[:, cols]).astype(BF16)
        acc_scr[:, cols] = alpha[:, cols] * acc_scr[:, cols] + _dot(vt, p)


def _pipelined_groups(ngroup, s_matmul, process):
    last = ngroup - 1
    s_matmul(0, 0)

    def pair_body(i, carry):
        g0 = 2 * i
        s_matmul(g0 + 1, 1)
        process(g0, 0)
        s_matmul(jnp.minimum(g0 + 2, last), 0)
        process(g0 + 1, 1)
        return carry

    lax.fori_loop(0, ngroup // 2, pair_body, 0)

    @pl.when(ngroup % 2 == 1)
    def _():
        process(last, 0)


def _bias_rows(tabt_ref, h, d_first, n_tiles):
    tiles = [tabt_ref[jnp.clip(d_first - u, 0, BIAS_TILES - 1), h] for u in range(n_tiles)]
    return jnp.concatenate(tiles, axis=0)


def _write_heads(o_ref, acc_scr, n_heads):
    for p in range(n_heads // 2):
        parts = []
        for h in (2 * p, 2 * p + 1):
            a = acc_scr[:, h * QCHUNK:(h + 1) * QCHUNK]
            parts.append(a[0:HEAD_DIM, :] * (1.0 / a[HEAD_DIM:HEAD_DIM + 1, :]))
        o_ref[0, :, p * LANES:(p + 1) * LANES] = jnp.concatenate(parts, axis=0).T.astype(o_ref.dtype)


def _moba_kernel(q_ref, k_ref, v_ref, tabt_ref, o_ref,
                 k_scr, vt_scr, km_scr, qst_scr, pick_scr, s_scr, mx_scr, m_scr, acc_scr):
    c = pl.program_id(2)
    t = k_ref.shape[1]
    nblk = t // MOBA_BLOCK
    nbp = km_scr.shape[0]
    group = C_HEADS // C_KV_HEADS

    @pl.when(c == 0)
    def _():
        kf = k_ref[0]
        k_scr[...] = kf.astype(BF16)
        km_scr[...] = jnp.zeros_like(km_scr)
        km_scr[0:nblk, :] = jnp.mean(kf.reshape(nblk, MOBA_BLOCK, LANES), axis=1)
        _fill_vt(lambda start: v_ref[0, pl.ds(start, LANES), :], vt_scr, t)

    own = (c * QCHUNK) // MOBA_BLOCK
    _stack_heads(q_ref, qst_scr, group, HEAD_DIM ** -0.5 * LOG2E)

    qsum = q_ref[0, :, 0:LANES]
    for p in range(1, group // 2):
        qsum = qsum + q_ref[0, :, p * LANES:(p + 1) * LANES]
    gate = _dot_nt_x(km_scr[...], qsum)
    blk = lax.broadcasted_iota(jnp.int32, (nbp, QCHUNK), 0)
    gate = jnp.where(blk < own, gate, -jnp.inf)
    rank = jnp.zeros((nbp, QCHUNK), jnp.int32)
    for mblk in range(nblk):
        gm = gate[mblk:mblk + 1, :]
        ahead = jnp.logical_or(gm > gate, jnp.logical_and(gm == gate, mblk < blk))
        rank = rank + jnp.where(ahead, 1, 0)
    chosen = jnp.logical_and(rank < MOBA_TOPK, blk < own)
    pick_scr[...] = jnp.where(chosen, 0.0, MASKED)

    m_scr[...] = jnp.full(m_scr.shape, M_INIT, F32)
    acc_scr[...] = jnp.zeros_like(acc_scr)
    grp = 2 * MOBA_BLOCK
    tiles_per_grp = grp // LANES
    ngroup = own // 2 + 1
    grow = lax.broadcasted_iota(jnp.int32, (grp, QCHUNK), 0)
    second = grow >= MOBA_BLOCK
    qpos = c * QCHUNK + lax.broadcasted_iota(jnp.int32, (grp, QCHUNK), 1)

    def score_stage(gi, slot):
        start = pl.multiple_of(gi * grp, grp)
        causal = jnp.where((gi * grp + grow) <= qpos, 0.0, MASKED)
        first = jnp.where(2 * gi < own, pick_scr[pl.ds(2 * gi, 1), :], causal)
        later = jnp.where(2 * gi + 1 < own, pick_scr[pl.ds(2 * gi + 1, 1), :], causal)
        okb = jnp.where(second, later, first)
        d_first = c - tiles_per_grp * gi
        _score_stage(k_scr[pl.ds(start, grp), :], qst_scr,
                     lambda g: _bias_rows(tabt_ref, g, d_first, tiles_per_grp) + okb, group, s_scr, mx_scr, slot)

    def softmax_stage(gi, slot):
        start = pl.multiple_of(gi * grp, grp)
        _softmax_stage(vt_scr[:, pl.ds(start, grp)], group, s_scr, mx_scr, slot, m_scr, acc_scr)

    _pipelined_groups(ngroup, score_stage, softmax_stage)
    _write_heads(o_ref, acc_scr, group)


def moba_attention(qc, kvc, tabt):
    b, t, _ = qc.shape
    assert t % MOBA_BLOCK == 0
    nq = t // QCHUNK
    group = C_HEADS // C_KV_HEADS
    qw = group * HEAD_DIM
    nbp = -(-(t // MOBA_BLOCK) // SUBLANES) * SUBLANES
    return pl.pallas_call(
        _moba_kernel,
        grid=(b, C_KV_HEADS, nq),
        in_specs=[pl.BlockSpec((1, QCHUNK, qw), lambda i, h, c: (i, c, h)),
                  pl.BlockSpec((1, t, LANES), lambda i, h, c: (i, 0, h)),
                  pl.BlockSpec((1, t, LANES), lambda i, h, c: (i, 0, C_KV_HEADS + h)),
                  pl.BlockSpec((BIAS_TILES, group, LANES, LANES), lambda i, h, c: (0, h, 0, 0))],
        out_specs=pl.BlockSpec((1, QCHUNK, qw), lambda i, h, c: (i, c, h)),
        out_shape=jax.ShapeDtypeStruct((b, t, C_HEADS * HEAD_DIM), BF16),
        scratch_shapes=[pltpu.VMEM((t, LANES), BF16), pltpu.VMEM((LANES, t), BF16),
                        pltpu.VMEM((nbp, LANES), F32),
                        pltpu.VMEM((group * QCHUNK, LANES), BF16),
                        pltpu.VMEM((nbp, QCHUNK), F32),
                        pltpu.VMEM((2, 2 * MOBA_BLOCK, group * QCHUNK), F32),
                        pltpu.VMEM((2, 1, group * QCHUNK), F32),
                        pltpu.VMEM((1, group * QCHUNK), F32),
                        pltpu.VMEM((LANES, group * QCHUNK), F32)],
        compiler_params=_cparams("parallel", "parallel", "arbitrary"),
        name="moba_attention",
    )(qc, kvc, kvc, tabt)


def _dsa_kernel(q_ref, kv_ref, qi_ref, ki_ref, gk_ref, tabt_ref, o_ref,
                k_scr, vt_scr, ki_scr, sc_scr, qst_scr, qist_scr, pre_scr, s_scr, mx_scr, m_scr, acc_scr, *, topk):
    c = pl.program_id(1)
    t = k_scr.shape[0]
    grp = KEY_GROUP * LANES
    ngroup = c // KEY_GROUP + 1
    sub = lax.broadcasted_iota(jnp.int32, (LANES, QCHUNK), 0)
    gsub = lax.broadcasted_iota(jnp.int32, (grp, QCHUNK), 0)
    qpos = c * QCHUNK + lax.broadcasted_iota(jnp.int32, (grp, QCHUNK), 1)

    @pl.when(c == 0)
    def _():
        k_scr[...] = kv_ref[0, :, 0:LANES].astype(BF16)
        ki_scr[...] = _rms(ki_ref[0], gk_ref[...]).astype(BF16)
        _fill_vt(lambda start: kv_ref[0, pl.ds(start, LANES), LANES:2 * LANES], vt_scr, t)

    _stack_heads(q_ref, qst_scr, D_HEADS, HEAD_DIM ** -0.5 * LOG2E)
    _stack_heads(qi_ref, qist_scr, IDX_HEADS, 1.0)
    w_t = (qi_ref[0, :, IDX_HEADS * IDX_DIM + LANES:IDX_HEADS * IDX_DIM + 2 * LANES]
           * ((IDX_HEADS * IDX_DIM) ** -0.5)).T

    def score_body(gi, carry):
        start = pl.multiple_of(gi * grp, grp)
        rel = jnp.maximum(_dot_nt(ki_scr[pl.ds(start, grp), :], qist_scr[...]), 0.0)
        sc = jnp.zeros((grp, QCHUNK), F32)
        for h in range(IDX_HEADS):
            sc = sc + rel[:, h * QCHUNK:(h + 1) * QCHUNK] * w_t[h:h + 1, :]
        sc = jnp.where(sc == 0.0, 0.0, sc)
        sc_scr[pl.ds(start, grp), :] = jnp.where((gi * grp + gsub) <= qpos, sc, -jnp.inf)
        return carry

    lax.fori_loop(0, ngroup, score_body, 0)

    def count(pred):
        def body(gi, acc):
            for u in range(KEY_GROUP):
                start = pl.multiple_of(gi * grp + u * LANES, LANES)
                acc = jnp.where(pred(sc_scr[pl.ds(start, LANES), :]), acc + 1, acc)
            return acc
        acc = lax.fori_loop(0, ngroup, body, jnp.zeros((LANES, QCHUNK), jnp.int32))
        return jnp.sum(acc, axis=0, keepdims=True)

    def key_to_float(key):
        return pltpu.bitcast(key ^ (lax.shift_right_arithmetic(key, 31) & 0x7FFFFFFF), F32)

    cnt0 = count(lambda st: st >= 0.0)
    code = jnp.where(cnt0 >= topk, 0, INT_MIN).astype(jnp.int32)

    def bit_body(i, code):
        cand = code + lax.shift_left(jnp.int32(1), 30 - i)
        cand_f = key_to_float(cand)
        cnt = count(lambda st: st >= cand_f)
        return jnp.where(cnt >= topk, cand, code)

    code = lax.fori_loop(0, 31, bit_body, code)
    thr = jnp.where(code == INT_MIN, -jnp.inf, key_to_float(jnp.maximum(code, INT_MIN + 1)))
    thr_eff = jnp.maximum(thr, LOWEST_F32)

    cnt_ge = count(lambda st: st >= thr_eff)
    tied = cnt_ge > topk

    @pl.when(jnp.max(jnp.where(tied, 1.0, 0.0)) > 0.5)
    def _():
        def tally_body(j, carry):
            ties_before, above = carry
            start = pl.multiple_of(j * LANES, LANES)
            st = sc_scr[pl.ds(start, LANES), :]
            pre_scr[pl.ds(j, 1), :] = ties_before
            ties_before = ties_before + jnp.sum(jnp.where(st == thr, 1.0, 0.0), axis=0, keepdims=True)
            above = above + jnp.sum(jnp.where(st > thr, 1.0, 0.0), axis=0, keepdims=True)
            return ties_before, above

        zero_row = jnp.zeros((1, QCHUNK), F32)
        _, above = lax.fori_loop(0, ngroup * KEY_GROUP, tally_body, (zero_row, zero_row))
        need = jnp.where(tied, topk - above, 3.0e38)
        tri = jnp.where(lax.broadcasted_iota(jnp.int32, (LANES, LANES), 1) <= sub, 1.0, 0.0).astype(BF16)

        def demote_body(gi, carry):
            tiles = []
            for u in range(KEY_GROUP):
                start = pl.multiple_of(gi * grp + u * LANES, LANES)
                tiles.append(sc_scr[pl.ds(start, LANES), :])
            eq01 = jnp.concatenate([jnp.where(st == thr, 1.0, 0.0).astype(BF16) for st in tiles], axis=1)
            within = _dot(tri, eq01)
            for u, st in enumerate(tiles):
                start = pl.multiple_of(gi * grp + u * LANES, LANES)
                rank = pre_scr[pl.ds(gi * KEY_GROUP + u, 1), :] + within[:, u * QCHUNK:(u + 1) * QCHUNK]
                drop = jnp.logical_and(st == thr, rank > need)
                sc_scr[pl.ds(start, LANES), :] = jnp.where(drop, -jnp.inf, st)
            return carry

        lax.fori_loop(0, ngroup, demote_body, 0)

    m_scr[...] = jnp.full(m_scr.shape, M_INIT, F32)
    acc_scr[...] = jnp.zeros_like(acc_scr)

    def score_stage(gi, slot):
        start = pl.multiple_of(gi * grp, grp)
        okb = jnp.where(sc_scr[pl.ds(start, grp), :] >= thr_eff, 0.0, MASKED)
        d_first = c - gi * KEY_GROUP
        _score_stage(k_scr[pl.ds(start, grp), :], qst_scr,
                     lambda h: _bias_rows(tabt_ref, h, d_first, KEY_GROUP) + okb, D_HEADS, s_scr, mx_scr, slot)

    def softmax_stage(gi, slot):
        start = pl.multiple_of(gi * grp, grp)
        _softmax_stage(vt_scr[:, pl.ds(start, grp)], D_HEADS, s_scr, mx_scr, slot, m_scr, acc_scr)

    _pipelined_groups(ngroup, score_stage, softmax_stage)
    _write_heads(o_ref, acc_scr, D_HEADS)


def dsa_attention(qd, kvd, idx, kidx_gain, tabt):
    b, t, _ = qd.shape
    topk = min(DSA_TOPK, t // 4)
    nq = t // QCHUNK
    gk = jnp.concatenate([kidx_gain, kidx_gain]).reshape(1, LANES)
    wq = D_HEADS * HEAD_DIM
    return pl.pallas_call(
        functools.partial(_dsa_kernel, topk=topk),
        grid=(b, nq),
        in_specs=[pl.BlockSpec((1, QCHUNK, wq), lambda i, c: (i, c, 0)),
                  pl.BlockSpec((1, t, 2 * LANES), lambda i, c: (i, 0, 0)),
                  pl.BlockSpec((1, QCHUNK, 4 * LANES), lambda i, c: (i, c, 0)),
                  pl.BlockSpec((1, t, LANES), lambda i, c: (i, 0, 2)),
                  pl.BlockSpec((1, LANES), lambda i, c: (0, 0)),
                  pl.BlockSpec((BIAS_TILES, D_HEADS, LANES, LANES), lambda i, c: (0, 0, 0, 0))],
        out_specs=pl.BlockSpec((1, QCHUNK, wq), lambda i, c: (i, c, 0)),
        out_shape=jax.ShapeDtypeStruct((b, t, wq), BF16),
        scratch_shapes=[pltpu.VMEM((t, LANES), BF16), pltpu.VMEM((LANES, t), BF16), pltpu.VMEM((t, LANES), BF16),
                        pltpu.VMEM((t, QCHUNK), F32),
                        pltpu.VMEM((D_HEADS * QCHUNK, LANES), BF16), pltpu.VMEM((IDX_HEADS * QCHUNK, LANES), BF16),
                        pltpu.VMEM((t // LANES, QCHUNK), F32),
                        pltpu.VMEM((2, KEY_GROUP * LANES, D_HEADS * QCHUNK), F32),
                        pltpu.VMEM((2, 1, D_HEADS * QCHUNK), F32),
                        pltpu.VMEM((1, D_HEADS * QCHUNK), F32),
                        pltpu.VMEM((LANES, D_HEADS * QCHUNK), F32)],
        compiler_params=_cparams("parallel", "arbitrary"),
        name="dsa_attention",
    )(qd, kvd, idx, idx, gk, tabt)


def _dup_heads(wcols, n_heads):
    d = wcols.shape[0]
    wh = wcols.reshape(d, n_heads, 1, HEAD_DIM)
    return jnp.broadcast_to(wh, (d, n_heads, 2, HEAD_DIM)).reshape(d, n_heads * 2 * HEAD_DIM)


def _even_weight(w_in):
    aq, akv = A_HEADS * HEAD_DIM, A_KV_HEADS * HEAD_DIM
    q = w_in[:, :aq]
    k = _dup_heads(w_in[:, aq:aq + akv], A_KV_HEADS)
    v = _dup_heads(w_in[:, aq + akv:aq + 2 * akv], A_KV_HEADS)
    rest = w_in[:, aq + 2 * akv:]
    return jnp.concatenate([q, k, v, rest], axis=1).astype(BF16)


def _odd_weight(w_in):
    d = w_in.shape[0]
    cq, ckv, dq = C_HEADS * HEAD_DIM, C_KV_HEADS * HEAD_DIM, D_HEADS * HEAD_DIM
    cuts = np.cumsum([cq, ckv, ckv, dq, HEAD_DIM, HEAD_DIM, IDX_HEADS * IDX_DIM, IDX_DIM]).tolist()
    qc, kc, vc, qd, kd, vd, qi, ki, wi = jnp.split(w_in, cuts, axis=1)
    pad = jnp.zeros((d, LANES - IDX_HEADS), w_in.dtype)
    cols = [qc, _dup_heads(kc, C_KV_HEADS), _dup_heads(vc, C_KV_HEADS),
            qd, _dup_heads(kd, 1), _dup_heads(vd, 1),
            qi, _dup_heads(ki, 1), wi, pad]
    return jnp.concatenate(cols, axis=1).astype(BF16)


def kernel(x, mem, rel_bias, norm_gains, mix_w_out, mem_norm, x_wq, x_wkv, x_wo, ff_w1, ff_w2, ev_w_in, a_sinks, b_mu, b_w0, b_w2, b_a0, b_a2, b_g2, b_k_k, b_k_a, b_r_k, b_ln_w, b_ln_b, od_w_in, d_kidx_norm):
    bsz, t, d = x.shape
    n = bsz * t
    depth = norm_gains.shape[0]
    mlen = mem.shape[1]
    tabt = bias_table(rel_bias, transposed=True)
    x2 = x.reshape(n, d)
    for layer in range(depth):
        i = layer // 2
        g = norm_gains[layer]
        if layer % 2 == 0:
            b_in = 3 * B_WIDTH + B_DECAY_RANK + B_A_RANK + B_GATE_RANK
            qkv, pb = norm_proj(x2, g[0], _even_weight(ev_w_in[i]), (1024, b_in), (F32, F32))
            ya = swa_attention(qkv.reshape(bsz, t, 1024), a_sinks[i], rel_bias, tabt)
            prep = rwkv_prep(pb.reshape(bsz, t, b_in), b_mu[i], b_w0[i], b_w2[i], b_a0[i], b_a2[i], b_g2[i],
                             b_k_k[i], b_k_a[i])
            yb = rwkv_scan(*prep, b_r_k[i].reshape(-1), b_ln_w[i], b_ln_b[i])
        else:
            qc, kvc, qd, kvd, idx = norm_proj(x2, g[0], _odd_weight(od_w_in[i]),
                                              (512, 512, 512, 256, 512), (F32,) * 5)
            ya = moba_attention(qc.reshape(bsz, t, 512), kvc.reshape(bsz, t, 512), tabt)
            yb = dsa_attention(qd.reshape(bsz, t, 512), kvd.reshape(bsz, t, 256), idx.reshape(bsz, t, 512),
                               d_kidx_norm[i], tabt)
        kmem, vmem = norm_proj(mem.reshape(bsz * mlen, d), mem_norm[layer], x_wkv[layer].astype(BF16),
                               (X_WIDTH, X_WIDTH), (BF16, BF16))
        x3 = post_mixer(ya, yb, x2.reshape(bsz, t, d), kmem.reshape(bsz, mlen, X_WIDTH),
                        vmem.reshape(bsz, mlen, X_WIDTH), mix_w_out[layer], x_wq[layer], x_wo[layer],
                        ff_w1[layer], ff_w2[layer], g)
        x2 = x3.reshape(n, d)
    return x2.reshape(bsz, t, d)
```

```python
import functools

import numpy as np
import jax
import jax.numpy as jnp
from jax import lax
from jax.experimental import pallas as pl
from jax.experimental.pallas import tpu as pltpu

F32 = jnp.float32
BF16 = jnp.bfloat16
HIGHEST = lax.Precision.HIGHEST

LANES = 128
SUBLANES = 8
HEAD_DIM = 64
EPS = 1e-6
A_HEADS, A_KV_HEADS, BAND = 8, 2, 128
B_HEADS, B_WIDTH = 8, 512
B_DECAY_RANK, B_A_RANK, B_GATE_RANK = 64, 64, 128
B_LN_EPS = 64e-5
RWKV_CHUNK = 64
C_HEADS, C_KV_HEADS, MOBA_BLOCK, MOBA_TOPK, QCHUNK = 8, 2, 256, 3, 128
D_HEADS, IDX_HEADS, IDX_DIM, DSA_TOPK = 8, 4, 64, 256
N_BUCKETS, BIAS_HEADS = 32, 8
BIAS_TILES = 9
X_HEADS, X_HEAD_DIM, X_WIDTH = 4, 128, 512

LOG2E = 1.4426950408889634
KEY_GROUP = 4
MASKED = -2e30
M_INIT = -1e30
INT_MIN = -(2 ** 31)
LOWEST_F32 = -3.4028234663852886e38
VMEM_LIMIT = 56 * 1024 * 1024


def _cparams(*sem):
    return pltpu.CompilerParams(dimension_semantics=tuple(sem), vmem_limit_bytes=VMEM_LIMIT)


def _dot(a, b):
    return jnp.dot(a, b, preferred_element_type=F32)


def _dot_nt(a, b):
    return lax.dot_general(a, b, (((1,), (1,)), ((), ())), preferred_element_type=F32)


def _dot_tn(a, b):
    return lax.dot_general(a, b, (((0,), (0,)), ((), ())), preferred_element_type=F32)


def _dot_x(a, b):
    return jnp.dot(a, b, preferred_element_type=F32, precision=HIGHEST)


def _dot_nt_x(a, b):
    return lax.dot_general(a, b, (((1,), (1,)), ((), ())), preferred_element_type=F32, precision=HIGHEST)


def _rms(xf, g):
    ms = jnp.mean(xf * xf, axis=-1, keepdims=True)
    return xf * lax.rsqrt(ms + EPS) * g


def _lane_half_masks(shape):
    lane = lax.broadcasted_iota(jnp.int32, shape, len(shape) - 1)
    lo = lane < HEAD_DIM
    return lo, jnp.logical_not(lo)


def _bucket_of_distance(n):
    exact = N_BUCKETS // 2
    if n < exact:
        return n
    j = 0
    while n ** 8 >= (exact ** 8) * (2 ** (3 * (j + 1))):
        j += 1
    return min(exact + j, N_BUCKETS - 1)


@functools.lru_cache(maxsize=None)
def _bucket_tiles(transposed):
    by_dist = np.array([_bucket_of_distance(n) for n in range(BIAS_TILES * LANES + LANES)], np.int32)
    q = np.arange(LANES)[None, :] if transposed else np.arange(LANES)[:, None]
    k = np.arange(LANES)[:, None] if transposed else np.arange(LANES)[None, :]
    tiles = [by_dist[np.maximum(d * LANES + q - k, 0)] for d in range(BIAS_TILES)]
    return np.stack(tiles).astype(np.int32)


def _bias_table_kernel(rb_ref, bkt_ref, out_ref, *, base2_shifted):
    bkt = bkt_ref[0]
    for h in range(BIAS_HEADS):
        acc = jnp.zeros((LANES, LANES), F32)
        for b in range(N_BUCKETS):
            acc = jnp.where(bkt == b, rb_ref[b, h], acc)
        if base2_shifted:
            acc = (acc - rb_ref[N_BUCKETS - 1, h]) * LOG2E
        out_ref[0, h] = acc


def bias_table(rel_bias, transposed=False):
    return pl.pallas_call(
        functools.partial(_bias_table_kernel, base2_shifted=transposed),
        grid=(BIAS_TILES,),
        in_specs=[pl.BlockSpec(memory_space=pltpu.SMEM),
                  pl.BlockSpec((1, LANES, LANES), lambda d: (d, 0, 0))],
        out_specs=pl.BlockSpec((1, BIAS_HEADS, LANES, LANES), lambda d: (d, 0, 0, 0)),
        out_shape=jax.ShapeDtypeStruct((BIAS_TILES, BIAS_HEADS, LANES, LANES), F32),
        compiler_params=_cparams("arbitrary"),
        name="bias_table",
    )(rel_bias, jnp.asarray(_bucket_tiles(transposed)))


def _norm_proj_kernel(x_ref, g_ref, w_ref, *out_refs, splits):
    h = _rms(x_ref[...], g_ref[...]).astype(BF16)
    res = _dot(h, w_ref[...])
    off = 0
    for o_ref, width in zip(out_refs, splits):
        o_ref[...] = res[:, off:off + width].astype(o_ref.dtype)
        off += width


def norm_proj(x2, gain, w, splits, out_dtypes, tm=256):
    n, d = x2.shape
    tm = min(tm, n)
    total = sum(splits)
    assert w.shape == (d, total) and n % tm == 0 and all(s % LANES == 0 for s in splits)
    return pl.pallas_call(
        functools.partial(_norm_proj_kernel, splits=tuple(splits)),
        grid=(n // tm,),
        in_specs=[pl.BlockSpec((tm, d), lambda i: (i, 0)),
                  pl.BlockSpec((1, d), lambda i: (0, 0)),
                  pl.BlockSpec((d, total), lambda i: (0, 0))],
        out_specs=[pl.BlockSpec((tm, s), lambda i: (i, 0)) for s in splits],
        out_shape=[jax.ShapeDtypeStruct((n, s), dt) for s, dt in zip(splits, out_dtypes)],
        compiler_params=_cparams("parallel"),
        name="norm_proj",
    )(x2, gain.reshape(1, d), w)


def _post_mixer_kernel(ya_ref, yb_ref, x_ref, k_ref, v_ref, wa_ref, wb_ref, wq_ref, wo_ref, w1_ref, w2_ref, g_ref,
                       o_ref, *, tf):
    x = x_ref[0]
    z = _dot(ya_ref[0], wa_ref[...]) + _dot(yb_ref[0], wb_ref[...])
    x = x + _rms(z, g_ref[1:2, :])
    q = _dot(_rms(x, g_ref[2:3, :]).astype(BF16), wq_ref[...])
    k = k_ref[0]
    v = v_ref[0]
    outs = []
    for hd in range(X_HEADS):
        sl = slice(hd * X_HEAD_DIM, (hd + 1) * X_HEAD_DIM)
        s = _dot_nt(q[:, sl].astype(BF16), k[:, sl]) * (X_HEAD_DIM ** -0.5)
        m = jnp.max(s, axis=-1, keepdims=True)
        p = jnp.exp(s - m)
        p = p / jnp.sum(p, axis=-1, keepdims=True)
        outs.append(_dot(p.astype(BF16), v[:, sl]))
    z = _dot(jnp.concatenate(outs, axis=-1).astype(BF16), wo_ref[...])
    x = x + _rms(z, g_ref[3:4, :])
    h = _rms(x, g_ref[4:5, :]).astype(BF16)
    acc = jnp.zeros(x.shape, F32)
    for j in range(w1_ref.shape[1] // tf):
        a = jnp.maximum(_dot(h, w1_ref[:, j * tf:(j + 1) * tf]), 0.0)
        acc = acc + _dot((a * a).astype(BF16), w2_ref[j * tf:(j + 1) * tf, :])
    o_ref[0] = x + _rms(acc, g_ref[5:6, :])


def post_mixer(ya, yb, x3, kmem, vmem, w_out, wq, wo, w1, w2, gains, tq=512, tf=1024):
    b, t, d = x3.shape
    tq = min(tq, t)
    m = kmem.shape[1]
    ka, kb = ya.shape[2], yb.shape[2]
    f = w1.shape[1]
    once = pl.Buffered(1)
    const = lambda shape: pl.BlockSpec(shape, lambda i, j: (0, 0), pipeline_mode=once)
    return pl.pallas_call(
        functools.partial(_post_mixer_kernel, tf=tf),
        grid=(b, t // tq),
        in_specs=[pl.BlockSpec((1, tq, ka), lambda i, j: (i, j, 0)),
                  pl.BlockSpec((1, tq, kb), lambda i, j: (i, j, 0)),
                  pl.BlockSpec((1, tq, d), lambda i, j: (i, j, 0)),
                  pl.BlockSpec((1, m, X_WIDTH), lambda i, j: (i, 0, 0)),
                  pl.BlockSpec((1, m, X_WIDTH), lambda i, j: (i, 0, 0)),
                  const((ka, d)), const((kb, d)), const((d, X_WIDTH)), const((X_WIDTH, d)),
                  const((d, f)), const((f, d)), const(gains.shape)],
        out_specs=pl.BlockSpec((1, tq, d), lambda i, j: (i, j, 0)),
        out_shape=jax.ShapeDtypeStruct((b, t, d), F32),
        compiler_params=_cparams("parallel", "parallel"),
        name="post_mixer",
    )(ya, yb, x3, kmem, vmem, w_out[:ka].astype(BF16), w_out[ka:].astype(BF16), wq.astype(BF16), wo.astype(BF16),
      w1.astype(BF16), w2.astype(BF16), gains)


def _swa_kernel(sink_ref, rb_ref, q_ref, kv_ref, tabt_ref, o_ref, qst_scr, vt_scr):
    n = pl.program_id(1)
    group = A_HEADS // A_KV_HEADS
    first_band = jnp.maximum(n - 1, 0)
    start = pl.multiple_of(first_band * BAND, BAND)
    slab = 2 * BAND
    _stack_heads(q_ref, qst_scr, A_HEADS, HEAD_DIM ** -0.5 * LOG2E)
    kpos = start + lax.broadcasted_iota(jnp.int32, (slab, BAND), 0)
    qpos = n * BAND + lax.broadcasted_iota(jnp.int32, (slab, BAND), 1)
    dist = qpos - kpos
    okb = jnp.where(jnp.logical_and(dist >= 0, dist < BAND), 0.0, MASKED)
    sub = lax.broadcasted_iota(jnp.int32, (LANES, slab), 0)
    koff = 0
    voff = A_KV_HEADS * LANES
    d_first = n - first_band
    scores, vts = [], []
    for hkv in range(A_KV_HEADS):
        k2 = kv_ref[0, pl.ds(start, slab), koff + hkv * LANES:koff + (hkv + 1) * LANES].astype(BF16)
        v2 = kv_ref[0, pl.ds(start, slab), voff + hkv * LANES:voff + (hkv + 1) * LANES]
        vt_scr[hkv] = jnp.where(sub < HEAD_DIM, v2.T, 1.0).astype(BF16)
        vts.append(vt_scr[hkv])
        s_all = _dot_nt(k2, qst_scr[hkv * group * BAND:(hkv + 1) * group * BAND, :])
        for g in range(group):
            h = hkv * group + g
            scores.append(s_all[:, g * BAND:(g + 1) * BAND] + (_bias_rows(tabt_ref, h, d_first, 2) + okb))
    sinks = [(sink_ref[h] - rb_ref[N_BUCKETS - 1, h]) * LOG2E for h in range(A_HEADS)]
    m = [jnp.maximum(jnp.max(scores[h], axis=0, keepdims=True), sinks[h]) for h in range(A_HEADS)]
    p = [jnp.exp2(scores[h] - m[h]).astype(BF16) for h in range(A_HEADS)]
    outs = []
    for hkv in range(A_KV_HEADS):
        pv = _dot(vts[hkv], jnp.concatenate(p[hkv * group:(hkv + 1) * group], axis=1))
        for g in range(group):
            h = hkv * group + g
            a = pv[:, g * BAND:(g + 1) * BAND]
            den = a[HEAD_DIM:HEAD_DIM + 1, :] + jnp.exp2(sinks[h] - m[h])
            outs.append(a[0:HEAD_DIM, :] * (1.0 / den))
    for pair in range(A_HEADS // 2):
        tile = jnp.concatenate(outs[2 * pair:2 * pair + 2], axis=0).T
        o_ref[0, :, pair * LANES:(pair + 1) * LANES] = tile.astype(o_ref.dtype)


def swa_attention(qkv, sinks, rel_bias, tabt):
    b, t, w = qkv.shape
    qw = A_HEADS * HEAD_DIM
    return pl.pallas_call(
        _swa_kernel,
        grid=(b, t // BAND),
        in_specs=[pl.BlockSpec(memory_space=pltpu.SMEM),
                  pl.BlockSpec(memory_space=pltpu.SMEM),
                  pl.BlockSpec((1, BAND, qw), lambda i, n: (i, n, 0)),
                  pl.BlockSpec((1, t, w - qw), lambda i, n: (i, 0, 1)),
                  pl.BlockSpec((2, A_HEADS, LANES, LANES), lambda i, n: (0, 0, 0, 0))],
        out_specs=pl.BlockSpec((1, BAND, qw), lambda i, n: (i, n, 0)),
        out_shape=jax.ShapeDtypeStruct((b, t, qw), BF16),
        scratch_shapes=[pltpu.VMEM((A_HEADS * BAND, LANES), BF16), pltpu.VMEM((A_KV_HEADS, LANES, 2 * BAND), BF16)],
        compiler_params=_cparams("parallel", "arbitrary"),
        name="swa_attention",
    )(sinks, rel_bias, qkv, qkv, tabt)


def _split3(x):
    p0 = x.astype(BF16)
    r1 = x - p0.astype(F32)
    p1 = r1.astype(BF16)
    p2 = (r1 - p1.astype(F32)).astype(BF16)
    return p0, p1, p2


def _dot_exact_rhs01(x, m01):
    p0, p1, p2 = _split3(x)
    return _dot(p0, m01) + _dot(p1, m01) + _dot(p2, m01)


def _dot_exact_lhs01(m01, x):
    p0, p1, p2 = _split3(x)
    return _dot(m01, p0) + _dot(m01, p1) + _dot(m01, p2)


def _head_sum_matrix(width):
    r = lax.broadcasted_iota(jnp.int32, (width, width), 0) // HEAD_DIM
    c = lax.broadcasted_iota(jnp.int32, (width, width), 1) // HEAD_DIM
    return jnp.where(r == c, 1.0, 0.0).astype(BF16)


def _sigmoid(z):
    return 1.0 / (1.0 + jnp.exp(-z))


def _softplus(z):
    return jnp.maximum(z, 0.0) + jnp.log(1.0 + jnp.exp(-jnp.abs(z)))


def _rwkv_prep_kernel(p_ref, pprev_ref, mu_ref, wa2_ref, g2_ref, w0_ref, a0_ref, kk_ref, ka_ref,
                      r_out, k_out, v_out, lw_out, kk_out, b_out, g_out):
    i = pl.program_id(1)
    p = p_ref[0]
    w = B_WIDTH
    last = pprev_ref[0, SUBLANES - 1:SUBLANES, :]
    last = jnp.where(i > 0, last, 0.0)
    row = lax.broadcasted_iota(jnp.int32, p.shape, 0)
    prev = jnp.where(row == 0, last, pltpu.roll(p, 1, 0))
    xs = p + (prev - p) * mu_ref[...]
    r = xs[:, 0:w]
    k = xs[:, w:2 * w]
    v = xs[:, 2 * w:3 * w]
    wa_lo = xs[:, 3 * w:3 * w + LANES]
    g_lo = xs[:, 3 * w + LANES:3 * w + 2 * LANES]
    lo, _ = _lane_half_masks(wa_lo.shape)
    wa_in = jnp.where(lo, jnp.tanh(wa_lo), wa_lo)
    wa = _dot_x(wa_in, wa2_ref[...])
    wlog = -_softplus(-(w0_ref[...] + wa[:, 0:w])) - 0.5
    a = _sigmoid(a0_ref[...] + wa[:, w:2 * w])
    g = _dot(_sigmoid(g_lo).astype(BF16), g2_ref[...])
    kk = k * kk_ref[...]
    ss = _dot_exact_rhs01(kk * kk, _head_sum_matrix(w))
    kk = kk / jnp.maximum(jnp.sqrt(ss), 1e-12)
    kmod = k * (1.0 + (a - 1.0) * ka_ref[...])
    r_out[0] = r
    k_out[0] = kmod
    v_out[0] = v
    lw_out[0] = -jnp.exp(wlog)
    kk_out[0] = kk
    b_out[0] = kk * a
    g_out[0] = g


def rwkv_prep(pb, mu, w0, w2, a0, a2, g2, k_k, k_a, tc=256):
    b, t, win = pb.shape
    tc = min(tc, t)
    w = B_WIDTH
    wa2 = jnp.zeros((LANES, 2 * w), F32)
    wa2 = wa2.at[:B_DECAY_RANK, :w].set(w2).at[B_DECAY_RANK:, w:].set(a2)
    row = lambda z: z.reshape(1, -1)
    vec = pl.BlockSpec((1, w), lambda i, j: (0, 0))
    outs = [jax.ShapeDtypeStruct((b, t, w), F32)] * 7
    return pl.pallas_call(
        _rwkv_prep_kernel,
        grid=(b, t // tc),
        in_specs=[pl.BlockSpec((1, tc, win), lambda i, j: (i, j, 0)),
                  pl.BlockSpec((1, SUBLANES, win), lambda i, j: (i, jnp.maximum(j * (tc // SUBLANES) - 1, 0), 0)),
                  pl.BlockSpec((1, win), lambda i, j: (0, 0)),
                  pl.BlockSpec((LANES, 2 * w), lambda i, j: (0, 0)),
                  pl.BlockSpec((B_GATE_RANK, w), lambda i, j: (0, 0)),
                  vec, vec, vec, vec],
        out_specs=[pl.BlockSpec((1, tc, w), lambda i, j: (i, j, 0))] * 7,
        out_shape=outs,
        compiler_params=_cparams("parallel", "arbitrary"),
        name="rwkv_prep",
    )(pb, pb, row(mu), wa2, g2.astype(BF16), row(w0), row(a0), row(k_k), row(k_a))


def _rwkv_scan_kernel(r_ref, k_ref, v_ref, lw_ref, kk_ref, b_ref, g_ref, rk_ref, lnw_ref, lnb_ref,
                      o_ref, st_scr, *, ch, nsub):
    step = pl.program_id(1)
    n2 = 2 * ch

    @pl.when(step == 0)
    def _():
        st_scr[...] = jnp.zeros_like(st_scr)

    ti = lax.broadcasted_iota(jnp.int32, (n2, n2), 0)
    si = lax.broadcasted_iota(jnp.int32, (n2, n2), 1)
    same = (ti // ch) == (si // ch)
    incl = jnp.logical_and(same, si <= ti)
    strict = jnp.logical_and(same, si < ti)
    eye = jnp.where(si == ti, 1.0, 0.0)
    levels = []
    s = 1
    while s < ch:
        same2s = (ti // (2 * s)) == (si // (2 * s))
        levels.append(jnp.logical_and(same2s, jnp.logical_and((ti // s) % 2 == 1, (si // s) % 2 == 0)))
        s *= 2
    tr = lax.broadcasted_iota(jnp.int32, (ch, ch), 0)
    tc = lax.broadcasted_iota(jnp.int32, (ch, ch), 1)
    tri01 = jnp.where(tc <= tr, 1.0, 0.0).astype(BF16)
    lo, hi = _lane_half_masks((ch, LANES))
    hsum = _head_sum_matrix(LANES)

    def stack2(z):
        return jnp.concatenate([jnp.where(lo, z, 0.0), jnp.where(hi, z, 0.0)], axis=0)

    def dup2(z):
        return jnp.concatenate([z, z], axis=0)

    npair = B_HEADS // 2
    chains = [(sub, pr) for sub in range(nsub) for pr in range(npair)]
    cums = [_dot_exact_lhs01(tri01, lw_ref[0, sub * ch:(sub + 1) * ch, :]) for sub in range(nsub)]
    a2, r2, b2, k2, v2, bh2, kh2, decay = [], [], [], [], [], [], [], []
    for sub, pr in chains:
        rows = slice(sub * ch, (sub + 1) * ch)
        sl = slice(pr * LANES, (pr + 1) * LANES)
        k, bb = k_ref[0, rows, sl], b_ref[0, rows, sl]
        cum = cums[sub][:, sl]
        tot = cum[ch - 1:ch, :]
        e_neg = jnp.exp(-cum)
        e_rest = jnp.exp(tot - cum)
        a2.append(stack2(kk_ref[0, rows, sl] * jnp.exp(cum - lw_ref[0, rows, sl])).astype(BF16))
        r2.append(stack2(r_ref[0, rows, sl] * jnp.exp(cum)))
        b2.append(dup2(bb * e_neg).astype(BF16))
        k2.append(dup2(k * e_neg).astype(BF16))
        v2.append(stack2(v_ref[0, rows, sl]).astype(BF16))
        bh2.append(stack2(bb * e_rest).astype(BF16))
        kh2.append(stack2(k * e_rest).astype(BF16))
        decay.append(jnp.exp(tot))
    nc = len(chains)
    gram = [_dot_nt(jnp.concatenate([a2[i], r2[i].astype(BF16)], axis=0), jnp.concatenate([b2[i], k2[i]], axis=0))
            for i in range(nc)]
    ab = [jnp.where(strict, g[0:n2, 0:n2], 0.0) for g in gram]
    ak = [jnp.where(strict, g[0:n2, n2:2 * n2], 0.0).astype(BF16) for g in gram]
    rb = [jnp.where(incl, g[n2:2 * n2, 0:n2], 0.0).astype(BF16) for g in gram]
    rk = [jnp.where(incl, g[n2:2 * n2, n2:2 * n2], 0.0).astype(BF16) for g in gram]
    tinv = [eye] * nc
    for lvl in levels:
        tb = [t.astype(BF16) for t in tinv]
        tm = [_dot(tb[i], jnp.where(lvl, ab[i], 0.0).astype(BF16)).astype(BF16) for i in range(nc)]
        tinv = [tinv[i] - _dot(tm[i], tb[i]) for i in range(nc)]
    tb = [t.astype(BF16) for t in tinv]
    akv = [_dot(ak[i], v2[i]).astype(BF16) for i in range(nc)]
    wu = [_dot(tb[i], jnp.concatenate([a2[i], akv[i]], axis=1)).astype(BF16) for i in range(nc)]
    rwu = [_dot(rb[i], wu[i]) for i in range(nc)]
    rw = [(r2[i] - rwu[i][:, 0:LANES]).astype(BF16) for i in range(nc)]
    y0 = [_dot(rk[i], v2[i]) - rwu[i][:, LANES:2 * LANES] for i in range(nc)]
    gm = [_dot_tn(bh2[i], wu[i][:, 0:LANES]).astype(BF16) for i in range(nc)]
    s0t = [_dot_tn(jnp.concatenate([v2[i], wu[i][:, LANES:2 * LANES]], axis=0),
                   jnp.concatenate([kh2[i], -bh2[i]], axis=0)) for i in range(nc)]
    states = [st_scr[pr] for pr in range(npair)]
    ys = [None] * nc
    for i, (sub, pr) in enumerate(chains):
        st = states[pr]
        stb = st.astype(BF16)
        y2 = _dot_nt(rw[i], stb) + y0[i]
        ys[i] = y2[0:ch, :] + y2[ch:n2, :]
        states[pr] = st * decay[i] - _dot_nt(stb, gm[i]) + s0t[i]
    for pr in range(npair):
        st_scr[pr] = states[pr]
    tstep = nsub * ch
    y_st = jnp.concatenate([ys[sub * npair + pr] for pr in range(npair) for sub in range(nsub)], axis=0)
    rkr_st = jnp.concatenate([r_ref[0, :, pr * LANES:(pr + 1) * LANES] * k_ref[0, :, pr * LANES:(pr + 1) * LANES]
                              * rk_ref[:, pr * LANES:(pr + 1) * LANES] for pr in range(npair)], axis=0)
    sums = _dot_exact_rhs01(jnp.concatenate([y_st, rkr_st], axis=0), hsum)
    yc = y_st - sums[0:npair * tstep, :] * (1.0 / HEAD_DIM)
    var = _dot_exact_rhs01(yc * yc, hsum) * (1.0 / HEAD_DIM)
    yn = yc * lax.rsqrt(var + B_LN_EPS)
    for pr in range(npair):
        sl = slice(pr * LANES, (pr + 1) * LANES)
        rows = slice(pr * tstep, (pr + 1) * tstep)
        bonus = sums[npair * tstep + pr * tstep:npair * tstep + (pr + 1) * tstep, :] * v_ref[0, :, sl]
        out = (yn[rows, :] * lnw_ref[:, sl] + lnb_ref[:, sl] + bonus) * g_ref[0, :, sl]
        o_ref[0, :, sl] = out.astype(o_ref.dtype)


def rwkv_scan(r, k, v, lw, kk, bb, g, r_k, ln_w, ln_b, ch=RWKV_CHUNK, nsub=4):
    b, t, w = r.shape
    ch = min(ch, t)
    nsub = min(nsub, t // ch)
    tstep = ch * nsub
    seq = pl.BlockSpec((1, tstep, w), lambda i, c: (i, c, 0))
    vec = pl.BlockSpec((1, w), lambda i, c: (0, 0))
    row = lambda z: z.reshape(1, w)
    return pl.pallas_call(
        functools.partial(_rwkv_scan_kernel, ch=ch, nsub=nsub),
        grid=(b, t // tstep),
        in_specs=[seq] * 7 + [vec] * 3,
        out_specs=seq,
        out_shape=jax.ShapeDtypeStruct((b, t, w), BF16),
        scratch_shapes=[pltpu.VMEM((B_HEADS // 2, LANES, LANES), F32)],
        compiler_params=_cparams("parallel", "arbitrary"),
        name="rwkv_scan",
    )(r, k, v, lw, kk, bb, g, row(r_k), row(ln_w), row(ln_b))


def _fill_vt(load_tile, store_tile, t):
    sub = lax.broadcasted_iota(jnp.int32, (LANES, LANES), 0)

    def body(j, carry):
        start = pl.multiple_of(j * LANES, LANES)
        store_tile(start, jnp.where(sub < HEAD_DIM, load_tile(start).T, 1.0).astype(BF16))
        return carry

    lax.fori_loop(0, t // LANES, body, 0)


def _stack_heads(src_ref, dst_scr, n_heads, scale):
    lo, hi = _lane_half_masks((QCHUNK, LANES))
    for h in range(n_heads):
        tile = src_ref[0, :, (h // 2) * LANES:(h // 2 + 1) * LANES]
        if scale != 1.0:
            tile = tile * scale
        dst_scr[h * QCHUNK:(h + 1) * QCHUNK, :] = jnp.where((lo, hi)[h % 2], tile, 0.0).astype(dst_scr.dtype)


def _score_stage(k_slab, qst_scr, mask_bias, n_heads, s_scr, mx_scr, slot):
    for pair in range(n_heads // 2):
        d = _dot_nt(k_slab(pair), qst_scr[2 * pair * QCHUNK:(2 * pair + 2) * QCHUNK, :])
        for e in range(2):
            h = 2 * pair + e
            cols = slice(h * QCHUNK, (h + 1) * QCHUNK)
            s = d[:, e * QCHUNK:(e + 1) * QCHUNK] + mask_bias(h)
            s_scr[slot, :, cols] = s
            mx_scr[slot, :, cols] = jnp.max(s, axis=0, keepdims=True)


def _softmax_stage(vt, n_heads, s_scr, mx_scr, slot, m_scr, acc_scr):
    m_old = m_scr[...]
    m_new = jnp.maximum(m_old, mx_scr[slot])
    alpha = jnp.exp2(m_old - m_new)
    m_scr[...] = m_new
    for pair in range(n_heads // 2):
        cols = slice(2 * pair * QCHUNK, (2 * pair + 2) * QCHUNK)
        p = jnp.exp2(s_scr[slot, :, cols] - m_new[:, cols]).astype(BF16)
        acc_scr[:, cols] = alpha[:, cols] * acc_scr[:, cols] + _dot(vt(pair), p)


def _pipelined_groups(ngroup, s_matmul, process):
    last = ngroup - 1
    s_matmul(0, 0)

    def pair_body(i, carry):
        g0 = 2 * i
        s_matmul(g0 + 1, 1)
        process(g0, 0)
        s_matmul(jnp.minimum(g0 + 2, last), 0)
        process(g0 + 1, 1)
        return carry

    lax.fori_loop(0, ngroup // 2, pair_body, 0)

    @pl.when(ngroup % 2 == 1)
    def _():
        process(last, 0)


def _bias_rows(tabt_ref, h, d_first, n_tiles):
    tiles = [tabt_ref[jnp.clip(d_first - u, 0, BIAS_TILES - 1), h] for u in range(n_tiles)]
    return jnp.concatenate(tiles, axis=0)


def _write_heads(o_ref, acc_scr, n_heads):
    for p in range(n_heads // 2):
        parts = []
        for h in (2 * p, 2 * p + 1):
            a = acc_scr[:, h * QCHUNK:(h + 1) * QCHUNK]
            parts.append(a[0:HEAD_DIM, :] * (1.0 / a[HEAD_DIM:HEAD_DIM + 1, :]))
        o_ref[0, :, p * LANES:(p + 1) * LANES] = jnp.concatenate(parts, axis=0).T.astype(o_ref.dtype)


def _moba_kernel(q_ref, k_ref, v_ref, tabt_ref, o_ref,
                 k_scr, vt_scr, km_scr, qst_scr, pick_scr, s_scr, mx_scr, m_scr, acc_scr):
    c = pl.program_id(1)
    t = k_ref.shape[1]
    nblk = t // MOBA_BLOCK
    nbp = km_scr.shape[1]
    group = C_HEADS // C_KV_HEADS
    pairs_per_kv = group // 2

    @pl.when(c == 0)
    def _():
        k_scr[...] = k_ref[0].astype(BF16)
        km_scr[...] = jnp.zeros_like(km_scr)
        for kvh in range(C_KV_HEADS):
            kf = k_ref[0, :, kvh * LANES:(kvh + 1) * LANES]
            km_scr[kvh, 0:nblk, :] = jnp.mean(kf.reshape(nblk, MOBA_BLOCK, LANES), axis=1)

            def store_vt(start, tile, kvh=kvh):
                vt_scr[kvh, :, pl.ds(start, LANES)] = tile

            _fill_vt(lambda start, kvh=kvh: v_ref[0, pl.ds(start, LANES), kvh * LANES:(kvh + 1) * LANES], store_vt, t)

    own = (c * QCHUNK) // MOBA_BLOCK
    _stack_heads(q_ref, qst_scr, C_HEADS, HEAD_DIM ** -0.5 * LOG2E)

    blk = lax.broadcasted_iota(jnp.int32, (nbp, QCHUNK), 0)
    for kvh in range(C_KV_HEADS):
        first_tile = kvh * pairs_per_kv
        qsum = q_ref[0, :, first_tile * LANES:(first_tile + 1) * LANES]
        for p in range(first_tile + 1, first_tile + pairs_per_kv):
            qsum = qsum + q_ref[0, :, p * LANES:(p + 1) * LANES]
        gate = _dot_nt_x(km_scr[kvh], qsum)
        gate = jnp.where(blk < own, gate, -jnp.inf)
        rank = jnp.zeros((nbp, QCHUNK), jnp.int32)
        for mblk in range(nblk):
            gm = gate[mblk:mblk + 1, :]
            ahead = jnp.logical_or(gm > gate, jnp.logical_and(gm == gate, mblk < blk))
            rank = rank + jnp.where(ahead, 1, 0)
        chosen = jnp.logical_and(rank < MOBA_TOPK, blk < own)
        pick_scr[kvh] = jnp.where(chosen, 0.0, MASKED)

    m_scr[...] = jnp.full(m_scr.shape, M_INIT, F32)
    acc_scr[...] = jnp.zeros_like(acc_scr)
    grp = 2 * MOBA_BLOCK
    tiles_per_grp = grp // LANES
    ngroup = own // 2 + 1
    grow = lax.broadcasted_iota(jnp.int32, (grp, QCHUNK), 0)
    second = grow >= MOBA_BLOCK
    qpos = c * QCHUNK + lax.broadcasted_iota(jnp.int32, (grp, QCHUNK), 1)

    def score_stage(gi, slot):
        start = pl.multiple_of(gi * grp, grp)
        causal = jnp.where((gi * grp + grow) <= qpos, 0.0, MASKED)
        okb = []
        for kvh in range(C_KV_HEADS):
            first = jnp.where(2 * gi < own, pick_scr[kvh, pl.ds(2 * gi, 1), :], causal)
            later = jnp.where(2 * gi + 1 < own, pick_scr[kvh, pl.ds(2 * gi + 1, 1), :], causal)
            okb.append(jnp.where(second, later, first))
        d_first = c - tiles_per_grp * gi
        _score_stage(lambda pair: k_scr[pl.ds(start, grp), (pair // pairs_per_kv) * LANES:(pair // pairs_per_kv + 1) * LANES],
                     qst_scr, lambda h: _bias_rows(tabt_ref, h, d_first, tiles_per_grp) + okb[h // group],
                     C_HEADS, s_scr, mx_scr, slot)

    def softmax_stage(gi, slot):
        start = pl.multiple_of(gi * grp, grp)
        _softmax_stage(lambda pair: vt_scr[pair // pairs_per_kv, :, pl.ds(start, grp)],
                       C_HEADS, s_scr, mx_scr, slot, m_scr, acc_scr)

    _pipelined_groups(ngroup, score_stage, softmax_stage)
    _write_heads(o_ref, acc_scr, C_HEADS)


def moba_attention(qc, kvc, tabt):
    b, t, _ = qc.shape
    assert t % MOBA_BLOCK == 0
    nq = t // QCHUNK
    qw = C_HEADS * HEAD_DIM
    kvw = C_KV_HEADS * LANES
    nbp = -(-(t // MOBA_BLOCK) // SUBLANES) * SUBLANES
    return pl.pallas_call(
        _moba_kernel,
        grid=(b, nq),
        in_specs=[pl.BlockSpec((1, QCHUNK, qw), lambda i, c: (i, c, 0)),
                  pl.BlockSpec((1, t, kvw), lambda i, c: (i, 0, 0)),
                  pl.BlockSpec((1, t, kvw), lambda i, c: (i, 0, 1)),
                  pl.BlockSpec((BIAS_TILES, C_HEADS, LANES, LANES), lambda i, c: (0, 0, 0, 0))],
        out_specs=pl.BlockSpec((1, QCHUNK, qw), lambda i, c: (i, c, 0)),
        out_shape=jax.ShapeDtypeStruct((b, t, qw), BF16),
        scratch_shapes=[pltpu.VMEM((t, kvw), BF16), pltpu.VMEM((C_KV_HEADS, LANES, t), BF16),
                        pltpu.VMEM((C_KV_HEADS, nbp, LANES), F32),
                        pltpu.VMEM((C_HEADS * QCHUNK, LANES), BF16),
                        pltpu.VMEM((C_KV_HEADS, nbp, QCHUNK), F32),
                        pltpu.VMEM((2, 2 * MOBA_BLOCK, C_HEADS * QCHUNK), F32),
                        pltpu.VMEM((2, 1, C_HEADS * QCHUNK), F32),
                        pltpu.VMEM((1, C_HEADS * QCHUNK), F32),
                        pltpu.VMEM((LANES, C_HEADS * QCHUNK), F32)],
        compiler_params=_cparams("parallel", "arbitrary"),
        name="moba_attention",
    )(qc, kvc, kvc, tabt)


def _dsa_kernel(q_ref, kv_ref, qi_ref, ki_ref, gk_ref, tabt_ref, o_ref,
                k_scr, vt_scr, ki_scr, sc_scr, qst_scr, qist_scr, pre_scr, s_scr, mx_scr, m_scr, acc_scr, *, topk):
    c = pl.program_id(1)
    t = k_scr.shape[0]
    grp = KEY_GROUP * LANES
    ngroup = c // KEY_GROUP + 1
    sub = lax.broadcasted_iota(jnp.int32, (LANES, QCHUNK), 0)
    gsub = lax.broadcasted_iota(jnp.int32, (grp, QCHUNK), 0)
    qpos = c * QCHUNK + lax.broadcasted_iota(jnp.int32, (grp, QCHUNK), 1)

    @pl.when(c == 0)
    def _():
        k_scr[...] = kv_ref[0, :, 0:LANES].astype(BF16)
        ki_scr[...] = _rms(ki_ref[0], gk_ref[...]).astype(BF16)
        def store_vt(start, tile):
            vt_scr[:, pl.ds(start, LANES)] = tile

        _fill_vt(lambda start: kv_ref[0, pl.ds(start, LANES), LANES:2 * LANES], store_vt, t)

    _stack_heads(q_ref, qst_scr, D_HEADS, HEAD_DIM ** -0.5 * LOG2E)
    _stack_heads(qi_ref, qist_scr, IDX_HEADS, 1.0)
    w_t = (qi_ref[0, :, IDX_HEADS * IDX_DIM + LANES:IDX_HEADS * IDX_DIM + 2 * LANES]
           * ((IDX_HEADS * IDX_DIM) ** -0.5)).T

    def score_body(gi, carry):
        start = pl.multiple_of(gi * grp, grp)
        rel = jnp.maximum(_dot_nt(ki_scr[pl.ds(start, grp), :], qist_scr[...]), 0.0)
        sc = jnp.zeros((grp, QCHUNK), F32)
        for h in range(IDX_HEADS):
            sc = sc + rel[:, h * QCHUNK:(h + 1) * QCHUNK] * w_t[h:h + 1, :]
        sc = jnp.where(sc == 0.0, 0.0, sc)
        sc_scr[pl.ds(start, grp), :] = jnp.where((gi * grp + gsub) <= qpos, sc, -jnp.inf)
        return carry

    lax.fori_loop(0, ngroup, score_body, 0)

    def count(pred):
        def body(gi, acc):
            for u in range(KEY_GROUP):
                start = pl.multiple_of(gi * grp + u * LANES, LANES)
                acc = jnp.where(pred(sc_scr[pl.ds(start, LANES), :]), acc + 1, acc)
            return acc
        acc = lax.fori_loop(0, ngroup, body, jnp.zeros((LANES, QCHUNK), jnp.int32))
        return jnp.sum(acc, axis=0, keepdims=True)

    def key_to_float(key):
        return pltpu.bitcast(key ^ (lax.shift_right_arithmetic(key, 31) & 0x7FFFFFFF), F32)

    cnt0 = count(lambda st: st >= 0.0)
    code = jnp.where(cnt0 >= topk, 0, INT_MIN).astype(jnp.int32)

    def bit_body(i, code):
        cand = code + lax.shift_left(jnp.int32(1), 30 - i)
        cand_f = key_to_float(cand)
        cnt = count(lambda st: st >= cand_f)
        return jnp.where(cnt >= topk, cand, code)

    code = lax.fori_loop(0, 31, bit_body, code)
    thr = jnp.where(code == INT_MIN, -jnp.inf, key_to_float(jnp.maximum(code, INT_MIN + 1)))
    thr_eff = jnp.maximum(thr, LOWEST_F32)

    cnt_ge = count(lambda st: st >= thr_eff)
    tied = cnt_ge > topk

    @pl.when(jnp.max(jnp.where(tied, 1.0, 0.0)) > 0.5)
    def _():
        def tally_body(j, ties_before):
            start = pl.multiple_of(j * LANES, LANES)
            pre_scr[pl.ds(j, 1), :] = ties_before
            is_tie = sc_scr[pl.ds(start, LANES), :] == thr
            return ties_before + jnp.sum(jnp.where(is_tie, 1.0, 0.0), axis=0, keepdims=True)

        ties = lax.fori_loop(0, ngroup * KEY_GROUP, tally_body, jnp.zeros((1, QCHUNK), F32))
        above = cnt_ge.astype(F32) - ties
        need = jnp.where(tied, topk - above, 3.0e38)
        tri = jnp.where(lax.broadcasted_iota(jnp.int32, (LANES, LANES), 1) <= sub, 1.0, 0.0).astype(BF16)

        def demote_body(gi, carry):
            tiles = []
            for u in range(KEY_GROUP):
                start = pl.multiple_of(gi * grp + u * LANES, LANES)
                tiles.append(sc_scr[pl.ds(start, LANES), :])
            eq01 = jnp.concatenate([jnp.where(st == thr, 1.0, 0.0).astype(BF16) for st in tiles], axis=1)
            within = _dot(tri, eq01)
            for u, st in enumerate(tiles):
                start = pl.multiple_of(gi * grp + u * LANES, LANES)
                rank = pre_scr[pl.ds(gi * KEY_GROUP + u, 1), :] + within[:, u * QCHUNK:(u + 1) * QCHUNK]
                drop = jnp.logical_and(st == thr, rank > need)
                sc_scr[pl.ds(start, LANES), :] = jnp.where(drop, -jnp.inf, st)
            return carry

        lax.fori_loop(0, ngroup, demote_body, 0)

    m_scr[...] = jnp.full(m_scr.shape, M_INIT, F32)
    acc_scr[...] = jnp.zeros_like(acc_scr)

    def score_stage(gi, slot):
        start = pl.multiple_of(gi * grp, grp)
        okb = jnp.where(sc_scr[pl.ds(start, grp), :] >= thr_eff, 0.0, MASKED)
        d_first = c - gi * KEY_GROUP
        _score_stage(lambda pair: k_scr[pl.ds(start, grp), :], qst_scr,
                     lambda h: _bias_rows(tabt_ref, h, d_first, KEY_GROUP) + okb, D_HEADS, s_scr, mx_scr, slot)

    def softmax_stage(gi, slot):
        start = pl.multiple_of(gi * grp, grp)
        _softmax_stage(lambda pair: vt_scr[:, pl.ds(start, grp)], D_HEADS, s_scr, mx_scr, slot, m_scr, acc_scr)

    _pipelined_groups(ngroup, score_stage, softmax_stage)
    _write_heads(o_ref, acc_scr, D_HEADS)


def dsa_attention(qd, kvd, idx, kidx_gain, tabt):
    b, t, _ = qd.shape
    topk = min(DSA_TOPK, t // 4)
    nq = t // QCHUNK
    gk = jnp.concatenate([kidx_gain, kidx_gain]).reshape(1, LANES)
    wq = D_HEADS * HEAD_DIM
    return pl.pallas_call(
        functools.partial(_dsa_kernel, topk=topk),
        grid=(b, nq),
        in_specs=[pl.BlockSpec((1, QCHUNK, wq), lambda i, c: (i, c, 0)),
                  pl.BlockSpec((1, t, 2 * LANES), lambda i, c: (i, 0, 0)),
                  pl.BlockSpec((1, QCHUNK, 4 * LANES), lambda i, c: (i, c, 0)),
                  pl.BlockSpec((1, t, LANES), lambda i, c: (i, 0, 2)),
                  pl.BlockSpec((1, LANES), lambda i, c: (0, 0)),
                  pl.BlockSpec((BIAS_TILES, D_HEADS, LANES, LANES), lambda i, c: (0, 0, 0, 0))],
        out_specs=pl.BlockSpec((1, QCHUNK, wq), lambda i, c: (i, c, 0)),
        out_shape=jax.ShapeDtypeStruct((b, t, wq), BF16),
        scratch_shapes=[pltpu.VMEM((t, LANES), BF16), pltpu.VMEM((LANES, t), BF16), pltpu.VMEM((t, LANES), BF16),
                        pltpu.VMEM((t, QCHUNK), F32),
                        pltpu.VMEM((D_HEADS * QCHUNK, LANES), BF16), pltpu.VMEM((IDX_HEADS * QCHUNK, LANES), BF16),
                        pltpu.VMEM((t // LANES, QCHUNK), F32),
                        pltpu.VMEM((2, KEY_GROUP * LANES, D_HEADS * QCHUNK), F32),
                        pltpu.VMEM((2, 1, D_HEADS * QCHUNK), F32),
                        pltpu.VMEM((1, D_HEADS * QCHUNK), F32),
                        pltpu.VMEM((LANES, D_HEADS * QCHUNK), F32)],
        compiler_params=_cparams("parallel", "arbitrary"),
        name="dsa_attention",
    )(qd, kvd, idx, idx, gk, tabt)


def _dup_heads(wcols, n_heads):
    d = wcols.shape[0]
    wh = wcols.reshape(d, n_heads, 1, HEAD_DIM)
    return jnp.broadcast_to(wh, (d, n_heads, 2, HEAD_DIM)).reshape(d, n_heads * 2 * HEAD_DIM)


def _even_weight(w_in):
    aq, akv = A_HEADS * HEAD_DIM, A_KV_HEADS * HEAD_DIM
    q = w_in[:, :aq]
    k = _dup_heads(w_in[:, aq:aq + akv], A_KV_HEADS)
    v = _dup_heads(w_in[:, aq + akv:aq + 2 * akv], A_KV_HEADS)
    rest = w_in[:, aq + 2 * akv:]
    return jnp.concatenate([q, k, v, rest], axis=1).astype(BF16)


def _odd_weight(w_in):
    d = w_in.shape[0]
    cq, ckv, dq = C_HEADS * HEAD_DIM, C_KV_HEADS * HEAD_DIM, D_HEADS * HEAD_DIM
    cuts = np.cumsum([cq, ckv, ckv, dq, HEAD_DIM, HEAD_DIM, IDX_HEADS * IDX_DIM, IDX_DIM]).tolist()
    qc, kc, vc, qd, kd, vd, qi, ki, wi = jnp.split(w_in, cuts, axis=1)
    pad = jnp.zeros((d, LANES - IDX_HEADS), w_in.dtype)
    cols = [qc, _dup_heads(kc, C_KV_HEADS), _dup_heads(vc, C_KV_HEADS),
            qd, _dup_heads(kd, 1), _dup_heads(vd, 1),
            qi, _dup_heads(ki, 1), wi, pad]
    return jnp.concatenate(cols, axis=1).astype(BF16)


def kernel(x, mem, rel_bias, norm_gains, mix_w_out, mem_norm, x_wq, x_wkv, x_wo, ff_w1, ff_w2, ev_w_in, a_sinks, b_mu, b_w0, b_w2, b_a0, b_a2, b_g2, b_k_k, b_k_a, b_r_k, b_ln_w, b_ln_b, od_w_in, d_kidx_norm):
    bsz, t, d = x.shape
    n = bsz * t
    depth = norm_gains.shape[0]
    mlen = mem.shape[1]
    tabt = bias_table(rel_bias, transposed=True)
    x2 = x.reshape(n, d)
    for layer in range(depth):
        i = layer // 2
        g = norm_gains[layer]
        if layer % 2 == 0:
            b_in = 3 * B_WIDTH + B_DECAY_RANK + B_A_RANK + B_GATE_RANK
            qkv, pb = norm_proj(x2, g[0], _even_weight(ev_w_in[i]), (1024, b_in), (F32, F32))
            ya = swa_attention(qkv.reshape(bsz, t, 1024), a_sinks[i], rel_bias, tabt)
            prep = rwkv_prep(pb.reshape(bsz, t, b_in), b_mu[i], b_w0[i], b_w2[i], b_a0[i], b_a2[i], b_g2[i],
                             b_k_k[i], b_k_a[i])
            yb = rwkv_scan(*prep, b_r_k[i].reshape(-1), b_ln_w[i], b_ln_b[i])
        else:
            qc, kvc, qd, kvd, idx = norm_proj(x2, g[0], _odd_weight(od_w_in[i]),
                                              (512, 512, 512, 256, 512), (F32,) * 5)
            ya = moba_attention(qc.reshape(bsz, t, 512), kvc.reshape(bsz, t, 512), tabt)
            yb = dsa_attention(qd.reshape(bsz, t, 512), kvd.reshape(bsz, t, 256), idx.reshape(bsz, t, 512),
                               d_kidx_norm[i], tabt)
        kmem, vmem = norm_proj(mem.reshape(bsz * mlen, d), mem_norm[layer], x_wkv[layer].astype(BF16),
                               (X_WIDTH, X_WIDTH), (BF16, BF16))
        x3 = post_mixer(ya, yb, x2.reshape(bsz, t, d), kmem.reshape(bsz, mlen, X_WIDTH),
                        vmem.reshape(bsz, mlen, X_WIDTH), mix_w_out[layer], x_wq[layer], x_wo[layer],
                        ff_w1[layer], ff_w2[layer], g)
        x2 = x3.reshape(n, d)
    return x2.reshape(bsz, t, d)
```

```python
import functools

import numpy as np
import jax
import jax.numpy as jnp
from jax import lax
from jax.experimental import pallas as pl
from jax.experimental.pallas import tpu as pltpu

F32 = jnp.float32
BF16 = jnp.bfloat16
HIGHEST = lax.Precision.HIGHEST

LANES = 128
SUBLANES = 8
HEAD_DIM = 64
EPS = 1e-6
A_HEADS, A_KV_HEADS, BAND = 8, 2, 128
B_HEADS, B_WIDTH = 8, 512
B_DECAY_RANK, B_A_RANK, B_GATE_RANK = 64, 64, 128
B_LN_EPS = 64e-5
RWKV_CHUNK = 64
C_HEADS, C_KV_HEADS, MOBA_BLOCK, MOBA_TOPK, QCHUNK = 8, 2, 256, 3, 128
D_HEADS, IDX_HEADS, IDX_DIM, DSA_TOPK = 8, 4, 64, 256
N_BUCKETS, BIAS_HEADS = 32, 8
BIAS_TILES = 9
X_HEADS, X_HEAD_DIM, X_WIDTH = 4, 128, 512

LOG2E = 1.4426950408889634
KEY_GROUP = 4
MASKED = -2e30
M_INIT = -1e30
INT_MIN = -(2 ** 31)
LOWEST_F32 = -3.4028234663852886e38
VMEM_LIMIT = 56 * 1024 * 1024


def _cparams(*sem):
    return pltpu.CompilerParams(dimension_semantics=tuple(sem), vmem_limit_bytes=VMEM_LIMIT)


def _dot(a, b):
    return jnp.dot(a, b, preferred_element_type=F32)


def _dot_nt(a, b):
    return lax.dot_general(a, b, (((1,), (1,)), ((), ())), preferred_element_type=F32)


def _dot_tn(a, b):
    return lax.dot_general(a, b, (((0,), (0,)), ((), ())), preferred_element_type=F32)


def _dot_x(a, b):
    return jnp.dot(a, b, preferred_element_type=F32, precision=HIGHEST)


def _dot_nt_x(a, b):
    return lax.dot_general(a, b, (((1,), (1,)), ((), ())), preferred_element_type=F32, precision=HIGHEST)


def _rms(xf, g):
    ms = jnp.mean(xf * xf, axis=-1, keepdims=True)
    return xf * lax.rsqrt(ms + EPS) * g


def _lane_half_masks(shape):
    lane = lax.broadcasted_iota(jnp.int32, shape, len(shape) - 1)
    lo = lane < HEAD_DIM
    return lo, jnp.logical_not(lo)


def _bucket_of_distance(n):
    exact = N_BUCKETS // 2
    if n < exact:
        return n
    j = 0
    while n ** 8 >= (exact ** 8) * (2 ** (3 * (j + 1))):
        j += 1
    return min(exact + j, N_BUCKETS - 1)


@functools.lru_cache(maxsize=None)
def _bucket_tiles(transposed):
    by_dist = np.array([_bucket_of_distance(n) for n in range(BIAS_TILES * LANES + LANES)], np.int32)
    q = np.arange(LANES)[None, :] if transposed else np.arange(LANES)[:, None]
    k = np.arange(LANES)[:, None] if transposed else np.arange(LANES)[None, :]
    tiles = [by_dist[np.maximum(d * LANES + q - k, 0)] for d in range(BIAS_TILES)]
    return np.stack(tiles).astype(np.int32)


def _bias_table_kernel(rb_ref, bkt_ref, out_ref, *, base2_shifted):
    bkt = bkt_ref[0]
    for h in range(BIAS_HEADS):
        acc = jnp.zeros((LANES, LANES), F32)
        for b in range(N_BUCKETS):
            acc = jnp.where(bkt == b, rb_ref[b, h], acc)
        if base2_shifted:
            acc = (acc - rb_ref[N_BUCKETS - 1, h]) * LOG2E
        out_ref[0, h] = acc


def bias_table(rel_bias, transposed=False):
    return pl.pallas_call(
        functools.partial(_bias_table_kernel, base2_shifted=transposed),
        grid=(BIAS_TILES,),
        in_specs=[pl.BlockSpec(memory_space=pltpu.SMEM),
                  pl.BlockSpec((1, LANES, LANES), lambda d: (d, 0, 0))],
        out_specs=pl.BlockSpec((1, BIAS_HEADS, LANES, LANES), lambda d: (d, 0, 0, 0)),
        out_shape=jax.ShapeDtypeStruct((BIAS_TILES, BIAS_HEADS, LANES, LANES), F32),
        compiler_params=_cparams("arbitrary"),
        name="bias_table",
    )(rel_bias, jnp.asarray(_bucket_tiles(transposed)))


def _norm_proj_kernel(x_ref, g_ref, w_ref, *out_refs, splits):
    h = _rms(x_ref[...], g_ref[...]).astype(BF16)
    res = _dot(h, w_ref[...])
    off = 0
    for o_ref, width in zip(out_refs, splits):
        o_ref[...] = res[:, off:off + width].astype(o_ref.dtype)
        off += width


def norm_proj(x2, gain, w, splits, out_dtypes, tm=256):
    n, d = x2.shape
    tm = min(tm, n)
    total = sum(splits)
    assert w.shape == (d, total) and n % tm == 0 and all(s % LANES == 0 for s in splits)
    return pl.pallas_call(
        functools.partial(_norm_proj_kernel, splits=tuple(splits)),
        grid=(n // tm,),
        in_specs=[pl.BlockSpec((tm, d), lambda i: (i, 0)),
                  pl.BlockSpec((1, d), lambda i: (0, 0)),
                  pl.BlockSpec((d, total), lambda i: (0, 0))],
        out_specs=[pl.BlockSpec((tm, s), lambda i: (i, 0)) for s in splits],
        out_shape=[jax.ShapeDtypeStruct((n, s), dt) for s, dt in zip(splits, out_dtypes)],
        compiler_params=_cparams("parallel"),
        name="norm_proj",
    )(x2, gain.reshape(1, d), w)


def _post_mixer_kernel(ya_ref, yb_ref, x_ref, k_ref, v_ref, wa_ref, wb_ref, wq_ref, wo_ref, w1_ref, w2_ref, g_ref,
                       o_ref, *, tf):
    x = x_ref[0]
    z = _dot(ya_ref[0], wa_ref[...]) + _dot(yb_ref[0], wb_ref[...])
    x = x + _rms(z, g_ref[1:2, :])
    q = _dot(_rms(x, g_ref[2:3, :]).astype(BF16), wq_ref[...])
    k = k_ref[0]
    v = v_ref[0]
    outs = []
    for hd in range(X_HEADS):
        sl = slice(hd * X_HEAD_DIM, (hd + 1) * X_HEAD_DIM)
        s = _dot_nt(q[:, sl].astype(BF16), k[:, sl]) * (X_HEAD_DIM ** -0.5)
        m = jnp.max(s, axis=-1, keepdims=True)
        p = jnp.exp(s - m)
        p = p / jnp.sum(p, axis=-1, keepdims=True)
        outs.append(_dot(p.astype(BF16), v[:, sl]))
    z = _dot(jnp.concatenate(outs, axis=-1).astype(BF16), wo_ref[...])
    x = x + _rms(z, g_ref[3:4, :])
    h = _rms(x, g_ref[4:5, :]).astype(BF16)
    acc = jnp.zeros(x.shape, F32)
    for j in range(w1_ref.shape[1] // tf):
        a = jnp.maximum(_dot(h, w1_ref[:, j * tf:(j + 1) * tf]), 0.0)
        acc = acc + _dot((a * a).astype(BF16), w2_ref[j * tf:(j + 1) * tf, :])
    o_ref[0] = x + _rms(acc, g_ref[5:6, :])


def post_mixer(ya, yb, x3, kmem, vmem, w_out, wq, wo, w1, w2, gains, tq=512, tf=1024):
    b, t, d = x3.shape
    tq = min(tq, t)
    m = kmem.shape[1]
    ka, kb = ya.shape[2], yb.shape[2]
    f = w1.shape[1]
    once = pl.Buffered(1)
    const = lambda shape: pl.BlockSpec(shape, lambda i, j: (0, 0), pipeline_mode=once)
    return pl.pallas_call(
        functools.partial(_post_mixer_kernel, tf=tf),
        grid=(b, t // tq),
        in_specs=[pl.BlockSpec((1, tq, ka), lambda i, j: (i, j, 0)),
                  pl.BlockSpec((1, tq, kb), lambda i, j: (i, j, 0)),
                  pl.BlockSpec((1, tq, d), lambda i, j: (i, j, 0)),
                  pl.BlockSpec((1, m, X_WIDTH), lambda i, j: (i, 0, 0)),
                  pl.BlockSpec((1, m, X_WIDTH), lambda i, j: (i, 0, 0)),
                  const((ka, d)), const((kb, d)), const((d, X_WIDTH)), const((X_WIDTH, d)),
                  const((d, f)), const((f, d)), const(gains.shape)],
        out_specs=pl.BlockSpec((1, tq, d), lambda i, j: (i, j, 0)),
        out_shape=jax.ShapeDtypeStruct((b, t, d), F32),
        compiler_params=_cparams("parallel", "parallel"),
        name="post_mixer",
    )(ya, yb, x3, kmem, vmem, w_out[:ka].astype(BF16), w_out[ka:].astype(BF16), wq.astype(BF16), wo.astype(BF16),
      w1.astype(BF16), w2.astype(BF16), gains)


def _swa_kernel(sink_ref, rb_ref, q_ref, kv_ref, tabt_ref, o_ref, qst_scr, vt_scr):
    n = pl.program_id(1)
    group = A_HEADS // A_KV_HEADS
    first_band = jnp.maximum(n - 1, 0)
    start = pl.multiple_of(first_band * BAND, BAND)
    slab = 2 * BAND
    _stack_heads(q_ref, qst_scr, A_HEADS, HEAD_DIM ** -0.5 * LOG2E)
    kpos = start + lax.broadcasted_iota(jnp.int32, (slab, BAND), 0)
    qpos = n * BAND + lax.broadcasted_iota(jnp.int32, (slab, BAND), 1)
    dist = qpos - kpos
    okb = jnp.where(jnp.logical_and(dist >= 0, dist < BAND), 0.0, MASKED)
    sub = lax.broadcasted_iota(jnp.int32, (LANES, slab), 0)
    koff = 0
    voff = A_KV_HEADS * LANES
    d_first = n - first_band
    scores, vts = [], []
    for hkv in range(A_KV_HEADS):
        k2 = kv_ref[0, pl.ds(start, slab), koff + hkv * LANES:koff + (hkv + 1) * LANES].astype(BF16)
        v2 = kv_ref[0, pl.ds(start, slab), voff + hkv * LANES:voff + (hkv + 1) * LANES]
        vt_scr[hkv] = jnp.where(sub < HEAD_DIM, v2.T, 1.0).astype(BF16)
        vts.append(vt_scr[hkv])
        s_all = _dot_nt(k2, qst_scr[hkv * group * BAND:(hkv + 1) * group * BAND, :])
        for g in range(group):
            h = hkv * group + g
            scores.append(s_all[:, g * BAND:(g + 1) * BAND] + (_bias_rows(tabt_ref, h, d_first, 2) + okb))
    sinks = [(sink_ref[h] - rb_ref[N_BUCKETS - 1, h]) * LOG2E for h in range(A_HEADS)]
    m = [jnp.maximum(jnp.max(scores[h], axis=0, keepdims=True), sinks[h]) for h in range(A_HEADS)]
    p = [jnp.exp2(scores[h] - m[h]).astype(BF16) for h in range(A_HEADS)]
    outs = []
    for hkv in range(A_KV_HEADS):
        pv = _dot(vts[hkv], jnp.concatenate(p[hkv * group:(hkv + 1) * group], axis=1))
        for g in range(group):
            h = hkv * group + g
            a = pv[:, g * BAND:(g + 1) * BAND]
            den = a[HEAD_DIM:HEAD_DIM + 1, :] + jnp.exp2(sinks[h] - m[h])
            outs.append(a[0:HEAD_DIM, :] * (1.0 / den))
    for pair in range(A_HEADS // 2):
        tile = jnp.concatenate(outs[2 * pair:2 * pair + 2], axis=0).T
        o_ref[0, :, pair * LANES:(pair + 1) * LANES] = tile.astype(o_ref.dtype)


def swa_attention(qkv, sinks, rel_bias, tabt):
    b, t, w = qkv.shape
    qw = A_HEADS * HEAD_DIM
    return pl.pallas_call(
        _swa_kernel,
        grid=(b, t // BAND),
        in_specs=[pl.BlockSpec(memory_space=pltpu.SMEM),
                  pl.BlockSpec(memory_space=pltpu.SMEM),
                  pl.BlockSpec((1, BAND, qw), lambda i, n: (i, n, 0)),
                  pl.BlockSpec((1, t, w - qw), lambda i, n: (i, 0, 1)),
                  pl.BlockSpec((2, A_HEADS, LANES, LANES), lambda i, n: (0, 0, 0, 0))],
        out_specs=pl.BlockSpec((1, BAND, qw), lambda i, n: (i, n, 0)),
        out_shape=jax.ShapeDtypeStruct((b, t, qw), BF16),
        scratch_shapes=[pltpu.VMEM((A_HEADS * BAND, LANES), BF16), pltpu.VMEM((A_KV_HEADS, LANES, 2 * BAND), BF16)],
        compiler_params=_cparams("parallel", "arbitrary"),
        name="swa_attention",
    )(sinks, rel_bias, qkv, qkv, tabt)


def _split3(x):
    p0 = x.astype(BF16)
    r1 = x - p0.astype(F32)
    p1 = r1.astype(BF16)
    p2 = (r1 - p1.astype(F32)).astype(BF16)
    return p0, p1, p2


def _dot_exact_rhs01(x, m01):
    p0, p1, p2 = _split3(x)
    return _dot(p0, m01) + _dot(p1, m01) + _dot(p2, m01)


def _dot_exact_lhs01(m01, x):
    p0, p1, p2 = _split3(x)
    return _dot(m01, p0) + _dot(m01, p1) + _dot(m01, p2)


def _head_sum_matrix(width):
    r = lax.broadcasted_iota(jnp.int32, (width, width), 0) // HEAD_DIM
    c = lax.broadcasted_iota(jnp.int32, (width, width), 1) // HEAD_DIM
    return jnp.where(r == c, 1.0, 0.0).astype(BF16)


def _sigmoid(z):
    return 1.0 / (1.0 + jnp.exp(-z))


def _softplus(z):
    return jnp.maximum(z, 0.0) + jnp.log(1.0 + jnp.exp(-jnp.abs(z)))


def _rwkv_prep_kernel(p_ref, pprev_ref, mu_ref, wa2_ref, g2_ref, w0_ref, a0_ref, kk_ref, ka_ref,
                      r_out, k_out, v_out, lw_out, kk_out, b_out, g_out):
    i = pl.program_id(1)
    p = p_ref[0]
    w = B_WIDTH
    last = pprev_ref[0, SUBLANES - 1:SUBLANES, :]
    last = jnp.where(i > 0, last, 0.0)
    row = lax.broadcasted_iota(jnp.int32, p.shape, 0)
    prev = jnp.where(row == 0, last, pltpu.roll(p, 1, 0))
    xs = p + (prev - p) * mu_ref[...]
    r = xs[:, 0:w]
    k = xs[:, w:2 * w]
    v = xs[:, 2 * w:3 * w]
    wa_lo = xs[:, 3 * w:3 * w + LANES]
    g_lo = xs[:, 3 * w + LANES:3 * w + 2 * LANES]
    lo, _ = _lane_half_masks(wa_lo.shape)
    wa_in = jnp.where(lo, jnp.tanh(wa_lo), wa_lo)
    in_hi = wa_in.astype(BF16)
    in_lo = (wa_in - in_hi.astype(F32)).astype(BF16)
    wa = _dot(in_hi, wa2_ref[0]) + (_dot(in_hi, wa2_ref[1]) + _dot(in_lo, wa2_ref[0]))
    wlog = -_softplus(-(w0_ref[...] + wa[:, 0:w])) - 0.5
    a = _sigmoid(a0_ref[...] + wa[:, w:2 * w])
    g = _dot(_sigmoid(g_lo).astype(BF16), g2_ref[...])
    kk = k * kk_ref[...]
    ss = _dot_exact_rhs01(kk * kk, _head_sum_matrix(w))
    kk = kk / jnp.maximum(jnp.sqrt(ss), 1e-12)
    kmod = k * (1.0 + (a - 1.0) * ka_ref[...])
    r_out[0] = r
    k_out[0] = kmod
    v_out[0] = v
    lw_out[0] = -jnp.exp(wlog)
    kk_out[0] = kk
    b_out[0] = kk * a
    g_out[0] = g


def rwkv_prep(pb, mu, w0, w2, a0, a2, g2, k_k, k_a, tc=256):
    b, t, win = pb.shape
    tc = min(tc, t)
    w = B_WIDTH
    wa2 = jnp.zeros((LANES, 2 * w), F32)
    wa2 = wa2.at[:B_DECAY_RANK, :w].set(w2).at[B_DECAY_RANK:, w:].set(a2)
    wa2_hi = wa2.astype(BF16)
    wa2 = jnp.stack([wa2_hi, (wa2 - wa2_hi.astype(F32)).astype(BF16)])
    row = lambda z: z.reshape(1, -1)
    vec = pl.BlockSpec((1, w), lambda i, j: (0, 0))
    outs = [jax.ShapeDtypeStruct((b, t, w), F32)] * 7
    return pl.pallas_call(
        _rwkv_prep_kernel,
        grid=(b, t // tc),
        in_specs=[pl.BlockSpec((1, tc, win), lambda i, j: (i, j, 0)),
                  pl.BlockSpec((1, SUBLANES, win), lambda i, j: (i, jnp.maximum(j * (tc // SUBLANES) - 1, 0), 0)),
                  pl.BlockSpec((1, win), lambda i, j: (0, 0)),
                  pl.BlockSpec((2, LANES, 2 * w), lambda i, j: (0, 0, 0)),
                  pl.BlockSpec((B_GATE_RANK, w), lambda i, j: (0, 0)),
                  vec, vec, vec, vec],
        out_specs=[pl.BlockSpec((1, tc, w), lambda i, j: (i, j, 0))] * 7,
        out_shape=outs,
        compiler_params=_cparams("parallel", "arbitrary"),
        name="rwkv_prep",
    )(pb, pb, row(mu), wa2, g2.astype(BF16), row(w0), row(a0), row(k_k), row(k_a))


def _rwkv_scan_kernel(r_ref, k_ref, v_ref, lw_ref, kk_ref, b_ref, g_ref, rk_ref, lnw_ref, lnb_ref,
                      o_ref, st_scr, *, ch, nsub):
    step = pl.program_id(1)
    n2 = 2 * ch

    @pl.when(step == 0)
    def _():
        st_scr[...] = jnp.zeros_like(st_scr)

    ti = lax.broadcasted_iota(jnp.int32, (n2, n2), 0)
    si = lax.broadcasted_iota(jnp.int32, (n2, n2), 1)
    same = (ti // ch) == (si // ch)
    incl = jnp.logical_and(same, si <= ti)
    strict = jnp.logical_and(same, si < ti)
    eye = jnp.where(si == ti, 1.0, 0.0)
    levels = []
    s = 1
    while s < ch:
        same2s = (ti // (2 * s)) == (si // (2 * s))
        levels.append(jnp.logical_and(same2s, jnp.logical_and((ti // s) % 2 == 1, (si // s) % 2 == 0)))
        s *= 2
    tr = lax.broadcasted_iota(jnp.int32, (ch, ch), 0)
    tc = lax.broadcasted_iota(jnp.int32, (ch, ch), 1)
    tri01 = jnp.where(tc <= tr, 1.0, 0.0).astype(BF16)
    lo, hi = _lane_half_masks((ch, LANES))
    hsum = _head_sum_matrix(LANES)

    def stack2(z):
        return jnp.concatenate([jnp.where(lo, z, 0.0), jnp.where(hi, z, 0.0)], axis=0)

    def dup2(z):
        return jnp.concatenate([z, z], axis=0)

    npair = B_HEADS // 2
    chains = [(sub, pr) for sub in range(nsub) for pr in range(npair)]
    cums = [_dot_exact_lhs01(tri01, lw_ref[0, sub * ch:(sub + 1) * ch, :]) for sub in range(nsub)]
    a2, r2, b2, k2, v2, bh2, kh2, decay = [], [], [], [], [], [], [], []
    for sub, pr in chains:
        rows = slice(sub * ch, (sub + 1) * ch)
        sl = slice(pr * LANES, (pr + 1) * LANES)
        k, bb = k_ref[0, rows, sl], b_ref[0, rows, sl]
        cum = cums[sub][:, sl]
        tot = cum[ch - 1:ch, :]
        e_neg = jnp.exp(-cum)
        e_rest = jnp.exp(tot - cum)
        a2.append(stack2(kk_ref[0, rows, sl] * jnp.exp(cum - lw_ref[0, rows, sl])).astype(BF16))
        r2.append(stack2(r_ref[0, rows, sl] * jnp.exp(cum)))
        b2.append(dup2(bb * e_neg).astype(BF16))
        k2.append(dup2(k * e_neg).astype(BF16))
        v2.append(stack2(v_ref[0, rows, sl]).astype(BF16))
        bh2.append(stack2(bb * e_rest).astype(BF16))
        kh2.append(stack2(k * e_rest).astype(BF16))
        decay.append(jnp.exp(tot))
    nc = len(chains)
    gram = [_dot_nt(jnp.concatenate([a2[i], r2[i].astype(BF16)], axis=0), jnp.concatenate([b2[i], k2[i]], axis=0))
            for i in range(nc)]
    ab = [jnp.where(strict, g[0:n2, 0:n2], 0.0) for g in gram]
    ak = [jnp.where(strict, g[0:n2, n2:2 * n2], 0.0).astype(BF16) for g in gram]
    rb = [jnp.where(incl, g[n2:2 * n2, 0:n2], 0.0).astype(BF16) for g in gram]
    rk = [jnp.where(incl, g[n2:2 * n2, n2:2 * n2], 0.0).astype(BF16) for g in gram]
    tinv = [eye] * nc
    for lvl in levels:
        tb = [t.astype(BF16) for t in tinv]
        tm = [_dot(tb[i], jnp.where(lvl, ab[i], 0.0).astype(BF16)).astype(BF16) for i in range(nc)]
        tinv = [tinv[i] - _dot(tm[i], tb[i]) for i in range(nc)]
    tb = [t.astype(BF16) for t in tinv]
    akv = [_dot(ak[i], v2[i]).astype(BF16) for i in range(nc)]
    wu = [_dot(tb[i], jnp.concatenate([a2[i], akv[i]], axis=1)).astype(BF16) for i in range(nc)]
    rwu = [_dot(rb[i], wu[i]) for i in range(nc)]
    rw = [(r2[i] - rwu[i][:, 0:LANES]).astype(BF16) for i in range(nc)]
    y0 = [_dot(rk[i], v2[i]) - rwu[i][:, LANES:2 * LANES] for i in range(nc)]
    gm = [_dot_tn(bh2[i], wu[i][:, 0:LANES]).astype(BF16) for i in range(nc)]
    s0t = [_dot_tn(jnp.concatenate([v2[i], wu[i][:, LANES:2 * LANES]], axis=0),
                   jnp.concatenate([kh2[i], -bh2[i]], axis=0)) for i in range(nc)]
    states = [st_scr[pr] for pr in range(npair)]
    ys = [None] * nc
    for i, (sub, pr) in enumerate(chains):
        st = states[pr]
        stb = st.astype(BF16)
        y2 = _dot_nt(rw[i], stb) + y0[i]
        ys[i] = y2[0:ch, :] + y2[ch:n2, :]
        states[pr] = st * decay[i] - _dot_nt(stb, gm[i]) + s0t[i]
    for pr in range(npair):
        st_scr[pr] = states[pr]
    tstep = nsub * ch
    y_st = jnp.concatenate([ys[sub * npair + pr] for pr in range(npair) for sub in range(nsub)], axis=0)
    rkr_st = jnp.concatenate([r_ref[0, :, pr * LANES:(pr + 1) * LANES] * k_ref[0, :, pr * LANES:(pr + 1) * LANES]
                              * rk_ref[:, pr * LANES:(pr + 1) * LANES] for pr in range(npair)], axis=0)
    sums = _dot_exact_rhs01(jnp.concatenate([y_st, rkr_st], axis=0), hsum)
    yc = y_st - sums[0:npair * tstep, :] * (1.0 / HEAD_DIM)
    var = _dot_exact_rhs01(yc * yc, hsum) * (1.0 / HEAD_DIM)
    yn = yc * lax.rsqrt(var + B_LN_EPS)
    for pr in range(npair):
        sl = slice(pr * LANES, (pr + 1) * LANES)
        rows = slice(pr * tstep, (pr + 1) * tstep)
        bonus = sums[npair * tstep + pr * tstep:npair * tstep + (pr + 1) * tstep, :] * v_ref[0, :, sl]
        out = (yn[rows, :] * lnw_ref[:, sl] + lnb_ref[:, sl] + bonus) * g_ref[0, :, sl]
        o_ref[0, :, sl] = out.astype(o_ref.dtype)


def rwkv_scan(r, k, v, lw, kk, bb, g, r_k, ln_w, ln_b, ch=RWKV_CHUNK, nsub=4):
    b, t, w = r.shape
    ch = min(ch, t)
    nsub = min(nsub, t // ch)
    tstep = ch * nsub
    seq = pl.BlockSpec((1, tstep, w), lambda i, c: (i, c, 0))
    vec = pl.BlockSpec((1, w), lambda i, c: (0, 0))
    row = lambda z: z.reshape(1, w)
    return pl.pallas_call(
        functools.partial(_rwkv_scan_kernel, ch=ch, nsub=nsub),
        grid=(b, t // tstep),
        in_specs=[seq] * 7 + [vec] * 3,
        out_specs=seq,
        out_shape=jax.ShapeDtypeStruct((b, t, w), BF16),
        scratch_shapes=[pltpu.VMEM((B_HEADS // 2, LANES, LANES), F32)],
        compiler_params=_cparams("parallel", "arbitrary"),
        name="rwkv_scan",
    )(r, k, v, lw, kk, bb, g, row(r_k), row(ln_w), row(ln_b))


def _fill_vt(load_tile, store_tile, t):
    sub = lax.broadcasted_iota(jnp.int32, (LANES, LANES), 0)

    def body(j, carry):
        start = pl.multiple_of(j * LANES, LANES)
        store_tile(start, jnp.where(sub < HEAD_DIM, load_tile(start).T, 1.0).astype(BF16))
        return carry

    lax.fori_loop(0, t // LANES, body, 0)


def _stack_heads(src_ref, dst_scr, n_heads, scale):
    lo, hi = _lane_half_masks((QCHUNK, LANES))
    for h in range(n_heads):
        tile = src_ref[0, :, (h // 2) * LANES:(h // 2 + 1) * LANES]
        if scale != 1.0:
            tile = tile * scale
        dst_scr[h * QCHUNK:(h + 1) * QCHUNK, :] = jnp.where((lo, hi)[h % 2], tile, 0.0).astype(dst_scr.dtype)


def _score_stage(k_slab, qst_scr, mask_bias, n_heads, s_scr, mx_scr, slot):
    for pair in range(n_heads // 2):
        d = _dot_nt(k_slab(pair), qst_scr[2 * pair * QCHUNK:(2 * pair + 2) * QCHUNK, :])
        for e in range(2):
            h = 2 * pair + e
            cols = slice(h * QCHUNK, (h + 1) * QCHUNK)
            s = d[:, e * QCHUNK:(e + 1) * QCHUNK] + mask_bias(h)
            s_scr[slot, :, cols] = s
            mx_scr[slot, :, cols] = jnp.max(s, axis=0, keepdims=True)


def _softmax_stage(vt, n_heads, s_scr, mx_scr, slot, m_scr, acc_scr):
    m_old = m_scr[...]
    m_new = jnp.maximum(m_old, mx_scr[slot])
    alpha = jnp.exp2(m_old - m_new)
    m_scr[...] = m_new
    for pair in range(n_heads // 2):
        cols = slice(2 * pair * QCHUNK, (2 * pair + 2) * QCHUNK)
        p = jnp.exp2(s_scr[slot, :, cols] - m_new[:, cols]).astype(BF16)
        acc_scr[:, cols] = alpha[:, cols] * acc_scr[:, cols] + _dot(vt(pair), p)


def _pipelined_groups(ngroup, s_matmul, process):
    last = ngroup - 1
    s_matmul(0, 0)

    def pair_body(i, carry):
        g0 = 2 * i
        s_matmul(g0 + 1, 1)
        process(g0, 0)
        s_matmul(jnp.minimum(g0 + 2, last), 0)
        process(g0 + 1, 1)
        return carry

    lax.fori_loop(0, ngroup // 2, pair_body, 0)

    @pl.when(ngroup % 2 == 1)
    def _():
        process(last, 0)


def _bias_rows(tabt_ref, h, d_first, n_tiles):
    tiles = [tabt_ref[jnp.clip(d_first - u, 0, BIAS_TILES - 1), h] for u in range(n_tiles)]
    return jnp.concatenate(tiles, axis=0)


def _write_heads(o_ref, acc_scr, n_heads):
    for p in range(n_heads // 2):
        parts = []
        for h in (2 * p, 2 * p + 1):
            a = acc_scr[:, h * QCHUNK:(h + 1) * QCHUNK]
            parts.append(a[0:HEAD_DIM, :] * (1.0 / a[HEAD_DIM:HEAD_DIM + 1, :]))
        o_ref[0, :, p * LANES:(p + 1) * LANES] = jnp.concatenate(parts, axis=0).T.astype(o_ref.dtype)


def _moba_kernel(q_ref, k_ref, v_ref, tabt_ref, o_ref,
                 k_scr, vt_scr, km_scr, qst_scr, pick_scr, s_scr, mx_scr, m_scr, acc_scr):
    c = pl.program_id(1)
    t = k_ref.shape[1]
    nblk = t // MOBA_BLOCK
    nbp = km_scr.shape[1]
    group = C_HEADS // C_KV_HEADS
    pairs_per_kv = group // 2

    @pl.when(c == 0)
    def _():
        k_scr[...] = k_ref[0].astype(BF16)
        km_scr[...] = jnp.zeros_like(km_scr)
        for kvh in range(C_KV_HEADS):
            kf = k_ref[0, :, kvh * LANES:(kvh + 1) * LANES]
            km_scr[kvh, 0:nblk, :] = jnp.mean(kf.reshape(nblk, MOBA_BLOCK, LANES), axis=1)

            def store_vt(start, tile, kvh=kvh):
                vt_scr[kvh, :, pl.ds(start, LANES)] = tile

            _fill_vt(lambda start, kvh=kvh: v_ref[0, pl.ds(start, LANES), kvh * LANES:(kvh + 1) * LANES], store_vt, t)

    own = (c * QCHUNK) // MOBA_BLOCK
    _stack_heads(q_ref, qst_scr, C_HEADS, HEAD_DIM ** -0.5 * LOG2E)

    blk = lax.broadcasted_iota(jnp.int32, (nbp, QCHUNK), 0)
    for kvh in range(C_KV_HEADS):
        first_tile = kvh * pairs_per_kv
        qsum = q_ref[0, :, first_tile * LANES:(first_tile + 1) * LANES]
        for p in range(first_tile + 1, first_tile + pairs_per_kv):
            qsum = qsum + q_ref[0, :, p * LANES:(p + 1) * LANES]
        gate = _dot_nt_x(km_scr[kvh], qsum)
        gate = jnp.where(blk < own, gate, -jnp.inf)
        rank = jnp.zeros((nbp, QCHUNK), jnp.int32)
        for mblk in range(nblk):
            gm = gate[mblk:mblk + 1, :]
            ahead = jnp.logical_or(gm > gate, jnp.logical_and(gm == gate, mblk < blk))
            rank = rank + jnp.where(ahead, 1, 0)
        chosen = jnp.logical_and(rank < MOBA_TOPK, blk < own)
        pick_scr[kvh] = jnp.where(chosen, 0.0, MASKED)

    m_scr[...] = jnp.full(m_scr.shape, M_INIT, F32)
    acc_scr[...] = jnp.zeros_like(acc_scr)
    grp = 2 * MOBA_BLOCK
    tiles_per_grp = grp // LANES
    ngroup = own // 2 + 1
    grow = lax.broadcasted_iota(jnp.int32, (grp, QCHUNK), 0)
    second = grow >= MOBA_BLOCK
    qpos = c * QCHUNK + lax.broadcasted_iota(jnp.int32, (grp, QCHUNK), 1)

    def score_stage(gi, slot):
        start = pl.multiple_of(gi * grp, grp)
        causal = jnp.where((gi * grp + grow) <= qpos, 0.0, MASKED)
        okb = []
        for kvh in range(C_KV_HEADS):
            first = jnp.where(2 * gi < own, pick_scr[kvh, pl.ds(2 * gi, 1), :], causal)
            later = jnp.where(2 * gi + 1 < own, pick_scr[kvh, pl.ds(2 * gi + 1, 1), :], causal)
            okb.append(jnp.where(second, later, first))
        d_first = c - tiles_per_grp * gi
        _score_stage(lambda pair: k_scr[pl.ds(start, grp), (pair // pairs_per_kv) * LANES:(pair // pairs_per_kv + 1) * LANES],
                     qst_scr, lambda h: _bias_rows(tabt_ref, h, d_first, tiles_per_grp) + okb[h // group],
                     C_HEADS, s_scr, mx_scr, slot)

    def softmax_stage(gi, slot):
        start = pl.multiple_of(gi * grp, grp)
        _softmax_stage(lambda pair: vt_scr[pair // pairs_per_kv, :, pl.ds(start, grp)],
                       C_HEADS, s_scr, mx_scr, slot, m_scr, acc_scr)

    _pipelined_groups(ngroup, score_stage, softmax_stage)
    _write_heads(o_ref, acc_scr, C_HEADS)


def moba_attention(qc, kvc, tabt):
    b, t, _ = qc.shape
    assert t % MOBA_BLOCK == 0
    nq = t // QCHUNK
    qw = C_HEADS * HEAD_DIM
    kvw = C_KV_HEADS * LANES
    nbp = -(-(t // MOBA_BLOCK) // SUBLANES) * SUBLANES
    return pl.pallas_call(
        _moba_kernel,
        grid=(b, nq),
        in_specs=[pl.BlockSpec((1, QCHUNK, qw), lambda i, c: (i, c, 0)),
                  pl.BlockSpec((1, t, kvw), lambda i, c: (i, 0, 0)),
                  pl.BlockSpec((1, t, kvw), lambda i, c: (i, 0, 1)),
                  pl.BlockSpec((BIAS_TILES, C_HEADS, LANES, LANES), lambda i, c: (0, 0, 0, 0))],
        out_specs=pl.BlockSpec((1, QCHUNK, qw), lambda i, c: (i, c, 0)),
        out_shape=jax.ShapeDtypeStruct((b, t, qw), BF16),
        scratch_shapes=[pltpu.VMEM((t, kvw), BF16), pltpu.VMEM((C_KV_HEADS, LANES, t), BF16),
                        pltpu.VMEM((C_KV_HEADS, nbp, LANES), F32),
                        pltpu.VMEM((C_HEADS * QCHUNK, LANES), BF16),
                        pltpu.VMEM((C_KV_HEADS, nbp, QCHUNK), F32),
                        pltpu.VMEM((2, 2 * MOBA_BLOCK, C_HEADS * QCHUNK), F32),
                        pltpu.VMEM((2, 1, C_HEADS * QCHUNK), F32),
                        pltpu.VMEM((1, C_HEADS * QCHUNK), F32),
                        pltpu.VMEM((LANES, C_HEADS * QCHUNK), F32)],
        compiler_params=_cparams("parallel", "arbitrary"),
        name="moba_attention",
    )(qc, kvc, kvc, tabt)


def _dsa_kernel(q_ref, kv_ref, qi_ref, ki_ref, gk_ref, tabt_ref, o_ref,
                k_scr, vt_scr, ki_scr, sc_scr, qst_scr, qist_scr, pre_scr, s_scr, mx_scr, m_scr, acc_scr, *, topk):
    c = pl.program_id(1)
    t = k_scr.shape[0]
    grp = KEY_GROUP * LANES
    ngroup = c // KEY_GROUP + 1
    sub = lax.broadcasted_iota(jnp.int32, (LANES, QCHUNK), 0)
    gsub = lax.broadcasted_iota(jnp.int32, (grp, QCHUNK), 0)
    qpos = c * QCHUNK + lax.broadcasted_iota(jnp.int32, (grp, QCHUNK), 1)

    @pl.when(c == 0)
    def _():
        k_scr[...] = kv_ref[0, :, 0:LANES].astype(BF16)
        ki_scr[...] = _rms(ki_ref[0], gk_ref[...]).astype(BF16)
        def store_vt(start, tile):
            vt_scr[:, pl.ds(start, LANES)] = tile

        _fill_vt(lambda start: kv_ref[0, pl.ds(start, LANES), LANES:2 * LANES], store_vt, t)

    _stack_heads(q_ref, qst_scr, D_HEADS, HEAD_DIM ** -0.5 * LOG2E)
    _stack_heads(qi_ref, qist_scr, IDX_HEADS, 1.0)
    w_t = (qi_ref[0, :, IDX_HEADS * IDX_DIM + LANES:IDX_HEADS * IDX_DIM + 2 * LANES]
           * ((IDX_HEADS * IDX_DIM) ** -0.5)).T

    def score_group(gi):
        start = pl.multiple_of(gi * grp, grp)
        rel = jnp.maximum(_dot_nt(ki_scr[pl.ds(start, grp), :], qist_scr[...]), 0.0)
        sc = jnp.zeros((grp, QCHUNK), F32)
        for h in range(IDX_HEADS):
            sc = sc + rel[:, h * QCHUNK:(h + 1) * QCHUNK] * w_t[h:h + 1, :]
        sc = jnp.where(sc == 0.0, 0.0, sc)
        sc_scr[pl.ds(start, grp), :] = jnp.where((gi * grp + gsub) <= qpos, sc, -jnp.inf)

    def score_pair(i, carry):
        score_group(2 * i)
        score_group(2 * i + 1)
        return carry

    lax.fori_loop(0, ngroup // 2, score_pair, 0)

    @pl.when(ngroup % 2 == 1)
    def _():
        score_group(ngroup - 1)

    def count(pred):
        def body(gi, acc):
            for u in range(KEY_GROUP):
                start = pl.multiple_of(gi * grp + u * LANES, LANES)
                acc = jnp.where(pred(sc_scr[pl.ds(start, LANES), :]), acc + 1, acc)
            return acc
        acc = lax.fori_loop(0, ngroup, body, jnp.zeros((LANES, QCHUNK), jnp.int32))
        return jnp.sum(acc, axis=0, keepdims=True)

    def key_to_float(key):
        return pltpu.bitcast(key ^ (lax.shift_right_arithmetic(key, 31) & 0x7FFFFFFF), F32)

    cnt0 = count(lambda st: st >= 0.0)
    code = jnp.where(cnt0 >= topk, 0, INT_MIN).astype(jnp.int32)

    def bit_body(i, code):
        cand = code + lax.shift_left(jnp.int32(1), 30 - i)
        cand_f = key_to_float(cand)
        cnt = count(lambda st: st >= cand_f)
        return jnp.where(cnt >= topk, cand, code)

    code = lax.fori_loop(0, 31, bit_body, code)
    thr = jnp.where(code == INT_MIN, -jnp.inf, key_to_float(jnp.maximum(code, INT_MIN + 1)))
    thr_eff = jnp.maximum(thr, LOWEST_F32)

    cnt_ge = count(lambda st: st >= thr_eff)
    tied = cnt_ge > topk

    @pl.when(jnp.max(jnp.where(tied, 1.0, 0.0)) > 0.5)
    def _():
        def tally_body(gi, ties_before):
            sums = []
            for u in range(KEY_GROUP):
                start = pl.multiple_of(gi * grp + u * LANES, LANES)
                is_tie = sc_scr[pl.ds(start, LANES), :] == thr
                sums.append(jnp.sum(jnp.where(is_tie, 1.0, 0.0), axis=0, keepdims=True))
            for u in range(KEY_GROUP):
                pre_scr[pl.ds(gi * KEY_GROUP + u, 1), :] = ties_before
                ties_before = ties_before + sums[u]
            return ties_before

        ties = lax.fori_loop(0, ngroup, tally_body, jnp.zeros((1, QCHUNK), F32))
        above = cnt_ge.astype(F32) - ties
        need = jnp.where(tied, topk - above, 3.0e38)
        tri = jnp.where(lax.broadcasted_iota(jnp.int32, (LANES, LANES), 1) <= sub, 1.0, 0.0).astype(BF16)

        def demote_body(gi, carry):
            tiles = []
            for u in range(KEY_GROUP):
                start = pl.multiple_of(gi * grp + u * LANES, LANES)
                tiles.append(sc_scr[pl.ds(start, LANES), :])
            eq01 = jnp.concatenate([jnp.where(st == thr, 1.0, 0.0).astype(BF16) for st in tiles], axis=1)
            within = _dot(tri, eq01)
            for u, st in enumerate(tiles):
                start = pl.multiple_of(gi * grp + u * LANES, LANES)
                rank = pre_scr[pl.ds(gi * KEY_GROUP + u, 1), :] + within[:, u * QCHUNK:(u + 1) * QCHUNK]
                drop = jnp.logical_and(st == thr, rank > need)
                sc_scr[pl.ds(start, LANES), :] = jnp.where(drop, -jnp.inf, st)
            return carry

        lax.fori_loop(0, ngroup, demote_body, 0)

    m_scr[...] = jnp.full(m_scr.shape, M_INIT, F32)
    acc_scr[...] = jnp.zeros_like(acc_scr)

    def score_stage(gi, slot):
        start = pl.multiple_of(gi * grp, grp)
        okb = jnp.where(sc_scr[pl.ds(start, grp), :] >= thr_eff, 0.0, MASKED)
        d_first = c - gi * KEY_GROUP
        _score_stage(lambda pair: k_scr[pl.ds(start, grp), :], qst_scr,
                     lambda h: _bias_rows(tabt_ref, h, d_first, KEY_GROUP) + okb, D_HEADS, s_scr, mx_scr, slot)

    def softmax_stage(gi, slot):
        start = pl.multiple_of(gi * grp, grp)
        _softmax_stage(lambda pair: vt_scr[:, pl.ds(start, grp)], D_HEADS, s_scr, mx_scr, slot, m_scr, acc_scr)

    _pipelined_groups(ngroup, score_stage, softmax_stage)
    _write_heads(o_ref, acc_scr, D_HEADS)


def dsa_attention(qd, kvd, idx, kidx_gain, tabt):
    b, t, _ = qd.shape
    topk = min(DSA_TOPK, t // 4)
    nq = t // QCHUNK
    gk = jnp.concatenate([kidx_gain, kidx_gain]).reshape(1, LANES)
    wq = D_HEADS * HEAD_DIM
    return pl.pallas_call(
        functools.partial(_dsa_kernel, topk=topk),
        grid=(b, nq),
        in_specs=[pl.BlockSpec((1, QCHUNK, wq), lambda i, c: (i, c, 0)),
                  pl.BlockSpec((1, t, 2 * LANES), lambda i, c: (i, 0, 0)),
                  pl.BlockSpec((1, QCHUNK, 4 * LANES), lambda i, c: (i, c, 0)),
                  pl.BlockSpec((1, t, LANES), lambda i, c: (i, 0, 2)),
                  pl.BlockSpec((1, LANES), lambda i, c: (0, 0)),
                  pl.BlockSpec((BIAS_TILES, D_HEADS, LANES, LANES), lambda i, c: (0, 0, 0, 0))],
        out_specs=pl.BlockSpec((1, QCHUNK, wq), lambda i, c: (i, c, 0)),
        out_shape=jax.ShapeDtypeStruct((b, t, wq), BF16),
        scratch_shapes=[pltpu.VMEM((t, LANES), BF16), pltpu.VMEM((LANES, t), BF16), pltpu.VMEM((t, LANES), BF16),
                        pltpu.VMEM((t, QCHUNK), F32),
                        pltpu.VMEM((D_HEADS * QCHUNK, LANES), BF16), pltpu.VMEM((IDX_HEADS * QCHUNK, LANES), BF16),
                        pltpu.VMEM((t // LANES, QCHUNK), F32),
                        pltpu.VMEM((2, KEY_GROUP * LANES, D_HEADS * QCHUNK), F32),
                        pltpu.VMEM((2, 1, D_HEADS * QCHUNK), F32),
                        pltpu.VMEM((1, D_HEADS * QCHUNK), F32),
                        pltpu.VMEM((LANES, D_HEADS * QCHUNK), F32)],
        compiler_params=_cparams("parallel", "arbitrary"),
        name="dsa_attention",
    )(qd, kvd, idx, idx, gk, tabt)


def _dup_heads(wcols, n_heads):
    d = wcols.shape[0]
    wh = wcols.reshape(d, n_heads, 1, HEAD_DIM)
    return jnp.broadcast_to(wh, (d, n_heads, 2, HEAD_DIM)).reshape(d, n_heads * 2 * HEAD_DIM)


def _even_weight(w_in):
    aq, akv = A_HEADS * HEAD_DIM, A_KV_HEADS * HEAD_DIM
    q = w_in[:, :aq]
    k = _dup_heads(w_in[:, aq:aq + akv], A_KV_HEADS)
    v = _dup_heads(w_in[:, aq + akv:aq + 2 * akv], A_KV_HEADS)
    rest = w_in[:, aq + 2 * akv:]
    return jnp.concatenate([q, k, v, rest], axis=1).astype(BF16)


def _odd_weight(w_in):
    d = w_in.shape[0]
    cq, ckv, dq = C_HEADS * HEAD_DIM, C_KV_HEADS * HEAD_DIM, D_HEADS * HEAD_DIM
    cuts = np.cumsum([cq, ckv, ckv, dq, HEAD_DIM, HEAD_DIM, IDX_HEADS * IDX_DIM, IDX_DIM]).tolist()
    qc, kc, vc, qd, kd, vd, qi, ki, wi = jnp.split(w_in, cuts, axis=1)
    pad = jnp.zeros((d, LANES - IDX_HEADS), w_in.dtype)
    cols = [qc, _dup_heads(kc, C_KV_HEADS), _dup_heads(vc, C_KV_HEADS),
            qd, _dup_heads(kd, 1), _dup_heads(vd, 1),
            qi, _dup_heads(ki, 1), wi, pad]
    return jnp.concatenate(cols, axis=1).astype(BF16)


def kernel(x, mem, rel_bias, norm_gains, mix_w_out, mem_norm, x_wq, x_wkv, x_wo, ff_w1, ff_w2, ev_w_in, a_sinks, b_mu, b_w0, b_w2, b_a0, b_a2, b_g2, b_k_k, b_k_a, b_r_k, b_ln_w, b_ln_b, od_w_in, d_kidx_norm):
    bsz, t, d = x.shape
    n = bsz * t
    depth = norm_gains.shape[0]
    mlen = mem.shape[1]
    tabt = bias_table(rel_bias, transposed=True)
    x2 = x.reshape(n, d)
    for layer in range(depth):
        i = layer // 2
        g = norm_gains[layer]
        if layer % 2 == 0:
            b_in = 3 * B_WIDTH + B_DECAY_RANK + B_A_RANK + B_GATE_RANK
            qkv, pb = norm_proj(x2, g[0], _even_weight(ev_w_in[i]), (1024, b_in), (F32, F32))
            ya = swa_attention(qkv.reshape(bsz, t, 1024), a_sinks[i], rel_bias, tabt)
            prep = rwkv_prep(pb.reshape(bsz, t, b_in), b_mu[i], b_w0[i], b_w2[i], b_a0[i], b_a2[i], b_g2[i],
                             b_k_k[i], b_k_a[i])
            yb = rwkv_scan(*prep, b_r_k[i].reshape(-1), b_ln_w[i], b_ln_b[i])
        else:
            qc, kvc, qd, kvd, idx = norm_proj(x2, g[0], _odd_weight(od_w_in[i]),
                                              (512, 512, 512, 256, 512), (F32,) * 5)
            ya = moba_attention(qc.reshape(bsz, t, 512), kvc.reshape(bsz, t, 512), tabt)
            yb = dsa_attention(qd.reshape(bsz, t, 512), kvd.reshape(bsz, t, 256), idx.reshape(bsz, t, 512),
                               d_kidx_norm[i], tabt)
        kmem, vmem = norm_proj(mem.reshape(bsz * mlen, d), mem_norm[layer], x_wkv[layer].astype(BF16),
                               (X_WIDTH, X_WIDTH), (BF16, BF16))
        x3 = post_mixer(ya, yb, x2.reshape(bsz, t, d), kmem.reshape(bsz, mlen, X_WIDTH),
                        vmem.reshape(bsz, mlen, X_WIDTH), mix_w_out[layer], x_wq[layer], x_wo[layer],
                        ff_w1[layer], ff_w2[layer], g)
        x2 = x3.reshape(n, d)
    return x2.reshape(bsz, t, d)
```

```python
import functools

import numpy as np
import jax
import jax.numpy as jnp
from jax import lax
from jax.experimental import pallas as pl
from jax.experimental.pallas import tpu as pltpu

F32 = jnp.float32
BF16 = jnp.bfloat16
HIGHEST = lax.Precision.HIGHEST

LANES = 128
SUBLANES = 8
HEAD_DIM = 64
EPS = 1e-6
A_HEADS, A_KV_HEADS, BAND = 8, 2, 128
B_HEADS, B_WIDTH = 8, 512
B_DECAY_RANK, B_A_RANK, B_GATE_RANK = 64, 64, 128
B_LN_EPS = 64e-5
RWKV_CHUNK = 64
C_HEADS, C_KV_HEADS, MOBA_BLOCK, MOBA_TOPK, QCHUNK = 8, 2, 256, 3, 128
D_HEADS, IDX_HEADS, IDX_DIM, DSA_TOPK = 8, 4, 64, 256
N_BUCKETS, BIAS_HEADS = 32, 8
BIAS_TILES = 9
X_HEADS, X_HEAD_DIM, X_WIDTH = 4, 128, 512

LOG2E = 1.4426950408889634
KEY_GROUP = 4
MASKED = -2e30
M_INIT = -1e30
INT_MIN = -(2 ** 31)
LOWEST_F32 = -3.4028234663852886e38
VMEM_LIMIT = 56 * 1024 * 1024


def _cparams(*sem):
    return pltpu.CompilerParams(dimension_semantics=tuple(sem), vmem_limit_bytes=VMEM_LIMIT)


def _dot(a, b):
    return jnp.dot(a, b, preferred_element_type=F32)


def _dot_nt(a, b):
    return lax.dot_general(a, b, (((1,), (1,)), ((), ())), preferred_element_type=F32)


def _dot_tn(a, b):
    return lax.dot_general(a, b, (((0,), (0,)), ((), ())), preferred_element_type=F32)


def _dot_x(a, b):
    return jnp.dot(a, b, preferred_element_type=F32, precision=HIGHEST)


def _dot_nt_x(a, b):
    return lax.dot_general(a, b, (((1,), (1,)), ((), ())), preferred_element_type=F32, precision=HIGHEST)


def _rms(xf, g):
    ms = jnp.mean(xf * xf, axis=-1, keepdims=True)
    return xf * lax.rsqrt(ms + EPS) * g


def _lane_half_masks(shape):
    lane = lax.broadcasted_iota(jnp.int32, shape, len(shape) - 1)
    lo = lane < HEAD_DIM
    return lo, jnp.logical_not(lo)


def _bucket_of_distance(n):
    exact = N_BUCKETS // 2
    if n < exact:
        return n
    j = 0
    while n ** 8 >= (exact ** 8) * (2 ** (3 * (j + 1))):
        j += 1
    return min(exact + j, N_BUCKETS - 1)


@functools.lru_cache(maxsize=None)
def _bucket_tiles(transposed):
    by_dist = np.array([_bucket_of_distance(n) for n in range(BIAS_TILES * LANES + LANES)], np.int32)
    q = np.arange(LANES)[None, :] if transposed else np.arange(LANES)[:, None]
    k = np.arange(LANES)[:, None] if transposed else np.arange(LANES)[None, :]
    tiles = [by_dist[np.maximum(d * LANES + q - k, 0)] for d in range(BIAS_TILES)]
    return np.stack(tiles).astype(np.int32)


def _bias_table_kernel(rb_ref, bkt_ref, out_ref, *, base2_shifted):
    bkt = bkt_ref[0]
    for h in range(BIAS_HEADS):
        acc = jnp.zeros((LANES, LANES), F32)
        for b in range(N_BUCKETS):
            acc = jnp.where(bkt == b, rb_ref[b, h], acc)
        if base2_shifted:
            acc = (acc - rb_ref[N_BUCKETS - 1, h]) * LOG2E
        out_ref[0, h] = acc


def bias_table(rel_bias, transposed=False):
    return pl.pallas_call(
        functools.partial(_bias_table_kernel, base2_shifted=transposed),
        grid=(BIAS_TILES,),
        in_specs=[pl.BlockSpec(memory_space=pltpu.SMEM),
                  pl.BlockSpec((1, LANES, LANES), lambda d: (d, 0, 0))],
        out_specs=pl.BlockSpec((1, BIAS_HEADS, LANES, LANES), lambda d: (d, 0, 0, 0)),
        out_shape=jax.ShapeDtypeStruct((BIAS_TILES, BIAS_HEADS, LANES, LANES), F32),
        compiler_params=_cparams("arbitrary"),
        name="bias_table",
    )(rel_bias, jnp.asarray(_bucket_tiles(transposed)))


def _norm_proj_kernel(x_ref, g_ref, w_ref, *out_refs, splits):
    h = _rms(x_ref[...], g_ref[...]).astype(BF16)
    res = _dot(h, w_ref[...])
    off = 0
    for o_ref, width in zip(out_refs, splits):
        o_ref[...] = res[:, off:off + width].astype(o_ref.dtype)
        off += width


def norm_proj(x2, gain, w, splits, out_dtypes, tm=256):
    n, d = x2.shape
    tm = min(tm, n)
    total = sum(splits)
    assert w.shape == (d, total) and n % tm == 0 and all(s % LANES == 0 for s in splits)
    return pl.pallas_call(
        functools.partial(_norm_proj_kernel, splits=tuple(splits)),
        grid=(n // tm,),
        in_specs=[pl.BlockSpec((tm, d), lambda i: (i, 0)),
                  pl.BlockSpec((1, d), lambda i: (0, 0)),
                  pl.BlockSpec((d, total), lambda i: (0, 0))],
        out_specs=[pl.BlockSpec((tm, s), lambda i: (i, 0)) for s in splits],
        out_shape=[jax.ShapeDtypeStruct((n, s), dt) for s, dt in zip(splits, out_dtypes)],
        compiler_params=_cparams("parallel"),
        name="norm_proj",
    )(x2, gain.reshape(1, d), w)


def _post_mixer_kernel(ya_ref, yb_ref, x_ref, k_ref, v_ref, wa_ref, wb_ref, wq_ref, wo_ref, w1_ref, w2_ref, g_ref,
                       o_ref, *, tf):
    x = x_ref[0]
    z = _dot(ya_ref[0], wa_ref[...]) + _dot(yb_ref[0], wb_ref[...])
    x = x + _rms(z, g_ref[1:2, :])
    q = _dot(_rms(x, g_ref[2:3, :]).astype(BF16), wq_ref[...])
    k = k_ref[0]
    v = v_ref[0]
    outs = []
    for hd in range(X_HEADS):
        sl = slice(hd * X_HEAD_DIM, (hd + 1) * X_HEAD_DIM)
        s = _dot_nt(q[:, sl].astype(BF16), k[:, sl]) * (X_HEAD_DIM ** -0.5)
        m = jnp.max(s, axis=-1, keepdims=True)
        p = jnp.exp(s - m)
        p = p / jnp.sum(p, axis=-1, keepdims=True)
        outs.append(_dot(p.astype(BF16), v[:, sl]))
    z = _dot(jnp.concatenate(outs, axis=-1).astype(BF16), wo_ref[...])
    x = x + _rms(z, g_ref[3:4, :])
    h = _rms(x, g_ref[4:5, :]).astype(BF16)
    acc = jnp.zeros(x.shape, F32)
    for j in range(w1_ref.shape[1] // tf):
        a = jnp.maximum(_dot(h, w1_ref[:, j * tf:(j + 1) * tf]), 0.0)
        acc = acc + _dot((a * a).astype(BF16), w2_ref[j * tf:(j + 1) * tf, :])
    o_ref[0] = x + _rms(acc, g_ref[5:6, :])


def post_mixer(ya, yb, x3, kmem, vmem, w_out, wq, wo, w1, w2, gains, tq=512, tf=1024):
    b, t, d = x3.shape
    tq = min(tq, t)
    m = kmem.shape[1]
    ka, kb = ya.shape[2], yb.shape[2]
    f = w1.shape[1]
    once = pl.Buffered(1)
    const = lambda shape: pl.BlockSpec(shape, lambda i, j: (0, 0), pipeline_mode=once)
    return pl.pallas_call(
        functools.partial(_post_mixer_kernel, tf=tf),
        grid=(b, t // tq),
        in_specs=[pl.BlockSpec((1, tq, ka), lambda i, j: (i, j, 0)),
                  pl.BlockSpec((1, tq, kb), lambda i, j: (i, j, 0)),
                  pl.BlockSpec((1, tq, d), lambda i, j: (i, j, 0)),
                  pl.BlockSpec((1, m, X_WIDTH), lambda i, j: (i, 0, 0)),
                  pl.BlockSpec((1, m, X_WIDTH), lambda i, j: (i, 0, 0)),
                  const((ka, d)), const((kb, d)), const((d, X_WIDTH)), const((X_WIDTH, d)),
                  const((d, f)), const((f, d)), const(gains.shape)],
        out_specs=pl.BlockSpec((1, tq, d), lambda i, j: (i, j, 0)),
        out_shape=jax.ShapeDtypeStruct((b, t, d), F32),
        compiler_params=_cparams("parallel", "parallel"),
        name="post_mixer",
    )(ya, yb, x3, kmem, vmem, w_out[:ka].astype(BF16), w_out[ka:].astype(BF16), wq.astype(BF16), wo.astype(BF16),
      w1.astype(BF16), w2.astype(BF16), gains)


def _swa_kernel(sink_ref, rb_ref, q_ref, kv_ref, tabt_ref, o_ref, qst_scr, vt_scr):
    n = pl.program_id(1)
    group = A_HEADS // A_KV_HEADS
    first_band = jnp.maximum(n - 1, 0)
    start = pl.multiple_of(first_band * BAND, BAND)
    slab = 2 * BAND
    _stack_heads(q_ref, qst_scr, A_HEADS, HEAD_DIM ** -0.5 * LOG2E)
    kpos = start + lax.broadcasted_iota(jnp.int32, (slab, BAND), 0)
    qpos = n * BAND + lax.broadcasted_iota(jnp.int32, (slab, BAND), 1)
    dist = qpos - kpos
    okb = jnp.where(jnp.logical_and(dist >= 0, dist < BAND), 0.0, MASKED)
    sub = lax.broadcasted_iota(jnp.int32, (LANES, slab), 0)
    koff = 0
    voff = A_KV_HEADS * LANES
    d_first = n - first_band
    scores, vts = [], []
    for hkv in range(A_KV_HEADS):
        k2 = kv_ref[0, pl.ds(start, slab), koff + hkv * LANES:koff + (hkv + 1) * LANES].astype(BF16)
        v2 = kv_ref[0, pl.ds(start, slab), voff + hkv * LANES:voff + (hkv + 1) * LANES]
        vt_scr[hkv] = jnp.where(sub < HEAD_DIM, v2.T, 1.0).astype(BF16)
        vts.append(vt_scr[hkv])
        s_all = _dot_nt(k2, qst_scr[hkv * group * BAND:(hkv + 1) * group * BAND, :])
        for g in range(group):
            h = hkv * group + g
            scores.append(s_all[:, g * BAND:(g + 1) * BAND] + (_bias_rows(tabt_ref, h, d_first, 2) + okb))
    sinks = [(sink_ref[h] - rb_ref[N_BUCKETS - 1, h]) * LOG2E for h in range(A_HEADS)]
    m = [jnp.maximum(jnp.max(scores[h], axis=0, keepdims=True), sinks[h]) for h in range(A_HEADS)]
    p = [jnp.exp2(scores[h] - m[h]).astype(BF16) for h in range(A_HEADS)]
    outs = []
    for hkv in range(A_KV_HEADS):
        pv = _dot(vts[hkv], jnp.concatenate(p[hkv * group:(hkv + 1) * group], axis=1))
        for g in range(group):
            h = hkv * group + g
            a = pv[:, g * BAND:(g + 1) * BAND]
            den = a[HEAD_DIM:HEAD_DIM + 1, :] + jnp.exp2(sinks[h] - m[h])
            outs.append(a[0:HEAD_DIM, :] * (1.0 / den))
    for pair in range(A_HEADS // 2):
        tile = jnp.concatenate(outs[2 * pair:2 * pair + 2], axis=0).T
        o_ref[0, :, pair * LANES:(pair + 1) * LANES] = tile.astype(o_ref.dtype)


def swa_attention(qkv, sinks, rel_bias, tabt):
    b, t, w = qkv.shape
    qw = A_HEADS * HEAD_DIM
    return pl.pallas_call(
        _swa_kernel,
        grid=(b, t // BAND),
        in_specs=[pl.BlockSpec(memory_space=pltpu.SMEM),
                  pl.BlockSpec(memory_space=pltpu.SMEM),
                  pl.BlockSpec((1, BAND, qw), lambda i, n: (i, n, 0)),
                  pl.BlockSpec((1, t, w - qw), lambda i, n: (i, 0, 1)),
                  pl.BlockSpec((2, A_HEADS, LANES, LANES), lambda i, n: (0, 0, 0, 0))],
        out_specs=pl.BlockSpec((1, BAND, qw), lambda i, n: (i, n, 0)),
        out_shape=jax.ShapeDtypeStruct((b, t, qw), BF16),
        scratch_shapes=[pltpu.VMEM((A_HEADS * BAND, LANES), BF16), pltpu.VMEM((A_KV_HEADS, LANES, 2 * BAND), BF16)],
        compiler_params=_cparams("parallel", "arbitrary"),
        name="swa_attention",
    )(sinks, rel_bias, qkv, qkv, tabt)


def _split3(x):
    p0 = x.astype(BF16)
    r1 = x - p0.astype(F32)
    p1 = r1.astype(BF16)
    p2 = (r1 - p1.astype(F32)).astype(BF16)
    return p0, p1, p2


def _dot_exact_rhs01(x, m01):
    p0, p1, p2 = _split3(x)
    return _dot(p0, m01) + _dot(p1, m01) + _dot(p2, m01)


def _dot_exact_lhs01(m01, x):
    p0, p1, p2 = _split3(x)
    return _dot(m01, p0) + _dot(m01, p1) + _dot(m01, p2)


def _head_sum_matrix(width):
    r = lax.broadcasted_iota(jnp.int32, (width, width), 0) // HEAD_DIM
    c = lax.broadcasted_iota(jnp.int32, (width, width), 1) // HEAD_DIM
    return jnp.where(r == c, 1.0, 0.0).astype(BF16)


def _sigmoid(z):
    return 1.0 / (1.0 + jnp.exp(-z))


def _softplus(z):
    return jnp.maximum(z, 0.0) + jnp.log(1.0 + jnp.exp(-jnp.abs(z)))


def _rwkv_prep_kernel(p_ref, pprev_ref, mu_ref, wa2_ref, g2_ref, w0_ref, a0_ref, kk_ref, ka_ref,
                      r_out, k_out, v_out, lw_out, kk_out, b_out, g_out):
    i = pl.program_id(1)
    p = p_ref[0]
    w = B_WIDTH
    last = pprev_ref[0, SUBLANES - 1:SUBLANES, :]
    last = jnp.where(i > 0, last, 0.0)
    row = lax.broadcasted_iota(jnp.int32, p.shape, 0)
    prev = jnp.where(row == 0, last, pltpu.roll(p, 1, 0))
    xs = p + (prev - p) * mu_ref[...]
    r = xs[:, 0:w]
    k = xs[:, w:2 * w]
    v = xs[:, 2 * w:3 * w]
    wa_lo = xs[:, 3 * w:3 * w + LANES]
    g_lo = xs[:, 3 * w + LANES:3 * w + 2 * LANES]
    lo, _ = _lane_half_masks(wa_lo.shape)
    wa_in = jnp.where(lo, jnp.tanh(wa_lo), wa_lo)
    in_hi = wa_in.astype(BF16)
    in_lo = (wa_in - in_hi.astype(F32)).astype(BF16)
    wa = _dot(in_hi, wa2_ref[0]) + (_dot(in_hi, wa2_ref[1]) + _dot(in_lo, wa2_ref[0]))
    wlog = -_softplus(-(w0_ref[...] + wa[:, 0:w])) - 0.5
    a = _sigmoid(a0_ref[...] + wa[:, w:2 * w])
    g = _dot(_sigmoid(g_lo).astype(BF16), g2_ref[...])
    kk = k * kk_ref[...]
    ss = _dot_exact_rhs01(kk * kk, _head_sum_matrix(w))
    kk = kk / jnp.maximum(jnp.sqrt(ss), 1e-12)
    kmod = k * (1.0 + (a - 1.0) * ka_ref[...])
    r_out[0] = r
    k_out[0] = kmod
    v_out[0] = v
    lw_out[0] = -jnp.exp(wlog)
    kk_out[0] = kk
    b_out[0] = kk * a
    g_out[0] = g


def _rwkv_scan_kernel(r_ref, k_ref, v_ref, lw_ref, kk_ref, b_ref, g_ref, rk_ref, lnw_ref, lnb_ref,
                      o_ref, st_scr, *, ch, nsub):
    step = pl.program_id(1)
    n2 = 2 * ch

    @pl.when(step == 0)
    def _():
        st_scr[...] = jnp.zeros_like(st_scr)

    ti = lax.broadcasted_iota(jnp.int32, (n2, n2), 0)
    si = lax.broadcasted_iota(jnp.int32, (n2, n2), 1)
    same = (ti // ch) == (si // ch)
    incl = jnp.logical_and(same, si <= ti)
    strict = jnp.logical_and(same, si < ti)
    eye = jnp.where(si == ti, 1.0, 0.0)
    levels = []
    s = 1
    while s < ch:
        same2s = (ti // (2 * s)) == (si // (2 * s))
        levels.append(jnp.logical_and(same2s, jnp.logical_and((ti // s) % 2 == 1, (si // s) % 2 == 0)))
        s *= 2
    tr = lax.broadcasted_iota(jnp.int32, (ch, ch), 0)
    tc = lax.broadcasted_iota(jnp.int32, (ch, ch), 1)
    tri01 = jnp.where(tc <= tr, 1.0, 0.0).astype(BF16)
    lo, hi = _lane_half_masks((ch, LANES))
    hsum = _head_sum_matrix(LANES)

    def stack2(z):
        return jnp.concatenate([jnp.where(lo, z, 0.0), jnp.where(hi, z, 0.0)], axis=0)

    def dup2(z):
        return jnp.concatenate([z, z], axis=0)

    npair = B_HEADS // 2
    chains = [(sub, pr) for sub in range(nsub) for pr in range(npair)]
    cums = [_dot_exact_lhs01(tri01, lw_ref[0, sub * ch:(sub + 1) * ch, :]) for sub in range(nsub)]
    a2, r2, b2, k2, v2, bh2, kh2, decay = [], [], [], [], [], [], [], []
    for sub, pr in chains:
        rows = slice(sub * ch, (sub + 1) * ch)
        sl = slice(pr * LANES, (pr + 1) * LANES)
        k, bb = k_ref[0, rows, sl], b_ref[0, rows, sl]
        cum = cums[sub][:, sl]
        tot = cum[ch - 1:ch, :]
        e_neg = jnp.exp(-cum)
        e_rest = jnp.exp(tot - cum)
        a2.append(stack2(kk_ref[0, rows, sl] * jnp.exp(cum - lw_ref[0, rows, sl])).astype(BF16))
        r2.append(stack2(r_ref[0, rows, sl] * jnp.exp(cum)))
        b2.append(dup2(bb * e_neg).astype(BF16))
        k2.append(dup2(k * e_neg).astype(BF16))
        v2.append(stack2(v_ref[0, rows, sl]).astype(BF16))
        bh2.append(stack2(bb * e_rest).astype(BF16))
        kh2.append(stack2(k * e_rest).astype(BF16))
        decay.append(jnp.exp(tot))
    nc = len(chains)
    gram = [_dot_nt(jnp.concatenate([a2[i], r2[i].astype(BF16)], axis=0), jnp.concatenate([b2[i], k2[i]], axis=0))
            for i in range(nc)]
    ab = [jnp.where(strict, g[0:n2, 0:n2], 0.0) for g in gram]
    ak = [jnp.where(strict, g[0:n2, n2:2 * n2], 0.0).astype(BF16) for g in gram]
    rb = [jnp.where(incl, g[n2:2 * n2, 0:n2], 0.0).astype(BF16) for g in gram]
    rk = [jnp.where(incl, g[n2:2 * n2, n2:2 * n2], 0.0).astype(BF16) for g in gram]
    tinv = [eye] * nc
    for lvl in levels:
        tb = [t.astype(BF16) for t in tinv]
        tm = [_dot(tb[i], jnp.where(lvl, ab[i], 0.0).astype(BF16)).astype(BF16) for i in range(nc)]
        tinv = [tinv[i] - _dot(tm[i], tb[i]) for i in range(nc)]
    tb = [t.astype(BF16) for t in tinv]
    akv = [_dot(ak[i], v2[i]).astype(BF16) for i in range(nc)]
    wu = [_dot(tb[i], jnp.concatenate([a2[i], akv[i]], axis=1)).astype(BF16) for i in range(nc)]
    rwu = [_dot(rb[i], wu[i]) for i in range(nc)]
    rw = [(r2[i] - rwu[i][:, 0:LANES]).astype(BF16) for i in range(nc)]
    y0 = [_dot(rk[i], v2[i]) - rwu[i][:, LANES:2 * LANES] for i in range(nc)]
    gm = [_dot_tn(bh2[i], wu[i][:, 0:LANES]).astype(BF16) for i in range(nc)]
    s0t = [_dot_tn(jnp.concatenate([v2[i], wu[i][:, LANES:2 * LANES]], axis=0),
                   jnp.concatenate([kh2[i], -bh2[i]], axis=0)) for i in range(nc)]
    states = [st_scr[pr] for pr in range(npair)]
    ys = [None] * nc
    for i, (sub, pr) in enumerate(chains):
        st = states[pr]
        stb = st.astype(BF16)
        y2 = _dot_nt(rw[i], stb) + y0[i]
        ys[i] = y2[0:ch, :] + y2[ch:n2, :]
        states[pr] = st * decay[i] - _dot_nt(stb, gm[i]) + s0t[i]
    for pr in range(npair):
        st_scr[pr] = states[pr]
    tstep = nsub * ch
    y_st = jnp.concatenate([ys[sub * npair + pr] for pr in range(npair) for sub in range(nsub)], axis=0)
    rkr_st = jnp.concatenate([r_ref[0, :, pr * LANES:(pr + 1) * LANES] * k_ref[0, :, pr * LANES:(pr + 1) * LANES]
                              * rk_ref[:, pr * LANES:(pr + 1) * LANES] for pr in range(npair)], axis=0)
    sums = _dot_exact_rhs01(jnp.concatenate([y_st, rkr_st], axis=0), hsum)
    yc = y_st - sums[0:npair * tstep, :] * (1.0 / HEAD_DIM)
    var = _dot_exact_rhs01(yc * yc, hsum) * (1.0 / HEAD_DIM)
    yn = yc * lax.rsqrt(var + B_LN_EPS)
    for pr in range(npair):
        sl = slice(pr * LANES, (pr + 1) * LANES)
        rows = slice(pr * tstep, (pr + 1) * tstep)
        bonus = sums[npair * tstep + pr * tstep:npair * tstep + (pr + 1) * tstep, :] * v_ref[0, :, sl]
        out = (yn[rows, :] * lnw_ref[:, sl] + lnb_ref[:, sl] + bonus) * g_ref[0, :, sl]
        o_ref[0, :, sl] = out.astype(o_ref.dtype)


def _rwkv_kernel(p_ref, pprev_ref, mu_ref, wa2_ref, g2_ref, w0_ref, a0_ref, kkw_ref, ka_ref, rk_ref, lnw_ref, lnb_ref,
                 o_ref, st_scr, r_s, k_s, v_s, lw_s, kk_s, b_s, g_s, *, ch, nsub):
    _rwkv_prep_kernel(p_ref, pprev_ref, mu_ref, wa2_ref, g2_ref, w0_ref, a0_ref, kkw_ref, ka_ref,
                      r_s, k_s, v_s, lw_s, kk_s, b_s, g_s)
    _rwkv_scan_kernel(r_s, k_s, v_s, lw_s, kk_s, b_s, g_s, rk_ref, lnw_ref, lnb_ref, o_ref, st_scr, ch=ch, nsub=nsub)


def rwkv_mix(pb, mu, w0, w2, a0, a2, g2, k_k, k_a, r_k, ln_w, ln_b, ch=RWKV_CHUNK, nsub=4):
    b, t, win = pb.shape
    w = B_WIDTH
    ch = min(ch, t)
    nsub = min(nsub, t // ch)
    tstep = ch * nsub
    wa2 = jnp.zeros((LANES, 2 * w), F32)
    wa2 = wa2.at[:B_DECAY_RANK, :w].set(w2).at[B_DECAY_RANK:, w:].set(a2)
    wa2_hi = wa2.astype(BF16)
    wa2 = jnp.stack([wa2_hi, (wa2 - wa2_hi.astype(F32)).astype(BF16)])
    row = lambda z: z.reshape(1, -1)
    vec = pl.BlockSpec((1, w), lambda i, j: (0, 0))
    return pl.pallas_call(
        functools.partial(_rwkv_kernel, ch=ch, nsub=nsub),
        grid=(b, t // tstep),
        in_specs=[pl.BlockSpec((1, tstep, win), lambda i, j: (i, j, 0)),
                  pl.BlockSpec((1, SUBLANES, win), lambda i, j: (i, jnp.maximum(j * (tstep // SUBLANES) - 1, 0), 0)),
                  pl.BlockSpec((1, win), lambda i, j: (0, 0)),
                  pl.BlockSpec((2, LANES, 2 * w), lambda i, j: (0, 0, 0)),
                  pl.BlockSpec((B_GATE_RANK, w), lambda i, j: (0, 0)),
                  vec, vec, vec, vec, vec, vec, vec],
        out_specs=pl.BlockSpec((1, tstep, w), lambda i, j: (i, j, 0)),
        out_shape=jax.ShapeDtypeStruct((b, t, w), BF16),
        scratch_shapes=[pltpu.VMEM((B_HEADS // 2, LANES, LANES), F32)] + [pltpu.VMEM((1, tstep, w), F32)] * 7,
        compiler_params=_cparams("parallel", "arbitrary"),
        name="rwkv_mix",
    )(pb, pb, row(mu), wa2, g2.astype(BF16), row(w0), row(a0), row(k_k), row(k_a), row(r_k), row(ln_w), row(ln_b))


def _fill_vt(load_tile, store_tile, t):
    sub = lax.broadcasted_iota(jnp.int32, (LANES, LANES), 0)

    def body(j, carry):
        start = pl.multiple_of(j * LANES, LANES)
        store_tile(start, jnp.where(sub < HEAD_DIM, load_tile(start).T, 1.0).astype(BF16))
        return carry

    lax.fori_loop(0, t // LANES, body, 0)


def _stack_heads(src_ref, dst_scr, n_heads, scale):
    lo, hi = _lane_half_masks((QCHUNK, LANES))
    for h in range(n_heads):
        tile = src_ref[0, :, (h // 2) * LANES:(h // 2 + 1) * LANES]
        if scale != 1.0:
            tile = tile * scale
        dst_scr[h * QCHUNK:(h + 1) * QCHUNK, :] = jnp.where((lo, hi)[h % 2], tile, 0.0).astype(dst_scr.dtype)


def _score_stage(k_slab, qst_scr, mask_bias, n_heads, s_scr, mx_scr, slot):
    for pair in range(n_heads // 2):
        d = _dot_nt(k_slab(pair), qst_scr[2 * pair * QCHUNK:(2 * pair + 2) * QCHUNK, :])
        for e in range(2):
            h = 2 * pair + e
            cols = slice(h * QCHUNK, (h + 1) * QCHUNK)
            s = d[:, e * QCHUNK:(e + 1) * QCHUNK] + mask_bias(h)
            s_scr[slot, :, cols] = s
            mx_scr[slot, :, cols] = jnp.max(s, axis=0, keepdims=True)


def _softmax_stage(vt, n_heads, s_scr, mx_scr, slot, m_scr, acc_scr):
    m_old = m_scr[...]
    m_new = jnp.maximum(m_old, mx_scr[slot])
    alpha = jnp.exp2(m_old - m_new)
    m_scr[...] = m_new
    for pair in range(n_heads // 2):
        cols = slice(2 * pair * QCHUNK, (2 * pair + 2) * QCHUNK)
        p = jnp.exp2(s_scr[slot, :, cols] - m_new[:, cols]).astype(BF16)
        acc_scr[:, cols] = alpha[:, cols] * acc_scr[:, cols] + _dot(vt(pair), p)


def _pipelined_groups(ngroup, s_matmul, process):
    last = ngroup - 1
    s_matmul(0, 0)

    def pair_body(i, carry):
        g0 = 2 * i
        s_matmul(g0 + 1, 1)
        process(g0, 0)
        s_matmul(jnp.minimum(g0 + 2, last), 0)
        process(g0 + 1, 1)
        return carry

    lax.fori_loop(0, ngroup // 2, pair_body, 0)

    @pl.when(ngroup % 2 == 1)
    def _():
        process(last, 0)


def _bias_rows(tabt_ref, h, d_first, n_tiles):
    tiles = [tabt_ref[jnp.clip(d_first - u, 0, BIAS_TILES - 1), h] for u in range(n_tiles)]
    return jnp.concatenate(tiles, axis=0)


def _write_heads(o_ref, acc_scr, n_heads):
    for p in range(n_heads // 2):
        parts = []
        for h in (2 * p, 2 * p + 1):
            a = acc_scr[:, h * QCHUNK:(h + 1) * QCHUNK]
            parts.append(a[0:HEAD_DIM, :] * (1.0 / a[HEAD_DIM:HEAD_DIM + 1, :]))
        o_ref[0, :, p * LANES:(p + 1) * LANES] = jnp.concatenate(parts, axis=0).T.astype(o_ref.dtype)


def _moba_kernel(q_ref, k_ref, v_ref, tabt_ref, o_ref,
                 k_scr, vt_scr, km_scr, qst_scr, pick_scr, s_scr, mx_scr, m_scr, acc_scr):
    c = pl.program_id(1)
    t = k_ref.shape[1]
    nblk = t // MOBA_BLOCK
    nbp = km_scr.shape[1]
    group = C_HEADS // C_KV_HEADS
    pairs_per_kv = group // 2

    @pl.when(c == 0)
    def _():
        k_scr[...] = k_ref[0].astype(BF16)
        km_scr[...] = jnp.zeros_like(km_scr)
        for kvh in range(C_KV_HEADS):
            kf = k_ref[0, :, kvh * LANES:(kvh + 1) * LANES]
            km_scr[kvh, 0:nblk, :] = jnp.mean(kf.reshape(nblk, MOBA_BLOCK, LANES), axis=1)

            def store_vt(start, tile, kvh=kvh):
                vt_scr[kvh, :, pl.ds(start, LANES)] = tile

            _fill_vt(lambda start, kvh=kvh: v_ref[0, pl.ds(start, LANES), kvh * LANES:(kvh + 1) * LANES], store_vt, t)

    own = (c * QCHUNK) // MOBA_BLOCK
    _stack_heads(q_ref, qst_scr, C_HEADS, HEAD_DIM ** -0.5 * LOG2E)

    blk = lax.broadcasted_iota(jnp.int32, (nbp, QCHUNK), 0)
    for kvh in range(C_KV_HEADS):
        first_tile = kvh * pairs_per_kv
        qsum = q_ref[0, :, first_tile * LANES:(first_tile + 1) * LANES]
        for p in range(first_tile + 1, first_tile + pairs_per_kv):
            qsum = qsum + q_ref[0, :, p * LANES:(p + 1) * LANES]
        gate = _dot_nt_x(km_scr[kvh], qsum)
        gate = jnp.where(blk < own, gate, -jnp.inf)
        rank = jnp.zeros((nbp, QCHUNK), jnp.int32)
        for mblk in range(nblk):
            gm = gate[mblk:mblk + 1, :]
            ahead = jnp.logical_or(gm > gate, jnp.logical_and(gm == gate, mblk < blk))
            rank = rank + jnp.where(ahead, 1, 0)
        chosen = jnp.logical_and(rank < MOBA_TOPK, blk < own)
        pick_scr[kvh] = jnp.where(chosen, 0.0, MASKED)

    m_scr[...] = jnp.full(m_scr.shape, M_INIT, F32)
    acc_scr[...] = jnp.zeros_like(acc_scr)
    grp = 2 * MOBA_BLOCK
    tiles_per_grp = grp // LANES
    ngroup = own // 2 + 1
    grow = lax.broadcasted_iota(jnp.int32, (grp, QCHUNK), 0)
    second = grow >= MOBA_BLOCK
    qpos = c * QCHUNK + lax.broadcasted_iota(jnp.int32, (grp, QCHUNK), 1)

    def score_stage(gi, slot):
        start = pl.multiple_of(gi * grp, grp)
        causal = jnp.where((gi * grp + grow) <= qpos, 0.0, MASKED)
        okb = []
        for kvh in range(C_KV_HEADS):
            first = jnp.where(2 * gi < own, pick_scr[kvh, pl.ds(2 * gi, 1), :], causal)
            later = jnp.where(2 * gi + 1 < own, pick_scr[kvh, pl.ds(2 * gi + 1, 1), :], causal)
            okb.append(jnp.where(second, later, first))
        d_first = c - tiles_per_grp * gi
        _score_stage(lambda pair: k_scr[pl.ds(start, grp), (pair // pairs_per_kv) * LANES:(pair // pairs_per_kv + 1) * LANES],
                     qst_scr, lambda h: _bias_rows(tabt_ref, h, d_first, tiles_per_grp) + okb[h // group],
                     C_HEADS, s_scr, mx_scr, slot)

    def softmax_stage(gi, slot):
        start = pl.multiple_of(gi * grp, grp)
        _softmax_stage(lambda pair: vt_scr[pair // pairs_per_kv, :, pl.ds(start, grp)],
                       C_HEADS, s_scr, mx_scr, slot, m_scr, acc_scr)

    _pipelined_groups(ngroup, score_stage, softmax_stage)
    _write_heads(o_ref, acc_scr, C_HEADS)


def moba_attention(qc, kvc, tabt):
    b, t, _ = qc.shape
    assert t % MOBA_BLOCK == 0
    nq = t // QCHUNK
    qw = C_HEADS * HEAD_DIM
    kvw = C_KV_HEADS * LANES
    nbp = -(-(t // MOBA_BLOCK) // SUBLANES) * SUBLANES
    return pl.pallas_call(
        _moba_kernel,
        grid=(b, nq),
        in_specs=[pl.BlockSpec((1, QCHUNK, qw), lambda i, c: (i, c, 0)),
                  pl.BlockSpec((1, t, kvw), lambda i, c: (i, 0, 0)),
                  pl.BlockSpec((1, t, kvw), lambda i, c: (i, 0, 1)),
                  pl.BlockSpec((BIAS_TILES, C_HEADS, LANES, LANES), lambda i, c: (0, 0, 0, 0))],
        out_specs=pl.BlockSpec((1, QCHUNK, qw), lambda i, c: (i, c, 0)),
        out_shape=jax.ShapeDtypeStruct((b, t, qw), BF16),
        scratch_shapes=[pltpu.VMEM((t, kvw), BF16), pltpu.VMEM((C_KV_HEADS, LANES, t), BF16),
                        pltpu.VMEM((C_KV_HEADS, nbp, LANES), F32),
                        pltpu.VMEM((C_HEADS * QCHUNK, LANES), BF16),
                        pltpu.VMEM((C_KV_HEADS, nbp, QCHUNK), F32),
                        pltpu.VMEM((2, 2 * MOBA_BLOCK, C_HEADS * QCHUNK), F32),
                        pltpu.VMEM((2, 1, C_HEADS * QCHUNK), F32),
                        pltpu.VMEM((1, C_HEADS * QCHUNK), F32),
                        pltpu.VMEM((LANES, C_HEADS * QCHUNK), F32)],
        compiler_params=_cparams("parallel", "arbitrary"),
        name="moba_attention",
    )(qc, kvc, kvc, tabt)


def _dsa_kernel(q_ref, kv_ref, qi_ref, ki_ref, gk_ref, tabt_ref, o_ref,
                k_scr, vt_scr, ki_scr, sc_scr, qst_scr, qist_scr, pre_scr, s_scr, mx_scr, m_scr, acc_scr, *, topk):
    c = pl.program_id(1)
    t = k_scr.shape[0]
    grp = KEY_GROUP * LANES
    ngroup = c // KEY_GROUP + 1
    sub = lax.broadcasted_iota(jnp.int32, (LANES, QCHUNK), 0)
    gsub = lax.broadcasted_iota(jnp.int32, (grp, QCHUNK), 0)
    qpos = c * QCHUNK + lax.broadcasted_iota(jnp.int32, (grp, QCHUNK), 1)

    @pl.when(c == 0)
    def _():
        k_scr[...] = kv_ref[0, :, 0:LANES].astype(BF16)
        ki_scr[...] = _rms(ki_ref[0], gk_ref[...]).astype(BF16)
        def store_vt(start, tile):
            vt_scr[:, pl.ds(start, LANES)] = tile

        _fill_vt(lambda start: kv_ref[0, pl.ds(start, LANES), LANES:2 * LANES], store_vt, t)

    _stack_heads(q_ref, qst_scr, D_HEADS, HEAD_DIM ** -0.5 * LOG2E)
    _stack_heads(qi_ref, qist_scr, IDX_HEADS, 1.0)
    w_t = (qi_ref[0, :, IDX_HEADS * IDX_DIM + LANES:IDX_HEADS * IDX_DIM + 2 * LANES]
           * ((IDX_HEADS * IDX_DIM) ** -0.5)).T

    def score_group(gi):
        start = pl.multiple_of(gi * grp, grp)
        rel = jnp.maximum(_dot_nt(ki_scr[pl.ds(start, grp), :], qist_scr[...]), 0.0)
        sc = jnp.zeros((grp, QCHUNK), F32)
        for h in range(IDX_HEADS):
            sc = sc + rel[:, h * QCHUNK:(h + 1) * QCHUNK] * w_t[h:h + 1, :]
        sc = jnp.where(sc == 0.0, 0.0, sc)
        sc_scr[pl.ds(start, grp), :] = jnp.where((gi * grp + gsub) <= qpos, sc, -jnp.inf)

    def score_pair(i, carry):
        score_group(2 * i)
        score_group(2 * i + 1)
        return carry

    lax.fori_loop(0, ngroup // 2, score_pair, 0)

    @pl.when(ngroup % 2 == 1)
    def _():
        score_group(ngroup - 1)

    def count(pred):
        def body(gi, acc):
            for u in range(KEY_GROUP):
                start = pl.multiple_of(gi * grp + u * LANES, LANES)
                acc = jnp.where(pred(sc_scr[pl.ds(start, LANES), :]), acc + 1, acc)
            return acc
        acc = lax.fori_loop(0, ngroup, body, jnp.zeros((LANES, QCHUNK), jnp.int32))
        return jnp.sum(acc, axis=0, keepdims=True)

    def key_to_float(key):
        return pltpu.bitcast(key ^ (lax.shift_right_arithmetic(key, 31) & 0x7FFFFFFF), F32)

    cnt0 = count(lambda st: st >= 0.0)
    code = jnp.where(cnt0 >= topk, 0, INT_MIN).astype(jnp.int32)

    def bit_body(i, code):
        cand = code + lax.shift_left(jnp.int32(1), 30 - i)
        cand_f = key_to_float(cand)
        cnt = count(lambda st: st >= cand_f)
        return jnp.where(cnt >= topk, cand, code)

    code = lax.fori_loop(0, 31, bit_body, code)
    thr = jnp.where(code == INT_MIN, -jnp.inf, key_to_float(jnp.maximum(code, INT_MIN + 1)))
    thr_eff = jnp.maximum(thr, LOWEST_F32)

    cnt_ge = count(lambda st: st >= thr_eff)
    tied = cnt_ge > topk

    @pl.when(jnp.max(jnp.where(tied, 1.0, 0.0)) > 0.5)
    def _():
        def tally_body(gi, ties_before):
            sums = []
            for u in range(KEY_GROUP):
                start = pl.multiple_of(gi * grp + u * LANES, LANES)
                is_tie = sc_scr[pl.ds(start, LANES), :] == thr
                sums.append(jnp.sum(jnp.where(is_tie, 1.0, 0.0), axis=0, keepdims=True))
            for u in range(KEY_GROUP):
                pre_scr[pl.ds(gi * KEY_GROUP + u, 1), :] = ties_before
                ties_before = ties_before + sums[u]
            return ties_before

        ties = lax.fori_loop(0, ngroup, tally_body, jnp.zeros((1, QCHUNK), F32))
        above = cnt_ge.astype(F32) - ties
        need = jnp.where(tied, topk - above, 3.0e38)
        tri = jnp.where(lax.broadcasted_iota(jnp.int32, (LANES, LANES), 1) <= sub, 1.0, 0.0).astype(BF16)

        def demote_body(gi, carry):
            tiles = []
            for u in range(KEY_GROUP):
                start = pl.multiple_of(gi * grp + u * LANES, LANES)
                tiles.append(sc_scr[pl.ds(start, LANES), :])
            eq01 = jnp.concatenate([jnp.where(st == thr, 1.0, 0.0).astype(BF16) for st in tiles], axis=1)
            within = _dot(tri, eq01)
            for u, st in enumerate(tiles):
                start = pl.multiple_of(gi * grp + u * LANES, LANES)
                rank = pre_scr[pl.ds(gi * KEY_GROUP + u, 1), :] + within[:, u * QCHUNK:(u + 1) * QCHUNK]
                drop = jnp.logical_and(st == thr, rank > need)
                sc_scr[pl.ds(start, LANES), :] = jnp.where(drop, -jnp.inf, st)
            return carry

        lax.fori_loop(0, ngroup, demote_body, 0)

    m_scr[...] = jnp.full(m_scr.shape, M_INIT, F32)
    acc_scr[...] = jnp.zeros_like(acc_scr)

    def score_stage(gi, slot):
        start = pl.multiple_of(gi * grp, grp)
        okb = jnp.where(sc_scr[pl.ds(start, grp), :] >= thr_eff, 0.0, MASKED)
        d_first = c - gi * KEY_GROUP
        _score_stage(lambda pair: k_scr[pl.ds(start, grp), :], qst_scr,
                     lambda h: _bias_rows(tabt_ref, h, d_first, KEY_GROUP) + okb, D_HEADS, s_scr, mx_scr, slot)

    def softmax_stage(gi, slot):
        start = pl.multiple_of(gi * grp, grp)
        _softmax_stage(lambda pair: vt_scr[:, pl.ds(start, grp)], D_HEADS, s_scr, mx_scr, slot, m_scr, acc_scr)

    _pipelined_groups(ngroup, score_stage, softmax_stage)
    _write_heads(o_ref, acc_scr, D_HEADS)


def dsa_attention(qd, kvd, idx, kidx_gain, tabt):
    b, t, _ = qd.shape
    topk = min(DSA_TOPK, t // 4)
    nq = t // QCHUNK
    gk = jnp.concatenate([kidx_gain, kidx_gain]).reshape(1, LANES)
    wq = D_HEADS * HEAD_DIM
    return pl.pallas_call(
        functools.partial(_dsa_kernel, topk=topk),
        grid=(b, nq),
        in_specs=[pl.BlockSpec((1, QCHUNK, wq), lambda i, c: (i, c, 0)),
                  pl.BlockSpec((1, t, 2 * LANES), lambda i, c: (i, 0, 0)),
                  pl.BlockSpec((1, QCHUNK, 4 * LANES), lambda i, c: (i, c, 0)),
                  pl.BlockSpec((1, t, LANES), lambda i, c: (i, 0, 2)),
                  pl.BlockSpec((1, LANES), lambda i, c: (0, 0)),
                  pl.BlockSpec((BIAS_TILES, D_HEADS, LANES, LANES), lambda i, c: (0, 0, 0, 0))],
        out_specs=pl.BlockSpec((1, QCHUNK, wq), lambda i, c: (i, c, 0)),
        out_shape=jax.ShapeDtypeStruct((b, t, wq), BF16),
        scratch_shapes=[pltpu.VMEM((t, LANES), BF16), pltpu.VMEM((LANES, t), BF16), pltpu.VMEM((t, LANES), BF16),
                        pltpu.VMEM((t, QCHUNK), F32),
                        pltpu.VMEM((D_HEADS * QCHUNK, LANES), BF16), pltpu.VMEM((IDX_HEADS * QCHUNK, LANES), BF16),
                        pltpu.VMEM((t // LANES, QCHUNK), F32),
                        pltpu.VMEM((2, KEY_GROUP * LANES, D_HEADS * QCHUNK), F32),
                        pltpu.VMEM((2, 1, D_HEADS * QCHUNK), F32),
                        pltpu.VMEM((1, D_HEADS * QCHUNK), F32),
                        pltpu.VMEM((LANES, D_HEADS * QCHUNK), F32)],
        compiler_params=_cparams("parallel", "arbitrary"),
        name="dsa_attention",
    )(qd, kvd, idx, idx, gk, tabt)


def _dup_heads(wcols, n_heads):
    d = wcols.shape[0]
    wh = wcols.reshape(d, n_heads, 1, HEAD_DIM)
    return jnp.broadcast_to(wh, (d, n_heads, 2, HEAD_DIM)).reshape(d, n_heads * 2 * HEAD_DIM)


def _even_weight(w_in):
    aq, akv = A_HEADS * HEAD_DIM, A_KV_HEADS * HEAD_DIM
    q = w_in[:, :aq]
    k = _dup_heads(w_in[:, aq:aq + akv], A_KV_HEADS)
    v = _dup_heads(w_in[:, aq + akv:aq + 2 * akv], A_KV_HEADS)
    rest = w_in[:, aq + 2 * akv:]
    return jnp.concatenate([q, k, v, rest], axis=1).astype(BF16)


def _odd_weight(w_in):
    d = w_in.shape[0]
    cq, ckv, dq = C_HEADS * HEAD_DIM, C_KV_HEADS * HEAD_DIM, D_HEADS * HEAD_DIM
    cuts = np.cumsum([cq, ckv, ckv, dq, HEAD_DIM, HEAD_DIM, IDX_HEADS * IDX_DIM, IDX_DIM]).tolist()
    qc, kc, vc, qd, kd, vd, qi, ki, wi = jnp.split(w_in, cuts, axis=1)
    pad = jnp.zeros((d, LANES - IDX_HEADS), w_in.dtype)
    cols = [qc, _dup_heads(kc, C_KV_HEADS), _dup_heads(vc, C_KV_HEADS),
            qd, _dup_heads(kd, 1), _dup_heads(vd, 1),
            qi, _dup_heads(ki, 1), wi, pad]
    return jnp.concatenate(cols, axis=1).astype(BF16)


def kernel(x, mem, rel_bias, norm_gains, mix_w_out, mem_norm, x_wq, x_wkv, x_wo, ff_w1, ff_w2, ev_w_in, a_sinks, b_mu, b_w0, b_w2, b_a0, b_a2, b_g2, b_k_k, b_k_a, b_r_k, b_ln_w, b_ln_b, od_w_in, d_kidx_norm):
    bsz, t, d = x.shape
    n = bsz * t
    depth = norm_gains.shape[0]
    mlen = mem.shape[1]
    tabt = bias_table(rel_bias, transposed=True)
    x2 = x.reshape(n, d)
    for layer in range(depth):
        i = layer // 2
        g = norm_gains[layer]
        if layer % 2 == 0:
            b_in = 3 * B_WIDTH + B_DECAY_RANK + B_A_RANK + B_GATE_RANK
            qkv, pb = norm_proj(x2, g[0], _even_weight(ev_w_in[i]), (1024, b_in), (F32, F32))
            ya = swa_attention(qkv.reshape(bsz, t, 1024), a_sinks[i], rel_bias, tabt)
            yb = rwkv_mix(pb.reshape(bsz, t, b_in), b_mu[i], b_w0[i], b_w2[i], b_a0[i], b_a2[i], b_g2[i],
                          b_k_k[i], b_k_a[i], b_r_k[i].reshape(-1), b_ln_w[i], b_ln_b[i])
        else:
            qc, kvc, qd, kvd, idx = norm_proj(x2, g[0], _odd_weight(od_w_in[i]),
                                              (512, 512, 512, 256, 512), (F32,) * 5)
            ya = moba_attention(qc.reshape(bsz, t, 512), kvc.reshape(bsz, t, 512), tabt)
            yb = dsa_attention(qd.reshape(bsz, t, 512), kvd.reshape(bsz, t, 256), idx.reshape(bsz, t, 512),
                               d_kidx_norm[i], tabt)
        kmem, vmem = norm_proj(mem.reshape(bsz * mlen, d), mem_norm[layer], x_wkv[layer].astype(BF16),
                               (X_WIDTH, X_WIDTH), (BF16, BF16))
        x3 = post_mixer(ya, yb, x2.reshape(bsz, t, d), kmem.reshape(bsz, mlen, X_WIDTH),
                        vmem.reshape(bsz, mlen, X_WIDTH), mix_w_out[layer], x_wq[layer], x_wo[layer],
                        ff_w1[layer], ff_w2[layer], g)
        x2 = x3.reshape(n, d)
    return x2.reshape(bsz, t, d)
```

```python
import functools

import numpy as np
import jax
import jax.numpy as jnp
from jax import lax
from jax.experimental import pallas as pl
from jax.experimental.pallas import tpu as pltpu

F32 = jnp.float32
BF16 = jnp.bfloat16
HIGHEST = lax.Precision.HIGHEST

LANES = 128
SUBLANES = 8
HEAD_DIM = 64
EPS = 1e-6
A_HEADS, A_KV_HEADS, BAND = 8, 2, 128
B_HEADS, B_WIDTH = 8, 512
B_DECAY_RANK, B_A_RANK, B_GATE_RANK = 64, 64, 128
B_LN_EPS = 64e-5
RWKV_CHUNK = 64
C_HEADS, C_KV_HEADS, MOBA_BLOCK, MOBA_TOPK, QCHUNK = 8, 2, 256, 3, 128
D_HEADS, IDX_HEADS, IDX_DIM, DSA_TOPK = 8, 4, 64, 256
N_BUCKETS, BIAS_HEADS = 32, 8
BIAS_TILES = 9
X_HEADS, X_HEAD_DIM, X_WIDTH = 4, 128, 512

LOG2E = 1.4426950408889634
KEY_GROUP = 4
MASKED = -2e30
M_INIT = -1e30
INT_MIN = -(2 ** 31)
LOWEST_F32 = -3.4028234663852886e38
VMEM_LIMIT = 56 * 1024 * 1024


def _cparams(*sem):
    return pltpu.CompilerParams(dimension_semantics=tuple(sem), vmem_limit_bytes=VMEM_LIMIT)


def _dot(a, b):
    return jnp.dot(a, b, preferred_element_type=F32)


def _dot_nt(a, b):
    return lax.dot_general(a, b, (((1,), (1,)), ((), ())), preferred_element_type=F32)


def _dot_tn(a, b):
    return lax.dot_general(a, b, (((0,), (0,)), ((), ())), preferred_element_type=F32)


def _dot_x(a, b):
    return jnp.dot(a, b, preferred_element_type=F32, precision=HIGHEST)


def _dot_nt_x(a, b):
    return lax.dot_general(a, b, (((1,), (1,)), ((), ())), preferred_element_type=F32, precision=HIGHEST)


def _rms(xf, g):
    ms = jnp.mean(xf * xf, axis=-1, keepdims=True)
    return xf * lax.rsqrt(ms + EPS) * g


def _lane_half_masks(shape):
    lane = lax.broadcasted_iota(jnp.int32, shape, len(shape) - 1)
    lo = lane < HEAD_DIM
    return lo, jnp.logical_not(lo)


def _bucket_of_distance(n):
    exact = N_BUCKETS // 2
    if n < exact:
        return n
    j = 0
    while n ** 8 >= (exact ** 8) * (2 ** (3 * (j + 1))):
        j += 1
    return min(exact + j, N_BUCKETS - 1)


@functools.lru_cache(maxsize=None)
def _bucket_tiles(transposed):
    by_dist = np.array([_bucket_of_distance(n) for n in range(BIAS_TILES * LANES + LANES)], np.int32)
    q = np.arange(LANES)[None, :] if transposed else np.arange(LANES)[:, None]
    k = np.arange(LANES)[:, None] if transposed else np.arange(LANES)[None, :]
    tiles = [by_dist[np.maximum(d * LANES + q - k, 0)] for d in range(BIAS_TILES)]
    return np.stack(tiles).astype(np.int32)


def _bias_table_kernel(rb_ref, bkt_ref, out_ref, *, base2_shifted):
    bkt = bkt_ref[0]
    for h in range(BIAS_HEADS):
        acc = jnp.zeros((LANES, LANES), F32)
        for b in range(N_BUCKETS):
            acc = jnp.where(bkt == b, rb_ref[b, h], acc)
        if base2_shifted:
            acc = (acc - rb_ref[N_BUCKETS - 1, h]) * LOG2E
        out_ref[0, h] = acc


def bias_table(rel_bias, transposed=False):
    return pl.pallas_call(
        functools.partial(_bias_table_kernel, base2_shifted=transposed),
        grid=(BIAS_TILES,),
        in_specs=[pl.BlockSpec(memory_space=pltpu.SMEM),
                  pl.BlockSpec((1, LANES, LANES), lambda d: (d, 0, 0))],
        out_specs=pl.BlockSpec((1, BIAS_HEADS, LANES, LANES), lambda d: (d, 0, 0, 0)),
        out_shape=jax.ShapeDtypeStruct((BIAS_TILES, BIAS_HEADS, LANES, LANES), F32),
        compiler_params=_cparams("arbitrary"),
        name="bias_table",
    )(rel_bias, jnp.asarray(_bucket_tiles(transposed)))


def _norm_proj_kernel(x_ref, g_ref, w_ref, *out_refs, splits):
    h = _rms(x_ref[...], g_ref[...]).astype(BF16)
    res = _dot(h, w_ref[...])
    off = 0
    for o_ref, width in zip(out_refs, splits):
        o_ref[...] = res[:, off:off + width].astype(o_ref.dtype)
        off += width


def norm_proj(x2, gain, w, splits, out_dtypes, tm=512):
    n, d = x2.shape
    tm = min(tm, n)
    total = sum(splits)
    assert w.shape == (d, total) and n % tm == 0 and all(s % LANES == 0 for s in splits)
    return pl.pallas_call(
        functools.partial(_norm_proj_kernel, splits=tuple(splits)),
        grid=(n // tm,),
        in_specs=[pl.BlockSpec((tm, d), lambda i: (i, 0)),
                  pl.BlockSpec((1, d), lambda i: (0, 0)),
                  pl.BlockSpec((d, total), lambda i: (0, 0), pipeline_mode=pl.Buffered(1))],
        out_specs=[pl.BlockSpec((tm, s), lambda i: (i, 0)) for s in splits],
        out_shape=[jax.ShapeDtypeStruct((n, s), dt) for s, dt in zip(splits, out_dtypes)],
        compiler_params=_cparams("parallel"),
        name="norm_proj",
    )(x2, gain.reshape(1, d), w)


def _post_mixer_kernel(ya_ref, yb_ref, x_ref, k_ref, v_ref, wa_ref, wb_ref, wq_ref, wo_ref, w1_ref, w2_ref, g_ref,
                       o_ref, *, tf):
    x = x_ref[0]
    z = _dot(ya_ref[0], wa_ref[...]) + _dot(yb_ref[0], wb_ref[...])
    x = x + _rms(z, g_ref[1:2, :])
    q = _dot(_rms(x, g_ref[2:3, :]).astype(BF16), wq_ref[...])
    k = k_ref[0]
    v = v_ref[0]
    outs = []
    for hd in range(X_HEADS):
        sl = slice(hd * X_HEAD_DIM, (hd + 1) * X_HEAD_DIM)
        s = _dot_nt(q[:, sl].astype(BF16), k[:, sl]) * (X_HEAD_DIM ** -0.5)
        m = jnp.max(s, axis=-1, keepdims=True)
        p = jnp.exp(s - m)
        p = p / jnp.sum(p, axis=-1, keepdims=True)
        outs.append(_dot(p.astype(BF16), v[:, sl]))
    z = _dot(jnp.concatenate(outs, axis=-1).astype(BF16), wo_ref[...])
    x = x + _rms(z, g_ref[3:4, :])
    h = _rms(x, g_ref[4:5, :]).astype(BF16)
    acc = jnp.zeros(x.shape, F32)
    for j in range(w1_ref.shape[1] // tf):
        a = jnp.maximum(_dot(h, w1_ref[:, j * tf:(j + 1) * tf]), 0.0)
        acc = acc + _dot((a * a).astype(BF16), w2_ref[j * tf:(j + 1) * tf, :])
    o_ref[0] = x + _rms(acc, g_ref[5:6, :])


def post_mixer(ya, yb, x3, kmem, vmem, w_out, wq, wo, w1, w2, gains, tq=512, tf=1024):
    b, t, d = x3.shape
    tq = min(tq, t)
    m = kmem.shape[1]
    ka, kb = ya.shape[2], yb.shape[2]
    f = w1.shape[1]
    once = pl.Buffered(1)
    const = lambda shape: pl.BlockSpec(shape, lambda i, j: (0, 0), pipeline_mode=once)
    return pl.pallas_call(
        functools.partial(_post_mixer_kernel, tf=tf),
        grid=(b, t // tq),
        in_specs=[pl.BlockSpec((1, tq, ka), lambda i, j: (i, j, 0)),
                  pl.BlockSpec((1, tq, kb), lambda i, j: (i, j, 0)),
                  pl.BlockSpec((1, tq, d), lambda i, j: (i, j, 0)),
                  pl.BlockSpec((1, m, X_WIDTH), lambda i, j: (i, 0, 0)),
                  pl.BlockSpec((1, m, X_WIDTH), lambda i, j: (i, 0, 0)),
                  const((ka, d)), const((kb, d)), const((d, X_WIDTH)), const((X_WIDTH, d)),
                  const((d, f)), const((f, d)), const(gains.shape)],
        out_specs=pl.BlockSpec((1, tq, d), lambda i, j: (i, j, 0)),
        out_shape=jax.ShapeDtypeStruct((b, t, d), F32),
        compiler_params=_cparams("parallel", "parallel"),
        name="post_mixer",
    )(ya, yb, x3, kmem, vmem, w_out[:ka].astype(BF16), w_out[ka:].astype(BF16), wq.astype(BF16), wo.astype(BF16),
      w1.astype(BF16), w2.astype(BF16), gains)


def _swa_kernel(sink_ref, rb_ref, q_ref, kv_ref, tabt_ref, o_ref, qst_scr, vt_scr):
    n = pl.program_id(1)
    group = A_HEADS // A_KV_HEADS
    first_band = jnp.maximum(n - 1, 0)
    start = pl.multiple_of(first_band * BAND, BAND)
    slab = 2 * BAND
    _stack_heads(q_ref, qst_scr, A_HEADS, HEAD_DIM ** -0.5 * LOG2E)
    kpos = start + lax.broadcasted_iota(jnp.int32, (slab, BAND), 0)
    qpos = n * BAND + lax.broadcasted_iota(jnp.int32, (slab, BAND), 1)
    dist = qpos - kpos
    okb = jnp.where(jnp.logical_and(dist >= 0, dist < BAND), 0.0, MASKED)
    sub = lax.broadcasted_iota(jnp.int32, (LANES, slab), 0)
    koff = 0
    voff = A_KV_HEADS * LANES
    d_first = n - first_band
    scores, vts = [], []
    for hkv in range(A_KV_HEADS):
        k2 = kv_ref[0, pl.ds(start, slab), koff + hkv * LANES:koff + (hkv + 1) * LANES].astype(BF16)
        v2 = kv_ref[0, pl.ds(start, slab), voff + hkv * LANES:voff + (hkv + 1) * LANES]
        vt_scr[hkv] = jnp.where(sub < HEAD_DIM, v2.T, 1.0).astype(BF16)
        vts.append(vt_scr[hkv])
        s_all = _dot_nt(k2, qst_scr[hkv * group * BAND:(hkv + 1) * group * BAND, :])
        for g in range(group):
            h = hkv * group + g
            scores.append(s_all[:, g * BAND:(g + 1) * BAND] + (_bias_rows(tabt_ref, h, d_first, 2) + okb))
    sinks = [(sink_ref[h] - rb_ref[N_BUCKETS - 1, h]) * LOG2E for h in range(A_HEADS)]
    m = [jnp.maximum(jnp.max(scores[h], axis=0, keepdims=True), sinks[h]) for h in range(A_HEADS)]
    p = [jnp.exp2(scores[h] - m[h]).astype(BF16) for h in range(A_HEADS)]
    outs = []
    for hkv in range(A_KV_HEADS):
        pv = _dot(vts[hkv], jnp.concatenate(p[hkv * group:(hkv + 1) * group], axis=1))
        for g in range(group):
            h = hkv * group + g
            a = pv[:, g * BAND:(g + 1) * BAND]
            den = a[HEAD_DIM:HEAD_DIM + 1, :] + jnp.exp2(sinks[h] - m[h])
            outs.append(a[0:HEAD_DIM, :] * (1.0 / den))
    for pair in range(A_HEADS // 2):
        tile = jnp.concatenate(outs[2 * pair:2 * pair + 2], axis=0).T
        o_ref[0, :, pair * LANES:(pair + 1) * LANES] = tile.astype(o_ref.dtype)


def swa_attention(qkv, sinks, rel_bias, tabt):
    b, t, w = qkv.shape
    qw = A_HEADS * HEAD_DIM
    return pl.pallas_call(
        _swa_kernel,
        grid=(b, t // BAND),
        in_specs=[pl.BlockSpec(memory_space=pltpu.SMEM),
                  pl.BlockSpec(memory_space=pltpu.SMEM),
                  pl.BlockSpec((1, BAND, qw), lambda i, n: (i, n, 0)),
                  pl.BlockSpec((1, t, w - qw), lambda i, n: (i, 0, 1)),
                  pl.BlockSpec((2, A_HEADS, LANES, LANES), lambda i, n: (0, 0, 0, 0))],
        out_specs=pl.BlockSpec((1, BAND, qw), lambda i, n: (i, n, 0)),
        out_shape=jax.ShapeDtypeStruct((b, t, qw), BF16),
        scratch_shapes=[pltpu.VMEM((A_HEADS * BAND, LANES), BF16), pltpu.VMEM((A_KV_HEADS, LANES, 2 * BAND), BF16)],
        compiler_params=_cparams("parallel", "arbitrary"),
        name="swa_attention",
    )(sinks, rel_bias, qkv, qkv, tabt)


def _split3(x):
    p0 = x.astype(BF16)
    r1 = x - p0.astype(F32)
    p1 = r1.astype(BF16)
    p2 = (r1 - p1.astype(F32)).astype(BF16)
    return p0, p1, p2


def _dot_exact_rhs01(x, m01):
    p0, p1, p2 = _split3(x)
    return _dot(p0, m01) + _dot(p1, m01) + _dot(p2, m01)


def _dot_exact_lhs01(m01, x):
    p0, p1, p2 = _split3(x)
    return _dot(m01, p0) + _dot(m01, p1) + _dot(m01, p2)


def _head_sum_matrix(width):
    r = lax.broadcasted_iota(jnp.int32, (width, width), 0) // HEAD_DIM
    c = lax.broadcasted_iota(jnp.int32, (width, width), 1) // HEAD_DIM
    return jnp.where(r == c, 1.0, 0.0).astype(BF16)


def _sigmoid(z):
    return 1.0 / (1.0 + jnp.exp(-z))


def _softplus(z):
    return jnp.maximum(z, 0.0) + jnp.log(1.0 + jnp.exp(-jnp.abs(z)))


def _rwkv_prep_kernel(p_ref, pprev_ref, mu_ref, wa2_ref, g2_ref, w0_ref, a0_ref, kk_ref, ka_ref,
                      r_out, k_out, v_out, lw_out, kk_out, b_out, g_out):
    i = pl.program_id(1)
    p = p_ref[0]
    w = B_WIDTH
    last = pprev_ref[0, SUBLANES - 1:SUBLANES, :]
    last = jnp.where(i > 0, last, 0.0)
    row = lax.broadcasted_iota(jnp.int32, p.shape, 0)
    prev = jnp.where(row == 0, last, pltpu.roll(p, 1, 0))
    xs = p + (prev - p) * mu_ref[...]
    r = xs[:, 0:w]
    k = xs[:, w:2 * w]
    v = xs[:, 2 * w:3 * w]
    wa_lo = xs[:, 3 * w:3 * w + LANES]
    g_lo = xs[:, 3 * w + LANES:3 * w + 2 * LANES]
    lo, _ = _lane_half_masks(wa_lo.shape)
    wa_in = jnp.where(lo, jnp.tanh(wa_lo), wa_lo)
    in_hi = wa_in.astype(BF16)
    in_lo = (wa_in - in_hi.astype(F32)).astype(BF16)
    wa = _dot(in_hi, wa2_ref[0]) + (_dot(in_hi, wa2_ref[1]) + _dot(in_lo, wa2_ref[0]))
    wlog = -_softplus(-(w0_ref[...] + wa[:, 0:w])) - 0.5
    a = _sigmoid(a0_ref[...] + wa[:, w:2 * w])
    g = _dot(_sigmoid(g_lo).astype(BF16), g2_ref[...])
    kk = k * kk_ref[...]
    ss = _dot_exact_rhs01(kk * kk, _head_sum_matrix(w))
    kk = kk / jnp.maximum(jnp.sqrt(ss), 1e-12)
    kmod = k * (1.0 + (a - 1.0) * ka_ref[...])
    r_out[0] = r
    k_out[0] = kmod
    v_out[0] = v
    lw_out[0] = -jnp.exp(wlog)
    kk_out[0] = kk
    b_out[0] = kk * a
    g_out[0] = g


def _rwkv_scan_kernel(r_ref, k_ref, v_ref, lw_ref, kk_ref, b_ref, g_ref, rk_ref, lnw_ref, lnb_ref,
                      o_ref, st_scr, *, ch, nsub):
    step = pl.program_id(1)
    n2 = 2 * ch

    @pl.when(step == 0)
    def _():
        st_scr[...] = jnp.zeros_like(st_scr)

    ti = lax.broadcasted_iota(jnp.int32, (n2, n2), 0)
    si = lax.broadcasted_iota(jnp.int32, (n2, n2), 1)
    same = (ti // ch) == (si // ch)
    incl = jnp.logical_and(same, si <= ti)
    strict = jnp.logical_and(same, si < ti)
    eye = jnp.where(si == ti, 1.0, 0.0)
    levels = []
    s = 1
    while s < ch:
        same2s = (ti // (2 * s)) == (si // (2 * s))
        levels.append(jnp.logical_and(same2s, jnp.logical_and((ti // s) % 2 == 1, (si // s) % 2 == 0)))
        s *= 2
    tr = lax.broadcasted_iota(jnp.int32, (ch, ch), 0)
    tc = lax.broadcasted_iota(jnp.int32, (ch, ch), 1)
    tri01 = jnp.where(tc <= tr, 1.0, 0.0).astype(BF16)
    lo, hi = _lane_half_masks((ch, LANES))
    hsum = _head_sum_matrix(LANES)

    def stack2(z):
        return jnp.concatenate([jnp.where(lo, z, 0.0), jnp.where(hi, z, 0.0)], axis=0)

    def dup2(z):
        return jnp.concatenate([z, z], axis=0)

    npair = B_HEADS // 2
    chains = [(sub, pr) for sub in range(nsub) for pr in range(npair)]
    cums = [_dot_exact_lhs01(tri01, lw_ref[0, sub * ch:(sub + 1) * ch, :]) for sub in range(nsub)]
    a2, r2, b2, k2, v2, bh2, kh2, decay = [], [], [], [], [], [], [], []
    for sub, pr in chains:
        rows = slice(sub * ch, (sub + 1) * ch)
        sl = slice(pr * LANES, (pr + 1) * LANES)
        k, bb = k_ref[0, rows, sl], b_ref[0, rows, sl]
        cum = cums[sub][:, sl]
        tot = cum[ch - 1:ch, :]
        e_neg = jnp.exp(-cum)
        e_rest = jnp.exp(tot - cum)
        a2.append(stack2(kk_ref[0, rows, sl] * jnp.exp(cum - lw_ref[0, rows, sl])).astype(BF16))
        r2.append(stack2(r_ref[0, rows, sl] * jnp.exp(cum)))
        b2.append(dup2(bb * e_neg).astype(BF16))
        k2.append(dup2(k * e_neg).astype(BF16))
        v2.append(stack2(v_ref[0, rows, sl]).astype(BF16))
        bh2.append(stack2(bb * e_rest).astype(BF16))
        kh2.append(stack2(k * e_rest).astype(BF16))
        decay.append(jnp.exp(tot))
    nc = len(chains)
    gram = [_dot_nt(jnp.concatenate([a2[i], r2[i].astype(BF16)], axis=0), jnp.concatenate([b2[i], k2[i]], axis=0))
            for i in range(nc)]
    ab = [jnp.where(strict, g[0:n2, 0:n2], 0.0) for g in gram]
    ak = [jnp.where(strict, g[0:n2, n2:2 * n2], 0.0).astype(BF16) for g in gram]
    rb = [jnp.where(incl, g[n2:2 * n2, 0:n2], 0.0).astype(BF16) for g in gram]
    rk = [jnp.where(incl, g[n2:2 * n2, n2:2 * n2], 0.0).astype(BF16) for g in gram]
    tinv = [eye] * nc
    for lvl in levels:
        tb = [t.astype(BF16) for t in tinv]
        tm = [_dot(tb[i], jnp.where(lvl, ab[i], 0.0).astype(BF16)).astype(BF16) for i in range(nc)]
        tinv = [tinv[i] - _dot(tm[i], tb[i]) for i in range(nc)]
    tb = [t.astype(BF16) for t in tinv]
    akv = [_dot(ak[i], v2[i]).astype(BF16) for i in range(nc)]
    wu = [_dot(tb[i], jnp.concatenate([a2[i], akv[i]], axis=1)).astype(BF16) for i in range(nc)]
    rwu = [_dot(rb[i], wu[i]) for i in range(nc)]
    rw = [(r2[i] - rwu[i][:, 0:LANES]).astype(BF16) for i in range(nc)]
    y0 = [_dot(rk[i], v2[i]) - rwu[i][:, LANES:2 * LANES] for i in range(nc)]
    gm = [_dot_tn(bh2[i], wu[i][:, 0:LANES]).astype(BF16) for i in range(nc)]
    s0t = [_dot_tn(jnp.concatenate([v2[i], wu[i][:, LANES:2 * LANES]], axis=0),
                   jnp.concatenate([kh2[i], -bh2[i]], axis=0)) for i in range(nc)]
    states = [st_scr[pr] for pr in range(npair)]
    ys = [None] * nc
    for i, (sub, pr) in enumerate(chains):
        st = states[pr]
        stb = st.astype(BF16)
        y2 = _dot_nt(rw[i], stb) + y0[i]
        ys[i] = y2[0:ch, :] + y2[ch:n2, :]
        states[pr] = st * decay[i] - _dot_nt(stb, gm[i]) + s0t[i]
    for pr in range(npair):
        st_scr[pr] = states[pr]
    tstep = nsub * ch
    y_st = jnp.concatenate([ys[sub * npair + pr] for pr in range(npair) for sub in range(nsub)], axis=0)
    rkr_st = jnp.concatenate([r_ref[0, :, pr * LANES:(pr + 1) * LANES] * k_ref[0, :, pr * LANES:(pr + 1) * LANES]
                              * rk_ref[:, pr * LANES:(pr + 1) * LANES] for pr in range(npair)], axis=0)
    sums = _dot_exact_rhs01(jnp.concatenate([y_st, rkr_st], axis=0), hsum)
    yc = y_st - sums[0:npair * tstep, :] * (1.0 / HEAD_DIM)
    var = _dot_exact_rhs01(yc * yc, hsum) * (1.0 / HEAD_DIM)
    yn = yc * lax.rsqrt(var + B_LN_EPS)
    for pr in range(npair):
        sl = slice(pr * LANES, (pr + 1) * LANES)
        rows = slice(pr * tstep, (pr + 1) * tstep)
        bonus = sums[npair * tstep + pr * tstep:npair * tstep + (pr + 1) * tstep, :] * v_ref[0, :, sl]
        out = (yn[rows, :] * lnw_ref[:, sl] + lnb_ref[:, sl] + bonus) * g_ref[0, :, sl]
        o_ref[0, :, sl] = out.astype(o_ref.dtype)


def _rwkv_kernel(p_ref, pprev_ref, mu_ref, wa2_ref, g2_ref, w0_ref, a0_ref, kkw_ref, ka_ref, rk_ref, lnw_ref, lnb_ref,
                 o_ref, st_scr, r_s, k_s, v_s, lw_s, kk_s, b_s, g_s, *, ch, nsub):
    _rwkv_prep_kernel(p_ref, pprev_ref, mu_ref, wa2_ref, g2_ref, w0_ref, a0_ref, kkw_ref, ka_ref,
                      r_s, k_s, v_s, lw_s, kk_s, b_s, g_s)
    _rwkv_scan_kernel(r_s, k_s, v_s, lw_s, kk_s, b_s, g_s, rk_ref, lnw_ref, lnb_ref, o_ref, st_scr, ch=ch, nsub=nsub)


def rwkv_mix(pb, mu, w0, w2, a0, a2, g2, k_k, k_a, r_k, ln_w, ln_b, ch=RWKV_CHUNK, nsub=4):
    b, t, win = pb.shape
    w = B_WIDTH
    ch = min(ch, t)
    nsub = min(nsub, t // ch)
    tstep = ch * nsub
    wa2 = jnp.zeros((LANES, 2 * w), F32)
    wa2 = wa2.at[:B_DECAY_RANK, :w].set(w2).at[B_DECAY_RANK:, w:].set(a2)
    wa2_hi = wa2.astype(BF16)
    wa2 = jnp.stack([wa2_hi, (wa2 - wa2_hi.astype(F32)).astype(BF16)])
    row = lambda z: z.reshape(1, -1)
    vec = pl.BlockSpec((1, w), lambda i, j: (0, 0))
    return pl.pallas_call(
        functools.partial(_rwkv_kernel, ch=ch, nsub=nsub),
        grid=(b, t // tstep),
        in_specs=[pl.BlockSpec((1, tstep, win), lambda i, j: (i, j, 0)),
                  pl.BlockSpec((1, SUBLANES, win), lambda i, j: (i, jnp.maximum(j * (tstep // SUBLANES) - 1, 0), 0)),
                  pl.BlockSpec((1, win), lambda i, j: (0, 0)),
                  pl.BlockSpec((2, LANES, 2 * w), lambda i, j: (0, 0, 0)),
                  pl.BlockSpec((B_GATE_RANK, w), lambda i, j: (0, 0)),
                  vec, vec, vec, vec, vec, vec, vec],
        out_specs=pl.BlockSpec((1, tstep, w), lambda i, j: (i, j, 0)),
        out_shape=jax.ShapeDtypeStruct((b, t, w), BF16),
        scratch_shapes=[pltpu.VMEM((B_HEADS // 2, LANES, LANES), F32)] + [pltpu.VMEM((1, tstep, w), F32)] * 7,
        compiler_params=_cparams("parallel", "arbitrary"),
        name="rwkv_mix",
    )(pb, pb, row(mu), wa2, g2.astype(BF16), row(w0), row(a0), row(k_k), row(k_a), row(r_k), row(ln_w), row(ln_b))


def _fill_vt(load_tile, store_tile, t):
    sub = lax.broadcasted_iota(jnp.int32, (LANES, LANES), 0)

    def body(j, carry):
        start = pl.multiple_of(j * LANES, LANES)
        store_tile(start, jnp.where(sub < HEAD_DIM, load_tile(start).T, 1.0).astype(BF16))
        return carry

    lax.fori_loop(0, t // LANES, body, 0)


def _stack_heads(src_ref, dst_scr, n_heads, scale):
    lo, hi = _lane_half_masks((QCHUNK, LANES))
    for h in range(n_heads):
        tile = src_ref[0, :, (h // 2) * LANES:(h // 2 + 1) * LANES]
        if scale != 1.0:
            tile = tile * scale
        dst_scr[h * QCHUNK:(h + 1) * QCHUNK, :] = jnp.where((lo, hi)[h % 2], tile, 0.0).astype(dst_scr.dtype)


def _score_stage(k_slab, qst_scr, mask_bias, n_heads, s_scr, mx_scr, slot):
    for pair in range(n_heads // 2):
        d = _dot_nt(k_slab(pair), qst_scr[2 * pair * QCHUNK:(2 * pair + 2) * QCHUNK, :])
        for e in range(2):
            h = 2 * pair + e
            cols = slice(h * QCHUNK, (h + 1) * QCHUNK)
            s = d[:, e * QCHUNK:(e + 1) * QCHUNK] + mask_bias(h)
            s_scr[slot, :, cols] = s
            mx_scr[slot, :, cols] = jnp.max(s, axis=0, keepdims=True)


def _softmax_stage(vt, n_heads, s_scr, mx_scr, slot, m_scr, acc_scr):
    m_old = m_scr[...]
    m_new = jnp.maximum(m_old, mx_scr[slot])
    alpha = jnp.exp2(m_old - m_new)
    m_scr[...] = m_new
    for pair in range(n_heads // 2):
        cols = slice(2 * pair * QCHUNK, (2 * pair + 2) * QCHUNK)
        p = jnp.exp2(s_scr[slot, :, cols] - m_new[:, cols]).astype(BF16)
        acc_scr[:, cols] = alpha[:, cols] * acc_scr[:, cols] + _dot(vt(pair), p)


def _pipelined_groups(ngroup, s_matmul, process):
    last = ngroup - 1
    s_matmul(0, 0)

    def pair_body(i, carry):
        g0 = 2 * i
        s_matmul(g0 + 1, 1)
        process(g0, 0)
        s_matmul(jnp.minimum(g0 + 2, last), 0)
        process(g0 + 1, 1)
        return carry

    lax.fori_loop(0, ngroup // 2, pair_body, 0)

    @pl.when(ngroup % 2 == 1)
    def _():
        process(last, 0)


def _bias_rows(tabt_ref, h, d_first, n_tiles):
    tiles = [tabt_ref[jnp.clip(d_first - u, 0, BIAS_TILES - 1), h] for u in range(n_tiles)]
    return jnp.concatenate(tiles, axis=0)


def _write_heads(o_ref, acc_scr, n_heads):
    for p in range(n_heads // 2):
        parts = []
        for h in (2 * p, 2 * p + 1):
            a = acc_scr[:, h * QCHUNK:(h + 1) * QCHUNK]
            parts.append(a[0:HEAD_DIM, :] * (1.0 / a[HEAD_DIM:HEAD_DIM + 1, :]))
        o_ref[0, :, p * LANES:(p + 1) * LANES] = jnp.concatenate(parts, axis=0).T.astype(o_ref.dtype)


def _moba_kernel(q_ref, k_ref, v_ref, tabt_ref, o_ref,
                 k_scr, vt_scr, km_scr, qst_scr, pick_scr, s_scr, mx_scr, m_scr, acc_scr):
    c = pl.program_id(1)
    t = k_ref.shape[1]
    nblk = t // MOBA_BLOCK
    nbp = km_scr.shape[1]
    group = C_HEADS // C_KV_HEADS
    pairs_per_kv = group // 2

    @pl.when(c == 0)
    def _():
        k_scr[...] = k_ref[0].astype(BF16)
        km_scr[...] = jnp.zeros_like(km_scr)
        for kvh in range(C_KV_HEADS):
            kf = k_ref[0, :, kvh * LANES:(kvh + 1) * LANES]
            km_scr[kvh, 0:nblk, :] = jnp.mean(kf.reshape(nblk, MOBA_BLOCK, LANES), axis=1)

            def store_vt(start, tile, kvh=kvh):
                vt_scr[kvh, :, pl.ds(start, LANES)] = tile

            _fill_vt(lambda start, kvh=kvh: v_ref[0, pl.ds(start, LANES), kvh * LANES:(kvh + 1) * LANES], store_vt, t)

    own = (c * QCHUNK) // MOBA_BLOCK
    _stack_heads(q_ref, qst_scr, C_HEADS, HEAD_DIM ** -0.5 * LOG2E)

    blk = lax.broadcasted_iota(jnp.int32, (nbp, QCHUNK), 0)
    for kvh in range(C_KV_HEADS):
        first_tile = kvh * pairs_per_kv
        qsum = q_ref[0, :, first_tile * LANES:(first_tile + 1) * LANES]
        for p in range(first_tile + 1, first_tile + pairs_per_kv):
            qsum = qsum + q_ref[0, :, p * LANES:(p + 1) * LANES]
        gate = _dot_nt_x(km_scr[kvh], qsum)
        gate = jnp.where(blk < own, gate, -jnp.inf)
        rank = jnp.zeros((nbp, QCHUNK), jnp.int32)
        for mblk in range(nblk):
            gm = gate[mblk:mblk + 1, :]
            ahead = jnp.logical_or(gm > gate, jnp.logical_and(gm == gate, mblk < blk))
            rank = rank + jnp.where(ahead, 1, 0)
        chosen = jnp.logical_and(rank < MOBA_TOPK, blk < own)
        pick_scr[kvh] = jnp.where(chosen, 0.0, MASKED)

    m_scr[...] = jnp.full(m_scr.shape, M_INIT, F32)
    acc_scr[...] = jnp.zeros_like(acc_scr)
    grp = 2 * MOBA_BLOCK
    tiles_per_grp = grp // LANES
    ngroup = own // 2 + 1
    grow = lax.broadcasted_iota(jnp.int32, (grp, QCHUNK), 0)
    second = grow >= MOBA_BLOCK
    qpos = c * QCHUNK + lax.broadcasted_iota(jnp.int32, (grp, QCHUNK), 1)

    def score_stage(gi, slot):
        start = pl.multiple_of(gi * grp, grp)
        causal = jnp.where((gi * grp + grow) <= qpos, 0.0, MASKED)
        okb = []
        for kvh in range(C_KV_HEADS):
            first = jnp.where(2 * gi < own, pick_scr[kvh, pl.ds(2 * gi, 1), :], causal)
            later = jnp.where(2 * gi + 1 < own, pick_scr[kvh, pl.ds(2 * gi + 1, 1), :], causal)
            okb.append(jnp.where(second, later, first))
        d_first = c - tiles_per_grp * gi
        _score_stage(lambda pair: k_scr[pl.ds(start, grp), (pair // pairs_per_kv) * LANES:(pair // pairs_per_kv + 1) * LANES],
                     qst_scr, lambda h: _bias_rows(tabt_ref, h, d_first, tiles_per_grp) + okb[h // group],
                     C_HEADS, s_scr, mx_scr, slot)

    def softmax_stage(gi, slot):
        start = pl.multiple_of(gi * grp, grp)
        _softmax_stage(lambda pair: vt_scr[pair // pairs_per_kv, :, pl.ds(start, grp)],
                       C_HEADS, s_scr, mx_scr, slot, m_scr, acc_scr)

    _pipelined_groups(ngroup, score_stage, softmax_stage)
    _write_heads(o_ref, acc_scr, C_HEADS)


def moba_attention(qc, kvc, tabt):
    b, t, _ = qc.shape
    assert t % MOBA_BLOCK == 0
    nq = t // QCHUNK
    qw = C_HEADS * HEAD_DIM
    kvw = C_KV_HEADS * LANES
    nbp = -(-(t // MOBA_BLOCK) // SUBLANES) * SUBLANES
    return pl.pallas_call(
        _moba_kernel,
        grid=(b, nq),
        in_specs=[pl.BlockSpec((1, QCHUNK, qw), lambda i, c: (i, c, 0)),
                  pl.BlockSpec((1, t, kvw), lambda i, c: (i, 0, 0)),
                  pl.BlockSpec((1, t, kvw), lambda i, c: (i, 0, 1)),
                  pl.BlockSpec((BIAS_TILES, C_HEADS, LANES, LANES), lambda i, c: (0, 0, 0, 0))],
        out_specs=pl.BlockSpec((1, QCHUNK, qw), lambda i, c: (i, c, 0)),
        out_shape=jax.ShapeDtypeStruct((b, t, qw), BF16),
        scratch_shapes=[pltpu.VMEM((t, kvw), BF16), pltpu.VMEM((C_KV_HEADS, LANES, t), BF16),
                        pltpu.VMEM((C_KV_HEADS, nbp, LANES), F32),
                        pltpu.VMEM((C_HEADS * QCHUNK, LANES), BF16),
                        pltpu.VMEM((C_KV_HEADS, nbp, QCHUNK), F32),
                        pltpu.VMEM((2, 2 * MOBA_BLOCK, C_HEADS * QCHUNK), F32),
                        pltpu.VMEM((2, 1, C_HEADS * QCHUNK), F32),
                        pltpu.VMEM((1, C_HEADS * QCHUNK), F32),
                        pltpu.VMEM((LANES, C_HEADS * QCHUNK), F32)],
        compiler_params=_cparams("parallel", "arbitrary"),
        name="moba_attention",
    )(qc, kvc, kvc, tabt)


def _dsa_kernel(q_ref, kv_ref, qi_ref, ki_ref, gk_ref, tabt_ref, o_ref,
                k_scr, vt_scr, ki_scr, sc_scr, qst_scr, qist_scr, pre_scr, s_scr, mx_scr, m_scr, acc_scr, *, topk):
    c = pl.program_id(1)
    t = k_scr.shape[0]
    grp = KEY_GROUP * LANES
    ngroup = c // KEY_GROUP + 1
    sub = lax.broadcasted_iota(jnp.int32, (LANES, QCHUNK), 0)
    gsub = lax.broadcasted_iota(jnp.int32, (grp, QCHUNK), 0)
    qpos = c * QCHUNK + lax.broadcasted_iota(jnp.int32, (grp, QCHUNK), 1)

    @pl.when(c == 0)
    def _():
        k_scr[...] = kv_ref[0, :, 0:LANES].astype(BF16)
        ki_scr[...] = _rms(ki_ref[0], gk_ref[...]).astype(BF16)
        def store_vt(start, tile):
            vt_scr[:, pl.ds(start, LANES)] = tile

        _fill_vt(lambda start: kv_ref[0, pl.ds(start, LANES), LANES:2 * LANES], store_vt, t)

    _stack_heads(q_ref, qst_scr, D_HEADS, HEAD_DIM ** -0.5 * LOG2E)
    _stack_heads(qi_ref, qist_scr, IDX_HEADS, 1.0)
    w_t = (qi_ref[0, :, IDX_HEADS * IDX_DIM + LANES:IDX_HEADS * IDX_DIM + 2 * LANES]
           * ((IDX_HEADS * IDX_DIM) ** -0.5)).T

    def score_group(gi):
        start = pl.multiple_of(gi * grp, grp)
        rel = jnp.maximum(_dot_nt(ki_scr[pl.ds(start, grp), :], qist_scr[...]), 0.0)
        sc = jnp.zeros((grp, QCHUNK), F32)
        for h in range(IDX_HEADS):
            sc = sc + rel[:, h * QCHUNK:(h + 1) * QCHUNK] * w_t[h:h + 1, :]
        sc = jnp.where(sc == 0.0, 0.0, sc)
        sc_scr[pl.ds(start, grp), :] = jnp.where((gi * grp + gsub) <= qpos, sc, -jnp.inf)

    def score_pair(i, carry):
        score_group(2 * i)
        score_group(2 * i + 1)
        return carry

    lax.fori_loop(0, ngroup // 2, score_pair, 0)

    @pl.when(ngroup % 2 == 1)
    def _():
        score_group(ngroup - 1)

    def count(pred):
        def body(gi, acc):
            for u in range(KEY_GROUP):
                start = pl.multiple_of(gi * grp + u * LANES, LANES)
                acc = jnp.where(pred(sc_scr[pl.ds(start, LANES), :]), acc + 1, acc)
            return acc
        acc = lax.fori_loop(0, ngroup, body, jnp.zeros((LANES, QCHUNK), jnp.int32))
        return jnp.sum(acc, axis=0, keepdims=True)

    def key_to_float(key):
        return pltpu.bitcast(key ^ (lax.shift_right_arithmetic(key, 31) & 0x7FFFFFFF), F32)

    cnt0 = count(lambda st: st >= 0.0)
    code = jnp.where(cnt0 >= topk, 0, INT_MIN).astype(jnp.int32)

    def bit_body(i, code):
        cand = code + lax.shift_left(jnp.int32(1), 30 - i)
        cand_f = key_to_float(cand)
        cnt = count(lambda st: st >= cand_f)
        return jnp.where(cnt >= topk, cand, code)

    code = lax.fori_loop(0, 31, bit_body, code)
    thr = jnp.where(code == INT_MIN, -jnp.inf, key_to_float(jnp.maximum(code, INT_MIN + 1)))
    thr_eff = jnp.maximum(thr, LOWEST_F32)

    cnt_ge = count(lambda st: st >= thr_eff)
    tied = cnt_ge > topk

    @pl.when(jnp.max(jnp.where(tied, 1.0, 0.0)) > 0.5)
    def _():
        def tally_body(gi, ties_before):
            sums = []
            for u in range(KEY_GROUP):
                start = pl.multiple_of(gi * grp + u * LANES, LANES)
                is_tie = sc_scr[pl.ds(start, LANES), :] == thr
                sums.append(jnp.sum(jnp.where(is_tie, 1.0, 0.0), axis=0, keepdims=True))
            for u in range(KEY_GROUP):
                pre_scr[pl.ds(gi * KEY_GROUP + u, 1), :] = ties_before
                ties_before = ties_before + sums[u]
            return ties_before

        ties = lax.fori_loop(0, ngroup, tally_body, jnp.zeros((1, QCHUNK), F32))
        above = cnt_ge.astype(F32) - ties
        need = jnp.where(tied, topk - above, 3.0e38)
        tri = jnp.where(lax.broadcasted_iota(jnp.int32, (LANES, LANES), 1) <= sub, 1.0, 0.0).astype(BF16)

        def demote_body(gi, carry):
            tiles = []
            for u in range(KEY_GROUP):
                start = pl.multiple_of(gi * grp + u * LANES, LANES)
                tiles.append(sc_scr[pl.ds(start, LANES), :])
            eq01 = jnp.concatenate([jnp.where(st == thr, 1.0, 0.0).astype(BF16) for st in tiles], axis=1)
            within = _dot(tri, eq01)
            for u, st in enumerate(tiles):
                start = pl.multiple_of(gi * grp + u * LANES, LANES)
                rank = pre_scr[pl.ds(gi * KEY_GROUP + u, 1), :] + within[:, u * QCHUNK:(u + 1) * QCHUNK]
                drop = jnp.logical_and(st == thr, rank > need)
                sc_scr[pl.ds(start, LANES), :] = jnp.where(drop, -jnp.inf, st)
            return carry

        lax.fori_loop(0, ngroup, demote_body, 0)

    m_scr[...] = jnp.full(m_scr.shape, M_INIT, F32)
    acc_scr[...] = jnp.zeros_like(acc_scr)

    def score_stage(gi, slot):
        start = pl.multiple_of(gi * grp, grp)
        okb = jnp.where(sc_scr[pl.ds(start, grp), :] >= thr_eff, 0.0, MASKED)
        d_first = c - gi * KEY_GROUP
        _score_stage(lambda pair: k_scr[pl.ds(start, grp), :], qst_scr,
                     lambda h: _bias_rows(tabt_ref, h, d_first, KEY_GROUP) + okb, D_HEADS, s_scr, mx_scr, slot)

    def softmax_stage(gi, slot):
        start = pl.multiple_of(gi * grp, grp)
        _softmax_stage(lambda pair: vt_scr[:, pl.ds(start, grp)], D_HEADS, s_scr, mx_scr, slot, m_scr, acc_scr)

    _pipelined_groups(ngroup, score_stage, softmax_stage)
    _write_heads(o_ref, acc_scr, D_HEADS)


def dsa_attention(qd, kvd, idx, kidx_gain, tabt):
    b, t, _ = qd.shape
    topk = min(DSA_TOPK, t // 4)
    nq = t // QCHUNK
    gk = jnp.concatenate([kidx_gain, kidx_gain]).reshape(1, LANES)
    wq = D_HEADS * HEAD_DIM
    return pl.pallas_call(
        functools.partial(_dsa_kernel, topk=topk),
        grid=(b, nq),
        in_specs=[pl.BlockSpec((1, QCHUNK, wq), lambda i, c: (i, c, 0)),
                  pl.BlockSpec((1, t, 2 * LANES), lambda i, c: (i, 0, 0)),
                  pl.BlockSpec((1, QCHUNK, 4 * LANES), lambda i, c: (i, c, 0)),
                  pl.BlockSpec((1, t, LANES), lambda i, c: (i, 0, 2)),
                  pl.BlockSpec((1, LANES), lambda i, c: (0, 0)),
                  pl.BlockSpec((BIAS_TILES, D_HEADS, LANES, LANES), lambda i, c: (0, 0, 0, 0))],
        out_specs=pl.BlockSpec((1, QCHUNK, wq), lambda i, c: (i, c, 0)),
        out_shape=jax.ShapeDtypeStruct((b, t, wq), BF16),
        scratch_shapes=[pltpu.VMEM((t, LANES), BF16), pltpu.VMEM((LANES, t), BF16), pltpu.VMEM((t, LANES), BF16),
                        pltpu.VMEM((t, QCHUNK), F32),
                        pltpu.VMEM((D_HEADS * QCHUNK, LANES), BF16), pltpu.VMEM((IDX_HEADS * QCHUNK, LANES), BF16),
                        pltpu.VMEM((t // LANES, QCHUNK), F32),
                        pltpu.VMEM((2, KEY_GROUP * LANES, D_HEADS * QCHUNK), F32),
                        pltpu.VMEM((2, 1, D_HEADS * QCHUNK), F32),
                        pltpu.VMEM((1, D_HEADS * QCHUNK), F32),
                        pltpu.VMEM((LANES, D_HEADS * QCHUNK), F32)],
        compiler_params=_cparams("parallel", "arbitrary"),
        name="dsa_attention",
    )(qd, kvd, idx, idx, gk, tabt)


def _dup_heads(w, first, n_heads):
    cols = []
    for h in range(n_heads):
        head = w[:, first + h * HEAD_DIM:first + (h + 1) * HEAD_DIM]
        cols += [head, head]
    return cols


def _even_weight(w_in):
    w = w_in.astype(BF16)
    aq, akv = A_HEADS * HEAD_DIM, A_KV_HEADS * HEAD_DIM
    cols = [w[:, :aq]] + _dup_heads(w, aq, A_KV_HEADS) + _dup_heads(w, aq + akv, A_KV_HEADS) + [w[:, aq + 2 * akv:]]
    return jnp.concatenate(cols, axis=1)


def _odd_weight(w_in):
    w = w_in.astype(BF16)
    cq, ckv, dq = C_HEADS * HEAD_DIM, C_KV_HEADS * HEAD_DIM, D_HEADS * HEAD_DIM
    o_kc, o_vc, o_qd = cq, cq + ckv, cq + 2 * ckv
    o_kd, o_vd, o_qi = o_qd + dq, o_qd + dq + HEAD_DIM, o_qd + dq + 2 * HEAD_DIM
    o_ki = o_qi + IDX_HEADS * IDX_DIM
    o_wi = o_ki + IDX_DIM
    pad = jnp.zeros((w.shape[0], LANES - IDX_HEADS), BF16)
    cols = ([w[:, :cq]] + _dup_heads(w, o_kc, C_KV_HEADS) + _dup_heads(w, o_vc, C_KV_HEADS)
            + [w[:, o_qd:o_kd]] + _dup_heads(w, o_kd, 1) + _dup_heads(w, o_vd, 1)
            + [w[:, o_qi:o_ki]] + _dup_heads(w, o_ki, 1) + [w[:, o_wi:o_wi + IDX_HEADS], pad])
    return jnp.concatenate(cols, axis=1)


def kernel(x, mem, rel_bias, norm_gains, mix_w_out, mem_norm, x_wq, x_wkv, x_wo, ff_w1, ff_w2, ev_w_in, a_sinks, b_mu, b_w0, b_w2, b_a0, b_a2, b_g2, b_k_k, b_k_a, b_r_k, b_ln_w, b_ln_b, od_w_in, d_kidx_norm):
    bsz, t, d = x.shape
    n = bsz * t
    depth = norm_gains.shape[0]
    mlen = mem.shape[1]
    tabt = bias_table(rel_bias, transposed=True)
    x2 = x.reshape(n, d)
    for layer in range(depth):
        i = layer // 2
        g = norm_gains[layer]
        if layer % 2 == 0:
            b_in = 3 * B_WIDTH + B_DECAY_RANK + B_A_RANK + B_GATE_RANK
            qkv, pb = norm_proj(x2, g[0], _even_weight(ev_w_in[i]), (1024, b_in), (F32, F32))
            ya = swa_attention(qkv.reshape(bsz, t, 1024), a_sinks[i], rel_bias, tabt)
            yb = rwkv_mix(pb.reshape(bsz, t, b_in), b_mu[i], b_w0[i], b_w2[i], b_a0[i], b_a2[i], b_g2[i],
                          b_k_k[i], b_k_a[i], b_r_k[i].reshape(-1), b_ln_w[i], b_ln_b[i])
        else:
            qc, kvc, qd, kvd, idx = norm_proj(x2, g[0], _odd_weight(od_w_in[i]),
                                              (512, 512, 512, 256, 512), (F32,) * 5)
            ya = moba_attention(qc.reshape(bsz, t, 512), kvc.reshape(bsz, t, 512), tabt)
            yb = dsa_attention(qd.reshape(bsz, t, 512), kvd.reshape(bsz, t, 256), idx.reshape(bsz, t, 512),
                               d_kidx_norm[i], tabt)
        kmem, vmem = norm_proj(mem.reshape(bsz * mlen, d), mem_norm[layer], x_wkv[layer].astype(BF16),
                               (X_WIDTH, X_WIDTH), (BF16, BF16))
        x3 = post_mixer(ya, yb, x2.reshape(bsz, t, d), kmem.reshape(bsz, mlen, X_WIDTH),
                        vmem.reshape(bsz, mlen, X_WIDTH), mix_w_out[layer], x_wq[layer], x_wo[layer],
                        ff_w1[layer], ff_w2[layer], g)
        x2 = x3.reshape(n, d)
    return x2.reshape(bsz, t, d)
```

```python
import functools

import numpy as np
import jax
import jax.numpy as jnp
from jax import lax
from jax.experimental import pallas as pl
from jax.experimental.pallas import tpu as pltpu

F32 = jnp.float32
BF16 = jnp.bfloat16
HIGHEST = lax.Precision.HIGHEST

LANES = 128
SUBLANES = 8
HEAD_DIM = 64
EPS = 1e-6
A_HEADS, A_KV_HEADS, BAND = 8, 2, 128
B_HEADS, B_WIDTH = 8, 512
B_DECAY_RANK, B_A_RANK, B_GATE_RANK = 64, 64, 128
B_LN_EPS = 64e-5
RWKV_CHUNK = 64
C_HEADS, C_KV_HEADS, MOBA_BLOCK, MOBA_TOPK, QCHUNK = 8, 2, 256, 3, 128
D_HEADS, IDX_HEADS, IDX_DIM, DSA_TOPK = 8, 4, 64, 256
N_BUCKETS, BIAS_HEADS = 32, 8
BIAS_TILES = 9
X_HEADS, X_HEAD_DIM, X_WIDTH = 4, 128, 512

LOG2E = 1.4426950408889634
KEY_GROUP = 4
MASKED = -2e30
M_INIT = -1e30
INT_MIN = -(2 ** 31)
LOWEST_F32 = -3.4028234663852886e38
VMEM_LIMIT = 56 * 1024 * 1024


def _cparams(*sem):
    return pltpu.CompilerParams(dimension_semantics=tuple(sem), vmem_limit_bytes=VMEM_LIMIT)


def _dot(a, b):
    return jnp.dot(a, b, preferred_element_type=F32)


def _dot_nt(a, b):
    return lax.dot_general(a, b, (((1,), (1,)), ((), ())), preferred_element_type=F32)


def _dot_tn(a, b):
    return lax.dot_general(a, b, (((0,), (0,)), ((), ())), preferred_element_type=F32)


def _dot_x(a, b):
    return jnp.dot(a, b, preferred_element_type=F32, precision=HIGHEST)


def _dot_nt_x(a, b):
    return lax.dot_general(a, b, (((1,), (1,)), ((), ())), preferred_element_type=F32, precision=HIGHEST)


def _rms(xf, g):
    ms = jnp.mean(xf * xf, axis=-1, keepdims=True)
    return xf * lax.rsqrt(ms + EPS) * g


def _lane_half_masks(shape):
    lane = lax.broadcasted_iota(jnp.int32, shape, len(shape) - 1)
    lo = lane < HEAD_DIM
    return lo, jnp.logical_not(lo)


def _bucket_of_distance(n):
    exact = N_BUCKETS // 2
    if n < exact:
        return n
    j = 0
    while n ** 8 >= (exact ** 8) * (2 ** (3 * (j + 1))):
        j += 1
    return min(exact + j, N_BUCKETS - 1)


@functools.lru_cache(maxsize=None)
def _bucket_tiles(transposed):
    by_dist = np.array([_bucket_of_distance(n) for n in range(BIAS_TILES * LANES + LANES)], np.int32)
    q = np.arange(LANES)[None, :] if transposed else np.arange(LANES)[:, None]
    k = np.arange(LANES)[:, None] if transposed else np.arange(LANES)[None, :]
    tiles = [by_dist[np.maximum(d * LANES + q - k, 0)] for d in range(BIAS_TILES)]
    return np.stack(tiles).astype(np.int32)


def _bias_table_kernel(rb_ref, bkt_ref, out_ref, *, base2_shifted):
    bkt = bkt_ref[0]
    for h in range(BIAS_HEADS):
        acc = jnp.zeros((LANES, LANES), F32)
        for b in range(N_BUCKETS):
            acc = jnp.where(bkt == b, rb_ref[b, h], acc)
        if base2_shifted:
            acc = (acc - rb_ref[N_BUCKETS - 1, h]) * LOG2E
        out_ref[0, h] = acc


def bias_table(rel_bias, transposed=False):
    return pl.pallas_call(
        functools.partial(_bias_table_kernel, base2_shifted=transposed),
        grid=(BIAS_TILES,),
        in_specs=[pl.BlockSpec(memory_space=pltpu.SMEM),
                  pl.BlockSpec((1, LANES, LANES), lambda d: (d, 0, 0))],
        out_specs=pl.BlockSpec((1, BIAS_HEADS, LANES, LANES), lambda d: (d, 0, 0, 0)),
        out_shape=jax.ShapeDtypeStruct((BIAS_TILES, BIAS_HEADS, LANES, LANES), F32),
        compiler_params=_cparams("arbitrary"),
        name="bias_table",
    )(rel_bias, jnp.asarray(_bucket_tiles(transposed)))


def _norm_proj_kernel(x_ref, g_ref, w_ref, *out_refs, splits):
    h = _rms(x_ref[...], g_ref[...]).astype(BF16)
    res = _dot(h, w_ref[...])
    off = 0
    for o_ref, width in zip(out_refs, splits):
        o_ref[...] = res[:, off:off + width].astype(o_ref.dtype)
        off += width


def norm_proj(x2, gain, w, splits, out_dtypes, tm=512):
    n, d = x2.shape
    tm = min(tm, n)
    total = sum(splits)
    assert w.shape == (d, total) and n % tm == 0 and all(s % LANES == 0 for s in splits)
    return pl.pallas_call(
        functools.partial(_norm_proj_kernel, splits=tuple(splits)),
        grid=(n // tm,),
        in_specs=[pl.BlockSpec((tm, d), lambda i: (i, 0)),
                  pl.BlockSpec((1, d), lambda i: (0, 0)),
                  pl.BlockSpec((d, total), lambda i: (0, 0), pipeline_mode=pl.Buffered(1))],
        out_specs=[pl.BlockSpec((tm, s), lambda i: (i, 0)) for s in splits],
        out_shape=[jax.ShapeDtypeStruct((n, s), dt) for s, dt in zip(splits, out_dtypes)],
        compiler_params=_cparams("parallel"),
        name="norm_proj",
    )(x2, gain.reshape(1, d), w)


def _post_mixer_kernel(ya_ref, yb_ref, x_ref, k_ref, v_ref, wmix_ref, wq_ref, wo_ref, w1_ref, w2_ref, g_ref,
                       o_ref, *, tf):
    x = x_ref[0]
    ka = ya_ref.shape[2]
    z = _dot(ya_ref[0], wmix_ref[0, 0:ka, :]) + _dot(yb_ref[0], wmix_ref[0, ka:, :])
    g_ref = g_ref.at[0]
    x = x + _rms(z, g_ref[1:2, :])
    q = _dot(_rms(x, g_ref[2:3, :]).astype(BF16), wq_ref[0])
    k = k_ref[0]
    v = v_ref[0]
    outs = []
    for hd in range(X_HEADS):
        sl = slice(hd * X_HEAD_DIM, (hd + 1) * X_HEAD_DIM)
        s = _dot_nt(q[:, sl].astype(BF16), k[:, sl]) * (X_HEAD_DIM ** -0.5)
        m = jnp.max(s, axis=-1, keepdims=True)
        p = jnp.exp(s - m)
        p = p / jnp.sum(p, axis=-1, keepdims=True)
        outs.append(_dot(p.astype(BF16), v[:, sl]))
    z = _dot(jnp.concatenate(outs, axis=-1).astype(BF16), wo_ref[0])
    x = x + _rms(z, g_ref[3:4, :])
    h = _rms(x, g_ref[4:5, :]).astype(BF16)
    acc = jnp.zeros(x.shape, F32)
    for j in range(w1_ref.shape[2] // tf):
        a = jnp.maximum(_dot(h, w1_ref[0, :, j * tf:(j + 1) * tf]), 0.0)
        acc = acc + _dot((a * a).astype(BF16), w2_ref[0, j * tf:(j + 1) * tf, :])
    o_ref[0] = x + _rms(acc, g_ref[5:6, :])


def post_mixer(ya, yb, x3, kmem, vmem, w_mix, wq, wo, w1, w2, gains, layer, tq=512, tf=1024):
    b, t, d = x3.shape
    tq = min(tq, t)
    m = kmem.shape[1]
    ka, kb = ya.shape[2], yb.shape[2]
    once = pl.Buffered(1)
    const = lambda arr: pl.BlockSpec((1,) + arr.shape[1:], lambda i, j: (layer, 0, 0), pipeline_mode=once)
    weights = [w_mix.astype(BF16), wq.astype(BF16), wo.astype(BF16), w1.astype(BF16), w2.astype(BF16), gains]
    return pl.pallas_call(
        functools.partial(_post_mixer_kernel, tf=tf),
        grid=(b, t // tq),
        in_specs=[pl.BlockSpec((1, tq, ka), lambda i, j: (i, j, 0)),
                  pl.BlockSpec((1, tq, kb), lambda i, j: (i, j, 0)),
                  pl.BlockSpec((1, tq, d), lambda i, j: (i, j, 0)),
                  pl.BlockSpec((1, m, X_WIDTH), lambda i, j: (i, 0, 0)),
                  pl.BlockSpec((1, m, X_WIDTH), lambda i, j: (i, 0, 0))] + [const(w) for w in weights],
        out_specs=pl.BlockSpec((1, tq, d), lambda i, j: (i, j, 0)),
        out_shape=jax.ShapeDtypeStruct((b, t, d), F32),
        compiler_params=_cparams("parallel", "parallel"),
        name="post_mixer",
    )(ya, yb, x3, kmem, vmem, *weights)


def _swa_kernel(sink_ref, rb_ref, q_ref, kv_ref, tabt_ref, o_ref, qst_scr, vt_scr):
    n = pl.program_id(1)
    group = A_HEADS // A_KV_HEADS
    first_band = jnp.maximum(n - 1, 0)
    start = pl.multiple_of(first_band * BAND, BAND)
    slab = 2 * BAND
    _stack_heads(q_ref, qst_scr, A_HEADS, HEAD_DIM ** -0.5 * LOG2E)
    kpos = start + lax.broadcasted_iota(jnp.int32, (slab, BAND), 0)
    qpos = n * BAND + lax.broadcasted_iota(jnp.int32, (slab, BAND), 1)
    dist = qpos - kpos
    okb = jnp.where(jnp.logical_and(dist >= 0, dist < BAND), 0.0, MASKED)
    sub = lax.broadcasted_iota(jnp.int32, (LANES, slab), 0)
    koff = 0
    voff = A_KV_HEADS * LANES
    d_first = n - first_band
    scores, vts = [], []
    for hkv in range(A_KV_HEADS):
        k2 = kv_ref[0, pl.ds(start, slab), koff + hkv * LANES:koff + (hkv + 1) * LANES].astype(BF16)
        v2 = kv_ref[0, pl.ds(start, slab), voff + hkv * LANES:voff + (hkv + 1) * LANES]
        vt_scr[hkv] = jnp.where(sub < HEAD_DIM, v2.T, 1.0).astype(BF16)
        vts.append(vt_scr[hkv])
        s_all = _dot_nt(k2, qst_scr[hkv * group * BAND:(hkv + 1) * group * BAND, :])
        for g in range(group):
            h = hkv * group + g
            scores.append(s_all[:, g * BAND:(g + 1) * BAND] + (_bias_rows(tabt_ref, h, d_first, 2) + okb))
    sinks = [(sink_ref[h] - rb_ref[N_BUCKETS - 1, h]) * LOG2E for h in range(A_HEADS)]
    m = [jnp.maximum(jnp.max(scores[h], axis=0, keepdims=True), sinks[h]) for h in range(A_HEADS)]
    p = [jnp.exp2(scores[h] - m[h]).astype(BF16) for h in range(A_HEADS)]
    outs = []
    for hkv in range(A_KV_HEADS):
        pv = _dot(vts[hkv], jnp.concatenate(p[hkv * group:(hkv + 1) * group], axis=1))
        for g in range(group):
            h = hkv * group + g
            a = pv[:, g * BAND:(g + 1) * BAND]
            den = a[HEAD_DIM:HEAD_DIM + 1, :] + jnp.exp2(sinks[h] - m[h])
            outs.append(a[0:HEAD_DIM, :] * (1.0 / den))
    for pair in range(A_HEADS // 2):
        tile = jnp.concatenate(outs[2 * pair:2 * pair + 2], axis=0).T
        o_ref[0, :, pair * LANES:(pair + 1) * LANES] = tile.astype(o_ref.dtype)


def swa_attention(qkv, sinks, rel_bias, tabt):
    b, t, w = qkv.shape
    qw = A_HEADS * HEAD_DIM
    return pl.pallas_call(
        _swa_kernel,
        grid=(b, t // BAND),
        in_specs=[pl.BlockSpec(memory_space=pltpu.SMEM),
                  pl.BlockSpec(memory_space=pltpu.SMEM),
                  pl.BlockSpec((1, BAND, qw), lambda i, n: (i, n, 0)),
                  pl.BlockSpec((1, t, w - qw), lambda i, n: (i, 0, 1)),
                  pl.BlockSpec((2, A_HEADS, LANES, LANES), lambda i, n: (0, 0, 0, 0))],
        out_specs=pl.BlockSpec((1, BAND, qw), lambda i, n: (i, n, 0)),
        out_shape=jax.ShapeDtypeStruct((b, t, qw), BF16),
        scratch_shapes=[pltpu.VMEM((A_HEADS * BAND, LANES), BF16), pltpu.VMEM((A_KV_HEADS, LANES, 2 * BAND), BF16)],
        compiler_params=_cparams("parallel", "arbitrary"),
        name="swa_attention",
    )(sinks, rel_bias, qkv, qkv, tabt)


def _split3(x):
    p0 = x.astype(BF16)
    r1 = x - p0.astype(F32)
    p1 = r1.astype(BF16)
    p2 = (r1 - p1.astype(F32)).astype(BF16)
    return p0, p1, p2


def _dot_exact_rhs01(x, m01):
    p0, p1, p2 = _split3(x)
    return _dot(p0, m01) + _dot(p1, m01) + _dot(p2, m01)


def _dot_exact_lhs01(m01, x):
    p0, p1, p2 = _split3(x)
    return _dot(m01, p0) + _dot(m01, p1) + _dot(m01, p2)


def _head_sum_matrix(width):
    r = lax.broadcasted_iota(jnp.int32, (width, width), 0) // HEAD_DIM
    c = lax.broadcasted_iota(jnp.int32, (width, width), 1) // HEAD_DIM
    return jnp.where(r == c, 1.0, 0.0).astype(BF16)


def _sigmoid(z):
    return 1.0 / (1.0 + jnp.exp(-z))


def _softplus(z):
    return jnp.maximum(z, 0.0) + jnp.log(1.0 + jnp.exp(-jnp.abs(z)))


def _rwkv_prep_kernel(p_ref, pprev_ref, mu_ref, wa2_ref, g2_ref, w0_ref, a0_ref, kk_ref, ka_ref,
                      r_out, k_out, v_out, lw_out, kk_out, b_out, g_out):
    i = pl.program_id(1)
    p = p_ref[0]
    w = B_WIDTH
    last = pprev_ref[0, SUBLANES - 1:SUBLANES, :]
    last = jnp.where(i > 0, last, 0.0)
    row = lax.broadcasted_iota(jnp.int32, p.shape, 0)
    prev = jnp.where(row == 0, last, pltpu.roll(p, 1, 0))
    xs = p + (prev - p) * mu_ref[...]
    r = xs[:, 0:w]
    k = xs[:, w:2 * w]
    v = xs[:, 2 * w:3 * w]
    wa_lo = xs[:, 3 * w:3 * w + LANES]
    g_lo = xs[:, 3 * w + LANES:3 * w + 2 * LANES]
    lo, _ = _lane_half_masks(wa_lo.shape)
    wa_in = jnp.where(lo, jnp.tanh(wa_lo), wa_lo)
    in_hi = wa_in.astype(BF16)
    in_lo = (wa_in - in_hi.astype(F32)).astype(BF16)
    wa = _dot(in_hi, wa2_ref[0]) + (_dot(in_hi, wa2_ref[1]) + _dot(in_lo, wa2_ref[0]))
    wlog = -_softplus(-(w0_ref[...] + wa[:, 0:w])) - 0.5
    a = _sigmoid(a0_ref[...] + wa[:, w:2 * w])
    g = _dot(_sigmoid(g_lo).astype(BF16), g2_ref[...])
    kk = k * kk_ref[...]
    ss = _dot_exact_rhs01(kk * kk, _head_sum_matrix(w))
    kk = kk / jnp.maximum(jnp.sqrt(ss), 1e-12)
    kmod = k * (1.0 + (a - 1.0) * ka_ref[...])
    r_out[0] = r
    k_out[0] = kmod
    v_out[0] = v
    lw_out[0] = -jnp.exp(wlog)
    kk_out[0] = kk
    b_out[0] = kk * a
    g_out[0] = g


def _rwkv_scan_kernel(r_ref, k_ref, v_ref, lw_ref, kk_ref, b_ref, g_ref, rk_ref, lnw_ref, lnb_ref,
                      o_ref, st_scr, *, ch, nsub):
    step = pl.program_id(1)
    n2 = 2 * ch

    @pl.when(step == 0)
    def _():
        st_scr[...] = jnp.zeros_like(st_scr)

    ti = lax.broadcasted_iota(jnp.int32, (n2, n2), 0)
    si = lax.broadcasted_iota(jnp.int32, (n2, n2), 1)
    same = (ti // ch) == (si // ch)
    incl = jnp.logical_and(same, si <= ti)
    strict = jnp.logical_and(same, si < ti)
    eye = jnp.where(si == ti, 1.0, 0.0)
    levels = []
    s = 1
    while s < ch:
        same2s = (ti // (2 * s)) == (si // (2 * s))
        levels.append(jnp.logical_and(same2s, jnp.logical_and((ti // s) % 2 == 1, (si // s) % 2 == 0)))
        s *= 2
    tr = lax.broadcasted_iota(jnp.int32, (ch, ch), 0)
    tc = lax.broadcasted_iota(jnp.int32, (ch, ch), 1)
    tri01 = jnp.where(tc <= tr, 1.0, 0.0).astype(BF16)
    lo, hi = _lane_half_masks((ch, LANES))
    hsum = _head_sum_matrix(LANES)

    def stack2(z):
        return jnp.concatenate([jnp.where(lo, z, 0.0), jnp.where(hi, z, 0.0)], axis=0)

    def dup2(z):
        return jnp.concatenate([z, z], axis=0)

    npair = B_HEADS // 2
    chains = [(sub, pr) for sub in range(nsub) for pr in range(npair)]
    cums = [_dot_exact_lhs01(tri01, lw_ref[0, sub * ch:(sub + 1) * ch, :]) for sub in range(nsub)]
    a2, r2, b2, k2, v2, bh2, kh2, decay = [], [], [], [], [], [], [], []
    for sub, pr in chains:
        rows = slice(sub * ch, (sub + 1) * ch)
        sl = slice(pr * LANES, (pr + 1) * LANES)
        k, bb = k_ref[0, rows, sl], b_ref[0, rows, sl]
        cum = cums[sub][:, sl]
        tot = cum[ch - 1:ch, :]
        e_neg = jnp.exp(-cum)
        e_rest = jnp.exp(tot - cum)
        a2.append(stack2(kk_ref[0, rows, sl] * jnp.exp(cum - lw_ref[0, rows, sl])).astype(BF16))
        r2.append(stack2(r_ref[0, rows, sl] * jnp.exp(cum)))
        b2.append(dup2(bb * e_neg).astype(BF16))
        k2.append(dup2(k * e_neg).astype(BF16))
        v2.append(stack2(v_ref[0, rows, sl]).astype(BF16))
        bh2.append(stack2(bb * e_rest).astype(BF16))
        kh2.append(stack2(k * e_rest).astype(BF16))
        decay.append(jnp.exp(tot))
    nc = len(chains)
    gram = [_dot_nt(jnp.concatenate([a2[i], r2[i].astype(BF16)], axis=0), jnp.concatenate([b2[i], k2[i]], axis=0))
            for i in range(nc)]
    ab = [jnp.where(strict, g[0:n2, 0:n2], 0.0) for g in gram]
    ak = [jnp.where(strict, g[0:n2, n2:2 * n2], 0.0).astype(BF16) for g in gram]
    rb = [jnp.where(incl, g[n2:2 * n2, 0:n2], 0.0).astype(BF16) for g in gram]
    rk = [jnp.where(incl, g[n2:2 * n2, n2:2 * n2], 0.0).astype(BF16) for g in gram]
    tinv = [eye] * nc
    for lvl in levels:
        tb = [t.astype(BF16) for t in tinv]
        tm = [_dot(tb[i], jnp.where(lvl, ab[i], 0.0).astype(BF16)).astype(BF16) for i in range(nc)]
        tinv = [tinv[i] - _dot(tm[i], tb[i]) for i in range(nc)]
    tb = [t.astype(BF16) for t in tinv]
    akv = [_dot(ak[i], v2[i]).astype(BF16) for i in range(nc)]
    wu = [_dot(tb[i], jnp.concatenate([a2[i], akv[i]], axis=1)).astype(BF16) for i in range(nc)]
    rwu = [_dot(rb[i], wu[i]) for i in range(nc)]
    rw = [(r2[i] - rwu[i][:, 0:LANES]).astype(BF16) for i in range(nc)]
    y0 = [_dot(rk[i], v2[i]) - rwu[i][:, LANES:2 * LANES] for i in range(nc)]
    gm = [_dot_tn(bh2[i], wu[i][:, 0:LANES]).astype(BF16) for i in range(nc)]
    s0t = [_dot_tn(jnp.concatenate([v2[i], wu[i][:, LANES:2 * LANES]], axis=0),
                   jnp.concatenate([kh2[i], -bh2[i]], axis=0)) for i in range(nc)]
    states = [st_scr[pr] for pr in range(npair)]
    ys = [None] * nc
    for i, (sub, pr) in enumerate(chains):
        st = states[pr]
        stb = st.astype(BF16)
        y2 = _dot_nt(rw[i], stb) + y0[i]
        ys[i] = y2[0:ch, :] + y2[ch:n2, :]
        states[pr] = st * decay[i] - _dot_nt(stb, gm[i]) + s0t[i]
    for pr in range(npair):
        st_scr[pr] = states[pr]
    tstep = nsub * ch
    y_st = jnp.concatenate([ys[sub * npair + pr] for pr in range(npair) for sub in range(nsub)], axis=0)
    rkr_st = jnp.concatenate([r_ref[0, :, pr * LANES:(pr + 1) * LANES] * k_ref[0, :, pr * LANES:(pr + 1) * LANES]
                              * rk_ref[:, pr * LANES:(pr + 1) * LANES] for pr in range(npair)], axis=0)
    sums = _dot_exact_rhs01(jnp.concatenate([y_st, rkr_st], axis=0), hsum)
    yc = y_st - sums[0:npair * tstep, :] * (1.0 / HEAD_DIM)
    var = _dot_exact_rhs01(yc * yc, hsum) * (1.0 / HEAD_DIM)
    yn = yc * lax.rsqrt(var + B_LN_EPS)
    for pr in range(npair):
        sl = slice(pr * LANES, (pr + 1) * LANES)
        rows = slice(pr * tstep, (pr + 1) * tstep)
        bonus = sums[npair * tstep + pr * tstep:npair * tstep + (pr + 1) * tstep, :] * v_ref[0, :, sl]
        out = (yn[rows, :] * lnw_ref[:, sl] + lnb_ref[:, sl] + bonus) * g_ref[0, :, sl]
        o_ref[0, :, sl] = out.astype(o_ref.dtype)


def _rwkv_kernel(p_ref, pprev_ref, mu_ref, wa2_ref, g2_ref, w0_ref, a0_ref, kkw_ref, ka_ref, rk_ref, lnw_ref, lnb_ref,
                 o_ref, st_scr, r_s, k_s, v_s, lw_s, kk_s, b_s, g_s, *, ch, nsub):
    _rwkv_prep_kernel(p_ref, pprev_ref, mu_ref, wa2_ref, g2_ref, w0_ref, a0_ref, kkw_ref, ka_ref,
                      r_s, k_s, v_s, lw_s, kk_s, b_s, g_s)
    _rwkv_scan_kernel(r_s, k_s, v_s, lw_s, kk_s, b_s, g_s, rk_ref, lnw_ref, lnb_ref, o_ref, st_scr, ch=ch, nsub=nsub)


def rwkv_mix(pb, mu, w0, w2, a0, a2, g2, k_k, k_a, r_k, ln_w, ln_b, ch=RWKV_CHUNK, nsub=4):
    b, t, win = pb.shape
    w = B_WIDTH
    ch = min(ch, t)
    nsub = min(nsub, t // ch)
    tstep = ch * nsub
    wa2 = jnp.zeros((LANES, 2 * w), F32)
    wa2 = wa2.at[:B_DECAY_RANK, :w].set(w2).at[B_DECAY_RANK:, w:].set(a2)
    wa2_hi = wa2.astype(BF16)
    wa2 = jnp.stack([wa2_hi, (wa2 - wa2_hi.astype(F32)).astype(BF16)])
    row = lambda z: z.reshape(1, -1)
    vec = pl.BlockSpec((1, w), lambda i, j: (0, 0))
    return pl.pallas_call(
        functools.partial(_rwkv_kernel, ch=ch, nsub=nsub),
        grid=(b, t // tstep),
        in_specs=[pl.BlockSpec((1, tstep, win), lambda i, j: (i, j, 0)),
                  pl.BlockSpec((1, SUBLANES, win), lambda i, j: (i, jnp.maximum(j * (tstep // SUBLANES) - 1, 0), 0)),
                  pl.BlockSpec((1, win), lambda i, j: (0, 0)),
                  pl.BlockSpec((2, LANES, 2 * w), lambda i, j: (0, 0, 0)),
                  pl.BlockSpec((B_GATE_RANK, w), lambda i, j: (0, 0)),
                  vec, vec, vec, vec, vec, vec, vec],
        out_specs=pl.BlockSpec((1, tstep, w), lambda i, j: (i, j, 0)),
        out_shape=jax.ShapeDtypeStruct((b, t, w), BF16),
        scratch_shapes=[pltpu.VMEM((B_HEADS // 2, LANES, LANES), F32)] + [pltpu.VMEM((1, tstep, w), F32)] * 7,
        compiler_params=_cparams("parallel", "arbitrary"),
        name="rwkv_mix",
    )(pb, pb, row(mu), wa2, g2.astype(BF16), row(w0), row(a0), row(k_k), row(k_a), row(r_k), row(ln_w), row(ln_b))


def _fill_vt(load_tile, store_tile, t):
    sub = lax.broadcasted_iota(jnp.int32, (LANES, LANES), 0)

    def body(j, carry):
        start = pl.multiple_of(j * LANES, LANES)
        store_tile(start, jnp.where(sub < HEAD_DIM, load_tile(start).T, 1.0).astype(BF16))
        return carry

    lax.fori_loop(0, t // LANES, body, 0)


def _stack_heads(src_ref, dst_scr, n_heads, scale):
    lo, hi = _lane_half_masks((QCHUNK, LANES))
    for h in range(n_heads):
        tile = src_ref[0, :, (h // 2) * LANES:(h // 2 + 1) * LANES]
        if scale != 1.0:
            tile = tile * scale
        dst_scr[h * QCHUNK:(h + 1) * QCHUNK, :] = jnp.where((lo, hi)[h % 2], tile, 0.0).astype(dst_scr.dtype)


def _score_stage(k_slab, qst_scr, mask_bias, n_heads, s_scr, mx_scr, slot):
    for pair in range(n_heads // 2):
        d = _dot_nt(k_slab(pair), qst_scr[2 * pair * QCHUNK:(2 * pair + 2) * QCHUNK, :])
        for e in range(2):
            h = 2 * pair + e
            cols = slice(h * QCHUNK, (h + 1) * QCHUNK)
            s = d[:, e * QCHUNK:(e + 1) * QCHUNK] + mask_bias(h)
            s_scr[slot, :, cols] = s
            mx_scr[slot, :, cols] = jnp.max(s, axis=0, keepdims=True)


def _softmax_stage(vt, n_heads, s_scr, mx_scr, slot, m_scr, acc_scr):
    m_old = m_scr[...]
    m_new = jnp.maximum(m_old, mx_scr[slot])
    alpha = jnp.exp2(m_old - m_new)
    m_scr[...] = m_new
    for pair in range(n_heads // 2):
        cols = slice(2 * pair * QCHUNK, (2 * pair + 2) * QCHUNK)
        p = jnp.exp2(s_scr[slot, :, cols] - m_new[:, cols]).astype(BF16)
        acc_scr[:, cols] = alpha[:, cols] * acc_scr[:, cols] + _dot(vt(pair), p)


def _pipelined_groups(ngroup, s_matmul, process):
    last = ngroup - 1
    s_matmul(0, 0)

    def pair_body(i, carry):
        g0 = 2 * i
        s_matmul(g0 + 1, 1)
        process(g0, 0)
        s_matmul(jnp.minimum(g0 + 2, last), 0)
        process(g0 + 1, 1)
        return carry

    lax.fori_loop(0, ngroup // 2, pair_body, 0)

    @pl.when(ngroup % 2 == 1)
    def _():
        process(last, 0)


def _bias_rows(tabt_ref, h, d_first, n_tiles):
    tiles = [tabt_ref[jnp.clip(d_first - u, 0, BIAS_TILES - 1), h] for u in range(n_tiles)]
    return jnp.concatenate(tiles, axis=0)


def _write_heads(o_ref, acc_scr, n_heads):
    for p in range(n_heads // 2):
        parts = []
        for h in (2 * p, 2 * p + 1):
            a = acc_scr[:, h * QCHUNK:(h + 1) * QCHUNK]
            parts.append(a[0:HEAD_DIM, :] * (1.0 / a[HEAD_DIM:HEAD_DIM + 1, :]))
        o_ref[0, :, p * LANES:(p + 1) * LANES] = jnp.concatenate(parts, axis=0).T.astype(o_ref.dtype)


def _moba_kernel(q_ref, k_ref, v_ref, tabt_ref, o_ref,
                 k_scr, vt_scr, km_scr, qst_scr, pick_scr, s_scr, mx_scr, m_scr, acc_scr):
    c = pl.program_id(1)
    t = k_ref.shape[1]
    nblk = t // MOBA_BLOCK
    nbp = km_scr.shape[1]
    group = C_HEADS // C_KV_HEADS
    pairs_per_kv = group // 2

    @pl.when(c == 0)
    def _():
        k_scr[...] = k_ref[0].astype(BF16)
        km_scr[...] = jnp.zeros_like(km_scr)
        for kvh in range(C_KV_HEADS):
            kf = k_ref[0, :, kvh * LANES:(kvh + 1) * LANES]
            km_scr[kvh, 0:nblk, :] = jnp.mean(kf.reshape(nblk, MOBA_BLOCK, LANES), axis=1)

            def store_vt(start, tile, kvh=kvh):
                vt_scr[kvh, :, pl.ds(start, LANES)] = tile

            _fill_vt(lambda start, kvh=kvh: v_ref[0, pl.ds(start, LANES), kvh * LANES:(kvh + 1) * LANES], store_vt, t)

    own = (c * QCHUNK) // MOBA_BLOCK
    _stack_heads(q_ref, qst_scr, C_HEADS, HEAD_DIM ** -0.5 * LOG2E)

    blk = lax.broadcasted_iota(jnp.int32, (nbp, QCHUNK), 0)
    for kvh in range(C_KV_HEADS):
        first_tile = kvh * pairs_per_kv
        qsum = q_ref[0, :, first_tile * LANES:(first_tile + 1) * LANES]
        for p in range(first_tile + 1, first_tile + pairs_per_kv):
            qsum = qsum + q_ref[0, :, p * LANES:(p + 1) * LANES]
        gate = _dot_nt_x(km_scr[kvh], qsum)
        gate = jnp.where(blk < own, gate, -jnp.inf)
        rank = jnp.zeros((nbp, QCHUNK), jnp.int32)
        for mblk in range(nblk):
            gm = gate[mblk:mblk + 1, :]
            ahead = jnp.logical_or(gm > gate, jnp.logical_and(gm == gate, mblk < blk))
            rank = rank + jnp.where(ahead, 1, 0)
        chosen = jnp.logical_and(rank < MOBA_TOPK, blk < own)
        pick_scr[kvh] = jnp.where(chosen, 0.0, MASKED)

    m_scr[...] = jnp.full(m_scr.shape, M_INIT, F32)
    acc_scr[...] = jnp.zeros_like(acc_scr)
    grp = 2 * MOBA_BLOCK
    tiles_per_grp = grp // LANES
    ngroup = own // 2 + 1
    grow = lax.broadcasted_iota(jnp.int32, (grp, QCHUNK), 0)
    second = grow >= MOBA_BLOCK
    qpos = c * QCHUNK + lax.broadcasted_iota(jnp.int32, (grp, QCHUNK), 1)

    def score_stage(gi, slot):
        start = pl.multiple_of(gi * grp, grp)
        causal = jnp.where((gi * grp + grow) <= qpos, 0.0, MASKED)
        okb = []
        for kvh in range(C_KV_HEADS):
            first = jnp.where(2 * gi < own, pick_scr[kvh, pl.ds(2 * gi, 1), :], causal)
            later = jnp.where(2 * gi + 1 < own, pick_scr[kvh, pl.ds(2 * gi + 1, 1), :], causal)
            okb.append(jnp.where(second, later, first))
        d_first = c - tiles_per_grp * gi
        _score_stage(lambda pair: k_scr[pl.ds(start, grp), (pair // pairs_per_kv) * LANES:(pair // pairs_per_kv + 1) * LANES],
                     qst_scr, lambda h: _bias_rows(tabt_ref, h, d_first, tiles_per_grp) + okb[h // group],
                     C_HEADS, s_scr, mx_scr, slot)

    def softmax_stage(gi, slot):
        start = pl.multiple_of(gi * grp, grp)
        _softmax_stage(lambda pair: vt_scr[pair // pairs_per_kv, :, pl.ds(start, grp)],
                       C_HEADS, s_scr, mx_scr, slot, m_scr, acc_scr)

    _pipelined_groups(ngroup, score_stage, softmax_stage)
    _write_heads(o_ref, acc_scr, C_HEADS)


def moba_attention(qc, kvc, tabt):
    b, t, _ = qc.shape
    assert t % MOBA_BLOCK == 0
    nq = t // QCHUNK
    qw = C_HEADS * HEAD_DIM
    kvw = C_KV_HEADS * LANES
    nbp = -(-(t // MOBA_BLOCK) // SUBLANES) * SUBLANES
    return pl.pallas_call(
        _moba_kernel,
        grid=(b, nq),
        in_specs=[pl.BlockSpec((1, QCHUNK, qw), lambda i, c: (i, c, 0)),
                  pl.BlockSpec((1, t, kvw), lambda i, c: (i, 0, 0)),
                  pl.BlockSpec((1, t, kvw), lambda i, c: (i, 0, 1)),
                  pl.BlockSpec((BIAS_TILES, C_HEADS, LANES, LANES), lambda i, c: (0, 0, 0, 0))],
        out_specs=pl.BlockSpec((1, QCHUNK, qw), lambda i, c: (i, c, 0)),
        out_shape=jax.ShapeDtypeStruct((b, t, qw), BF16),
        scratch_shapes=[pltpu.VMEM((t, kvw), BF16), pltpu.VMEM((C_KV_HEADS, LANES, t), BF16),
                        pltpu.VMEM((C_KV_HEADS, nbp, LANES), F32),
                        pltpu.VMEM((C_HEADS * QCHUNK, LANES), BF16),
                        pltpu.VMEM((C_KV_HEADS, nbp, QCHUNK), F32),
                        pltpu.VMEM((2, 2 * MOBA_BLOCK, C_HEADS * QCHUNK), F32),
                        pltpu.VMEM((2, 1, C_HEADS * QCHUNK), F32),
                        pltpu.VMEM((1, C_HEADS * QCHUNK), F32),
                        pltpu.VMEM((LANES, C_HEADS * QCHUNK), F32)],
        compiler_params=_cparams("parallel", "arbitrary"),
        name="moba_attention",
    )(qc, kvc, kvc, tabt)


def _dsa_kernel(q_ref, kv_ref, qi_ref, ki_ref, gk_ref, tabt_ref, o_ref,
                k_scr, vt_scr, ki_scr, sc_scr, qst_scr, qist_scr, pre_scr, s_scr, mx_scr, m_scr, acc_scr, *, topk):
    c = pl.program_id(1)
    t = k_scr.shape[0]
    grp = KEY_GROUP * LANES
    ngroup = c // KEY_GROUP + 1
    sub = lax.broadcasted_iota(jnp.int32, (LANES, QCHUNK), 0)
    gsub = lax.broadcasted_iota(jnp.int32, (grp, QCHUNK), 0)
    qpos = c * QCHUNK + lax.broadcasted_iota(jnp.int32, (grp, QCHUNK), 1)

    @pl.when(c == 0)
    def _():
        k_scr[...] = kv_ref[0, :, 0:LANES].astype(BF16)
        ki_scr[...] = _rms(ki_ref[0], gk_ref[...]).astype(BF16)
        def store_vt(start, tile):
            vt_scr[:, pl.ds(start, LANES)] = tile

        _fill_vt(lambda start: kv_ref[0, pl.ds(start, LANES), LANES:2 * LANES], store_vt, t)

    _stack_heads(q_ref, qst_scr, D_HEADS, HEAD_DIM ** -0.5 * LOG2E)
    _stack_heads(qi_ref, qist_scr, IDX_HEADS, 1.0)
    w_t = (qi_ref[0, :, IDX_HEADS * IDX_DIM + LANES:IDX_HEADS * IDX_DIM + 2 * LANES]
           * ((IDX_HEADS * IDX_DIM) ** -0.5)).T

    def score_group(gi):
        start = pl.multiple_of(gi * grp, grp)
        rel = jnp.maximum(_dot_nt(ki_scr[pl.ds(start, grp), :], qist_scr[...]), 0.0)
        sc = jnp.zeros((grp, QCHUNK), F32)
        for h in range(IDX_HEADS):
            sc = sc + rel[:, h * QCHUNK:(h + 1) * QCHUNK] * w_t[h:h + 1, :]
        sc = jnp.where(sc == 0.0, 0.0, sc)
        sc_scr[pl.ds(start, grp), :] = jnp.where((gi * grp + gsub) <= qpos, sc, -jnp.inf)

    def score_pair(i, carry):
        score_group(2 * i)
        score_group(2 * i + 1)
        return carry

    lax.fori_loop(0, ngroup // 2, score_pair, 0)

    @pl.when(ngroup % 2 == 1)
    def _():
        score_group(ngroup - 1)

    def count(pred):
        def body(gi, acc):
            for u in range(KEY_GROUP):
                start = pl.multiple_of(gi * grp + u * LANES, LANES)
                acc = jnp.where(pred(sc_scr[pl.ds(start, LANES), :]), acc + 1, acc)
            return acc
        acc = lax.fori_loop(0, ngroup, body, jnp.zeros((LANES, QCHUNK), jnp.int32))
        return jnp.sum(acc, axis=0, keepdims=True)

    def key_to_float(key):
        return pltpu.bitcast(key ^ (lax.shift_right_arithmetic(key, 31) & 0x7FFFFFFF), F32)

    cnt0 = count(lambda st: st >= 0.0)
    code = jnp.where(cnt0 >= topk, 0, INT_MIN).astype(jnp.int32)

    def bit_body(i, code):
        cand = code + lax.shift_left(jnp.int32(1), 30 - i)
        cand_f = key_to_float(cand)
        cnt = count(lambda st: st >= cand_f)
        return jnp.where(cnt >= topk, cand, code)

    code = lax.fori_loop(0, 31, bit_body, code)
    thr = jnp.where(code == INT_MIN, -jnp.inf, key_to_float(jnp.maximum(code, INT_MIN + 1)))
    thr_eff = jnp.maximum(thr, LOWEST_F32)

    cnt_ge = count(lambda st: st >= thr_eff)
    tied = cnt_ge > topk

    @pl.when(jnp.max(jnp.where(tied, 1.0, 0.0)) > 0.5)
    def _():
        def tally_body(gi, ties_before):
            sums = []
            for u in range(KEY_GROUP):
                start = pl.multiple_of(gi * grp + u * LANES, LANES)
                is_tie = sc_scr[pl.ds(start, LANES), :] == thr
                sums.append(jnp.sum(jnp.where(is_tie, 1.0, 0.0), axis=0, keepdims=True))
            for u in range(KEY_GROUP):
                pre_scr[pl.ds(gi * KEY_GROUP + u, 1), :] = ties_before
                ties_before = ties_before + sums[u]
            return ties_before

        ties = lax.fori_loop(0, ngroup, tally_body, jnp.zeros((1, QCHUNK), F32))
        above = cnt_ge.astype(F32) - ties
        need = jnp.where(tied, topk - above, 3.0e38)
        tri = jnp.where(lax.broadcasted_iota(jnp.int32, (LANES, LANES), 1) <= sub, 1.0, 0.0).astype(BF16)

        def demote_body(gi, carry):
            tiles = []
            for u in range(KEY_GROUP):
                start = pl.multiple_of(gi * grp + u * LANES, LANES)
                tiles.append(sc_scr[pl.ds(start, LANES), :])
            eq01 = jnp.concatenate([jnp.where(st == thr, 1.0, 0.0).astype(BF16) for st in tiles], axis=1)
            within = _dot(tri, eq01)
            for u, st in enumerate(tiles):
                start = pl.multiple_of(gi * grp + u * LANES, LANES)
                rank = pre_scr[pl.ds(gi * KEY_GROUP + u, 1), :] + within[:, u * QCHUNK:(u + 1) * QCHUNK]
                drop = jnp.logical_and(st == thr, rank > need)
                sc_scr[pl.ds(start, LANES), :] = jnp.where(drop, -jnp.inf, st)
            return carry

        lax.fori_loop(0, ngroup, demote_body, 0)

    m_scr[...] = jnp.full(m_scr.shape, M_INIT, F32)
    acc_scr[...] = jnp.zeros_like(acc_scr)

    def score_stage(gi, slot):
        start = pl.multiple_of(gi * grp, grp)
        okb = jnp.where(sc_scr[pl.ds(start, grp), :] >= thr_eff, 0.0, MASKED)
        d_first = c - gi * KEY_GROUP
        _score_stage(lambda pair: k_scr[pl.ds(start, grp), :], qst_scr,
                     lambda h: _bias_rows(tabt_ref, h, d_first, KEY_GROUP) + okb, D_HEADS, s_scr, mx_scr, slot)

    def softmax_stage(gi, slot):
        start = pl.multiple_of(gi * grp, grp)
        _softmax_stage(lambda pair: vt_scr[:, pl.ds(start, grp)], D_HEADS, s_scr, mx_scr, slot, m_scr, acc_scr)

    _pipelined_groups(ngroup, score_stage, softmax_stage)
    _write_heads(o_ref, acc_scr, D_HEADS)


def dsa_attention(qd, kvd, idx, kidx_gain, tabt):
    b, t, _ = qd.shape
    topk = min(DSA_TOPK, t // 4)
    nq = t // QCHUNK
    gk = jnp.concatenate([kidx_gain, kidx_gain]).reshape(1, LANES)
    wq = D_HEADS * HEAD_DIM
    return pl.pallas_call(
        functools.partial(_dsa_kernel, topk=topk),
        grid=(b, nq),
        in_specs=[pl.BlockSpec((1, QCHUNK, wq), lambda i, c: (i, c, 0)),
                  pl.BlockSpec((1, t, 2 * LANES), lambda i, c: (i, 0, 0)),
                  pl.BlockSpec((1, QCHUNK, 4 * LANES), lambda i, c: (i, c, 0)),
                  pl.BlockSpec((1, t, LANES), lambda i, c: (i, 0, 2)),
                  pl.BlockSpec((1, LANES), lambda i, c: (0, 0)),
                  pl.BlockSpec((BIAS_TILES, D_HEADS, LANES, LANES), lambda i, c: (0, 0, 0, 0))],
        out_specs=pl.BlockSpec((1, QCHUNK, wq), lambda i, c: (i, c, 0)),
        out_shape=jax.ShapeDtypeStruct((b, t, wq), BF16),
        scratch_shapes=[pltpu.VMEM((t, LANES), BF16), pltpu.VMEM((LANES, t), BF16), pltpu.VMEM((t, LANES), BF16),
                        pltpu.VMEM((t, QCHUNK), F32),
                        pltpu.VMEM((D_HEADS * QCHUNK, LANES), BF16), pltpu.VMEM((IDX_HEADS * QCHUNK, LANES), BF16),
                        pltpu.VMEM((t // LANES, QCHUNK), F32),
                        pltpu.VMEM((2, KEY_GROUP * LANES, D_HEADS * QCHUNK), F32),
                        pltpu.VMEM((2, 1, D_HEADS * QCHUNK), F32),
                        pltpu.VMEM((1, D_HEADS * QCHUNK), F32),
                        pltpu.VMEM((LANES, D_HEADS * QCHUNK), F32)],
        compiler_params=_cparams("parallel", "arbitrary"),
        name="dsa_attention",
    )(qd, kvd, idx, idx, gk, tabt)


def _dup_heads(w, first, n_heads):
    cols = []
    for h in range(n_heads):
        head = w[:, first + h * HEAD_DIM:first + (h + 1) * HEAD_DIM]
        cols += [head, head]
    return cols


def _even_weight(w_in):
    w = w_in.astype(BF16)
    aq, akv = A_HEADS * HEAD_DIM, A_KV_HEADS * HEAD_DIM
    cols = [w[:, :aq]] + _dup_heads(w, aq, A_KV_HEADS) + _dup_heads(w, aq + akv, A_KV_HEADS) + [w[:, aq + 2 * akv:]]
    return jnp.concatenate(cols, axis=1)


def _odd_weight(w_in):
    w = w_in.astype(BF16)
    cq, ckv, dq = C_HEADS * HEAD_DIM, C_KV_HEADS * HEAD_DIM, D_HEADS * HEAD_DIM
    o_kc, o_vc, o_qd = cq, cq + ckv, cq + 2 * ckv
    o_kd, o_vd, o_qi = o_qd + dq, o_qd + dq + HEAD_DIM, o_qd + dq + 2 * HEAD_DIM
    o_ki = o_qi + IDX_HEADS * IDX_DIM
    o_wi = o_ki + IDX_DIM
    pad = jnp.zeros((w.shape[0], LANES - IDX_HEADS), BF16)
    cols = ([w[:, :cq]] + _dup_heads(w, o_kc, C_KV_HEADS) + _dup_heads(w, o_vc, C_KV_HEADS)
            + [w[:, o_qd:o_kd]] + _dup_heads(w, o_kd, 1) + _dup_heads(w, o_vd, 1)
            + [w[:, o_qi:o_ki]] + _dup_heads(w, o_ki, 1) + [w[:, o_wi:o_wi + IDX_HEADS], pad])
    return jnp.concatenate(cols, axis=1)


def kernel(x, mem, rel_bias, norm_gains, mix_w_out, mem_norm, x_wq, x_wkv, x_wo, ff_w1, ff_w2, ev_w_in, a_sinks, b_mu, b_w0, b_w2, b_a0, b_a2, b_g2, b_k_k, b_k_a, b_r_k, b_ln_w, b_ln_b, od_w_in, d_kidx_norm):
    bsz, t, d = x.shape
    n = bsz * t
    depth = norm_gains.shape[0]
    mlen = mem.shape[1]
    tabt = bias_table(rel_bias, transposed=True)
    x2 = x.reshape(n, d)
    for layer in range(depth):
        i = layer // 2
        g = norm_gains[layer]
        if layer % 2 == 0:
            b_in = 3 * B_WIDTH + B_DECAY_RANK + B_A_RANK + B_GATE_RANK
            qkv, pb = norm_proj(x2, g[0], _even_weight(ev_w_in[i]), (1024, b_in), (F32, F32))
            ya = swa_attention(qkv.reshape(bsz, t, 1024), a_sinks[i], rel_bias, tabt)
            yb = rwkv_mix(pb.reshape(bsz, t, b_in), b_mu[i], b_w0[i], b_w2[i], b_a0[i], b_a2[i], b_g2[i],
                          b_k_k[i], b_k_a[i], b_r_k[i].reshape(-1), b_ln_w[i], b_ln_b[i])
        else:
            qc, kvc, qd, kvd, idx = norm_proj(x2, g[0], _odd_weight(od_w_in[i]),
                                              (512, 512, 512, 256, 512), (F32,) * 5)
            ya = moba_attention(qc.reshape(bsz, t, 512), kvc.reshape(bsz, t, 512), tabt)
            yb = dsa_attention(qd.reshape(bsz, t, 512), kvd.reshape(bsz, t, 256), idx.reshape(bsz, t, 512),
                               d_kidx_norm[i], tabt)
        kmem, vmem = norm_proj(mem.reshape(bsz * mlen, d), mem_norm[layer], x_wkv[layer].astype(BF16),
                               (X_WIDTH, X_WIDTH), (BF16, BF16))
        x3 = post_mixer(ya, yb, x2.reshape(bsz, t, d), kmem.reshape(bsz, mlen, X_WIDTH),
                        vmem.reshape(bsz, mlen, X_WIDTH), mix_w_out, x_wq, x_wo, ff_w1, ff_w2, norm_gains, layer)
        x2 = x3.reshape(n, d)
    return x2.reshape(bsz, t, d)
```

```python
import functools

import numpy as np
import jax
import jax.numpy as jnp
from jax import lax
from jax.experimental import pallas as pl
from jax.experimental.pallas import tpu as pltpu

F32 = jnp.float32
BF16 = jnp.bfloat16
HIGHEST = lax.Precision.HIGHEST

LANES = 128
SUBLANES = 8
HEAD_DIM = 64
EPS = 1e-6
A_HEADS, A_KV_HEADS, BAND = 8, 2, 128
B_HEADS, B_WIDTH = 8, 512
B_DECAY_RANK, B_A_RANK, B_GATE_RANK = 64, 64, 128
B_LN_EPS = 64e-5
RWKV_CHUNK = 64
C_HEADS, C_KV_HEADS, MOBA_BLOCK, MOBA_TOPK, QCHUNK = 8, 2, 256, 3, 128
D_HEADS, IDX_HEADS, IDX_DIM, DSA_TOPK = 8, 4, 64, 256
N_BUCKETS, BIAS_HEADS = 32, 8
BIAS_TILES = 9
X_HEADS, X_HEAD_DIM, X_WIDTH = 4, 128, 512

LOG2E = 1.4426950408889634
KEY_GROUP = 4
MASKED = -2e30
M_INIT = -1e30
INT_MIN = -(2 ** 31)
LOWEST_F32 = -3.4028234663852886e38
VMEM_LIMIT = 56 * 1024 * 1024


def _cparams(*sem):
    return pltpu.CompilerParams(dimension_semantics=tuple(sem), vmem_limit_bytes=VMEM_LIMIT)


def _dot(a, b):
    return jnp.dot(a, b, preferred_element_type=F32)


def _dot_nt(a, b):
    return lax.dot_general(a, b, (((1,), (1,)), ((), ())), preferred_element_type=F32)


def _dot_tn(a, b):
    return lax.dot_general(a, b, (((0,), (0,)), ((), ())), preferred_element_type=F32)


def _dot_x(a, b):
    return jnp.dot(a, b, preferred_element_type=F32, precision=HIGHEST)


def _dot_nt_x(a, b):
    return lax.dot_general(a, b, (((1,), (1,)), ((), ())), preferred_element_type=F32, precision=HIGHEST)


def _rms(xf, g):
    ms = jnp.mean(xf * xf, axis=-1, keepdims=True)
    return xf * lax.rsqrt(ms + EPS) * g


def _lane_half_masks(shape):
    lane = lax.broadcasted_iota(jnp.int32, shape, len(shape) - 1)
    lo = lane < HEAD_DIM
    return lo, jnp.logical_not(lo)


def _bucket_of_distance(n):
    exact = N_BUCKETS // 2
    if n < exact:
        return n
    j = 0
    while n ** 8 >= (exact ** 8) * (2 ** (3 * (j + 1))):
        j += 1
    return min(exact + j, N_BUCKETS - 1)


@functools.lru_cache(maxsize=None)
def _bucket_tiles(transposed):
    by_dist = np.array([_bucket_of_distance(n) for n in range(BIAS_TILES * LANES + LANES)], np.int32)
    q = np.arange(LANES)[None, :] if transposed else np.arange(LANES)[:, None]
    k = np.arange(LANES)[:, None] if transposed else np.arange(LANES)[None, :]
    tiles = [by_dist[np.maximum(d * LANES + q - k, 0)] for d in range(BIAS_TILES)]
    return np.stack(tiles).astype(np.int32)


def _bias_table_kernel(rb_ref, bkt_ref, out_ref, *, base2_shifted):
    bkt = bkt_ref[0]
    for h in range(BIAS_HEADS):
        acc = jnp.zeros((LANES, LANES), F32)
        for b in range(N_BUCKETS):
            acc = jnp.where(bkt == b, rb_ref[b, h], acc)
        if base2_shifted:
            acc = (acc - rb_ref[N_BUCKETS - 1, h]) * LOG2E
        out_ref[0, h] = acc


def bias_table(rel_bias, transposed=False):
    return pl.pallas_call(
        functools.partial(_bias_table_kernel, base2_shifted=transposed),
        grid=(BIAS_TILES,),
        in_specs=[pl.BlockSpec(memory_space=pltpu.SMEM),
                  pl.BlockSpec((1, LANES, LANES), lambda d: (d, 0, 0))],
        out_specs=pl.BlockSpec((1, BIAS_HEADS, LANES, LANES), lambda d: (d, 0, 0, 0)),
        out_shape=jax.ShapeDtypeStruct((BIAS_TILES, BIAS_HEADS, LANES, LANES), F32),
        compiler_params=_cparams("arbitrary"),
        name="bias_table",
    )(rel_bias, jnp.asarray(_bucket_tiles(transposed)))


def _norm_proj_kernel(x_ref, g_ref, w_ref, *out_refs, splits):
    h = _rms(x_ref[...], g_ref[...]).astype(BF16)
    res = _dot(h, w_ref[...])
    off = 0
    for o_ref, width in zip(out_refs, splits):
        o_ref[...] = res[:, off:off + width].astype(o_ref.dtype)
        off += width


def norm_proj(x2, gain, w, splits, out_dtypes, tm=512):
    n, d = x2.shape
    tm = min(tm, n)
    total = sum(splits)
    assert w.shape == (d, total) and n % tm == 0 and all(s % LANES == 0 for s in splits)
    return pl.pallas_call(
        functools.partial(_norm_proj_kernel, splits=tuple(splits)),
        grid=(n // tm,),
        in_specs=[pl.BlockSpec((tm, d), lambda i: (i, 0)),
                  pl.BlockSpec((1, d), lambda i: (0, 0)),
                  pl.BlockSpec((d, total), lambda i: (0, 0), pipeline_mode=pl.Buffered(1))],
        out_specs=[pl.BlockSpec((tm, s), lambda i: (i, 0)) for s in splits],
        out_shape=[jax.ShapeDtypeStruct((n, s), dt) for s, dt in zip(splits, out_dtypes)],
        compiler_params=_cparams("parallel"),
        name="norm_proj",
    )(x2, gain.reshape(1, d), w)


def _post_mixer_kernel(ya_ref, yb_ref, x_ref, k_ref, v_ref, wa_ref, wb_ref, wq_ref, wo_ref, w1_ref, w2_ref, g_ref,
                       o_ref, *, tf):
    x = x_ref[0]
    z = _dot(ya_ref[0], wa_ref[...]) + _dot(yb_ref[0], wb_ref[...])
    x = x + _rms(z, g_ref[1:2, :])
    q = _dot(_rms(x, g_ref[2:3, :]).astype(BF16), wq_ref[...])
    k = k_ref[0]
    v = v_ref[0]
    outs = []
    for hd in range(X_HEADS):
        sl = slice(hd * X_HEAD_DIM, (hd + 1) * X_HEAD_DIM)
        s = _dot_nt(q[:, sl].astype(BF16), k[:, sl]) * (X_HEAD_DIM ** -0.5)
        m = jnp.max(s, axis=-1, keepdims=True)
        p = jnp.exp(s - m)
        p = p / jnp.sum(p, axis=-1, keepdims=True)
        outs.append(_dot(p.astype(BF16), v[:, sl]))
    z = _dot(jnp.concatenate(outs, axis=-1).astype(BF16), wo_ref[...])
    x = x + _rms(z, g_ref[3:4, :])
    h = _rms(x, g_ref[4:5, :]).astype(BF16)
    acc = jnp.zeros(x.shape, F32)
    for j in range(w1_ref.shape[1] // tf):
        a = jnp.maximum(_dot(h, w1_ref[:, j * tf:(j + 1) * tf]), 0.0)
        acc = acc + _dot((a * a).astype(BF16), w2_ref[j * tf:(j + 1) * tf, :])
    o_ref[0] = x + _rms(acc, g_ref[5:6, :])


def post_mixer(ya, yb, x3, kmem, vmem, w_out, wq, wo, w1, w2, gains, tq=512, tf=1024):
    b, t, d = x3.shape
    tq = min(tq, t)
    m = kmem.shape[1]
    ka, kb = ya.shape[2], yb.shape[2]
    f = w1.shape[1]
    once = pl.Buffered(1)
    const = lambda shape: pl.BlockSpec(shape, lambda i, j: (0, 0), pipeline_mode=once)
    return pl.pallas_call(
        functools.partial(_post_mixer_kernel, tf=tf),
        grid=(b, t // tq),
        in_specs=[pl.BlockSpec((1, tq, ka), lambda i, j: (i, j, 0)),
                  pl.BlockSpec((1, tq, kb), lambda i, j: (i, j, 0)),
                  pl.BlockSpec((1, tq, d), lambda i, j: (i, j, 0)),
                  pl.BlockSpec((1, m, X_WIDTH), lambda i, j: (i, 0, 0)),
                  pl.BlockSpec((1, m, X_WIDTH), lambda i, j: (i, 0, 0)),
                  const((ka, d)), const((kb, d)), const((d, X_WIDTH)), const((X_WIDTH, d)),
                  const((d, f)), const((f, d)), const(gains.shape)],
        out_specs=pl.BlockSpec((1, tq, d), lambda i, j: (i, j, 0)),
        out_shape=jax.ShapeDtypeStruct((b, t, d), F32),
        compiler_params=_cparams("parallel", "parallel"),
        name="post_mixer",
    )(ya, yb, x3, kmem, vmem, w_out[:ka].astype(BF16), w_out[ka:].astype(BF16), wq.astype(BF16), wo.astype(BF16),
      w1.astype(BF16), w2.astype(BF16), gains)


def _swa_kernel(sink_ref, rb_ref, q_ref, kv_ref, tabt_ref, o_ref, qst_scr, vt_scr):
    n = pl.program_id(1)
    group = A_HEADS // A_KV_HEADS
    first_band = jnp.maximum(n - 1, 0)
    start = pl.multiple_of(first_band * BAND, BAND)
    slab = 2 * BAND
    _stack_heads(q_ref, qst_scr, A_HEADS, HEAD_DIM ** -0.5 * LOG2E)
    kpos = start + lax.broadcasted_iota(jnp.int32, (slab, BAND), 0)
    qpos = n * BAND + lax.broadcasted_iota(jnp.int32, (slab, BAND), 1)
    dist = qpos - kpos
    okb = jnp.where(jnp.logical_and(dist >= 0, dist < BAND), 0.0, MASKED)
    sub = lax.broadcasted_iota(jnp.int32, (LANES, slab), 0)
    koff = 0
    voff = A_KV_HEADS * LANES
    d_first = n - first_band
    scores, vts = [], []
    for hkv in range(A_KV_HEADS):
        k2 = kv_ref[0, pl.ds(start, slab), koff + hkv * LANES:koff + (hkv + 1) * LANES].astype(BF16)
        v2 = kv_ref[0, pl.ds(start, slab), voff + hkv * LANES:voff + (hkv + 1) * LANES]
        vt_scr[hkv] = jnp.where(sub < HEAD_DIM, v2.T, 1.0).astype(BF16)
        vts.append(vt_scr[hkv])
        s_all = _dot_nt(k2, qst_scr[hkv * group * BAND:(hkv + 1) * group * BAND, :])
        for g in range(group):
            h = hkv * group + g
            scores.append(s_all[:, g * BAND:(g + 1) * BAND] + (_bias_rows(tabt_ref, h, d_first, 2) + okb))
    sinks = [(sink_ref[h] - rb_ref[N_BUCKETS - 1, h]) * LOG2E for h in range(A_HEADS)]
    m = [jnp.maximum(jnp.max(scores[h], axis=0, keepdims=True), sinks[h]) for h in range(A_HEADS)]
    p = [jnp.exp2(scores[h] - m[h]).astype(BF16) for h in range(A_HEADS)]
    outs = []
    for hkv in range(A_KV_HEADS):
        pv = _dot(vts[hkv], jnp.concatenate(p[hkv * group:(hkv + 1) * group], axis=1))
        for g in range(group):
            h = hkv * group + g
            a = pv[:, g * BAND:(g + 1) * BAND]
            den = a[HEAD_DIM:HEAD_DIM + 1, :] + jnp.exp2(sinks[h] - m[h])
            outs.append(a[0:HEAD_DIM, :] * (1.0 / den))
    for pair in range(A_HEADS // 2):
        tile = jnp.concatenate(outs[2 * pair:2 * pair + 2], axis=0).T
        o_ref[0, :, pair * LANES:(pair + 1) * LANES] = tile.astype(o_ref.dtype)


def swa_attention(qkv, sinks, rel_bias, tabt):
    b, t, w = qkv.shape
    qw = A_HEADS * HEAD_DIM
    return pl.pallas_call(
        _swa_kernel,
        grid=(b, t // BAND),
        in_specs=[pl.BlockSpec(memory_space=pltpu.SMEM),
                  pl.BlockSpec(memory_space=pltpu.SMEM),
                  pl.BlockSpec((1, BAND, qw), lambda i, n: (i, n, 0)),
                  pl.BlockSpec((1, t, w - qw), lambda i, n: (i, 0, 1)),
                  pl.BlockSpec((2, A_HEADS, LANES, LANES), lambda i, n: (0, 0, 0, 0))],
        out_specs=pl.BlockSpec((1, BAND, qw), lambda i, n: (i, n, 0)),
        out_shape=jax.ShapeDtypeStruct((b, t, qw), BF16),
        scratch_shapes=[pltpu.VMEM((A_HEADS * BAND, LANES), BF16), pltpu.VMEM((A_KV_HEADS, LANES, 2 * BAND), BF16)],
        compiler_params=_cparams("parallel", "arbitrary"),
        name="swa_attention",
    )(sinks, rel_bias, qkv, qkv, tabt)


def _split3(x):
    p0 = x.astype(BF16)
    r1 = x - p0.astype(F32)
    p1 = r1.astype(BF16)
    p2 = (r1 - p1.astype(F32)).astype(BF16)
    return p0, p1, p2


def _dot_exact_rhs01(x, m01):
    p0, p1, p2 = _split3(x)
    return _dot(p0, m01) + _dot(p1, m01) + _dot(p2, m01)


def _dot_exact_lhs01(m01, x):
    p0, p1, p2 = _split3(x)
    return _dot(m01, p0) + _dot(m01, p1) + _dot(m01, p2)


def _head_sum_matrix(width):
    r = lax.broadcasted_iota(jnp.int32, (width, width), 0) // HEAD_DIM
    c = lax.broadcasted_iota(jnp.int32, (width, width), 1) // HEAD_DIM
    return jnp.where(r == c, 1.0, 0.0).astype(BF16)


def _sigmoid(z):
    return 1.0 / (1.0 + jnp.exp(-z))


def _softplus(z):
    return jnp.maximum(z, 0.0) + jnp.log(1.0 + jnp.exp(-jnp.abs(z)))


def _rwkv_prep_kernel(p_ref, pprev_ref, mu_ref, wa2_ref, g2_ref, w0_ref, a0_ref, kk_ref, ka_ref,
                      r_out, k_out, v_out, lw_out, kk_out, b_out, g_out):
    i = pl.program_id(1)
    p = p_ref[0]
    w = B_WIDTH
    last = pprev_ref[0, SUBLANES - 1:SUBLANES, :]
    last = jnp.where(i > 0, last, 0.0)
    row = lax.broadcasted_iota(jnp.int32, p.shape, 0)
    prev = jnp.where(row == 0, last, pltpu.roll(p, 1, 0))
    xs = p + (prev - p) * mu_ref[...]
    r = xs[:, 0:w]
    k = xs[:, w:2 * w]
    v = xs[:, 2 * w:3 * w]
    wa_lo = xs[:, 3 * w:3 * w + LANES]
    g_lo = xs[:, 3 * w + LANES:3 * w + 2 * LANES]
    lo, _ = _lane_half_masks(wa_lo.shape)
    wa_in = jnp.where(lo, jnp.tanh(wa_lo), wa_lo)
    in_hi = wa_in.astype(BF16)
    in_lo = (wa_in - in_hi.astype(F32)).astype(BF16)
    wa = _dot(in_hi, wa2_ref[0]) + (_dot(in_hi, wa2_ref[1]) + _dot(in_lo, wa2_ref[0]))
    wlog = -_softplus(-(w0_ref[...] + wa[:, 0:w])) - 0.5
    a = _sigmoid(a0_ref[...] + wa[:, w:2 * w])
    g = _dot(_sigmoid(g_lo).astype(BF16), g2_ref[...])
    kk = k * kk_ref[...]
    ss = _dot_exact_rhs01(kk * kk, _head_sum_matrix(w))
    kk = kk / jnp.maximum(jnp.sqrt(ss), 1e-12)
    kmod = k * (1.0 + (a - 1.0) * ka_ref[...])
    r_out[0] = r
    k_out[0] = kmod
    v_out[0] = v
    lw_out[0] = -jnp.exp(wlog)
    kk_out[0] = kk
    b_out[0] = kk * a
    g_out[0] = g


def _rwkv_scan_kernel(r_ref, k_ref, v_ref, lw_ref, kk_ref, b_ref, g_ref, rk_ref, lnw_ref, lnb_ref,
                      o_ref, st_scr, *, ch, nsub):
    step = pl.program_id(1)
    n2 = 2 * ch

    @pl.when(step == 0)
    def _():
        st_scr[...] = jnp.zeros_like(st_scr)

    ti = lax.broadcasted_iota(jnp.int32, (n2, n2), 0)
    si = lax.broadcasted_iota(jnp.int32, (n2, n2), 1)
    same = (ti // ch) == (si // ch)
    incl = jnp.logical_and(same, si <= ti)
    strict = jnp.logical_and(same, si < ti)
    eye = jnp.where(si == ti, 1.0, 0.0)
    levels = []
    s = 1
    while s < ch:
        same2s = (ti // (2 * s)) == (si // (2 * s))
        levels.append(jnp.logical_and(same2s, jnp.logical_and((ti // s) % 2 == 1, (si // s) % 2 == 0)))
        s *= 2
    tr = lax.broadcasted_iota(jnp.int32, (ch, ch), 0)
    tc = lax.broadcasted_iota(jnp.int32, (ch, ch), 1)
    tri01 = jnp.where(tc <= tr, 1.0, 0.0).astype(BF16)
    lo, hi = _lane_half_masks((ch, LANES))
    hsum = _head_sum_matrix(LANES)

    def stack2(z):
        return jnp.concatenate([jnp.where(lo, z, 0.0), jnp.where(hi, z, 0.0)], axis=0)

    def dup2(z):
        return jnp.concatenate([z, z], axis=0)

    npair = B_HEADS // 2
    chains = [(sub, pr) for sub in range(nsub) for pr in range(npair)]
    cums = [_dot_exact_lhs01(tri01, lw_ref[0, sub * ch:(sub + 1) * ch, :]) for sub in range(nsub)]
    a2, r2, b2, k2, v2, bh2, kh2, decay = [], [], [], [], [], [], [], []
    for sub, pr in chains:
        rows = slice(sub * ch, (sub + 1) * ch)
        sl = slice(pr * LANES, (pr + 1) * LANES)
        k, bb = k_ref[0, rows, sl], b_ref[0, rows, sl]
        cum = cums[sub][:, sl]
        tot = cum[ch - 1:ch, :]
        e_neg = jnp.exp(-cum)
        e_rest = jnp.exp(tot - cum)
        a2.append(stack2(kk_ref[0, rows, sl] * jnp.exp(cum - lw_ref[0, rows, sl])).astype(BF16))
        r2.append(stack2(r_ref[0, rows, sl] * jnp.exp(cum)))
        b2.append(dup2(bb * e_neg).astype(BF16))
        k2.append(dup2(k * e_neg).astype(BF16))
        v2.append(stack2(v_ref[0, rows, sl]).astype(BF16))
        bh2.append(stack2(bb * e_rest).astype(BF16))
        kh2.append(stack2(k * e_rest).astype(BF16))
        decay.append(jnp.exp(tot))
    nc = len(chains)
    gram = [_dot_nt(jnp.concatenate([a2[i], r2[i].astype(BF16)], axis=0), jnp.concatenate([b2[i], k2[i]], axis=0))
            for i in range(nc)]
    ab = [jnp.where(strict, g[0:n2, 0:n2], 0.0) for g in gram]
    ak = [jnp.where(strict, g[0:n2, n2:2 * n2], 0.0).astype(BF16) for g in gram]
    rb = [jnp.where(incl, g[n2:2 * n2, 0:n2], 0.0).astype(BF16) for g in gram]
    rk = [jnp.where(incl, g[n2:2 * n2, n2:2 * n2], 0.0).astype(BF16) for g in gram]
    tinv = [eye] * nc
    for lvl in levels:
        tb = [t.astype(BF16) for t in tinv]
        tm = [_dot(tb[i], jnp.where(lvl, ab[i], 0.0).astype(BF16)).astype(BF16) for i in range(nc)]
        tinv = [tinv[i] - _dot(tm[i], tb[i]) for i in range(nc)]
    tb = [t.astype(BF16) for t in tinv]
    akv = [_dot(ak[i], v2[i]).astype(BF16) for i in range(nc)]
    wu = [_dot(tb[i], jnp.concatenate([a2[i], akv[i]], axis=1)).astype(BF16) for i in range(nc)]
    rwu = [_dot(rb[i], wu[i]) for i in range(nc)]
    rw = [(r2[i] - rwu[i][:, 0:LANES]).astype(BF16) for i in range(nc)]
    y0 = [_dot(rk[i], v2[i]) - rwu[i][:, LANES:2 * LANES] for i in range(nc)]
    gm = [_dot_tn(bh2[i], wu[i][:, 0:LANES]).astype(BF16) for i in range(nc)]
    s0t = [_dot_tn(jnp.concatenate([v2[i], wu[i][:, LANES:2 * LANES]], axis=0),
                   jnp.concatenate([kh2[i], -bh2[i]], axis=0)) for i in range(nc)]
    states = [st_scr[pr] for pr in range(npair)]
    ys = [None] * nc
    for i, (sub, pr) in enumerate(chains):
        st = states[pr]
        stb = st.astype(BF16)
        y2 = _dot_nt(rw[i], stb) + y0[i]
        ys[i] = y2[0:ch, :] + y2[ch:n2, :]
        states[pr] = st * decay[i] - _dot_nt(stb, gm[i]) + s0t[i]
    for pr in range(npair):
        st_scr[pr] = states[pr]
    tstep = nsub * ch
    y_st = jnp.concatenate([ys[sub * npair + pr] for pr in range(npair) for sub in range(nsub)], axis=0)
    rkr_st = jnp.concatenate([r_ref[0, :, pr * LANES:(pr + 1) * LANES] * k_ref[0, :, pr * LANES:(pr + 1) * LANES]
                              * rk_ref[:, pr * LANES:(pr + 1) * LANES] for pr in range(npair)], axis=0)
    sums = _dot_exact_rhs01(jnp.concatenate([y_st, rkr_st], axis=0), hsum)
    yc = y_st - sums[0:npair * tstep, :] * (1.0 / HEAD_DIM)
    var = _dot_exact_rhs01(yc * yc, hsum) * (1.0 / HEAD_DIM)
    yn = yc * lax.rsqrt(var + B_LN_EPS)
    for pr in range(npair):
        sl = slice(pr * LANES, (pr + 1) * LANES)
        rows = slice(pr * tstep, (pr + 1) * tstep)
        bonus = sums[npair * tstep + pr * tstep:npair * tstep + (pr + 1) * tstep, :] * v_ref[0, :, sl]
        out = (yn[rows, :] * lnw_ref[:, sl] + lnb_ref[:, sl] + bonus) * g_ref[0, :, sl]
        o_ref[0, :, sl] = out.astype(o_ref.dtype)


def _rwkv_kernel(p_ref, pprev_ref, mu_ref, wa2_ref, g2_ref, w0_ref, a0_ref, kkw_ref, ka_ref, rk_ref, lnw_ref, lnb_ref,
                 o_ref, st_scr, r_s, k_s, v_s, lw_s, kk_s, b_s, g_s, *, ch, nsub):
    _rwkv_prep_kernel(p_ref, pprev_ref, mu_ref, wa2_ref, g2_ref, w0_ref, a0_ref, kkw_ref, ka_ref,
                      r_s, k_s, v_s, lw_s, kk_s, b_s, g_s)
    _rwkv_scan_kernel(r_s, k_s, v_s, lw_s, kk_s, b_s, g_s, rk_ref, lnw_ref, lnb_ref, o_ref, st_scr, ch=ch, nsub=nsub)


def rwkv_mix(pb, mu, w0, w2, a0, a2, g2, k_k, k_a, r_k, ln_w, ln_b, ch=RWKV_CHUNK, nsub=4):
    b, t, win = pb.shape
    w = B_WIDTH
    ch = min(ch, t)
    nsub = min(nsub, t // ch)
    tstep = ch * nsub
    wa2 = jnp.zeros((LANES, 2 * w), F32)
    wa2 = wa2.at[:B_DECAY_RANK, :w].set(w2).at[B_DECAY_RANK:, w:].set(a2)
    wa2_hi = wa2.astype(BF16)
    wa2 = jnp.stack([wa2_hi, (wa2 - wa2_hi.astype(F32)).astype(BF16)])
    row = lambda z: z.reshape(1, -1)
    vec = pl.BlockSpec((1, w), lambda i, j: (0, 0))
    return pl.pallas_call(
        functools.partial(_rwkv_kernel, ch=ch, nsub=nsub),
        grid=(b, t // tstep),
        in_specs=[pl.BlockSpec((1, tstep, win), lambda i, j: (i, j, 0)),
                  pl.BlockSpec((1, SUBLANES, win), lambda i, j: (i, jnp.maximum(j * (tstep // SUBLANES) - 1, 0), 0)),
                  pl.BlockSpec((1, win), lambda i, j: (0, 0)),
                  pl.BlockSpec((2, LANES, 2 * w), lambda i, j: (0, 0, 0)),
                  pl.BlockSpec((B_GATE_RANK, w), lambda i, j: (0, 0)),
                  vec, vec, vec, vec, vec, vec, vec],
        out_specs=pl.BlockSpec((1, tstep, w), lambda i, j: (i, j, 0)),
        out_shape=jax.ShapeDtypeStruct((b, t, w), BF16),
        scratch_shapes=[pltpu.VMEM((B_HEADS // 2, LANES, LANES), F32)] + [pltpu.VMEM((1, tstep, w), F32)] * 7,
        compiler_params=_cparams("parallel", "arbitrary"),
        name="rwkv_mix",
    )(pb, pb, row(mu), wa2, g2.astype(BF16), row(w0), row(a0), row(k_k), row(k_a), row(r_k), row(ln_w), row(ln_b))


def _fill_vt(load_tile, store_tile, t):
    sub = lax.broadcasted_iota(jnp.int32, (LANES, LANES), 0)

    def body(j, carry):
        start = pl.multiple_of(j * LANES, LANES)
        store_tile(start, jnp.where(sub < HEAD_DIM, load_tile(start).T, 1.0).astype(BF16))
        return carry

    lax.fori_loop(0, t // LANES, body, 0)


def _stack_heads(src_ref, dst_scr, n_heads, scale):
    lo, hi = _lane_half_masks((QCHUNK, LANES))
    for h in range(n_heads):
        tile = src_ref[0, :, (h // 2) * LANES:(h // 2 + 1) * LANES]
        if scale != 1.0:
            tile = tile * scale
        dst_scr[h * QCHUNK:(h + 1) * QCHUNK, :] = jnp.where((lo, hi)[h % 2], tile, 0.0).astype(dst_scr.dtype)


def _score_stage(k_slab, qst_scr, mask_bias, n_heads, s_scr, mx_scr, slot):
    for pair in range(n_heads // 2):
        d = _dot_nt(k_slab(pair), qst_scr[2 * pair * QCHUNK:(2 * pair + 2) * QCHUNK, :])
        for e in range(2):
            h = 2 * pair + e
            cols = slice(h * QCHUNK, (h + 1) * QCHUNK)
            s = d[:, e * QCHUNK:(e + 1) * QCHUNK] + mask_bias(h)
            s_scr[slot, :, cols] = s
            mx_scr[slot, :, cols] = jnp.max(s, axis=0, keepdims=True)


def _softmax_stage(vt, n_heads, s_scr, mx_scr, slot, m_scr, acc_scr):
    m_old = m_scr[...]
    m_new = jnp.maximum(m_old, mx_scr[slot])
    alpha = jnp.exp2(m_old - m_new)
    m_scr[...] = m_new
    for pair in range(n_heads // 2):
        cols = slice(2 * pair * QCHUNK, (2 * pair + 2) * QCHUNK)
        p = jnp.exp2(s_scr[slot, :, cols] - m_new[:, cols]).astype(BF16)
        acc_scr[:, cols] = alpha[:, cols] * acc_scr[:, cols] + _dot(vt(pair), p)


def _pipelined_groups(ngroup, s_matmul, process):
    last = ngroup - 1
    s_matmul(0, 0)

    def pair_body(i, carry):
        g0 = 2 * i
        s_matmul(g0 + 1, 1)
        process(g0, 0)
        s_matmul(jnp.minimum(g0 + 2, last), 0)
        process(g0 + 1, 1)
        return carry

    lax.fori_loop(0, ngroup // 2, pair_body, 0)

    @pl.when(ngroup % 2 == 1)
    def _():
        process(last, 0)


def _bias_rows(tabt_ref, h, d_first, n_tiles):
    tiles = [tabt_ref[jnp.clip(d_first - u, 0, BIAS_TILES - 1), h] for u in range(n_tiles)]
    return jnp.concatenate(tiles, axis=0)


def _write_heads(o_ref, acc_scr, n_heads):
    for p in range(n_heads // 2):
        parts = []
        for h in (2 * p, 2 * p + 1):
            a = acc_scr[:, h * QCHUNK:(h + 1) * QCHUNK]
            parts.append(a[0:HEAD_DIM, :] * (1.0 / a[HEAD_DIM:HEAD_DIM + 1, :]))
        o_ref[0, :, p * LANES:(p + 1) * LANES] = jnp.concatenate(parts, axis=0).T.astype(o_ref.dtype)


def _moba_kernel(q_ref, k_ref, v_ref, tabt_ref, o_ref,
                 k_scr, vt_scr, km_scr, qst_scr, pick_scr, s_scr, mx_scr, m_scr, acc_scr):
    c = pl.program_id(1)
    t = k_ref.shape[1]
    nblk = t // MOBA_BLOCK
    nbp = km_scr.shape[1]
    group = C_HEADS // C_KV_HEADS
    pairs_per_kv = group // 2

    @pl.when(c == 0)
    def _():
        k_scr[...] = k_ref[0].astype(BF16)
        km_scr[...] = jnp.zeros_like(km_scr)
        for kvh in range(C_KV_HEADS):
            kf = k_ref[0, :, kvh * LANES:(kvh + 1) * LANES]
            km_scr[kvh, 0:nblk, :] = jnp.mean(kf.reshape(nblk, MOBA_BLOCK, LANES), axis=1)

            def store_vt(start, tile, kvh=kvh):
                vt_scr[kvh, :, pl.ds(start, LANES)] = tile

            _fill_vt(lambda start, kvh=kvh: v_ref[0, pl.ds(start, LANES), kvh * LANES:(kvh + 1) * LANES], store_vt, t)

    own = (c * QCHUNK) // MOBA_BLOCK
    _stack_heads(q_ref, qst_scr, C_HEADS, HEAD_DIM ** -0.5 * LOG2E)

    blk = lax.broadcasted_iota(jnp.int32, (nbp, QCHUNK), 0)
    for kvh in range(C_KV_HEADS):
        first_tile = kvh * pairs_per_kv
        qsum = q_ref[0, :, first_tile * LANES:(first_tile + 1) * LANES]
        for p in range(first_tile + 1, first_tile + pairs_per_kv):
            qsum = qsum + q_ref[0, :, p * LANES:(p + 1) * LANES]
        gate = _dot_nt_x(km_scr[kvh], qsum)
        gate = jnp.where(blk < own, gate, -jnp.inf)
        rank = jnp.zeros((nbp, QCHUNK), jnp.int32)
        for mblk in range(nblk):
            gm = gate[mblk:mblk + 1, :]
            ahead = jnp.logical_or(gm > gate, jnp.logical_and(gm == gate, mblk < blk))
            rank = rank + jnp.where(ahead, 1, 0)
        chosen = jnp.logical_and(rank < MOBA_TOPK, blk < own)
        pick_scr[kvh] = jnp.where(chosen, 0.0, MASKED)

    m_scr[...] = jnp.full(m_scr.shape, M_INIT, F32)
    acc_scr[...] = jnp.zeros_like(acc_scr)
    grp = 2 * MOBA_BLOCK
    tiles_per_grp = grp // LANES
    ngroup = own // 2 + 1
    grow = lax.broadcasted_iota(jnp.int32, (grp, QCHUNK), 0)
    second = grow >= MOBA_BLOCK
    qpos = c * QCHUNK + lax.broadcasted_iota(jnp.int32, (grp, QCHUNK), 1)

    def score_stage(gi, slot):
        start = pl.multiple_of(gi * grp, grp)
        causal = jnp.where((gi * grp + grow) <= qpos, 0.0, MASKED)
        okb = []
        for kvh in range(C_KV_HEADS):
            first = jnp.where(2 * gi < own, pick_scr[kvh, pl.ds(2 * gi, 1), :], causal)
            later = jnp.where(2 * gi + 1 < own, pick_scr[kvh, pl.ds(2 * gi + 1, 1), :], causal)
            okb.append(jnp.where(second, later, first))
        d_first = c - tiles_per_grp * gi
        _score_stage(lambda pair: k_scr[pl.ds(start, grp), (pair // pairs_per_kv) * LANES:(pair // pairs_per_kv + 1) * LANES],
                     qst_scr, lambda h: _bias_rows(tabt_ref, h, d_first, tiles_per_grp) + okb[h // group],
                     C_HEADS, s_scr, mx_scr, slot)

    def softmax_stage(gi, slot):
        start = pl.multiple_of(gi * grp, grp)
        _softmax_stage(lambda pair: vt_scr[pair // pairs_per_kv, :, pl.ds(start, grp)],
                       C_HEADS, s_scr, mx_scr, slot, m_scr, acc_scr)

    _pipelined_groups(ngroup, score_stage, softmax_stage)
    _write_heads(o_ref, acc_scr, C_HEADS)


def moba_attention(qc, kvc, tabt):
    b, t, _ = qc.shape
    assert t % MOBA_BLOCK == 0
    nq = t // QCHUNK
    qw = C_HEADS * HEAD_DIM
    kvw = C_KV_HEADS * LANES
    nbp = -(-(t // MOBA_BLOCK) // SUBLANES) * SUBLANES
    return pl.pallas_call(
        _moba_kernel,
        grid=(b, nq),
        in_specs=[pl.BlockSpec((1, QCHUNK, qw), lambda i, c: (i, c, 0)),
                  pl.BlockSpec((1, t, kvw), lambda i, c: (i, 0, 0)),
                  pl.BlockSpec((1, t, kvw), lambda i, c: (i, 0, 1)),
                  pl.BlockSpec((BIAS_TILES, C_HEADS, LANES, LANES), lambda i, c: (0, 0, 0, 0))],
        out_specs=pl.BlockSpec((1, QCHUNK, qw), lambda i, c: (i, c, 0)),
        out_shape=jax.ShapeDtypeStruct((b, t, qw), BF16),
        scratch_shapes=[pltpu.VMEM((t, kvw), BF16), pltpu.VMEM((C_KV_HEADS, LANES, t), BF16),
                        pltpu.VMEM((C_KV_HEADS, nbp, LANES), F32),
                        pltpu.VMEM((C_HEADS * QCHUNK, LANES), BF16),
                        pltpu.VMEM((C_KV_HEADS, nbp, QCHUNK), F32),
                        pltpu.VMEM((2, 2 * MOBA_BLOCK, C_HEADS * QCHUNK), F32),
                        pltpu.VMEM((2, 1, C_HEADS * QCHUNK), F32),
                        pltpu.VMEM((1, C_HEADS * QCHUNK), F32),
                        pltpu.VMEM((LANES, C_HEADS * QCHUNK), F32)],
        compiler_params=_cparams("parallel", "arbitrary"),
        name="moba_attention",
    )(qc, kvc, kvc, tabt)


def _dsa_kernel(q_ref, kv_ref, qi_ref, ki_ref, gk_ref, tabt_ref, o_ref,
                k_scr, vt_scr, ki_scr, sc_scr, qst_scr, qist_scr, pre_scr, s_scr, mx_scr, m_scr, acc_scr, *, topk):
    c = pl.program_id(1)
    t = k_scr.shape[0]
    grp = KEY_GROUP * LANES
    ngroup = c // KEY_GROUP + 1
    sub = lax.broadcasted_iota(jnp.int32, (LANES, QCHUNK), 0)
    gsub = lax.broadcasted_iota(jnp.int32, (grp, QCHUNK), 0)
    qpos = c * QCHUNK + lax.broadcasted_iota(jnp.int32, (grp, QCHUNK), 1)

    @pl.when(c == 0)
    def _():
        k_scr[...] = kv_ref[0, :, 0:LANES].astype(BF16)
        ki_scr[...] = _rms(ki_ref[0], gk_ref[...]).astype(BF16)
        def store_vt(start, tile):
            vt_scr[:, pl.ds(start, LANES)] = tile

        _fill_vt(lambda start: kv_ref[0, pl.ds(start, LANES), LANES:2 * LANES], store_vt, t)

    _stack_heads(q_ref, qst_scr, D_HEADS, HEAD_DIM ** -0.5 * LOG2E)
    _stack_heads(qi_ref, qist_scr, IDX_HEADS, 1.0)
    w_t = (qi_ref[0, :, IDX_HEADS * IDX_DIM + LANES:IDX_HEADS * IDX_DIM + 2 * LANES]
           * ((IDX_HEADS * IDX_DIM) ** -0.5)).T

    def score_group(gi):
        start = pl.multiple_of(gi * grp, grp)
        rel = jnp.maximum(_dot_nt(ki_scr[pl.ds(start, grp), :], qist_scr[...]), 0.0)
        sc = jnp.zeros((grp, QCHUNK), F32)
        for h in range(IDX_HEADS):
            sc = sc + rel[:, h * QCHUNK:(h + 1) * QCHUNK] * w_t[h:h + 1, :]
        sc = jnp.where(sc == 0.0, 0.0, sc)
        sc_scr[pl.ds(start, grp), :] = jnp.where((gi * grp + gsub) <= qpos, sc, -jnp.inf)

    def score_pair(i, carry):
        score_group(2 * i)
        score_group(2 * i + 1)
        return carry

    lax.fori_loop(0, ngroup // 2, score_pair, 0)

    @pl.when(ngroup % 2 == 1)
    def _():
        score_group(ngroup - 1)

    def count(pred):
        def body(gi, acc):
            for u in range(KEY_GROUP):
                start = pl.multiple_of(gi * grp + u * LANES, LANES)
                acc = jnp.where(pred(sc_scr[pl.ds(start, LANES), :]), acc + 1, acc)
            return acc
        acc = lax.fori_loop(0, ngroup, body, jnp.zeros((LANES, QCHUNK), jnp.int32))
        return jnp.sum(acc, axis=0, keepdims=True)

    def key_to_float(key):
        return pltpu.bitcast(key ^ (lax.shift_right_arithmetic(key, 31) & 0x7FFFFFFF), F32)

    cnt0 = count(lambda st: st >= 0.0)
    code = jnp.where(cnt0 >= topk, 0, INT_MIN).astype(jnp.int32)

    def bit_body(i, code):
        cand = code + lax.shift_left(jnp.int32(1), 30 - i)
        cand_f = key_to_float(cand)
        cnt = count(lambda st: st >= cand_f)
        return jnp.where(cnt >= topk, cand, code)

    code = lax.fori_loop(0, 31, bit_body, code)
    thr = jnp.where(code == INT_MIN, -jnp.inf, key_to_float(jnp.maximum(code, INT_MIN + 1)))
    thr_eff = jnp.maximum(thr, LOWEST_F32)

    cnt_ge = count(lambda st: st >= thr_eff)
    tied = cnt_ge > topk

    @pl.when(jnp.max(jnp.where(tied, 1.0, 0.0)) > 0.5)
    def _():
        def tally_body(gi, ties_before):
            sums = []
            for u in range(KEY_GROUP):
                start = pl.multiple_of(gi * grp + u * LANES, LANES)
                is_tie = sc_scr[pl.ds(start, LANES), :] == thr
                sums.append(jnp.sum(jnp.where(is_tie, 1.0, 0.0), axis=0, keepdims=True))
            for u in range(KEY_GROUP):
                pre_scr[pl.ds(gi * KEY_GROUP + u, 1), :] = ties_before
                ties_before = ties_before + sums[u]
            return ties_before

        ties = lax.fori_loop(0, ngroup, tally_body, jnp.zeros((1, QCHUNK), F32))
        above = cnt_ge.astype(F32) - ties
        need = jnp.where(tied, topk - above, 3.0e38)
        tri = jnp.where(lax.broadcasted_iota(jnp.int32, (LANES, LANES), 1) <= sub, 1.0, 0.0).astype(BF16)

        def demote_group(gi):
            tiles = []
            for u in range(KEY_GROUP):
                start = pl.multiple_of(gi * grp + u * LANES, LANES)
                tiles.append(sc_scr[pl.ds(start, LANES), :])
            eq01 = jnp.concatenate([jnp.where(st == thr, 1.0, 0.0).astype(BF16) for st in tiles], axis=1)
            within = _dot(tri, eq01)
            for u, st in enumerate(tiles):
                start = pl.multiple_of(gi * grp + u * LANES, LANES)
                rank = pre_scr[pl.ds(gi * KEY_GROUP + u, 1), :] + within[:, u * QCHUNK:(u + 1) * QCHUNK]
                drop = jnp.logical_and(st == thr, rank > need)
                sc_scr[pl.ds(start, LANES), :] = jnp.where(drop, -jnp.inf, st)

        def demote_pair(i, carry):
            demote_group(2 * i)
            demote_group(2 * i + 1)
            return carry

        lax.fori_loop(0, ngroup // 2, demote_pair, 0)

        @pl.when(ngroup % 2 == 1)
        def _():
            demote_group(ngroup - 1)

    m_scr[...] = jnp.full(m_scr.shape, M_INIT, F32)
    acc_scr[...] = jnp.zeros_like(acc_scr)

    def score_stage(gi, slot):
        start = pl.multiple_of(gi * grp, grp)
        okb = jnp.where(sc_scr[pl.ds(start, grp), :] >= thr_eff, 0.0, MASKED)
        d_first = c - gi * KEY_GROUP
        _score_stage(lambda pair: k_scr[pl.ds(start, grp), :], qst_scr,
                     lambda h: _bias_rows(tabt_ref, h, d_first, KEY_GROUP) + okb, D_HEADS, s_scr, mx_scr, slot)

    def softmax_stage(gi, slot):
        start = pl.multiple_of(gi * grp, grp)
        _softmax_stage(lambda pair: vt_scr[:, pl.ds(start, grp)], D_HEADS, s_scr, mx_scr, slot, m_scr, acc_scr)

    _pipelined_groups(ngroup, score_stage, softmax_stage)
    _write_heads(o_ref, acc_scr, D_HEADS)


def dsa_attention(qd, kvd, idx, kidx_gain, tabt):
    b, t, _ = qd.shape
    topk = min(DSA_TOPK, t // 4)
    nq = t // QCHUNK
    gk = jnp.concatenate([kidx_gain, kidx_gain]).reshape(1, LANES)
    wq = D_HEADS * HEAD_DIM
    return pl.pallas_call(
        functools.partial(_dsa_kernel, topk=topk),
        grid=(b, nq),
        in_specs=[pl.BlockSpec((1, QCHUNK, wq), lambda i, c: (i, c, 0)),
                  pl.BlockSpec((1, t, 2 * LANES), lambda i, c: (i, 0, 0)),
                  pl.BlockSpec((1, QCHUNK, 4 * LANES), lambda i, c: (i, c, 0)),
                  pl.BlockSpec((1, t, LANES), lambda i, c: (i, 0, 2)),
                  pl.BlockSpec((1, LANES), lambda i, c: (0, 0)),
                  pl.BlockSpec((BIAS_TILES, D_HEADS, LANES, LANES), lambda i, c: (0, 0, 0, 0))],
        out_specs=pl.BlockSpec((1, QCHUNK, wq), lambda i, c: (i, c, 0)),
        out_shape=jax.ShapeDtypeStruct((b, t, wq), BF16),
        scratch_shapes=[pltpu.VMEM((t, LANES), BF16), pltpu.VMEM((LANES, t), BF16), pltpu.VMEM((t, LANES), BF16),
                        pltpu.VMEM((t, QCHUNK), F32),
                        pltpu.VMEM((D_HEADS * QCHUNK, LANES), BF16), pltpu.VMEM((IDX_HEADS * QCHUNK, LANES), BF16),
                        pltpu.VMEM((t // LANES, QCHUNK), F32),
                        pltpu.VMEM((2, KEY_GROUP * LANES, D_HEADS * QCHUNK), F32),
                        pltpu.VMEM((2, 1, D_HEADS * QCHUNK), F32),
                        pltpu.VMEM((1, D_HEADS * QCHUNK), F32),
                        pltpu.VMEM((LANES, D_HEADS * QCHUNK), F32)],
        compiler_params=_cparams("parallel", "arbitrary"),
        name="dsa_attention",
    )(qd, kvd, idx, idx, gk, tabt)


def _dup_heads(w, first, n_heads):
    cols = []
    for h in range(n_heads):
        head = w[:, first + h * HEAD_DIM:first + (h + 1) * HEAD_DIM]
        cols += [head, head]
    return cols


def _even_weight(w_in):
    w = w_in.astype(BF16)
    aq, akv = A_HEADS * HEAD_DIM, A_KV_HEADS * HEAD_DIM
    cols = [w[:, :aq]] + _dup_heads(w, aq, A_KV_HEADS) + _dup_heads(w, aq + akv, A_KV_HEADS) + [w[:, aq + 2 * akv:]]
    return jnp.concatenate(cols, axis=1)


def _odd_weight(w_in):
    w = w_in.astype(BF16)
    cq, ckv, dq = C_HEADS * HEAD_DIM, C_KV_HEADS * HEAD_DIM, D_HEADS * HEAD_DIM
    o_kc, o_vc, o_qd = cq, cq + ckv, cq + 2 * ckv
    o_kd, o_vd, o_qi = o_qd + dq, o_qd + dq + HEAD_DIM, o_qd + dq + 2 * HEAD_DIM
    o_ki = o_qi + IDX_HEADS * IDX_DIM
    o_wi = o_ki + IDX_DIM
    pad = jnp.zeros((w.shape[0], LANES - IDX_HEADS), BF16)
    cols = ([w[:, :cq]] + _dup_heads(w, o_kc, C_KV_HEADS) + _dup_heads(w, o_vc, C_KV_HEADS)
            + [w[:, o_qd:o_kd]] + _dup_heads(w, o_kd, 1) + _dup_heads(w, o_vd, 1)
            + [w[:, o_qi:o_ki]] + _dup_heads(w, o_ki, 1) + [w[:, o_wi:o_wi + IDX_HEADS], pad])
    return jnp.concatenate(cols, axis=1)


def kernel(x, mem, rel_bias, norm_gains, mix_w_out, mem_norm, x_wq, x_wkv, x_wo, ff_w1, ff_w2, ev_w_in, a_sinks, b_mu, b_w0, b_w2, b_a0, b_a2, b_g2, b_k_k, b_k_a, b_r_k, b_ln_w, b_ln_b, od_w_in, d_kidx_norm):
    bsz, t, d = x.shape
    n = bsz * t
    depth = norm_gains.shape[0]
    mlen = mem.shape[1]
    tabt = bias_table(rel_bias, transposed=True)
    x2 = x.reshape(n, d)
    for layer in range(depth):
        i = layer // 2
        g = norm_gains[layer]
        if layer % 2 == 0:
            b_in = 3 * B_WIDTH + B_DECAY_RANK + B_A_RANK + B_GATE_RANK
            qkv, pb = norm_proj(x2, g[0], _even_weight(ev_w_in[i]), (1024, b_in), (F32, F32))
            ya = swa_attention(qkv.reshape(bsz, t, 1024), a_sinks[i], rel_bias, tabt)
            yb = rwkv_mix(pb.reshape(bsz, t, b_in), b_mu[i], b_w0[i], b_w2[i], b_a0[i], b_a2[i], b_g2[i],
                          b_k_k[i], b_k_a[i], b_r_k[i].reshape(-1), b_ln_w[i], b_ln_b[i])
        else:
            qc, kvc, qd, kvd, idx = norm_proj(x2, g[0], _odd_weight(od_w_in[i]),
                                              (512, 512, 512, 256, 512), (F32,) * 5)
            ya = moba_attention(qc.reshape(bsz, t, 512), kvc.reshape(bsz, t, 512), tabt)
            yb = dsa_attention(qd.reshape(bsz, t, 512), kvd.reshape(bsz, t, 256), idx.reshape(bsz, t, 512),
                               d_kidx_norm[i], tabt)
        kmem, vmem = norm_proj(mem.reshape(bsz * mlen, d), mem_norm[layer], x_wkv[layer].astype(BF16),
                               (X_WIDTH, X_WIDTH), (BF16, BF16))
        x3 = post_mixer(ya, yb, x2.reshape(bsz, t, d), kmem.reshape(bsz, mlen, X_WIDTH),
                        vmem.reshape(bsz, mlen, X_WIDTH), mix_w_out[layer], x_wq[layer], x_wo[layer],
                        ff_w1[layer], ff_w2[layer], g)
        x2 = x3.reshape(n, d)
    return x2.reshape(bsz, t, d)
```

```python
import functools

import numpy as np
import jax
import jax.numpy as jnp
from jax import lax
from jax.experimental import pallas as pl
from jax.experimental.pallas import tpu as pltpu

F32 = jnp.float32
BF16 = jnp.bfloat16
HIGHEST = lax.Precision.HIGHEST

LANES = 128
SUBLANES = 8
HEAD_DIM = 64
EPS = 1e-6
A_HEADS, A_KV_HEADS, BAND = 8, 2, 128
SWA_BANDS_PER_STEP = 2
B_HEADS, B_WIDTH = 8, 512
B_DECAY_RANK, B_A_RANK, B_GATE_RANK = 64, 64, 128
B_LN_EPS = 64e-5
RWKV_CHUNK = 64
C_HEADS, C_KV_HEADS, MOBA_BLOCK, MOBA_TOPK, QCHUNK = 8, 2, 256, 3, 128
D_HEADS, IDX_HEADS, IDX_DIM, DSA_TOPK = 8, 4, 64, 256
N_BUCKETS, BIAS_HEADS = 32, 8
BIAS_TILES = 9
X_HEADS, X_HEAD_DIM, X_WIDTH = 4, 128, 512

LOG2E = 1.4426950408889634
KEY_GROUP = 4
MASKED = -2e30
M_INIT = -1e30
INT_MIN = -(2 ** 31)
LOWEST_F32 = -3.4028234663852886e38
VMEM_LIMIT = 56 * 1024 * 1024


def _cparams(*sem):
    return pltpu.CompilerParams(dimension_semantics=tuple(sem), vmem_limit_bytes=VMEM_LIMIT)


def _dot(a, b):
    return jnp.dot(a, b, preferred_element_type=F32)


def _dot_nt(a, b):
    return lax.dot_general(a, b, (((1,), (1,)), ((), ())), preferred_element_type=F32)


def _dot_tn(a, b):
    return lax.dot_general(a, b, (((0,), (0,)), ((), ())), preferred_element_type=F32)


def _dot_nt_x(a, b):
    return lax.dot_general(a, b, (((1,), (1,)), ((), ())), preferred_element_type=F32, precision=HIGHEST)


def _rms(xf, g):
    ms = jnp.mean(xf * xf, axis=-1, keepdims=True)
    return xf * lax.rsqrt(ms + EPS) * g


def _lane_half_masks(shape):
    lane = lax.broadcasted_iota(jnp.int32, shape, len(shape) - 1)
    lo = lane < HEAD_DIM
    return lo, jnp.logical_not(lo)


def _bucket_of_distance(n):
    exact = N_BUCKETS // 2
    if n < exact:
        return n
    j = 0
    while n ** 8 >= (exact ** 8) * (2 ** (3 * (j + 1))):
        j += 1
    return min(exact + j, N_BUCKETS - 1)


@functools.lru_cache(maxsize=None)
def _bucket_tiles(transposed):
    by_dist = np.array([_bucket_of_distance(n) for n in range(BIAS_TILES * LANES + LANES)], np.int32)
    q = np.arange(LANES)[None, :] if transposed else np.arange(LANES)[:, None]
    k = np.arange(LANES)[:, None] if transposed else np.arange(LANES)[None, :]
    tiles = [by_dist[np.maximum(d * LANES + q - k, 0)] for d in range(BIAS_TILES)]
    return np.stack(tiles).astype(np.int32)


def _bias_table_kernel(rb_ref, bkt_ref, out_ref, *, base2_shifted):
    bkt = bkt_ref[0]
    for h in range(BIAS_HEADS):
        acc = jnp.zeros((LANES, LANES), F32)
        for b in range(N_BUCKETS):
            acc = jnp.where(bkt == b, rb_ref[b, h], acc)
        if base2_shifted:
            acc = (acc - rb_ref[N_BUCKETS - 1, h]) * LOG2E
        out_ref[0, h] = acc


def bias_table(rel_bias, transposed=False):
    return pl.pallas_call(
        functools.partial(_bias_table_kernel, base2_shifted=transposed),
        grid=(BIAS_TILES,),
        in_specs=[pl.BlockSpec(memory_space=pltpu.SMEM),
                  pl.BlockSpec((1, LANES, LANES), lambda d: (d, 0, 0))],
        out_specs=pl.BlockSpec((1, BIAS_HEADS, LANES, LANES), lambda d: (d, 0, 0, 0)),
        out_shape=jax.ShapeDtypeStruct((BIAS_TILES, BIAS_HEADS, LANES, LANES), F32),
        compiler_params=_cparams("arbitrary"),
        name="bias_table",
    )(rel_bias, jnp.asarray(_bucket_tiles(transposed)))


def _norm_proj_kernel(x_ref, g_ref, w_ref, *out_refs, splits):
    h = _rms(x_ref[...], g_ref[...]).astype(BF16)
    res = _dot(h, w_ref[...])
    off = 0
    for o_ref, width in zip(out_refs, splits):
        o_ref[...] = res[:, off:off + width].astype(o_ref.dtype)
        off += width


def norm_proj(x2, gain, w, splits, out_dtypes, tm=512):
    n, d = x2.shape
    tm = min(tm, n)
    total = sum(splits)
    assert w.shape == (d, total) and n % tm == 0 and all(s % LANES == 0 for s in splits)
    return pl.pallas_call(
        functools.partial(_norm_proj_kernel, splits=tuple(splits)),
        grid=(n // tm,),
        in_specs=[pl.BlockSpec((tm, d), lambda i: (i, 0)),
                  pl.BlockSpec((1, d), lambda i: (0, 0)),
                  pl.BlockSpec((d, total), lambda i: (0, 0), pipeline_mode=pl.Buffered(1))],
        out_specs=[pl.BlockSpec((tm, s), lambda i: (i, 0)) for s in splits],
        out_shape=[jax.ShapeDtypeStruct((n, s), dt) for s, dt in zip(splits, out_dtypes)],
        compiler_params=_cparams("parallel"),
        name="norm_proj",
    )(x2, gain.reshape(1, d), w)


def _post_mixer_kernel(ya_ref, yb_ref, x_ref, k_ref, v_ref, wa_ref, wb_ref, wq_ref, wo_ref, w1_ref, w2_ref, g_ref,
                       o_ref, *, tf):
    x = x_ref[0]
    z = _dot(ya_ref[0], wa_ref[...]) + _dot(yb_ref[0], wb_ref[...])
    x = x + _rms(z, g_ref[1:2, :])
    q = _dot(_rms(x, g_ref[2:3, :]).astype(BF16), wq_ref[...])
    k = k_ref[0]
    v = v_ref[0]
    outs = []
    for hd in range(X_HEADS):
        sl = slice(hd * X_HEAD_DIM, (hd + 1) * X_HEAD_DIM)
        s = _dot_nt(q[:, sl].astype(BF16), k[:, sl]) * (X_HEAD_DIM ** -0.5)
        m = jnp.max(s, axis=-1, keepdims=True)
        p = jnp.exp(s - m)
        p = p / jnp.sum(p, axis=-1, keepdims=True)
        outs.append(_dot(p.astype(BF16), v[:, sl]))
    z = _dot(jnp.concatenate(outs, axis=-1).astype(BF16), wo_ref[...])
    x = x + _rms(z, g_ref[3:4, :])
    h = _rms(x, g_ref[4:5, :]).astype(BF16)
    acc = jnp.zeros(x.shape, F32)
    for j in range(w1_ref.shape[1] // tf):
        a = jnp.maximum(_dot(h, w1_ref[:, j * tf:(j + 1) * tf]), 0.0)
        acc = acc + _dot((a * a).astype(BF16), w2_ref[j * tf:(j + 1) * tf, :])
    o_ref[0] = x + _rms(acc, g_ref[5:6, :])


def post_mixer(ya, yb, x3, kmem, vmem, w_out, wq, wo, w1, w2, gains, tq=512, tf=1024):
    b, t, d = x3.shape
    tq = min(tq, t)
    m = kmem.shape[1]
    ka, kb = ya.shape[2], yb.shape[2]
    f = w1.shape[1]
    once = pl.Buffered(1)
    const = lambda shape: pl.BlockSpec(shape, lambda i, j: (0, 0), pipeline_mode=once)
    return pl.pallas_call(
        functools.partial(_post_mixer_kernel, tf=tf),
        grid=(b, t // tq),
        in_specs=[pl.BlockSpec((1, tq, ka), lambda i, j: (i, j, 0)),
                  pl.BlockSpec((1, tq, kb), lambda i, j: (i, j, 0)),
                  pl.BlockSpec((1, tq, d), lambda i, j: (i, j, 0)),
                  pl.BlockSpec((1, m, X_WIDTH), lambda i, j: (i, 0, 0)),
                  pl.BlockSpec((1, m, X_WIDTH), lambda i, j: (i, 0, 0)),
                  const((ka, d)), const((kb, d)), const((d, X_WIDTH)), const((X_WIDTH, d)),
                  const((d, f)), const((f, d)), const(gains.shape)],
        out_specs=pl.BlockSpec((1, tq, d), lambda i, j: (i, j, 0)),
        out_shape=jax.ShapeDtypeStruct((b, t, d), F32),
        compiler_params=_cparams("parallel", "parallel"),
        name="post_mixer",
    )(ya, yb, x3, kmem, vmem, w_out[:ka].astype(BF16), w_out[ka:].astype(BF16), wq.astype(BF16), wo.astype(BF16),
      w1.astype(BF16), w2.astype(BF16), gains)


def _swa_kernel(sink_ref, rb_ref, q_ref, kv_ref, tabt_ref, o_ref, qst_scr, vt_scr):
    for part in range(SWA_BANDS_PER_STEP):
        _swa_band(pl.program_id(1) * SWA_BANDS_PER_STEP + part, part, sink_ref, rb_ref, q_ref, kv_ref, tabt_ref, o_ref,
                  qst_scr.at[part], vt_scr.at[part])


def _swa_band(n, part, sink_ref, rb_ref, q_ref, kv_ref, tabt_ref, o_ref, qst_scr, vt_scr):
    rows = slice(part * BAND, (part + 1) * BAND)
    group = A_HEADS // A_KV_HEADS
    first_band = jnp.maximum(n - 1, 0)
    start = pl.multiple_of(first_band * BAND, BAND)
    slab = 2 * BAND
    _stack_heads(q_ref, qst_scr, A_HEADS, HEAD_DIM ** -0.5 * LOG2E, rows)
    kpos = start + lax.broadcasted_iota(jnp.int32, (slab, BAND), 0)
    qpos = n * BAND + lax.broadcasted_iota(jnp.int32, (slab, BAND), 1)
    dist = qpos - kpos
    okb = jnp.where(jnp.logical_and(dist >= 0, dist < BAND), 0.0, MASKED)
    sub = lax.broadcasted_iota(jnp.int32, (LANES, slab), 0)
    koff = 0
    voff = A_KV_HEADS * LANES
    d_first = n - first_band
    scores, vts = [], []
    for hkv in range(A_KV_HEADS):
        k2 = kv_ref[0, pl.ds(start, slab), koff + hkv * LANES:koff + (hkv + 1) * LANES].astype(BF16)
        v2 = kv_ref[0, pl.ds(start, slab), voff + hkv * LANES:voff + (hkv + 1) * LANES]
        vt_scr[hkv] = jnp.where(sub < HEAD_DIM, v2.T, 1.0).astype(BF16)
        vts.append(vt_scr[hkv])
        s_all = _dot_nt(k2, qst_scr[hkv * group * BAND:(hkv + 1) * group * BAND, :])
        for g in range(group):
            h = hkv * group + g
            scores.append(s_all[:, g * BAND:(g + 1) * BAND] + (_bias_rows(tabt_ref, h, d_first, 2) + okb))
    sinks = [(sink_ref[h] - rb_ref[N_BUCKETS - 1, h]) * LOG2E for h in range(A_HEADS)]
    m = [jnp.maximum(jnp.max(scores[h], axis=0, keepdims=True), sinks[h]) for h in range(A_HEADS)]
    p = [jnp.exp2(scores[h] - m[h]).astype(BF16) for h in range(A_HEADS)]
    outs = []
    for hkv in range(A_KV_HEADS):
        pv = _dot(vts[hkv], jnp.concatenate(p[hkv * group:(hkv + 1) * group], axis=1))
        for g in range(group):
            h = hkv * group + g
            a = pv[:, g * BAND:(g + 1) * BAND]
            den = a[HEAD_DIM:HEAD_DIM + 1, :] + jnp.exp2(sinks[h] - m[h])
            outs.append(a[0:HEAD_DIM, :] * (1.0 / den))
    for pair in range(A_HEADS // 2):
        tile = jnp.concatenate(outs[2 * pair:2 * pair + 2], axis=0).T
        o_ref[0, rows, pair * LANES:(pair + 1) * LANES] = tile.astype(o_ref.dtype)


def swa_attention(qkv, sinks, rel_bias, tabt):
    b, t, w = qkv.shape
    qw = A_HEADS * HEAD_DIM
    tq = SWA_BANDS_PER_STEP * BAND
    assert t % tq == 0
    return pl.pallas_call(
        _swa_kernel,
        grid=(b, t // tq),
        in_specs=[pl.BlockSpec(memory_space=pltpu.SMEM),
                  pl.BlockSpec(memory_space=pltpu.SMEM),
                  pl.BlockSpec((1, tq, qw), lambda i, n: (i, n, 0)),
                  pl.BlockSpec((1, t, w - qw), lambda i, n: (i, 0, 1)),
                  pl.BlockSpec((2, A_HEADS, LANES, LANES), lambda i, n: (0, 0, 0, 0))],
        out_specs=pl.BlockSpec((1, tq, qw), lambda i, n: (i, n, 0)),
        out_shape=jax.ShapeDtypeStruct((b, t, qw), BF16),
        scratch_shapes=[pltpu.VMEM((SWA_BANDS_PER_STEP, A_HEADS * BAND, LANES), BF16),
                        pltpu.VMEM((SWA_BANDS_PER_STEP, A_KV_HEADS, LANES, 2 * BAND), BF16)],
        compiler_params=_cparams("parallel", "arbitrary"),
        name="swa_attention",
    )(sinks, rel_bias, qkv, qkv, tabt)


def _split3(x):
    p0 = x.astype(BF16)
    r1 = x - p0.astype(F32)
    p1 = r1.astype(BF16)
    p2 = (r1 - p1.astype(F32)).astype(BF16)
    return p0, p1, p2


def _dot_exact_rhs01(x, m01):
    p0, p1, p2 = _split3(x)
    return _dot(p0, m01) + _dot(p1, m01) + _dot(p2, m01)


def _dot_exact_lhs01(m01, x):
    p0, p1, p2 = _split3(x)
    return _dot(m01, p0) + _dot(m01, p1) + _dot(m01, p2)


def _head_sum_matrix(width):
    r = lax.broadcasted_iota(jnp.int32, (width, width), 0) // HEAD_DIM
    c = lax.broadcasted_iota(jnp.int32, (width, width), 1) // HEAD_DIM
    return jnp.where(r == c, 1.0, 0.0).astype(BF16)


def _sigmoid(z):
    return 1.0 / (1.0 + jnp.exp(-z))


def _softplus(z):
    return jnp.maximum(z, 0.0) + jnp.log(1.0 + jnp.exp(-jnp.abs(z)))


def _rwkv_prep_kernel(p_ref, pprev_ref, mu_ref, wa2_ref, g2_ref, w0_ref, a0_ref, kk_ref, ka_ref,
                      r_out, k_out, v_out, lw_out, kk_out, b_out, g_out):
    i = pl.program_id(1)
    p = p_ref[0]
    w = B_WIDTH
    last = pprev_ref[0, SUBLANES - 1:SUBLANES, :]
    last = jnp.where(i > 0, last, 0.0)
    row = lax.broadcasted_iota(jnp.int32, p.shape, 0)
    prev = jnp.where(row == 0, last, pltpu.roll(p, 1, 0))
    xs = p + (prev - p) * mu_ref[...]
    r = xs[:, 0:w]
    k = xs[:, w:2 * w]
    v = xs[:, 2 * w:3 * w]
    wa_lo = xs[:, 3 * w:3 * w + LANES]
    g_lo = xs[:, 3 * w + LANES:3 * w + 2 * LANES]
    lo, _ = _lane_half_masks(wa_lo.shape)
    wa_in = jnp.where(lo, jnp.tanh(wa_lo), wa_lo)
    in_hi = wa_in.astype(BF16)
    in_lo = (wa_in - in_hi.astype(F32)).astype(BF16)
    wa = _dot(in_hi, wa2_ref[0]) + (_dot(in_hi, wa2_ref[1]) + _dot(in_lo, wa2_ref[0]))
    wlog = -_softplus(-(w0_ref[...] + wa[:, 0:w])) - 0.5
    a = _sigmoid(a0_ref[...] + wa[:, w:2 * w])
    g = _dot(_sigmoid(g_lo).astype(BF16), g2_ref[...])
    kk = k * kk_ref[...]
    ss = _dot_exact_rhs01(kk * kk, _head_sum_matrix(w))
    kk = kk / jnp.maximum(jnp.sqrt(ss), 1e-12)
    kmod = k * (1.0 + (a - 1.0) * ka_ref[...])
    r_out[0] = r
    k_out[0] = kmod
    v_out[0] = v
    lw_out[0] = -jnp.exp(wlog)
    kk_out[0] = kk
    b_out[0] = kk * a
    g_out[0] = g


def _rwkv_scan_kernel(r_ref, k_ref, v_ref, lw_ref, kk_ref, b_ref, g_ref, rk_ref, lnw_ref, lnb_ref,
                      o_ref, st_scr, *, ch, nsub):
    step = pl.program_id(1)
    n2 = 2 * ch

    @pl.when(step == 0)
    def _():
        st_scr[...] = jnp.zeros_like(st_scr)

    ti = lax.broadcasted_iota(jnp.int32, (n2, n2), 0)
    si = lax.broadcasted_iota(jnp.int32, (n2, n2), 1)
    same = (ti // ch) == (si // ch)
    incl = jnp.logical_and(same, si <= ti)
    strict = jnp.logical_and(same, si < ti)
    eye = jnp.where(si == ti, 1.0, 0.0)
    levels = []
    s = 1
    while s < ch:
        same2s = (ti // (2 * s)) == (si // (2 * s))
        levels.append(jnp.logical_and(same2s, jnp.logical_and((ti // s) % 2 == 1, (si // s) % 2 == 0)))
        s *= 2
    tr = lax.broadcasted_iota(jnp.int32, (ch, ch), 0)
    tc = lax.broadcasted_iota(jnp.int32, (ch, ch), 1)
    tri01 = jnp.where(tc <= tr, 1.0, 0.0).astype(BF16)
    lo, hi = _lane_half_masks((ch, LANES))
    hsum = _head_sum_matrix(LANES)

    def stack2(z):
        return jnp.concatenate([jnp.where(lo, z, 0.0), jnp.where(hi, z, 0.0)], axis=0)

    def dup2(z):
        return jnp.concatenate([z, z], axis=0)

    npair = B_HEADS // 2
    chains = [(sub, pr) for sub in range(nsub) for pr in range(npair)]
    cums = [_dot_exact_lhs01(tri01, lw_ref[0, sub * ch:(sub + 1) * ch, :]) for sub in range(nsub)]
    a2, r2, b2, k2, v2, bh2, kh2, decay = [], [], [], [], [], [], [], []
    for sub, pr in chains:
        rows = slice(sub * ch, (sub + 1) * ch)
        sl = slice(pr * LANES, (pr + 1) * LANES)
        k, bb = k_ref[0, rows, sl], b_ref[0, rows, sl]
        cum = cums[sub][:, sl]
        tot = cum[ch - 1:ch, :]
        e_neg = jnp.exp(-cum)
        e_rest = jnp.exp(tot - cum)
        a2.append(stack2(kk_ref[0, rows, sl] * jnp.exp(cum - lw_ref[0, rows, sl])).astype(BF16))
        r2.append(stack2(r_ref[0, rows, sl] * jnp.exp(cum)))
        b2.append(dup2(bb * e_neg).astype(BF16))
        k2.append(dup2(k * e_neg).astype(BF16))
        v2.append(stack2(v_ref[0, rows, sl]).astype(BF16))
        bh2.append(stack2(bb * e_rest).astype(BF16))
        kh2.append(stack2(k * e_rest).astype(BF16))
        decay.append(jnp.exp(tot))
    nc = len(chains)
    gram = [_dot_nt(jnp.concatenate([a2[i], r2[i].astype(BF16)], axis=0), jnp.concatenate([b2[i], k2[i]], axis=0))
            for i in range(nc)]
    ab = [jnp.where(strict, g[0:n2, 0:n2], 0.0) for g in gram]
    ak = [jnp.where(strict, g[0:n2, n2:2 * n2], 0.0).astype(BF16) for g in gram]
    rb = [jnp.where(incl, g[n2:2 * n2, 0:n2], 0.0).astype(BF16) for g in gram]
    rk = [jnp.where(incl, g[n2:2 * n2, n2:2 * n2], 0.0).astype(BF16) for g in gram]
    tinv = [eye] * nc
    for lvl in levels:
        tb = [t.astype(BF16) for t in tinv]
        tm = [_dot(tb[i], jnp.where(lvl, ab[i], 0.0).astype(BF16)).astype(BF16) for i in range(nc)]
        tinv = [tinv[i] - _dot(tm[i], tb[i]) for i in range(nc)]
    tb = [t.astype(BF16) for t in tinv]
    akv = [_dot(ak[i], v2[i]).astype(BF16) for i in range(nc)]
    wu = [_dot(tb[i], jnp.concatenate([a2[i], akv[i]], axis=1)).astype(BF16) for i in range(nc)]
    rwu = [_dot(rb[i], wu[i]) for i in range(nc)]
    rw = [(r2[i] - rwu[i][:, 0:LANES]).astype(BF16) for i in range(nc)]
    y0 = [_dot(rk[i], v2[i]) - rwu[i][:, LANES:2 * LANES] for i in range(nc)]
    gm = [_dot_tn(bh2[i], wu[i][:, 0:LANES]).astype(BF16) for i in range(nc)]
    s0t = [_dot_tn(jnp.concatenate([v2[i], wu[i][:, LANES:2 * LANES]], axis=0),
                   jnp.concatenate([kh2[i], -bh2[i]], axis=0)) for i in range(nc)]
    states = [st_scr[pr] for pr in range(npair)]
    ys = [None] * nc
    for i, (sub, pr) in enumerate(chains):
        st = states[pr]
        stb = st.astype(BF16)
        y2 = _dot_nt(rw[i], stb) + y0[i]
        ys[i] = y2[0:ch, :] + y2[ch:n2, :]
        states[pr] = st * decay[i] - _dot_nt(stb, gm[i]) + s0t[i]
    for pr in range(npair):
        st_scr[pr] = states[pr]
    tstep = nsub * ch
    y_st = jnp.concatenate([ys[sub * npair + pr] for pr in range(npair) for sub in range(nsub)], axis=0)
    rkr_st = jnp.concatenate([r_ref[0, :, pr * LANES:(pr + 1) * LANES] * k_ref[0, :, pr * LANES:(pr + 1) * LANES]
                              * rk_ref[:, pr * LANES:(pr + 1) * LANES] for pr in range(npair)], axis=0)
    sums = _dot_exact_rhs01(jnp.concatenate([y_st, rkr_st], axis=0), hsum)
    yc = y_st - sums[0:npair * tstep, :] * (1.0 / HEAD_DIM)
    var = _dot_exact_rhs01(yc * yc, hsum) * (1.0 / HEAD_DIM)
    yn = yc * lax.rsqrt(var + B_LN_EPS)
    for pr in range(npair):
        sl = slice(pr * LANES, (pr + 1) * LANES)
        rows = slice(pr * tstep, (pr + 1) * tstep)
        bonus = sums[npair * tstep + pr * tstep:npair * tstep + (pr + 1) * tstep, :] * v_ref[0, :, sl]
        out = (yn[rows, :] * lnw_ref[:, sl] + lnb_ref[:, sl] + bonus) * g_ref[0, :, sl]
        o_ref[0, :, sl] = out.astype(o_ref.dtype)


def _rwkv_kernel(p_ref, pprev_ref, mu_ref, wa2_ref, g2_ref, w0_ref, a0_ref, kkw_ref, ka_ref, rk_ref, lnw_ref, lnb_ref,
                 o_ref, st_scr, r_s, k_s, v_s, lw_s, kk_s, b_s, g_s, *, ch, nsub):
    _rwkv_prep_kernel(p_ref, pprev_ref, mu_ref, wa2_ref, g2_ref, w0_ref, a0_ref, kkw_ref, ka_ref,
                      r_s, k_s, v_s, lw_s, kk_s, b_s, g_s)
    _rwkv_scan_kernel(r_s, k_s, v_s, lw_s, kk_s, b_s, g_s, rk_ref, lnw_ref, lnb_ref, o_ref, st_scr, ch=ch, nsub=nsub)


def rwkv_mix(pb, mu, w0, w2, a0, a2, g2, k_k, k_a, r_k, ln_w, ln_b, ch=RWKV_CHUNK, nsub=4):
    b, t, win = pb.shape
    w = B_WIDTH
    ch = min(ch, t)
    nsub = min(nsub, t // ch)
    tstep = ch * nsub
    wa2 = jnp.zeros((LANES, 2 * w), F32)
    wa2 = wa2.at[:B_DECAY_RANK, :w].set(w2).at[B_DECAY_RANK:, w:].set(a2)
    wa2_hi = wa2.astype(BF16)
    wa2 = jnp.stack([wa2_hi, (wa2 - wa2_hi.astype(F32)).astype(BF16)])
    row = lambda z: z.reshape(1, -1)
    vec = pl.BlockSpec((1, w), lambda i, j: (0, 0))
    return pl.pallas_call(
        functools.partial(_rwkv_kernel, ch=ch, nsub=nsub),
        grid=(b, t // tstep),
        in_specs=[pl.BlockSpec((1, tstep, win), lambda i, j: (i, j, 0)),
                  pl.BlockSpec((1, SUBLANES, win), lambda i, j: (i, jnp.maximum(j * (tstep // SUBLANES) - 1, 0), 0)),
                  pl.BlockSpec((1, win), lambda i, j: (0, 0)),
                  pl.BlockSpec((2, LANES, 2 * w), lambda i, j: (0, 0, 0)),
                  pl.BlockSpec((B_GATE_RANK, w), lambda i, j: (0, 0)),
                  vec, vec, vec, vec, vec, vec, vec],
        out_specs=pl.BlockSpec((1, tstep, w), lambda i, j: (i, j, 0)),
        out_shape=jax.ShapeDtypeStruct((b, t, w), BF16),
        scratch_shapes=[pltpu.VMEM((B_HEADS // 2, LANES, LANES), F32)] + [pltpu.VMEM((1, tstep, w), F32)] * 7,
        compiler_params=_cparams("parallel", "arbitrary"),
        name="rwkv_mix",
    )(pb, pb, row(mu), wa2, g2.astype(BF16), row(w0), row(a0), row(k_k), row(k_a), row(r_k), row(ln_w), row(ln_b))


def _fill_vt(load_tile, store_tile, t):
    sub = lax.broadcasted_iota(jnp.int32, (LANES, LANES), 0)

    def body(j, carry):
        start = pl.multiple_of(j * LANES, LANES)
        store_tile(start, jnp.where(sub < HEAD_DIM, load_tile(start).T, 1.0).astype(BF16))
        return carry

    lax.fori_loop(0, t // LANES, body, 0)


def _stack_heads(src_ref, dst_scr, n_heads, scale, rows=slice(None)):
    lo, hi = _lane_half_masks((QCHUNK, LANES))
    for h in range(n_heads):
        tile = src_ref[0, rows, (h // 2) * LANES:(h // 2 + 1) * LANES]
        if scale != 1.0:
            tile = tile * scale
        dst_scr[h * QCHUNK:(h + 1) * QCHUNK, :] = jnp.where((lo, hi)[h % 2], tile, 0.0).astype(dst_scr.dtype)


def _score_stage(k_slab, qst_scr, mask_bias, n_heads, s_scr, mx_scr, slot):
    for pair in range(n_heads // 2):
        d = _dot_nt(k_slab(pair), qst_scr[2 * pair * QCHUNK:(2 * pair + 2) * QCHUNK, :])
        for e in range(2):
            h = 2 * pair + e
            cols = slice(h * QCHUNK, (h + 1) * QCHUNK)
            s = d[:, e * QCHUNK:(e + 1) * QCHUNK] + mask_bias(h)
            s_scr[slot, :, cols] = s
            mx_scr[slot, :, cols] = jnp.max(s, axis=0, keepdims=True)


def _softmax_stage(vt, n_heads, s_scr, mx_scr, slot, m_scr, acc_scr):
    m_old = m_scr[...]
    m_new = jnp.maximum(m_old, mx_scr[slot])
    alpha = jnp.exp2(m_old - m_new)
    m_scr[...] = m_new
    for pair in range(n_heads // 2):
        cols = slice(2 * pair * QCHUNK, (2 * pair + 2) * QCHUNK)
        p = jnp.exp2(s_scr[slot, :, cols] - m_new[:, cols]).astype(BF16)
        acc_scr[:, cols] = alpha[:, cols] * acc_scr[:, cols] + _dot(vt(pair), p)


def _pipelined_groups(ngroup, s_matmul, process):
    last = ngroup - 1
    s_matmul(0, 0)

    def pair_body(i, carry):
        g0 = 2 * i
        s_matmul(g0 + 1, 1)
        process(g0, 0)
        s_matmul(jnp.minimum(g0 + 2, last), 0)
        process(g0 + 1, 1)
        return carry

    lax.fori_loop(0, ngroup // 2, pair_body, 0)

    @pl.when(ngroup % 2 == 1)
    def _():
        process(last, 0)


def _bias_rows(tabt_ref, h, d_first, n_tiles):
    tiles = [tabt_ref[jnp.clip(d_first - u, 0, BIAS_TILES - 1), h] for u in range(n_tiles)]
    return jnp.concatenate(tiles, axis=0)


def _write_heads(o_ref, acc_scr, n_heads):
    for p in range(n_heads // 2):
        parts = []
        for h in (2 * p, 2 * p + 1):
            a = acc_scr[:, h * QCHUNK:(h + 1) * QCHUNK]
            parts.append(a[0:HEAD_DIM, :] * (1.0 / a[HEAD_DIM:HEAD_DIM + 1, :]))
        o_ref[0, :, p * LANES:(p + 1) * LANES] = jnp.concatenate(parts, axis=0).T.astype(o_ref.dtype)


def _moba_kernel(q_ref, k_ref, v_ref, tabt_ref, o_ref,
                 k_scr, vt_scr, km_scr, qst_scr, pick_scr, s_scr, mx_scr, m_scr, acc_scr):
    c = pl.program_id(1)
    t = k_ref.shape[1]
    nblk = t // MOBA_BLOCK
    nbp = km_scr.shape[1]
    group = C_HEADS // C_KV_HEADS
    pairs_per_kv = group // 2

    @pl.when(c == 0)
    def _():
        k_scr[...] = k_ref[0].astype(BF16)
        km_scr[...] = jnp.zeros_like(km_scr)
        for kvh in range(C_KV_HEADS):
            kf = k_ref[0, :, kvh * LANES:(kvh + 1) * LANES]
            km_scr[kvh, 0:nblk, :] = jnp.mean(kf.reshape(nblk, MOBA_BLOCK, LANES), axis=1)

            def store_vt(start, tile, kvh=kvh):
                vt_scr[kvh, :, pl.ds(start, LANES)] = tile

            _fill_vt(lambda start, kvh=kvh: v_ref[0, pl.ds(start, LANES), kvh * LANES:(kvh + 1) * LANES], store_vt, t)

    own = (c * QCHUNK) // MOBA_BLOCK
    _stack_heads(q_ref, qst_scr, C_HEADS, HEAD_DIM ** -0.5 * LOG2E)

    blk = lax.broadcasted_iota(jnp.int32, (nbp, QCHUNK), 0)
    for kvh in range(C_KV_HEADS):
        first_tile = kvh * pairs_per_kv
        qsum = q_ref[0, :, first_tile * LANES:(first_tile + 1) * LANES]
        for p in range(first_tile + 1, first_tile + pairs_per_kv):
            qsum = qsum + q_ref[0, :, p * LANES:(p + 1) * LANES]
        gate = _dot_nt_x(km_scr[kvh], qsum)
        gate = jnp.where(blk < own, gate, -jnp.inf)
        rank = jnp.zeros((nbp, QCHUNK), jnp.int32)
        for mblk in range(nblk):
            gm = gate[mblk:mblk + 1, :]
            ahead = jnp.logical_or(gm > gate, jnp.logical_and(gm == gate, mblk < blk))
            rank = rank + jnp.where(ahead, 1, 0)
        chosen = jnp.logical_and(rank < MOBA_TOPK, blk < own)
        pick_scr[kvh] = jnp.where(chosen, 0.0, MASKED)

    m_scr[...] = jnp.full(m_scr.shape, M_INIT, F32)
    acc_scr[...] = jnp.zeros_like(acc_scr)
    grp = 2 * MOBA_BLOCK
    tiles_per_grp = grp // LANES
    ngroup = own // 2 + 1
    grow = lax.broadcasted_iota(jnp.int32, (grp, QCHUNK), 0)
    second = grow >= MOBA_BLOCK
    qpos = c * QCHUNK + lax.broadcasted_iota(jnp.int32, (grp, QCHUNK), 1)

    def score_stage(gi, slot):
        start = pl.multiple_of(gi * grp, grp)
        causal = jnp.where((gi * grp + grow) <= qpos, 0.0, MASKED)
        okb = []
        for kvh in range(C_KV_HEADS):
            first = jnp.where(2 * gi < own, pick_scr[kvh, pl.ds(2 * gi, 1), :], causal)
            later = jnp.where(2 * gi + 1 < own, pick_scr[kvh, pl.ds(2 * gi + 1, 1), :], causal)
            okb.append(jnp.where(second, later, first))
        d_first = c - tiles_per_grp * gi
        _score_stage(lambda pair: k_scr[pl.ds(start, grp), (pair // pairs_per_kv) * LANES:(pair // pairs_per_kv + 1) * LANES],
                     qst_scr, lambda h: _bias_rows(tabt_ref, h, d_first, tiles_per_grp) + okb[h // group],
                     C_HEADS, s_scr, mx_scr, slot)

    def softmax_stage(gi, slot):
        start = pl.multiple_of(gi * grp, grp)
        _softmax_stage(lambda pair: vt_scr[pair // pairs_per_kv, :, pl.ds(start, grp)],
                       C_HEADS, s_scr, mx_scr, slot, m_scr, acc_scr)

    _pipelined_groups(ngroup, score_stage, softmax_stage)
    _write_heads(o_ref, acc_scr, C_HEADS)


def moba_attention(qc, kvc, tabt):
    b, t, _ = qc.shape
    assert t % MOBA_BLOCK == 0
    nq = t // QCHUNK
    qw = C_HEADS * HEAD_DIM
    kvw = C_KV_HEADS * LANES
    nbp = -(-(t // MOBA_BLOCK) // SUBLANES) * SUBLANES
    return pl.pallas_call(
        _moba_kernel,
        grid=(b, nq),
        in_specs=[pl.BlockSpec((1, QCHUNK, qw), lambda i, c: (i, c, 0)),
                  pl.BlockSpec((1, t, kvw), lambda i, c: (i, 0, 0)),
                  pl.BlockSpec((1, t, kvw), lambda i, c: (i, 0, 1)),
                  pl.BlockSpec((BIAS_TILES, C_HEADS, LANES, LANES), lambda i, c: (0, 0, 0, 0))],
        out_specs=pl.BlockSpec((1, QCHUNK, qw), lambda i, c: (i, c, 0)),
        out_shape=jax.ShapeDtypeStruct((b, t, qw), BF16),
        scratch_shapes=[pltpu.VMEM((t, kvw), BF16), pltpu.VMEM((C_KV_HEADS, LANES, t), BF16),
                        pltpu.VMEM((C_KV_HEADS, nbp, LANES), F32),
                        pltpu.VMEM((C_HEADS * QCHUNK, LANES), BF16),
                        pltpu.VMEM((C_KV_HEADS, nbp, QCHUNK), F32),
                        pltpu.VMEM((2, 2 * MOBA_BLOCK, C_HEADS * QCHUNK), F32),
                        pltpu.VMEM((2, 1, C_HEADS * QCHUNK), F32),
                        pltpu.VMEM((1, C_HEADS * QCHUNK), F32),
                        pltpu.VMEM((LANES, C_HEADS * QCHUNK), F32)],
        compiler_params=_cparams("parallel", "arbitrary"),
        name="moba_attention",
    )(qc, kvc, kvc, tabt)


def _dsa_kernel(q_ref, kv_ref, qi_ref, ki_ref, gk_ref, tabt_ref, o_ref,
                k_scr, vt_scr, ki_scr, sc_scr, qst_scr, qist_scr, pre_scr, s_scr, mx_scr, m_scr, acc_scr, *, topk):
    c = pl.program_id(1)
    t = k_scr.shape[0]
    grp = KEY_GROUP * LANES
    ngroup = c // KEY_GROUP + 1
    sub = lax.broadcasted_iota(jnp.int32, (LANES, QCHUNK), 0)
    gsub = lax.broadcasted_iota(jnp.int32, (grp, QCHUNK), 0)
    qpos = c * QCHUNK + lax.broadcasted_iota(jnp.int32, (grp, QCHUNK), 1)

    @pl.when(c == 0)
    def _():
        k_scr[...] = kv_ref[0, :, 0:LANES].astype(BF16)
        ki_scr[...] = _rms(ki_ref[0], gk_ref[...]).astype(BF16)
        def store_vt(start, tile):
            vt_scr[:, pl.ds(start, LANES)] = tile

        _fill_vt(lambda start: kv_ref[0, pl.ds(start, LANES), LANES:2 * LANES], store_vt, t)

    _stack_heads(q_ref, qst_scr, D_HEADS, HEAD_DIM ** -0.5 * LOG2E)
    _stack_heads(qi_ref, qist_scr, IDX_HEADS, 1.0)
    w_t = (qi_ref[0, :, IDX_HEADS * IDX_DIM + LANES:IDX_HEADS * IDX_DIM + 2 * LANES]
           * ((IDX_HEADS * IDX_DIM) ** -0.5)).T

    def score_group(gi):
        start = pl.multiple_of(gi * grp, grp)
        rel = jnp.maximum(_dot_nt(ki_scr[pl.ds(start, grp), :], qist_scr[...]), 0.0)
        sc = jnp.zeros((grp, QCHUNK), F32)
        for h in range(IDX_HEADS):
            sc = sc + rel[:, h * QCHUNK:(h + 1) * QCHUNK] * w_t[h:h + 1, :]
        sc = jnp.where(sc == 0.0, 0.0, sc)
        sc_scr[pl.ds(start, grp), :] = jnp.where((gi * grp + gsub) <= qpos, sc, -jnp.inf)

    def score_pair(i, carry):
        score_group(2 * i)
        score_group(2 * i + 1)
        return carry

    lax.fori_loop(0, ngroup // 2, score_pair, 0)

    @pl.when(ngroup % 2 == 1)
    def _():
        score_group(ngroup - 1)

    def count(pred):
        def body(gi, acc):
            for u in range(KEY_GROUP):
                start = pl.multiple_of(gi * grp + u * LANES, LANES)
                acc = jnp.where(pred(sc_scr[pl.ds(start, LANES), :]), acc + 1, acc)
            return acc
        acc = lax.fori_loop(0, ngroup, body, jnp.zeros((LANES, QCHUNK), jnp.int32))
        return jnp.sum(acc, axis=0, keepdims=True)

    def key_to_float(key):
        return pltpu.bitcast(key ^ (lax.shift_right_arithmetic(key, 31) & 0x7FFFFFFF), F32)

    cnt0 = count(lambda st: st >= 0.0)
    code = jnp.where(cnt0 >= topk, 0, INT_MIN).astype(jnp.int32)

    def bit_body(i, code):
        cand = code + lax.shift_left(jnp.int32(1), 30 - i)
        cand_f = key_to_float(cand)
        cnt = count(lambda st: st >= cand_f)
        return jnp.where(cnt >= topk, cand, code)

    code = lax.fori_loop(0, 31, bit_body, code)
    thr = jnp.where(code == INT_MIN, -jnp.inf, key_to_float(jnp.maximum(code, INT_MIN + 1)))
    thr_eff = jnp.maximum(thr, LOWEST_F32)

    cnt_ge = count(lambda st: st >= thr_eff)
    tied = cnt_ge > topk

    @pl.when(jnp.max(jnp.where(tied, 1.0, 0.0)) > 0.5)
    def _():
        def tally_body(gi, ties_before):
            sums = []
            for u in range(KEY_GROUP):
                start = pl.multiple_of(gi * grp + u * LANES, LANES)
                is_tie = sc_scr[pl.ds(start, LANES), :] == thr
                sums.append(jnp.sum(jnp.where(is_tie, 1.0, 0.0), axis=0, keepdims=True))
            for u in range(KEY_GROUP):
                pre_scr[pl.ds(gi * KEY_GROUP + u, 1), :] = ties_before
                ties_before = ties_before + sums[u]
            return ties_before

        ties = lax.fori_loop(0, ngroup, tally_body, jnp.zeros((1, QCHUNK), F32))
        above = cnt_ge.astype(F32) - ties
        need = jnp.where(tied, topk - above, 3.0e38)
        tri = jnp.where(lax.broadcasted_iota(jnp.int32, (LANES, LANES), 1) <= sub, 1.0, 0.0).astype(BF16)

        def demote_group(gi):
            tiles = []
            for u in range(KEY_GROUP):
                start = pl.multiple_of(gi * grp + u * LANES, LANES)
                tiles.append(sc_scr[pl.ds(start, LANES), :])
            eq01 = jnp.concatenate([jnp.where(st == thr, 1.0, 0.0).astype(BF16) for st in tiles], axis=1)
            within = _dot(tri, eq01)
            for u, st in enumerate(tiles):
                start = pl.multiple_of(gi * grp + u * LANES, LANES)
                rank = pre_scr[pl.ds(gi * KEY_GROUP + u, 1), :] + within[:, u * QCHUNK:(u + 1) * QCHUNK]
                drop = jnp.logical_and(st == thr, rank > need)
                sc_scr[pl.ds(start, LANES), :] = jnp.where(drop, -jnp.inf, st)

        def demote_pair(i, carry):
            demote_group(2 * i)
            demote_group(2 * i + 1)
            return carry

        lax.fori_loop(0, ngroup // 2, demote_pair, 0)

        @pl.when(ngroup % 2 == 1)
        def _():
            demote_group(ngroup - 1)

    m_scr[...] = jnp.full(m_scr.shape, M_INIT, F32)
    acc_scr[...] = jnp.zeros_like(acc_scr)

    def score_stage(gi, slot):
        start = pl.multiple_of(gi * grp, grp)
        okb = jnp.where(sc_scr[pl.ds(start, grp), :] >= thr_eff, 0.0, MASKED)
        d_first = c - gi * KEY_GROUP
        _score_stage(lambda pair: k_scr[pl.ds(start, grp), :], qst_scr,
                     lambda h: _bias_rows(tabt_ref, h, d_first, KEY_GROUP) + okb, D_HEADS, s_scr, mx_scr, slot)

    def softmax_stage(gi, slot):
        start = pl.multiple_of(gi * grp, grp)
        _softmax_stage(lambda pair: vt_scr[:, pl.ds(start, grp)], D_HEADS, s_scr, mx_scr, slot, m_scr, acc_scr)

    _pipelined_groups(ngroup, score_stage, softmax_stage)
    _write_heads(o_ref, acc_scr, D_HEADS)


def dsa_attention(qd, kvd, idx, kidx_gain, tabt):
    b, t, _ = qd.shape
    topk = min(DSA_TOPK, t // 4)
    nq = t // QCHUNK
    gk = jnp.concatenate([kidx_gain, kidx_gain]).reshape(1, LANES)
    wq = D_HEADS * HEAD_DIM
    return pl.pallas_call(
        functools.partial(_dsa_kernel, topk=topk),
        grid=(b, nq),
        in_specs=[pl.BlockSpec((1, QCHUNK, wq), lambda i, c: (i, c, 0)),
                  pl.BlockSpec((1, t, 2 * LANES), lambda i, c: (i, 0, 0)),
                  pl.BlockSpec((1, QCHUNK, 4 * LANES), lambda i, c: (i, c, 0)),
                  pl.BlockSpec((1, t, LANES), lambda i, c: (i, 0, 2)),
                  pl.BlockSpec((1, LANES), lambda i, c: (0, 0)),
                  pl.BlockSpec((BIAS_TILES, D_HEADS, LANES, LANES), lambda i, c: (0, 0, 0, 0))],
        out_specs=pl.BlockSpec((1, QCHUNK, wq), lambda i, c: (i, c, 0)),
        out_shape=jax.ShapeDtypeStruct((b, t, wq), BF16),
        scratch_shapes=[pltpu.VMEM((t, LANES), BF16), pltpu.VMEM((LANES, t), BF16), pltpu.VMEM((t, LANES), BF16),
                        pltpu.VMEM((t, QCHUNK), F32),
                        pltpu.VMEM((D_HEADS * QCHUNK, LANES), BF16), pltpu.VMEM((IDX_HEADS * QCHUNK, LANES), BF16),
                        pltpu.VMEM((t // LANES, QCHUNK), F32),
                        pltpu.VMEM((2, KEY_GROUP * LANES, D_HEADS * QCHUNK), F32),
                        pltpu.VMEM((2, 1, D_HEADS * QCHUNK), F32),
                        pltpu.VMEM((1, D_HEADS * QCHUNK), F32),
                        pltpu.VMEM((LANES, D_HEADS * QCHUNK), F32)],
        compiler_params=_cparams("parallel", "arbitrary"),
        name="dsa_attention",
    )(qd, kvd, idx, idx, gk, tabt)


def _dup_heads(w, first, n_heads):
    cols = []
    for h in range(n_heads):
        head = w[:, first + h * HEAD_DIM:first + (h + 1) * HEAD_DIM]
        cols += [head, head]
    return cols


def _even_weight(w_in):
    w = w_in.astype(BF16)
    aq, akv = A_HEADS * HEAD_DIM, A_KV_HEADS * HEAD_DIM
    cols = [w[:, :aq]] + _dup_heads(w, aq, A_KV_HEADS) + _dup_heads(w, aq + akv, A_KV_HEADS) + [w[:, aq + 2 * akv:]]
    return jnp.concatenate(cols, axis=1)


def _odd_weight(w_in):
    w = w_in.astype(BF16)
    cq, ckv, dq = C_HEADS * HEAD_DIM, C_KV_HEADS * HEAD_DIM, D_HEADS * HEAD_DIM
    o_kc, o_vc, o_qd = cq, cq + ckv, cq + 2 * ckv
    o_kd, o_vd, o_qi = o_qd + dq, o_qd + dq + HEAD_DIM, o_qd + dq + 2 * HEAD_DIM
    o_ki = o_qi + IDX_HEADS * IDX_DIM
    o_wi = o_ki + IDX_DIM
    pad = jnp.zeros((w.shape[0], LANES - IDX_HEADS), BF16)
    cols = ([w[:, :cq]] + _dup_heads(w, o_kc, C_KV_HEADS) + _dup_heads(w, o_vc, C_KV_HEADS)
            + [w[:, o_qd:o_kd]] + _dup_heads(w, o_kd, 1) + _dup_heads(w, o_vd, 1)
            + [w[:, o_qi:o_ki]] + _dup_heads(w, o_ki, 1) + [w[:, o_wi:o_wi + IDX_HEADS], pad])
    return jnp.concatenate(cols, axis=1)


def kernel(x, mem, rel_bias, norm_gains, mix_w_out, mem_norm, x_wq, x_wkv, x_wo, ff_w1, ff_w2, ev_w_in, a_sinks, b_mu, b_w0, b_w2, b_a0, b_a2, b_g2, b_k_k, b_k_a, b_r_k, b_ln_w, b_ln_b, od_w_in, d_kidx_norm):
    bsz, t, d = x.shape
    n = bsz * t
    depth = norm_gains.shape[0]
    mlen = mem.shape[1]
    tabt = bias_table(rel_bias, transposed=True)
    x2 = x.reshape(n, d)
    for layer in range(depth):
        i = layer // 2
        g = norm_gains[layer]
        if layer % 2 == 0:
            b_in = 3 * B_WIDTH + B_DECAY_RANK + B_A_RANK + B_GATE_RANK
            qkv, pb = norm_proj(x2, g[0], _even_weight(ev_w_in[i]), (1024, b_in), (F32, F32))
            ya = swa_attention(qkv.reshape(bsz, t, 1024), a_sinks[i], rel_bias, tabt)
            yb = rwkv_mix(pb.reshape(bsz, t, b_in), b_mu[i], b_w0[i], b_w2[i], b_a0[i], b_a2[i], b_g2[i],
                          b_k_k[i], b_k_a[i], b_r_k[i].reshape(-1), b_ln_w[i], b_ln_b[i])
        else:
            qc, kvc, qd, kvd, idx = norm_proj(x2, g[0], _odd_weight(od_w_in[i]),
                                              (512, 512, 512, 256, 512), (F32,) * 5)
            ya = moba_attention(qc.reshape(bsz, t, 512), kvc.reshape(bsz, t, 512), tabt)
            yb = dsa_attention(qd.reshape(bsz, t, 512), kvd.reshape(bsz, t, 256), idx.reshape(bsz, t, 512),
                               d_kidx_norm[i], tabt)
        kmem, vmem = norm_proj(mem.reshape(bsz * mlen, d), mem_norm[layer], x_wkv[layer].astype(BF16),
                               (X_WIDTH, X_WIDTH), (BF16, BF16))
        x3 = post_mixer(ya, yb, x2.reshape(bsz, t, d), kmem.reshape(bsz, mlen, X_WIDTH),
                        vmem.reshape(bsz, mlen, X_WIDTH), mix_w_out[layer], x_wq[layer], x_wo[layer],
                        ff_w1[layer], ff_w2[layer], g)
        x2 = x3.reshape(n, d)
    return x2.reshape(bsz, t, d)
```
